```python
import math
import jax, jax.numpy as jnp
from jax import lax
import numpy as np

D_MODEL = 1024
BATCH = 8
SEQ = 4096
DEPTH = 1

N_ATT_HEADS = 4
ATT_HEAD_DIM = 64
ATT_V_DIM = 2 * ATT_HEAD_DIM
QK_WIDTH = N_ATT_HEADS * 2 * ATT_HEAD_DIM
ATT_WIDTH = N_ATT_HEADS * ATT_V_DIM
Q_BLOCK = 128
ROPE_THETA = 10000.0
SSM_WIDTH = D_MODEL - ATT_WIDTH
SSM_GROUP = 16
SSM_GROUPS = SSM_WIDTH // SSM_GROUP
SSM_STATE = 64
DT_MIN = 1e-3
DT_MAX = 1e-1
MIX_WIDTH = ATT_WIDTH + SSM_WIDTH
IN_WIDTH = 2 * QK_WIDTH + ATT_WIDTH + SSM_WIDTH
N_EXPERTS = 256
TOP_K = 8
N_GROUPS = 8
TOPK_GROUPS = 4
EXPERT_DIM = 256
SHARED_DIM = 256
ROUTED_SCALE = 2.5
MOE_BLOCK = 128
EPS = 1e-6

kernel_name = 'hymba_diffattn_s5_moe_adaln'


def rms_norm(t, g):
    tf = t.astype(jnp.float32)
    y = tf * lax.rsqrt(jnp.mean(tf * tf, axis=-1, keepdims=True) + EPS)
    return (y * g.astype(jnp.float32)).astype(t.dtype)


def modulate(h, shift, scale):
    return h * (1.0 + scale[:, None, :]) + shift[:, None, :]


def rope_tables(seq_len):
    inv_freq = 1.0 / (ROPE_THETA ** (jnp.arange(0, ATT_HEAD_DIM, 2, dtype=jnp.float32) / ATT_HEAD_DIM))
    ang = jnp.arange(seq_len, dtype=jnp.float32)[:, None] * inv_freq[None, :]
    return jnp.cos(ang)[None, :, None, None, :], jnp.sin(ang)[None, :, None, None, :]


def apply_rope(t, cos, sin):
    t = t.astype(jnp.float32)
    t1, t2 = jnp.split(t, 2, axis=-1)
    return jnp.concatenate([t1 * cos - t2 * sin, t1 * sin + t2 * cos], axis=-1)


def lambda_init(layer):
    return 0.8 - 0.6 * math.exp(-0.3 * layer)


def diff_attention(q, k, v, q_g, k_g, lq1, lk1, lq2, lk2, sub_g, cos, sin, lam_init):
    bsz, seq_len, _ = q.shape
    out_dtype = q.dtype
    q = apply_rope(rms_norm(q.reshape(bsz, seq_len, N_ATT_HEADS, 2, ATT_HEAD_DIM), q_g), cos, sin)
    k = apply_rope(rms_norm(k.reshape(bsz, seq_len, N_ATT_HEADS, 2, ATT_HEAD_DIM), k_g), cos, sin)
    v = v.reshape(bsz, seq_len, N_ATT_HEADS, ATT_V_DIM).astype(jnp.float32)
    f32 = jnp.float32
    lam = (jnp.exp(jnp.sum(lq1.astype(f32) * lk1.astype(f32)))
           - jnp.exp(jnp.sum(lq2.astype(f32) * lk2.astype(f32))) + lam_init)
    scale = ATT_HEAD_DIM ** -0.5
    k_pos = jnp.arange(seq_len)

    def one_block(i):
        q_blk = lax.dynamic_slice_in_dim(q, i * Q_BLOCK, Q_BLOCK, axis=1)
        s = jnp.einsum('bqhcd,bkhcd->bhcqk', q_blk, k) * scale
        q_pos = i * Q_BLOCK + jnp.arange(Q_BLOCK)
        causal = k_pos[None, :] <= q_pos[:, None]
        p = jax.nn.softmax(jnp.where(causal, s, -jnp.inf), axis=-1)
        w = p[:, :, 0] - lam * p[:, :, 1]
        return jnp.einsum('bhqk,bkhe->bqhe', w, v)

    o = lax.map(one_block, jnp.arange(seq_len // Q_BLOCK))
    o = jnp.moveaxis(o, 0, 1).reshape(bsz, seq_len, N_ATT_HEADS, ATT_V_DIM)
    o = rms_norm(o, sub_g) * (1.0 - lam_init)
    return o.reshape(bsz, seq_len, ATT_WIDTH).astype(out_dtype)


def complex_scan_combine(earlier, later):
    a1r, a1i, b1r, b1i = earlier
    a2r, a2i, b2r, b2i = later
    return (a2r * a1r - a2i * a1i,
            a2r * a1i + a2i * a1r,
            a2r * b1r - a2i * b1i + b2r,
            a2r * b1i + a2i * b1r + b2i)


def s5_branch(u, a_re, a_im, log_dt, b_re, b_im, c_re, c_im, d_skip, w_glu, norm_g):
    bsz, seq_len, _ = u.shape
    f32 = jnp.float32
    uf = u.astype(f32).reshape(bsz, seq_len, SSM_GROUPS, SSM_GROUP)
    a_re = a_re.astype(f32)
    a_im = a_im.astype(f32)
    dt = jnp.exp(log_dt.astype(f32))[:, None]
    mag = jnp.exp(a_re * dt)
    abar_re = mag * jnp.cos(a_im * dt)
    abar_im = mag * jnp.sin(a_im * dt)
    den = a_re * a_re + a_im * a_im
    nr = abar_re - 1.0
    f_re = ((nr * a_re + abar_im * a_im) / den)[..., None]
    f_im = ((abar_im * a_re - nr * a_im) / den)[..., None]
    b_re = b_re.astype(f32)
    b_im = b_im.astype(f32)
    bb_re = f_re * b_re - f_im * b_im
    bb_im = f_re * b_im + f_im * b_re
    bu_re = jnp.einsum('blgc,gpc->blgp', uf, bb_re)
    bu_im = jnp.einsum('blgc,gpc->blgp', uf, bb_im)
    a_seq_re = jnp.broadcast_to(abar_re, (1, seq_len, SSM_GROUPS, SSM_STATE))
    a_seq_im = jnp.broadcast_to(abar_im, (1, seq_len, SSM_GROUPS, SSM_STATE))
    _, _, s_re, s_im = lax.associative_scan(complex_scan_combine, (a_seq_re, a_seq_im, bu_re, bu_im), axis=1)
    y = (jnp.einsum('blgp,gcp->blgc', s_re, c_re.astype(f32))
         - jnp.einsum('blgp,gcp->blgc', s_im, c_im.astype(f32))
         + d_skip.astype(f32) * uf)
    y = jax.nn.gelu(y.reshape(bsz, seq_len, SSM_WIDTH))
    y = y * jax.nn.sigmoid(y @ w_glu.astype(f32))
    return rms_norm(y, norm_g).astype(u.dtype)


def swiglu(t, w_g, w_u, w_d):
    return (jax.nn.silu(t @ w_g) * (t @ w_u)) @ w_d


def route(hf, w_router, r_bias):
    n_tok = hf.shape[0]
    per_group = N_EXPERTS // N_GROUPS
    s = jax.nn.sigmoid((hf @ w_router).astype(jnp.float32))
    sb = s + r_bias.astype(jnp.float32)
    grp_score = lax.top_k(sb.reshape(n_tok, N_GROUPS, per_group), 2)[0].sum(axis=-1)
    _, g_idx = lax.top_k(grp_score, TOPK_GROUPS)
    g_mask = jax.nn.one_hot(g_idx, N_GROUPS, dtype=jnp.float32).sum(axis=-2) > 0
    e_mask = jnp.repeat(g_mask, per_group, axis=-1)
    _, e_idx = lax.top_k(jnp.where(e_mask, sb, -jnp.inf), TOP_K)
    w = jnp.take_along_axis(s, e_idx, axis=-1)
    w = w / jnp.sum(w, axis=-1, keepdims=True) * ROUTED_SCALE
    return e_idx, w


def routed_experts(hf, e_idx, gate_w, w_g, w_u, w_d):
    n_tok = hf.shape[0]
    n_assign = n_tok * TOP_K
    n_blocks = (n_assign + N_EXPERTS * (MOE_BLOCK - 1) + MOE_BLOCK - 1) // MOE_BLOCK
    e_flat = e_idx.reshape(n_assign)
    t_flat = jnp.repeat(jnp.arange(n_tok, dtype=jnp.int32), TOP_K)
    g_flat = gate_w.reshape(n_assign).astype(hf.dtype)
    order = jnp.argsort(e_flat)
    e_sorted = e_flat[order]
    counts = jnp.bincount(e_flat, length=N_EXPERTS)
    starts = jnp.cumsum(counts) - counts
    padded = (counts + MOE_BLOCK - 1) // MOE_BLOCK * MOE_BLOCK
    pad_ends = jnp.cumsum(padded)
    pad_starts = pad_ends - padded
    dest = pad_starts[e_sorted] + jnp.arange(n_assign, dtype=jnp.int32) - starts[e_sorted]
    slot_tok = jnp.zeros((n_blocks * MOE_BLOCK,), jnp.int32).at[dest].set(t_flat[order])
    slot_gate = jnp.zeros((n_blocks * MOE_BLOCK,), hf.dtype).at[dest].set(g_flat[order])
    block_start = jnp.arange(n_blocks, dtype=jnp.int32) * MOE_BLOCK
    block_expert = jnp.minimum(jnp.searchsorted(pad_ends, block_start, side='right'), N_EXPERTS - 1)

    def one_block(acc, blk):
        tok, gate, e = blk
        yb = swiglu(hf[tok], w_g[e], w_u[e], w_d[e]) * gate[:, None]
        return acc.at[tok].add(yb), None

    out, _ = lax.scan(one_block, jnp.zeros_like(hf),
                      (slot_tok.reshape(n_blocks, MOE_BLOCK), slot_gate.reshape(n_blocks, MOE_BLOCK), block_expert))
    return out


def setup_inputs(seed: int = 0) -> dict:
    key = jax.random.key(seed)
    ks = jax.random.split(key, 33)
    nrm = jax.random.normal
    L, D = DEPTH, D_MODEL
    G, P, C = SSM_GROUPS, SSM_STATE, SSM_GROUP
    E, DE, DS = N_EXPERTS, EXPERT_DIM, SHARED_DIM
    x = nrm(ks[0], (BATCH, SEQ, D), jnp.float32)
    c = nrm(ks[1], (BATCH, D), jnp.float32)
    norm1_g = 1.0 + 0.02 * nrm(ks[2], (L, D))
    norm2_g = 1.0 + 0.02 * nrm(ks[3], (L, D))
    w_ada = 0.5 * D ** -0.5 * nrm(ks[4], (L, D, 6 * D))
    b_ada = 0.02 * nrm(ks[5], (L, 6 * D))
    w_in = D ** -0.5 * nrm(ks[6], (L, D, IN_WIDTH))
    q_norm_g = 1.0 + 0.02 * nrm(ks[7], (L, ATT_HEAD_DIM))
    k_norm_g = 1.0 + 0.02 * nrm(ks[8], (L, ATT_HEAD_DIM))
    lambda_q1 = 0.1 * nrm(ks[9], (L, ATT_HEAD_DIM))
    lambda_k1 = 0.1 * nrm(ks[10], (L, ATT_HEAD_DIM))
    lambda_q2 = 0.1 * nrm(ks[11], (L, ATT_HEAD_DIM))
    lambda_k2 = 0.1 * nrm(ks[12], (L, ATT_HEAD_DIM))
    subln_g = 1.0 + 0.02 * nrm(ks[13], (L, ATT_V_DIM))
    ssm_a_re = -0.5 * (1.0 + 0.05 * jax.random.uniform(ks[14], (L, G, P), minval=-1.0, maxval=1.0))
    ssm_a_im = math.pi * jnp.arange(P, dtype=jnp.float32) + 0.01 * nrm(ks[15], (L, G, P))
    ssm_log_dt = jax.random.uniform(ks[16], (L, G), minval=math.log(DT_MIN), maxval=math.log(DT_MAX))
    ssm_b_re = (2 * C) ** -0.5 * nrm(ks[17], (L, G, P, C))
    ssm_b_im = (2 * C) ** -0.5 * nrm(ks[18], (L, G, P, C))
    ssm_c_re = (2 * P) ** -0.5 * nrm(ks[19], (L, G, C, P))
    ssm_c_im = (2 * P) ** -0.5 * nrm(ks[20], (L, G, C, P))
    ssm_d = nrm(ks[21], (L, G, C))
    w_glu = SSM_WIDTH ** -0.5 * nrm(ks[22], (L, SSM_WIDTH, SSM_WIDTH))
    ssm_norm_g = 1.0 + 0.02 * nrm(ks[23], (L, SSM_WIDTH))
    w_out = MIX_WIDTH ** -0.5 * nrm(ks[24], (L, MIX_WIDTH, D))
    w_router = D ** -0.5 * nrm(ks[25], (L, D, E))
    router_bias = 0.01 * nrm(ks[26], (L, E))
    w_gate_e = D ** -0.5 * nrm(ks[27], (L, E, D, DE))
    w_up_e = D ** -0.5 * nrm(ks[28], (L, E, D, DE))
    w_down_e = DE ** -0.5 * nrm(ks[29], (L, E, DE, D))
    w_gate_s = D ** -0.5 * nrm(ks[30], (L, D, DS))
    w_up_s = D ** -0.5 * nrm(ks[31], (L, D, DS))
    w_down_s = DS ** -0.5 * nrm(ks[32], (L, DS, D))
    return {'x': x, 'c': c, 'norm1_g': norm1_g, 'norm2_g': norm2_g, 'w_ada': w_ada, 'b_ada': b_ada,
            'w_in': w_in, 'q_norm_g': q_norm_g, 'k_norm_g': k_norm_g,
            'lambda_q1': lambda_q1, 'lambda_k1': lambda_k1, 'lambda_q2': lambda_q2, 'lambda_k2': lambda_k2,
            'subln_g': subln_g, 'ssm_a_re': ssm_a_re, 'ssm_a_im': ssm_a_im, 'ssm_log_dt': ssm_log_dt,
            'ssm_b_re': ssm_b_re, 'ssm_b_im': ssm_b_im, 'ssm_c_re': ssm_c_re, 'ssm_c_im': ssm_c_im,
            'ssm_d': ssm_d, 'w_glu': w_glu, 'ssm_norm_g': ssm_norm_g, 'w_out': w_out,
            'w_router': w_router, 'router_bias': router_bias,
            'w_gate_e': w_gate_e, 'w_up_e': w_up_e, 'w_down_e': w_down_e,
            'w_gate_s': w_gate_s, 'w_up_s': w_up_s, 'w_down_s': w_down_s}


def reference(x, c, norm1_g, norm2_g, w_ada, b_ada, w_in, q_norm_g, k_norm_g,
              lambda_q1, lambda_k1, lambda_q2, lambda_k2, subln_g,
              ssm_a_re, ssm_a_im, ssm_log_dt, ssm_b_re, ssm_b_im, ssm_c_re, ssm_c_im,
              ssm_d, w_glu, ssm_norm_g, w_out, w_router, router_bias,
              w_gate_e, w_up_e, w_down_e, w_gate_s, w_up_s, w_down_s):
    bsz, seq_len, _ = x.shape
    cos, sin = rope_tables(seq_len)
    split_at = [QK_WIDTH, 2 * QK_WIDTH, 2 * QK_WIDTH + ATT_WIDTH]
    for layer in range(DEPTH):
        mod = jax.nn.silu(c) @ w_ada[layer] + b_ada[layer]
        sh1, sc1, g1, sh2, sc2, g2 = jnp.split(mod, 6, axis=-1)
        h = modulate(rms_norm(x, norm1_g[layer]), sh1, sc1)
        q, k, v, u = jnp.split(h @ w_in[layer], split_at, axis=-1)
        att = diff_attention(q, k, v, q_norm_g[layer], k_norm_g[layer],
                             lambda_q1[layer], lambda_k1[layer], lambda_q2[layer], lambda_k2[layer],
                             subln_g[layer], cos, sin, lambda_init(layer))
        ssm = s5_branch(u, ssm_a_re[layer], ssm_a_im[layer], ssm_log_dt[layer],
                        ssm_b_re[layer], ssm_b_im[layer], ssm_c_re[layer], ssm_c_im[layer],
                        ssm_d[layer], w_glu[layer], ssm_norm_g[layer])
        mix = jnp.concatenate([att, ssm], axis=-1) @ w_out[layer]
        x = x + g1[:, None, :] * mix
        h = modulate(rms_norm(x, norm2_g[layer]), sh2, sc2).reshape(bsz * seq_len, D_MODEL)
        e_idx, gate_w = route(h, w_router[layer], router_bias[layer])
        ffn = (routed_experts(h, e_idx, gate_w, w_gate_e[layer], w_up_e[layer], w_down_e[layer])
               + swiglu(h, w_gate_s[layer], w_up_s[layer], w_down_s[layer]))
        x = x + g2[:, None, :] * ffn.reshape(bsz, seq_len, D_MODEL)
    return x
```

```python
import functools
import math

import jax
import jax.numpy as jnp
from jax import lax
from jax.experimental import pallas as pl
from jax.experimental.pallas import tpu as pltpu

F32 = jnp.float32
BF16 = jnp.bfloat16
I32 = jnp.int32

LANES = 128
SUBLANES = 8

N_ATT_HEADS = 4
ATT_HEAD_DIM = 64
ATT_V_DIM = 2 * ATT_HEAD_DIM
QK_WIDTH = N_ATT_HEADS * 2 * ATT_HEAD_DIM
ATT_WIDTH = N_ATT_HEADS * ATT_V_DIM
ROPE_THETA = 10000.0
SSM_GROUP = 16
SSM_GROUPS = 32
SSM_STATE = 64
SSM_WIDTH = SSM_GROUPS * SSM_GROUP
SSM_CHUNK = SUBLANES
SSM_LANE_BLOCKS = SSM_WIDTH // LANES
GROUPS_PER_BLOCK = LANES // SSM_GROUP
N_EXPERTS = 256
TOP_K = 8
N_GROUPS = 8
TOPK_GROUPS = 4
EXPERTS_PER_GROUP = N_EXPERTS // N_GROUPS
EXPERT_DIM = 256
SHARED_DIM = 256
ROUTED_SCALE = 2.5
EPS = 1e-6

NT_DIMS = (((1,), (1,)), ((), ()))

VMEM_LIMIT = 48 * 1024 * 1024


def _dot(a, b):
    return jnp.dot(a, b, preferred_element_type=F32)


def _sigmoid(x):
    return 1.0 / (1.0 + jnp.exp(-x))


def _split_bf16(x):
    hi = x.astype(BF16)
    lo = (x - hi.astype(F32)).astype(BF16)
    return hi, lo


def _adaln_kernel(c_ref, w_ref, b_ref, o_ref):
    c = c_ref[...]
    sc = c * _sigmoid(c)
    o_ref[...] = jnp.dot(sc, w_ref[...], preferred_element_type=F32,
                         precision=lax.Precision.HIGHEST) + b_ref[...]


def _adaln(c, w, b):
    bsz, d = c.shape
    n = w.shape[1]
    tn = 1024
    return pl.pallas_call(
        _adaln_kernel,
        out_shape=jax.ShapeDtypeStruct((bsz, n), F32),
        grid=(n // tn,),
        in_specs=[pl.BlockSpec((bsz, d), lambda j: (0, 0)),
                  pl.BlockSpec((d, tn), lambda j: (0, j)),
                  pl.BlockSpec((1, tn), lambda j: (0, j))],
        out_specs=pl.BlockSpec((bsz, tn), lambda j: (0, j)),
        compiler_params=pltpu.CompilerParams(dimension_semantics=("arbitrary",),
                                             vmem_limit_bytes=VMEM_LIMIT),
        name="adaln",
    )(c, w, b.reshape(1, n))


def _inproj_kernel(x_ref, mod_ref, n1g_ref, w_ref, seg_ref, qg_ref, kg_ref, cos_ref, sin_ref,
                   q_ref, k_ref, v_ref, u_ref):
    x = x_ref[...]
    mod = mod_ref[0]
    shift, scale = mod[0:1], mod[1:2]
    ms = jnp.mean(x * x, axis=-1, keepdims=True)
    h = x * lax.rsqrt(ms + EPS) * n1g_ref[...]
    h = h * (1.0 + scale) + shift
    z = _dot(h.astype(BF16), w_ref[...])
    seg = seg_ref[...]
    cos = cos_ref[...]
    sin = sin_ref[...]
    lane = lax.broadcasted_iota(I32, cos.shape, 1)
    first_half = (lane % ATT_HEAD_DIM) < (ATT_HEAD_DIM // 2)

    def norm_rope(t, g_ref, out_scale):
        hi, lo = _split_bf16(t * t)
        msq = _dot(hi, seg) + _dot(lo, seg)
        tn = t * lax.rsqrt(msq + EPS) * g_ref[...]
        outs = []
        for hd in range(N_ATT_HEADS):
            th = tn[:, hd * LANES:(hd + 1) * LANES]
            partner = jnp.where(first_half,
                                pltpu.roll(th, LANES - ATT_HEAD_DIM // 2, 1),
                                pltpu.roll(th, ATT_HEAD_DIM // 2, 1))
            outs.append((th * cos + partner * sin) * out_scale)
        return jnp.concatenate(outs, axis=1)

    q_ref[...] = norm_rope(z[:, :QK_WIDTH], qg_ref, ATT_HEAD_DIM ** -0.5).astype(BF16)
    k_ref[...] = norm_rope(z[:, QK_WIDTH:2 * QK_WIDTH], kg_ref, 1.0).astype(BF16)
    v_ref[...] = z[:, 2 * QK_WIDTH:2 * QK_WIDTH + ATT_WIDTH].astype(BF16)
    u_ref[...] = z[:, 2 * QK_WIDTH + ATT_WIDTH:]


def _inproj(x2d, mod3, n1g, w_in_bf, seg, qg, kg, cos_t, sin_t, seq_len, tm):
    n_tok, d = x2d.shape
    tiles_per_seq = seq_len // tm
    in_width = w_in_bf.shape[1]
    row = lambda i: (i, 0)
    const = lambda i: (0, 0)
    return pl.pallas_call(
        _inproj_kernel,
        out_shape=(jax.ShapeDtypeStruct((n_tok, QK_WIDTH), BF16),
                   jax.ShapeDtypeStruct((n_tok, QK_WIDTH), BF16),
                   jax.ShapeDtypeStruct((n_tok, ATT_WIDTH), BF16),
                   jax.ShapeDtypeStruct((n_tok, SSM_WIDTH), F32)),
        grid=(n_tok // tm,),
        in_specs=[pl.BlockSpec((tm, d), row),
                  pl.BlockSpec((1, 6, d), lambda i: (i // tiles_per_seq, 0, 0)),
                  pl.BlockSpec((1, d), const),
                  pl.BlockSpec((d, in_width), const),
                  pl.BlockSpec((QK_WIDTH, QK_WIDTH), const),
                  pl.BlockSpec((1, QK_WIDTH), const),
                  pl.BlockSpec((1, QK_WIDTH), const),
                  pl.BlockSpec((tm, LANES), lambda i: (i % tiles_per_seq, 0)),
                  pl.BlockSpec((tm, LANES), lambda i: (i % tiles_per_seq, 0))],
        out_specs=(pl.BlockSpec((tm, QK_WIDTH), row),
                   pl.BlockSpec((tm, QK_WIDTH), row),
                   pl.BlockSpec((tm, ATT_WIDTH), row),
                   pl.BlockSpec((tm, SSM_WIDTH), row)),
        compiler_params=pltpu.CompilerParams(dimension_semantics=("arbitrary",),
                                             vmem_limit_bytes=VMEM_LIMIT),
        name="inproj",
    )(x2d, mod3, n1g, w_in_bf, seg, qg, kg, cos_t, sin_t)


def _attn_kernel(qi_ref, kj_ref, q_ref, k_ref, v_ref, lam_ref, sg_ref, o_ref,
                 m_ref, l_ref, acc_ref, *, out_scale):
    p = pl.program_id(2)
    qi = qi_ref[p]
    kj = kj_ref[p]

    @pl.when(kj == 0)
    def _():
        m_ref[...] = jnp.full(m_ref.shape, -jnp.inf, F32)
        l_ref[...] = jnp.zeros(l_ref.shape, F32)
        acc_ref[...] = jnp.zeros(acc_ref.shape, F32)

    def step(masked):
        q = q_ref[...]
        k = k_ref[...]
        v = v_ref[...]
        lane = lax.broadcasted_iota(I32, q.shape, 1)
        zero = jnp.zeros_like(q)
        maps = (jnp.where(lane < ATT_HEAD_DIM, q, zero), jnp.where(lane >= ATT_HEAD_DIM, q, zero))
        for c in range(2):
            s = lax.dot_general(maps[c], k, NT_DIMS, preferred_element_type=F32)
            if masked:
                r = lax.broadcasted_iota(I32, s.shape, 0)
                col = lax.broadcasted_iota(I32, s.shape, 1)
                s = jnp.where(col <= r, s, -jnp.inf)
            m_old = m_ref[c]
            m_new = jnp.maximum(m_old, jnp.max(s, axis=-1, keepdims=True))
            alpha = jnp.exp(m_old - m_new)
            pr = jnp.exp(s - m_new)
            l_ref[c] = alpha * l_ref[c] + jnp.sum(pr, axis=-1, keepdims=True)
            acc_ref[c] = alpha * acc_ref[c] + _dot(pr.astype(BF16), v)
            m_ref[c] = m_new

    @pl.when(kj < qi)
    def _():
        step(False)

    @pl.when(kj == qi)
    def _():
        step(True)
        o = acc_ref[0] / l_ref[0] - lam_ref[...] * (acc_ref[1] / l_ref[1])
        ms = jnp.mean(o * o, axis=-1, keepdims=True)
        o_ref[...] = (o * lax.rsqrt(ms + EPS) * sg_ref[...] * out_scale).astype(o_ref.dtype)


def _attention(q, k, v, lam_row, subln_g, bsz, seq_len, tq, out_scale):
    nq = seq_len // tq
    pairs = [(i, j) for i in range(nq) for j in range(i + 1)]
    qi = jnp.asarray([p[0] for p in pairs], I32)
    kj = jnp.asarray([p[1] for p in pairs], I32)
    q_map = lambda b, h, p, qi, kj: (b * nq + qi[p], h)
    k_map = lambda b, h, p, qi, kj: (b * nq + kj[p], h)
    const = lambda b, h, p, qi, kj: (0, 0)
    return pl.pallas_call(
        functools.partial(_attn_kernel, out_scale=out_scale),
        out_shape=jax.ShapeDtypeStruct((bsz * seq_len, ATT_WIDTH), BF16),
        grid_spec=pltpu.PrefetchScalarGridSpec(
            num_scalar_prefetch=2,
            grid=(bsz, N_ATT_HEADS, len(pairs)),
            in_specs=[pl.BlockSpec((tq, LANES), q_map),
                      pl.BlockSpec((tq, LANES), k_map),
                      pl.BlockSpec((tq, LANES), k_map),
                      pl.BlockSpec((1, LANES), const),
                      pl.BlockSpec((1, LANES), const)],
            out_specs=pl.BlockSpec((tq, LANES), q_map),
            scratch_shapes=[pltpu.VMEM((2, tq, 1), F32),
                            pltpu.VMEM((2, tq, 1), F32),
                            pltpu.VMEM((2, tq, ATT_V_DIM), F32)]),
        compiler_params=pltpu.CompilerParams(
            dimension_semantics=("arbitrary", "arbitrary", "arbitrary"),
            vmem_limit_bytes=VMEM_LIMIT),
        name="attn",
    )(qi, kj, q, k, v, lam_row, subln_g)


def _ssm_mats(a_re, a_im, log_dt, b_re, b_im, c_re, c_im, d_skip):
    a_re, a_im, b_re, b_im, c_re, c_im, d_skip = (
        t.astype(F32) for t in (a_re, a_im, b_re, b_im, c_re, c_im, d_skip))
    dt = jnp.exp(log_dt.astype(F32))[:, None]
    mag = jnp.exp(a_re * dt)
    abar_re = mag * jnp.cos(a_im * dt)
    abar_im = mag * jnp.sin(a_im * dt)
    den = a_re * a_re + a_im * a_im
    nr = abar_re - 1.0
    f_re = ((nr * a_re + abar_im * a_im) / den)[..., None]
    f_im = ((abar_im * a_re - nr * a_im) / den)[..., None]
    bb_re = f_re * b_re - f_im * b_im
    bb_im = f_re * b_im + f_im * b_re
    steps = jnp.arange(SSM_CHUNK + 1, dtype=F32)[:, None, None]
    pmag = jnp.exp(a_re * dt * steps)
    pw_re = pmag * jnp.cos(a_im * dt * steps)
    pw_im = pmag * jnp.sin(a_im * dt * steps)
    ca_re = c_re[None] * pw_re[:, :, None, :] - c_im[None] * pw_im[:, :, None, :]
    ca_im = c_re[None] * pw_im[:, :, None, :] + c_im[None] * pw_re[:, :, None, :]
    hp = lax.Precision.HIGHEST
    lag = (jnp.einsum('mgcp,gpd->mgcd', ca_re[:SSM_CHUNK], bb_re, precision=hp)
           - jnp.einsum('mgcp,gpd->mgcd', ca_im[:SSM_CHUNK], bb_im, precision=hp))
    lag = lag.at[0].add(d_skip[:, :, None] * jnp.eye(SSM_GROUP, dtype=F32)[None])
    jt = jnp.arange(SSM_CHUNK)
    diff = jt[None, :] - jt[:, None]
    toep = jnp.where((diff >= 0)[:, :, None, None, None],
                     lag[jnp.clip(diff, 0, SSM_CHUNK - 1)], 0.0)
    nb, gpb = SSM_LANE_BLOCKS, GROUPS_PER_BLOCK
    eye_g = jnp.eye(gpb, dtype=F32)
    toep = toep.reshape(SSM_CHUNK, SSM_CHUNK, nb, gpb, SSM_GROUP, SSM_GROUP)
    kmat = jnp.einsum('jtbgcd,gh->bjgdthc', toep, eye_g)
    kmat = kmat.reshape(nb, SSM_CHUNK * LANES, SSM_CHUNK * LANES)
    rev = SSM_CHUNK - 1 - jt
    w_re = pw_re[rev][..., None] * bb_re[None] - pw_im[rev][..., None] * bb_im[None]
    w_im = pw_re[rev][..., None] * bb_im[None] + pw_im[rev][..., None] * bb_re[None]

    def in_to_state(w):
        w = w.reshape(SSM_CHUNK, nb, gpb, SSM_STATE, SSM_GROUP)
        return jnp.einsum('jbgpd,gh->bjgdhp', w, eye_g).reshape(nb, SSM_CHUNK * LANES, gpb * SSM_STATE)

    bmat = jnp.concatenate([in_to_state(w_re), in_to_state(w_im)], axis=-1)

    def state_to_out(ca):
        ca = ca[1:].reshape(SSM_CHUNK, nb, gpb, SSM_GROUP, SSM_STATE)
        return jnp.einsum('tbgcp,gh->bgpthc', ca, eye_g).reshape(nb, gpb * SSM_STATE, SSM_CHUNK * LANES)

    cmat = jnp.concatenate([state_to_out(ca_re), -state_to_out(ca_im)], axis=1)
    a8 = jnp.concatenate([pw_re[SSM_CHUNK].reshape(nb, 1, gpb * SSM_STATE),
                          pw_im[SSM_CHUNK].reshape(nb, 1, gpb * SSM_STATE)], axis=-1)
    return kmat.astype(BF16), bmat.astype(BF16), cmat.astype(BF16), a8


def _ssm_kernel(u_ref, km_ref, bm_ref, cm_ref, a8_ref, y_ref, carry_ref, se_ref, sp_ref):
    n_chunks = se_ref.shape[0]
    half = se_ref.shape[1] // 2

    @pl.when(pl.program_id(2) == 0)
    def _():
        carry_ref[...] = jnp.zeros(carry_ref.shape, F32)

    ucat = jnp.concatenate([u_ref[pl.ds(j, n_chunks, stride=SSM_CHUNK), :] for j in range(SSM_CHUNK)],
                           axis=1).astype(BF16)
    se_ref[...] = _dot(ucat, bm_ref[0])
    a8 = a8_ref[0]
    a_re, a_im = a8[:, :half], a8[:, half:]

    def body(m, carry):
        c_re, c_im = carry
        sp_ref[pl.ds(m, 1), :] = jnp.concatenate([c_re, c_im], axis=1)
        row = se_ref[pl.ds(m, 1), :]
        return (a_re * c_re - a_im * c_im + row[:, :half],
                a_re * c_im + a_im * c_re + row[:, half:])

    c_re, c_im = lax.fori_loop(0, n_chunks, body, (carry_ref[:, :half], carry_ref[:, half:]))
    carry_ref[...] = jnp.concatenate([c_re, c_im], axis=1)
    y = _dot(ucat, km_ref[0]) + _dot(sp_ref[...].astype(BF16), cm_ref[0])
    for t in range(SSM_CHUNK):
        y_ref[pl.ds(t, n_chunks, stride=SSM_CHUNK), :] = y[:, t * LANES:(t + 1) * LANES]


def _ssm(u, kmat, bmat, cmat, a8, bsz, seq_len, tt):
    nt = seq_len // tt
    n_chunks = tt // SSM_CHUNK
    width = SSM_CHUNK * LANES
    u_map = lambda g, b, i: (b * nt + i, g)
    w_map = lambda g, b, i: (g, 0, 0)
    return pl.pallas_call(
        _ssm_kernel,
        out_shape=jax.ShapeDtypeStruct((bsz * seq_len, SSM_WIDTH), F32),
        grid=(SSM_LANE_BLOCKS, bsz, nt),
        in_specs=[pl.BlockSpec((tt, LANES), u_map),
                  pl.BlockSpec((1, width, width), w_map),
                  pl.BlockSpec((1, width, width), w_map),
                  pl.BlockSpec((1, width, width), w_map),
                  pl.BlockSpec((1, 1, width), w_map)],
        out_specs=pl.BlockSpec((tt, LANES), u_map),
        scratch_shapes=[pltpu.VMEM((1, width), F32),
                        pltpu.VMEM((n_chunks, width), F32),
                        pltpu.VMEM((n_chunks, width), F32)],
        compiler_params=pltpu.CompilerParams(
            dimension_semantics=("arbitrary", "arbitrary", "arbitrary"),
            vmem_limit_bytes=VMEM_LIMIT),
        name="ssm",
    )(u, kmat, bmat, cmat, a8)


def _first_max(v, ids, n):
    m = jnp.max(v, axis=0, keepdims=True)
    ix = jnp.min(jnp.where(v == m, ids, n), axis=0, keepdims=True)
    return m, ix


def _mix_route_kernel(x_ref, y_ref, att_ref, mod_ref, wglu_ref, sng_ref, woa_ref, wos_ref, n2g_ref,
                      wrh_ref, wrl_ref, rb_ref, wgu_ref, wds_ref,
                      xs_ref, h2r_ref, eidx_ref, gate_ref, rank_ref, cnt_ref, carry_ref):
    tm = x_ref.shape[0]

    @pl.when(pl.program_id(0) == 0)
    def _():
        carry_ref[...] = jnp.zeros(carry_ref.shape, F32)

    mod = mod_ref[0]
    gate1, shift2, scale2, gate2 = mod[2:3], mod[3:4], mod[4:5], mod[5:6]

    y = y_ref[...]
    g = 0.5 * y * (1.0 + jnp.tanh(math.sqrt(2.0 / math.pi) * (y + 0.044715 * (y * y * y))))
    glu = g * _sigmoid(_dot(g.astype(BF16), wglu_ref[...]))
    ssm = glu * lax.rsqrt(jnp.mean(glu * glu, axis=-1, keepdims=True) + EPS) * sng_ref[...]

    mix = _dot(att_ref[...], woa_ref[...]) + _dot(ssm.astype(BF16), wos_ref[...])
    x1 = x_ref[...] + gate1 * mix
    h2 = x1 * lax.rsqrt(jnp.mean(x1 * x1, axis=-1, keepdims=True) + EPS) * n2g_ref[...]
    h2 = h2 * (1.0 + scale2) + shift2
    for s in range(SUBLANES):
        h2r_ref[pl.ds(s, tm, stride=SUBLANES), :] = h2[:, s * LANES:(s + 1) * LANES]
    hb, h_lo = _split_bf16(h2)

    gu = _dot(hb, wgu_ref[...])
    gs, us = gu[:, :SHARED_DIM], gu[:, SHARED_DIM:]
    act = (gs * _sigmoid(gs) * us).astype(BF16)
    xs_ref[...] = x1 + gate2 * _dot(act, wds_ref[...])

    wrh = wrh_ref[...]
    logits = (lax.dot_general(wrh, hb, NT_DIMS, preferred_element_type=F32)
              + lax.dot_general(wrh, h_lo, NT_DIMS, preferred_element_type=F32)
              + lax.dot_general(wrl_ref[...], hb, NT_DIMS, preferred_element_type=F32))
    score = _sigmoid(logits)
    biased = score + rb_ref[...]
    neg = -jnp.inf

    ids_g = lax.broadcasted_iota(I32, (EXPERTS_PER_GROUP, tm), 0)
    group_rows = []
    for gi in range(N_GROUPS):
        blk = biased[gi * EXPERTS_PER_GROUP:(gi + 1) * EXPERTS_PER_GROUP, :]
        m1, i1 = _first_max(blk, ids_g, EXPERTS_PER_GROUP)
        m2 = jnp.max(jnp.where(ids_g == i1, neg, blk), axis=0, keepdims=True)
        group_rows.append(m1 + m2)
    cur = jnp.concatenate(group_rows, axis=0)
    ids_8 = lax.broadcasted_iota(I32, cur.shape, 0)
    picked = jnp.zeros(cur.shape, F32)
    for _ in range(TOPK_GROUPS):
        _, ix = _first_max(cur, ids_8, N_GROUPS)
        hit = ids_8 == ix
        picked = jnp.where(hit, 1.0, picked)
        cur = jnp.where(hit, neg, cur)
    e_mask = jnp.concatenate(
        [jnp.broadcast_to(picked[gi:gi + 1, :], (EXPERTS_PER_GROUP, tm)) for gi in range(N_GROUPS)], axis=0)
    cand = jnp.where(e_mask > 0.0, biased, neg)

    ids_e = lax.broadcasted_iota(I32, cand.shape, 0)
    sel = jnp.zeros(cand.shape, F32)
    idx_rows, w_rows = [], []
    for _ in range(TOP_K):
        _, ix = _first_max(cand, ids_e, N_EXPERTS)
        hit = ids_e == ix
        idx_rows.append(ix)
        w_rows.append(jnp.sum(jnp.where(hit, score, 0.0), axis=0, keepdims=True))
        sel = jnp.where(hit, 1.0, sel)
        cand = jnp.where(hit, neg, cand)
    w_sum = w_rows[0]
    for w in w_rows[1:]:
        w_sum = w_sum + w
    gate_ref[...] = jnp.concatenate([w / w_sum * ROUTED_SCALE for w in w_rows], axis=0)
    eidx_ref[...] = jnp.concatenate(idx_rows, axis=0)

    t_row = lax.broadcasted_iota(I32, (tm, tm), 0)
    t_col = lax.broadcasted_iota(I32, (tm, tm), 1)
    earlier = jnp.where(t_row < t_col, 1.0, 0.0).astype(BF16)
    before = _dot(sel.astype(BF16), earlier) + carry_ref[...]
    rank_ref[...] = jnp.concatenate(
        [jnp.sum(jnp.where(ids_e == ix, before, 0.0), axis=0, keepdims=True) for ix in idx_rows],
        axis=0).astype(I32)
    carry_ref[...] = carry_ref[...] + jnp.sum(sel, axis=1, keepdims=True)
    cnt_ref[...] = carry_ref[...]


def _mix_route(x2d, y, att, mod3, wglu, sng, woa, wos, n2g, wrh, wrl, rb, wgu, wds, seq_len, tm):
    n_tok, d = x2d.shape
    tiles_per_seq = seq_len // tm
    row = lambda i: (i, 0)
    col = lambda i: (0, i)
    const = lambda i: (0, 0)
    full = lambda a: pl.BlockSpec(a.shape, const)
    return pl.pallas_call(
        _mix_route_kernel,
        out_shape=(jax.ShapeDtypeStruct((n_tok, d), F32),
                   jax.ShapeDtypeStruct((n_tok * SUBLANES, LANES), F32),
                   jax.ShapeDtypeStruct((TOP_K, n_tok), I32),
                   jax.ShapeDtypeStruct((TOP_K, n_tok), F32),
                   jax.ShapeDtypeStruct((TOP_K, n_tok), I32),
                   jax.ShapeDtypeStruct((N_EXPERTS, 1), F32)),
        grid=(n_tok // tm,),
        in_specs=[pl.BlockSpec((tm, d), row),
                  pl.BlockSpec((tm, SSM_WIDTH), row),
                  pl.BlockSpec((tm, ATT_WIDTH), row),
                  pl.BlockSpec((1, 6, d), lambda i: (i // tiles_per_seq, 0, 0)),
                  full(wglu), full(sng), full(woa), full(wos), full(n2g),
                  full(wrh), full(wrl), full(rb), full(wgu), full(wds)],
        out_specs=(pl.BlockSpec((tm, d), row),
                   pl.BlockSpec((tm * SUBLANES, LANES), row),
                   pl.BlockSpec((TOP_K, tm), col),
                   pl.BlockSpec((TOP_K, tm), col),
                   pl.BlockSpec((TOP_K, tm), col),
                   pl.BlockSpec((N_EXPERTS, 1), const)),
        scratch_shapes=[pltpu.VMEM((N_EXPERTS, 1), F32)],
        compiler_params=pltpu.CompilerParams(dimension_semantics=("arbitrary",),
                                             vmem_limit_bytes=VMEM_LIMIT),
        name="mix_route",
    )(x2d, y, att, mod3, wglu, sng, woa, wos, n2g, wrh, wrl, rb, wgu, wds)


def _row(ref, r):
    return ref.at[pl.ds(pl.multiple_of(r * SUBLANES, SUBLANES), SUBLANES), :]


def _dispatch_kernel(ps_ref, e_ref, r_ref, h2r_ref, xs_ref, sem):
    td = e_ref.shape[1]
    base = pl.program_id(0) * td

    def copies(t):
        return [pltpu.make_async_copy(_row(h2r_ref, base + t),
                                      _row(xs_ref, ps_ref[e_ref[k, t]] + r_ref[k, t]), sem)
                for k in range(TOP_K)]

    def issue(t, carry):
        for cp in copies(t):
            cp.start()
        return carry

    def drain(t, carry):
        for cp in copies(t):
            cp.wait()
        return carry

    lax.fori_loop(0, td, issue, 0)
    lax.fori_loop(0, td, drain, 0)


def _dispatch(pad_starts, eidx, rank, h2r, n_slots, td):
    n_tok = eidx.shape[1]
    smem_block = pl.BlockSpec((TOP_K, td), lambda i, ps: (0, i), memory_space=pltpu.SMEM)
    return pl.pallas_call(
        _dispatch_kernel,
        out_shape=jax.ShapeDtypeStruct((n_slots * SUBLANES, LANES), F32),
        grid_spec=pltpu.PrefetchScalarGridSpec(
            num_scalar_prefetch=1,
            grid=(n_tok // td,),
            in_specs=[smem_block, smem_block, pl.BlockSpec(memory_space=pl.ANY)],
            out_specs=pl.BlockSpec(memory_space=pl.ANY),
            scratch_shapes=[pltpu.SemaphoreType.DMA]),
        compiler_params=pltpu.CompilerParams(dimension_semantics=("arbitrary",),
                                             vmem_limit_bytes=VMEM_LIMIT),
        name="dispatch",
    )(pad_starts, eidx, rank, h2r)


def _expert_kernel(be_ref, nv_ref, last_ref, xs_ref, wg_ref, wu_ref, wd_ref, ys_ref,
                   wg_bf, wu_bf, wd_bf):
    i = pl.program_id(0)
    blk = xs_ref.shape[0] // SUBLANES
    expert = be_ref[i]
    n_valid = nv_ref[i]
    changed = jnp.logical_or(i == 0, expert != be_ref[jnp.maximum(i - 1, 0)])

    @pl.when(jnp.logical_and(n_valid > 0, changed))
    def _():
        wg_bf[...] = wg_ref[0].astype(BF16)
        wu_bf[...] = wu_ref[0].astype(BF16)
        wd_bf[...] = wd_ref[0].astype(BF16)

    @pl.when(n_valid > 0)
    def _():
        x = jnp.concatenate([xs_ref[pl.ds(s, blk, stride=SUBLANES), :] for s in range(SUBLANES)], axis=1)
        rows = lax.broadcasted_iota(I32, (blk, 1), 0)
        x = jnp.where(rows < n_valid, x, 0.0).astype(BF16)
        g = _dot(x, wg_bf[...])
        u = _dot(x, wu_bf[...])
        act = (g * _sigmoid(g) * u).astype(BF16)
        y = _dot(act, wd_bf[...])
        for s in range(SUBLANES):
            ys_ref[pl.ds(s, blk, stride=SUBLANES), :] = y[:, s * LANES:(s + 1) * LANES]


def _experts(block_expert, block_valid, last_block, xs, w_g, w_u, w_d, blk):
    n_blocks = block_expert.shape[0]
    d, de = w_g.shape[1], w_g.shape[2]
    x_map = lambda i, be, nv, last: (jnp.minimum(i, last[0]), 0)
    w_map = lambda i, be, nv, last: (be[i], 0, 0)
    return pl.pallas_call(
        _expert_kernel,
        out_shape=jax.ShapeDtypeStruct(xs.shape, F32),
        grid_spec=pltpu.PrefetchScalarGridSpec(
            num_scalar_prefetch=3,
            grid=(n_blocks,),
            in_specs=[pl.BlockSpec((blk * SUBLANES, LANES), x_map),
                      pl.BlockSpec((1, d, de), w_map),
                      pl.BlockSpec((1, d, de), w_map),
                      pl.BlockSpec((1, de, d), w_map)],
            out_specs=pl.BlockSpec((blk * SUBLANES, LANES), x_map),
            scratch_shapes=[pltpu.VMEM((d, de), BF16),
                            pltpu.VMEM((d, de), BF16),
                            pltpu.VMEM((de, d), BF16)]),
        compiler_params=pltpu.CompilerParams(dimension_semantics=("arbitrary",),
                                             vmem_limit_bytes=VMEM_LIMIT),
        name="experts",
    )(block_expert, block_valid, last_block, xs, w_g, w_u, w_d)


def _combine_kernel(ps_ref, e_ref, r_ref, gate_ref, xs_ref, mod_ref, ys_ref, o_ref, buf, sem):
    tc = e_ref.shape[1]

    def copies(t):
        return [pltpu.make_async_copy(_row(ys_ref, ps_ref[e_ref[k, t]] + r_ref[k, t]),
                                      _row(buf, k * tc + t), sem)
                for k in range(TOP_K)]

    def issue(t, carry):
        for cp in copies(t):
            cp.start()
        return carry

    def drain(t, carry):
        for cp in copies(t):
            cp.wait()
        return carry

    lax.fori_loop(0, tc, issue, 0)
    lax.fori_loop(0, tc, drain, 0)

    gate2 = mod_ref[0][5:6]
    acc = [jnp.zeros((tc, LANES), F32) for _ in range(SUBLANES)]
    for k in range(TOP_K):
        gk = jnp.transpose(jnp.broadcast_to(gate_ref[k:k + 1, :], (LANES, tc)))
        for s in range(SUBLANES):
            acc[s] = acc[s] + gk * buf[pl.ds(k * tc * SUBLANES + s, tc, stride=SUBLANES), :]
    o_ref[...] = xs_ref[...] + gate2 * jnp.concatenate(acc, axis=1)


def _combine(pad_starts, eidx, rank, gate, xs, mod3, ys, seq_len, tc):
    n_tok, d = xs.shape
    tiles_per_seq = seq_len // tc
    smem_block = pl.BlockSpec((TOP_K, tc), lambda i, ps: (0, i), memory_space=pltpu.SMEM)
    return pl.pallas_call(
        _combine_kernel,
        out_shape=jax.ShapeDtypeStruct((n_tok, d), F32),
        grid_spec=pltpu.PrefetchScalarGridSpec(
            num_scalar_prefetch=1,
            grid=(n_tok // tc,),
            in_specs=[smem_block, smem_block,
                      pl.BlockSpec((TOP_K, tc), lambda i, ps: (0, i)),
                      pl.BlockSpec((tc, d), lambda i, ps: (i, 0)),
                      pl.BlockSpec((1, 6, d), lambda i, ps: (i // tiles_per_seq, 0, 0)),
                      pl.BlockSpec(memory_space=pl.ANY)],
            out_specs=pl.BlockSpec((tc, d), lambda i, ps: (i, 0)),
            scratch_shapes=[pltpu.VMEM((TOP_K * tc * SUBLANES, LANES), F32),
                            pltpu.SemaphoreType.DMA]),
        compiler_params=pltpu.CompilerParams(dimension_semantics=("arbitrary",),
                                             vmem_limit_bytes=VMEM_LIMIT),
        name="combine",
    )(pad_starts, eidx, rank, gate, xs, mod3, ys)


def _tiles(seq_len):
    pick = lambda pref: min(pref, seq_len)
    return dict(inproj=pick(512), attn=pick(512), ssm=pick(1024), mix=pick(512),
                dispatch=pick(512), combine=pick(256), expert_block=256)


def _rope_tables(seq_len):
    half = ATT_HEAD_DIM // 2
    inv_freq = 1.0 / (ROPE_THETA ** (jnp.arange(0, ATT_HEAD_DIM, 2, dtype=F32) / ATT_HEAD_DIM))
    ang = jnp.arange(seq_len, dtype=F32)[:, None] * inv_freq[None, :]
    cos, sin = jnp.cos(ang), jnp.sin(ang)
    reps = LANES // half
    sign = jnp.tile(jnp.concatenate([-jnp.ones((half,), F32), jnp.ones((half,), F32)]), reps // 2)
    return jnp.tile(cos, (1, reps)), jnp.tile(sin, (1, reps)) * sign[None, :]


def _block_tables(counts, blk, n_blocks):
    padded = (counts + blk - 1) // blk * blk
    pad_ends = jnp.cumsum(padded)
    pad_starts = pad_ends - padded
    used = pad_ends[-1] // blk
    last = jnp.maximum(used - 1, 0)
    starts = jnp.arange(n_blocks, dtype=I32) * blk
    expert = jnp.minimum(jnp.searchsorted(pad_ends, starts, side='right'), N_EXPERTS - 1).astype(I32)
    expert = jnp.where(starts < pad_ends[-1], expert, expert[last])
    valid = jnp.clip(counts[expert] - (starts - pad_starts[expert]), 0, blk)
    valid = jnp.where(starts < pad_ends[-1], valid, 0).astype(I32)
    return pad_starts.astype(I32), expert, valid, last.reshape(1).astype(I32)


def kernel(x, c, norm1_g, norm2_g, w_ada, b_ada, w_in, q_norm_g, k_norm_g, lambda_q1, lambda_k1, lambda_q2, lambda_k2, subln_g, ssm_a_re, ssm_a_im, ssm_log_dt, ssm_b_re, ssm_b_im, ssm_c_re, ssm_c_im, ssm_d, w_glu, ssm_norm_g, w_out, w_router, router_bias, w_gate_e, w_up_e, w_down_e, w_gate_s, w_up_s, w_down_s):
    bsz, seq_len, d = x.shape
    n_tok = bsz * seq_len
    tiles = _tiles(seq_len)
    blk = tiles['expert_block']
    n_blocks = (n_tok * TOP_K + N_EXPERTS * (blk - 1) + blk - 1) // blk
    cos_t, sin_t = _rope_tables(seq_len)
    seg = jnp.kron(jnp.eye(QK_WIDTH // ATT_HEAD_DIM, dtype=F32),
                   jnp.full((ATT_HEAD_DIM, ATT_HEAD_DIM), 1.0 / ATT_HEAD_DIM, F32)).astype(BF16)
    reps = QK_WIDTH // ATT_HEAD_DIM
    x2d = x.reshape(n_tok, d).astype(F32)
    for layer in range(w_ada.shape[0]):
        lam_init = 0.8 - 0.6 * math.exp(-0.3 * layer)
        mod3 = _adaln(c.astype(F32), w_ada[layer].astype(F32), b_ada[layer].astype(F32)).reshape(bsz, 6, d)
        q, k, v, u = _inproj(
            x2d, mod3, norm1_g[layer].astype(F32).reshape(1, d), w_in[layer].astype(BF16), seg,
            jnp.tile(q_norm_g[layer].astype(F32), reps).reshape(1, QK_WIDTH),
            jnp.tile(k_norm_g[layer].astype(F32), reps).reshape(1, QK_WIDTH),
            cos_t, sin_t, seq_len, tiles['inproj'])
        lam = (jnp.exp(jnp.sum(lambda_q1[layer].astype(F32) * lambda_k1[layer].astype(F32)))
               - jnp.exp(jnp.sum(lambda_q2[layer].astype(F32) * lambda_k2[layer].astype(F32))) + lam_init)
        att = _attention(q, k, v, jnp.full((1, LANES), lam, F32),
                         subln_g[layer].astype(F32).reshape(1, ATT_V_DIM),
                         bsz, seq_len, tiles['attn'], 1.0 - lam_init)
        kmat, bmat, cmat, a8 = _ssm_mats(ssm_a_re[layer], ssm_a_im[layer], ssm_log_dt[layer],
                                         ssm_b_re[layer], ssm_b_im[layer], ssm_c_re[layer],
                                         ssm_c_im[layer], ssm_d[layer])
        y = _ssm(u, kmat, bmat, cmat, a8, bsz, seq_len, tiles['ssm'])
        wr_t = jnp.transpose(w_router[layer].astype(F32))
        wrh, wrl = _split_bf16(wr_t)
        xs, h2r, eidx, gate, rank, counts = _mix_route(
            x2d, y, att, mod3, w_glu[layer].astype(BF16),
            ssm_norm_g[layer].astype(F32).reshape(1, SSM_WIDTH),
            w_out[layer, :ATT_WIDTH].astype(BF16), w_out[layer, ATT_WIDTH:].astype(BF16),
            norm2_g[layer].astype(F32).reshape(1, d), wrh, wrl,
            router_bias[layer].astype(F32).reshape(N_EXPERTS, 1),
            jnp.concatenate([w_gate_s[layer], w_up_s[layer]], axis=1).astype(BF16),
            w_down_s[layer].astype(BF16), seq_len, tiles['mix'])
        pad_starts, block_expert, block_valid, last_block = _block_tables(
            counts.reshape(N_EXPERTS).astype(I32), blk, n_blocks)
        x_slots = _dispatch(pad_starts, eidx, rank, h2r, n_blocks * blk, tiles['dispatch'])
        y_slots = _experts(block_expert, block_valid, last_block, x_slots,
                           w_gate_e[layer], w_up_e[layer], w_down_e[layer], blk)
        x2d = _combine(pad_starts, eidx, rank, gate, xs, mod3, y_slots, seq_len, tiles['combine'])
    return x2d.reshape(bsz, seq_len, d).astype(x.dtype)
```

```python
import functools
import math

import jax
import jax.numpy as jnp
from jax import lax
from jax.experimental import pallas as pl
from jax.experimental.pallas import tpu as pltpu

F32 = jnp.float32
BF16 = jnp.bfloat16
I32 = jnp.int32

LANES = 128
SUBLANES = 8

N_ATT_HEADS = 4
ATT_HEAD_DIM = 64
ATT_V_DIM = 2 * ATT_HEAD_DIM
QK_WIDTH = N_ATT_HEADS * 2 * ATT_HEAD_DIM
ATT_WIDTH = N_ATT_HEADS * ATT_V_DIM
ROPE_THETA = 10000.0
SSM_GROUP = 16
SSM_GROUPS = 32
SSM_STATE = 64
SSM_WIDTH = SSM_GROUPS * SSM_GROUP
SSM_CHUNK = SUBLANES
SSM_LANE_BLOCKS = SSM_WIDTH // LANES
GROUPS_PER_BLOCK = LANES // SSM_GROUP
N_EXPERTS = 256
TOP_K = 8
N_GROUPS = 8
TOPK_GROUPS = 4
EXPERTS_PER_GROUP = N_EXPERTS // N_GROUPS
EXPERT_DIM = 256
SHARED_DIM = 256
ROUTED_SCALE = 2.5
EPS = 1e-6

NT_DIMS = (((1,), (1,)), ((), ()))

VMEM_LIMIT = 48 * 1024 * 1024


def _dot(a, b):
    return jnp.dot(a, b, preferred_element_type=F32)


def _sigmoid(x):
    return 1.0 / (1.0 + jnp.exp(-x))


def _split_bf16(x):
    hi = x.astype(BF16)
    lo = (x - hi.astype(F32)).astype(BF16)
    return hi, lo


def _adaln_kernel(c_ref, w_ref, b_ref, o_ref):
    c = c_ref[...]
    sc = c * _sigmoid(c)
    o_ref[...] = jnp.dot(sc, w_ref[...], preferred_element_type=F32,
                         precision=lax.Precision.HIGHEST) + b_ref[...]


def _adaln(c, w, b):
    bsz, d = c.shape
    n = w.shape[1]
    tn = 1024
    return pl.pallas_call(
        _adaln_kernel,
        out_shape=jax.ShapeDtypeStruct((bsz, n), F32),
        grid=(n // tn,),
        in_specs=[pl.BlockSpec((bsz, d), lambda j: (0, 0)),
                  pl.BlockSpec((d, tn), lambda j: (0, j)),
                  pl.BlockSpec((1, tn), lambda j: (0, j))],
        out_specs=pl.BlockSpec((bsz, tn), lambda j: (0, j)),
        compiler_params=pltpu.CompilerParams(dimension_semantics=("arbitrary",),
                                             vmem_limit_bytes=VMEM_LIMIT),
        name="adaln",
    )(c, w, b.reshape(1, n))


def _inproj_kernel(x_ref, mod_ref, n1g_ref, w_ref, seg_ref, qg_ref, kg_ref, cos_ref, sin_ref,
                   q_ref, k_ref, v_ref, u_ref):
    x = x_ref[...]
    mod = mod_ref[0]
    shift, scale = mod[0:1], mod[1:2]
    ms = jnp.mean(x * x, axis=-1, keepdims=True)
    h = x * lax.rsqrt(ms + EPS) * n1g_ref[...]
    h = h * (1.0 + scale) + shift
    z = _dot(h.astype(BF16), w_ref[...])
    seg = seg_ref[...]
    cos = cos_ref[...]
    sin = sin_ref[...]
    lane = lax.broadcasted_iota(I32, cos.shape, 1)
    first_half = (lane % ATT_HEAD_DIM) < (ATT_HEAD_DIM // 2)

    def norm_rope(t, g_ref, out_scale):
        hi, lo = _split_bf16(t * t)
        msq = _dot(hi, seg) + _dot(lo, seg)
        tn = t * lax.rsqrt(msq + EPS) * g_ref[...]
        outs = []
        for hd in range(N_ATT_HEADS):
            th = tn[:, hd * LANES:(hd + 1) * LANES]
            partner = jnp.where(first_half,
                                pltpu.roll(th, LANES - ATT_HEAD_DIM // 2, 1),
                                pltpu.roll(th, ATT_HEAD_DIM // 2, 1))
            outs.append((th * cos + partner * sin) * out_scale)
        return jnp.concatenate(outs, axis=1)

    q_ref[...] = norm_rope(z[:, :QK_WIDTH], qg_ref, ATT_HEAD_DIM ** -0.5).astype(BF16)
    k_ref[...] = norm_rope(z[:, QK_WIDTH:2 * QK_WIDTH], kg_ref, 1.0).astype(BF16)
    v_ref[...] = z[:, 2 * QK_WIDTH:2 * QK_WIDTH + ATT_WIDTH].astype(BF16)
    u_ref[...] = z[:, 2 * QK_WIDTH + ATT_WIDTH:]


def _inproj(x2d, mod3, n1g, w_in_bf, seg, qg, kg, cos_t, sin_t, seq_len, tm):
    n_tok, d = x2d.shape
    tiles_per_seq = seq_len // tm
    in_width = w_in_bf.shape[1]
    row = lambda i: (i, 0)
    const = lambda i: (0, 0)
    return pl.pallas_call(
        _inproj_kernel,
        out_shape=(jax.ShapeDtypeStruct((n_tok, QK_WIDTH), BF16),
                   jax.ShapeDtypeStruct((n_tok, QK_WIDTH), BF16),
                   jax.ShapeDtypeStruct((n_tok, ATT_WIDTH), BF16),
                   jax.ShapeDtypeStruct((n_tok, SSM_WIDTH), F32)),
        grid=(n_tok // tm,),
        in_specs=[pl.BlockSpec((tm, d), row),
                  pl.BlockSpec((1, 6, d), lambda i: (i // tiles_per_seq, 0, 0)),
                  pl.BlockSpec((1, d), const),
                  pl.BlockSpec((d, in_width), const),
                  pl.BlockSpec((QK_WIDTH, QK_WIDTH), const),
                  pl.BlockSpec((1, QK_WIDTH), const),
                  pl.BlockSpec((1, QK_WIDTH), const),
                  pl.BlockSpec((tm, LANES), lambda i: (i % tiles_per_seq, 0)),
                  pl.BlockSpec((tm, LANES), lambda i: (i % tiles_per_seq, 0))],
        out_specs=(pl.BlockSpec((tm, QK_WIDTH), row),
                   pl.BlockSpec((tm, QK_WIDTH), row),
                   pl.BlockSpec((tm, ATT_WIDTH), row),
                   pl.BlockSpec((tm, SSM_WIDTH), row)),
        compiler_params=pltpu.CompilerParams(dimension_semantics=("arbitrary",),
                                             vmem_limit_bytes=VMEM_LIMIT),
        name="inproj",
    )(x2d, mod3, n1g, w_in_bf, seg, qg, kg, cos_t, sin_t)


def _attn_kernel(qi_ref, kj_ref, q_ref, k_ref, v_ref, lam_ref, sg_ref, o_ref,
                 m_ref, l_ref, acc_ref, *, out_scale):
    p = pl.program_id(2)
    qi = qi_ref[p]
    kj = kj_ref[p]

    @pl.when(kj == 0)
    def _():
        m_ref[...] = jnp.full(m_ref.shape, -jnp.inf, F32)
        l_ref[...] = jnp.zeros(l_ref.shape, F32)
        acc_ref[...] = jnp.zeros(acc_ref.shape, F32)

    def step(masked):
        q = q_ref[...]
        k = k_ref[...]
        v = v_ref[...]
        lane = lax.broadcasted_iota(I32, q.shape, 1)
        zero = jnp.zeros_like(q)
        maps = (jnp.where(lane < ATT_HEAD_DIM, q, zero), jnp.where(lane >= ATT_HEAD_DIM, q, zero))
        for c in range(2):
            s = lax.dot_general(maps[c], k, NT_DIMS, preferred_element_type=F32)
            if masked:
                r = lax.broadcasted_iota(I32, s.shape, 0)
                col = lax.broadcasted_iota(I32, s.shape, 1)
                s = jnp.where(col <= r, s, -jnp.inf)
            m_old = m_ref[c]
            m_new = jnp.maximum(m_old, jnp.max(s, axis=-1, keepdims=True))
            alpha = jnp.exp(m_old - m_new)
            pr = jnp.exp(s - jnp.concatenate([m_new] * (s.shape[1] // LANES), axis=1))
            l_ref[c] = alpha * l_ref[c] + jnp.sum(pr, axis=-1, keepdims=True)
            acc_ref[c] = alpha * acc_ref[c] + _dot(pr.astype(BF16), v)
            m_ref[c] = m_new

    @pl.when(kj < qi)
    def _():
        step(False)

    @pl.when(kj == qi)
    def _():
        step(True)
        o = acc_ref[0] / l_ref[0] - lam_ref[...] * (acc_ref[1] / l_ref[1])
        ms = jnp.mean(o * o, axis=-1, keepdims=True)
        o_ref[...] = (o * lax.rsqrt(ms + EPS) * sg_ref[...] * out_scale).astype(o_ref.dtype)


def _attention(q, k, v, lam_row, subln_g, bsz, seq_len, tq, out_scale):
    nq = seq_len // tq
    pairs = [(i, j) for i in range(nq) for j in range(i + 1)]
    qi = jnp.asarray([p[0] for p in pairs], I32)
    kj = jnp.asarray([p[1] for p in pairs], I32)
    q_map = lambda b, h, p, qi, kj: (b * nq + qi[p], h)
    k_map = lambda b, h, p, qi, kj: (b * nq + kj[p], h)
    const = lambda b, h, p, qi, kj: (0, 0)
    return pl.pallas_call(
        functools.partial(_attn_kernel, out_scale=out_scale),
        out_shape=jax.ShapeDtypeStruct((bsz * seq_len, ATT_WIDTH), BF16),
        grid_spec=pltpu.PrefetchScalarGridSpec(
            num_scalar_prefetch=2,
            grid=(bsz, N_ATT_HEADS, len(pairs)),
            in_specs=[pl.BlockSpec((tq, LANES), q_map),
                      pl.BlockSpec((tq, LANES), k_map),
                      pl.BlockSpec((tq, LANES), k_map),
                      pl.BlockSpec((1, LANES), const),
                      pl.BlockSpec((1, LANES), const)],
            out_specs=pl.BlockSpec((tq, LANES), q_map),
            scratch_shapes=[pltpu.VMEM((2, tq, LANES), F32),
                            pltpu.VMEM((2, tq, LANES), F32),
                            pltpu.VMEM((2, tq, ATT_V_DIM), F32)]),
        compiler_params=pltpu.CompilerParams(
            dimension_semantics=("arbitrary", "arbitrary", "arbitrary"),
            vmem_limit_bytes=VMEM_LIMIT),
        name="attn",
    )(qi, kj, q, k, v, lam_row, subln_g)


def _ssm_mats(a_re, a_im, log_dt, b_re, b_im, c_re, c_im, d_skip):
    a_re, a_im, b_re, b_im, c_re, c_im, d_skip = (
        t.astype(F32) for t in (a_re, a_im, b_re, b_im, c_re, c_im, d_skip))
    dt = jnp.exp(log_dt.astype(F32))[:, None]
    mag = jnp.exp(a_re * dt)
    abar_re = mag * jnp.cos(a_im * dt)
    abar_im = mag * jnp.sin(a_im * dt)
    den = a_re * a_re + a_im * a_im
    nr = abar_re - 1.0
    f_re = ((nr * a_re + abar_im * a_im) / den)[..., None]
    f_im = ((abar_im * a_re - nr * a_im) / den)[..., None]
    bb_re = f_re * b_re - f_im * b_im
    bb_im = f_re * b_im + f_im * b_re
    steps = jnp.arange(SSM_CHUNK + 1, dtype=F32)[:, None, None]
    pmag = jnp.exp(a_re * dt * steps)
    pw_re = pmag * jnp.cos(a_im * dt * steps)
    pw_im = pmag * jnp.sin(a_im * dt * steps)
    ca_re = c_re[None] * pw_re[:, :, None, :] - c_im[None] * pw_im[:, :, None, :]
    ca_im = c_re[None] * pw_im[:, :, None, :] + c_im[None] * pw_re[:, :, None, :]
    hp = lax.Precision.HIGHEST
    lag = (jnp.einsum('mgcp,gpd->mgcd', ca_re[:SSM_CHUNK], bb_re, precision=hp)
           - jnp.einsum('mgcp,gpd->mgcd', ca_im[:SSM_CHUNK], bb_im, precision=hp))
    lag = lag.at[0].add(d_skip[:, :, None] * jnp.eye(SSM_GROUP, dtype=F32)[None])
    jt = jnp.arange(SSM_CHUNK)
    diff = jt[None, :] - jt[:, None]
    toep = jnp.where((diff >= 0)[:, :, None, None, None],
                     lag[jnp.clip(diff, 0, SSM_CHUNK - 1)], 0.0)
    nb, gpb = SSM_LANE_BLOCKS, GROUPS_PER_BLOCK
    eye_g = jnp.eye(gpb, dtype=F32)
    toep = toep.reshape(SSM_CHUNK, SSM_CHUNK, nb, gpb, SSM_GROUP, SSM_GROUP)
    kmat = jnp.einsum('jtbgcd,gh->bjgdthc', toep, eye_g)
    kmat = kmat.reshape(nb, SSM_CHUNK * LANES, SSM_CHUNK * LANES)
    rev = SSM_CHUNK - 1 - jt
    w_re = pw_re[rev][..., None] * bb_re[None] - pw_im[rev][..., None] * bb_im[None]
    w_im = pw_re[rev][..., None] * bb_im[None] + pw_im[rev][..., None] * bb_re[None]

    def in_to_state(w):
        w = w.reshape(SSM_CHUNK, nb, gpb, SSM_STATE, SSM_GROUP)
        return jnp.einsum('jbgpd,gh->bjgdhp', w, eye_g).reshape(nb, SSM_CHUNK * LANES, gpb * SSM_STATE)

    bmat = jnp.concatenate([in_to_state(w_re), in_to_state(w_im)], axis=-1)

    def state_to_out(ca):
        ca = ca[1:].reshape(SSM_CHUNK, nb, gpb, SSM_GROUP, SSM_STATE)
        return jnp.einsum('tbgcp,gh->bgpthc', ca, eye_g).reshape(nb, gpb * SSM_STATE, SSM_CHUNK * LANES)

    cmat = jnp.concatenate([state_to_out(ca_re), -state_to_out(ca_im)], axis=1)
    a8 = jnp.concatenate([pw_re[SSM_CHUNK].reshape(nb, 1, gpb * SSM_STATE),
                          pw_im[SSM_CHUNK].reshape(nb, 1, gpb * SSM_STATE)], axis=-1)
    return kmat.astype(BF16), bmat.astype(BF16), cmat.astype(BF16), a8


def _ssm_kernel(u_ref, km_ref, bm_ref, cm_ref, a8_ref, y_ref, carry_ref, se_ref, sp_ref):
    n_chunks = se_ref.shape[0]
    half = se_ref.shape[1] // 2

    @pl.when(pl.program_id(2) == 0)
    def _():
        carry_ref[...] = jnp.zeros(carry_ref.shape, F32)

    ucat = jnp.concatenate([u_ref[pl.ds(j, n_chunks, stride=SSM_CHUNK), :] for j in range(SSM_CHUNK)],
                           axis=1).astype(BF16)
    se_ref[...] = _dot(ucat, bm_ref[0])
    a8 = a8_ref[0]
    a_re, a_im = a8[:, :half], a8[:, half:]

    def body(m, carry):
        c_re, c_im = carry
        sp_ref[pl.ds(m, 1), :] = jnp.concatenate([c_re, c_im], axis=1)
        row = se_ref[pl.ds(m, 1), :]
        return (a_re * c_re - a_im * c_im + row[:, :half],
                a_re * c_im + a_im * c_re + row[:, half:])

    c_re, c_im = lax.fori_loop(0, n_chunks, body, (carry_ref[:, :half], carry_ref[:, half:]))
    carry_ref[...] = jnp.concatenate([c_re, c_im], axis=1)
    y = _dot(ucat, km_ref[0]) + _dot(sp_ref[...].astype(BF16), cm_ref[0])
    for t in range(SSM_CHUNK):
        y_ref[pl.ds(t, n_chunks, stride=SSM_CHUNK), :] = y[:, t * LANES:(t + 1) * LANES]


def _ssm(u, kmat, bmat, cmat, a8, bsz, seq_len, tt):
    nt = seq_len // tt
    n_chunks = tt // SSM_CHUNK
    width = SSM_CHUNK * LANES
    u_map = lambda g, b, i: (b * nt + i, g)
    w_map = lambda g, b, i: (g, 0, 0)
    return pl.pallas_call(
        _ssm_kernel,
        out_shape=jax.ShapeDtypeStruct((bsz * seq_len, SSM_WIDTH), F32),
        grid=(SSM_LANE_BLOCKS, bsz, nt),
        in_specs=[pl.BlockSpec((tt, LANES), u_map),
                  pl.BlockSpec((1, width, width), w_map),
                  pl.BlockSpec((1, width, width), w_map),
                  pl.BlockSpec((1, width, width), w_map),
                  pl.BlockSpec((1, 1, width), w_map)],
        out_specs=pl.BlockSpec((tt, LANES), u_map),
        scratch_shapes=[pltpu.VMEM((1, width), F32),
                        pltpu.VMEM((n_chunks, width), F32),
                        pltpu.VMEM((n_chunks, width), F32)],
        compiler_params=pltpu.CompilerParams(
            dimension_semantics=("arbitrary", "arbitrary", "arbitrary"),
            vmem_limit_bytes=VMEM_LIMIT),
        name="ssm",
    )(u, kmat, bmat, cmat, a8)


def _first_max(v, ids, n):
    m = jnp.max(v, axis=0, keepdims=True)
    ix = jnp.min(jnp.where(v == m, ids, n), axis=0, keepdims=True)
    return m, ix


def _mix_route_kernel(x_ref, y_ref, att_ref, mod_ref, wglu_ref, sng_ref, woa_ref, wos_ref, n2g_ref,
                      wrh_ref, wrl_ref, rb_ref, wgu_ref, wds_ref,
                      xs_ref, h2r_ref, eidx_ref, gate_ref, rank_ref, cnt_ref, carry_ref):
    tm = x_ref.shape[0]

    @pl.when(pl.program_id(0) == 0)
    def _():
        carry_ref[...] = jnp.zeros(carry_ref.shape, F32)

    mod = mod_ref[0]
    gate1, shift2, scale2, gate2 = mod[2:3], mod[3:4], mod[4:5], mod[5:6]

    y = y_ref[...]
    g = 0.5 * y * (1.0 + jnp.tanh(math.sqrt(2.0 / math.pi) * (y + 0.044715 * (y * y * y))))
    glu = g * _sigmoid(_dot(g.astype(BF16), wglu_ref[...]))
    ssm = glu * lax.rsqrt(jnp.mean(glu * glu, axis=-1, keepdims=True) + EPS) * sng_ref[...]

    mix = _dot(att_ref[...], woa_ref[...]) + _dot(ssm.astype(BF16), wos_ref[...])
    x1 = x_ref[...] + gate1 * mix
    h2 = x1 * lax.rsqrt(jnp.mean(x1 * x1, axis=-1, keepdims=True) + EPS) * n2g_ref[...]
    h2 = h2 * (1.0 + scale2) + shift2
    for s in range(SUBLANES):
        h2r_ref[pl.ds(s, tm, stride=SUBLANES), :] = h2[:, s * LANES:(s + 1) * LANES]
    hb, h_lo = _split_bf16(h2)

    gu = _dot(hb, wgu_ref[...])
    gs, us = gu[:, :SHARED_DIM], gu[:, SHARED_DIM:]
    act = (gs * _sigmoid(gs) * us).astype(BF16)
    xs_ref[...] = x1 + gate2 * _dot(act, wds_ref[...])

    wrh = wrh_ref[...]
    logits = (lax.dot_general(wrh, hb, NT_DIMS, preferred_element_type=F32)
              + lax.dot_general(wrh, h_lo, NT_DIMS, preferred_element_type=F32)
              + lax.dot_general(wrl_ref[...], hb, NT_DIMS, preferred_element_type=F32))
    score = _sigmoid(logits)
    biased = score + rb_ref[...]
    neg = -jnp.inf

    ids_g = lax.broadcasted_iota(I32, (EXPERTS_PER_GROUP, tm), 0)
    group_rows = []
    for gi in range(N_GROUPS):
        blk = biased[gi * EXPERTS_PER_GROUP:(gi + 1) * EXPERTS_PER_GROUP, :]
        m1, i1 = _first_max(blk, ids_g, EXPERTS_PER_GROUP)
        m2 = jnp.max(jnp.where(ids_g == i1, neg, blk), axis=0, keepdims=True)
        group_rows.append(m1 + m2)
    cur = jnp.concatenate(group_rows, axis=0)
    ids_8 = lax.broadcasted_iota(I32, cur.shape, 0)
    picked = jnp.zeros(cur.shape, F32)
    for _ in range(TOPK_GROUPS):
        _, ix = _first_max(cur, ids_8, N_GROUPS)
        hit = ids_8 == ix
        picked = jnp.where(hit, 1.0, picked)
        cur = jnp.where(hit, neg, cur)
    e_mask = jnp.concatenate(
        [jnp.broadcast_to(picked[gi:gi + 1, :], (EXPERTS_PER_GROUP, tm)) for gi in range(N_GROUPS)], axis=0)
    cand = jnp.where(e_mask > 0.0, biased, neg)

    ids_e = lax.broadcasted_iota(I32, cand.shape, 0)
    sel = jnp.zeros(cand.shape, F32)
    idx_rows, w_rows = [], []
    for _ in range(TOP_K):
        _, ix = _first_max(cand, ids_e, N_EXPERTS)
        hit = ids_e == ix
        idx_rows.append(ix)
        w_rows.append(jnp.sum(jnp.where(hit, score, 0.0), axis=0, keepdims=True))
        sel = jnp.where(hit, 1.0, sel)
        cand = jnp.where(hit, neg, cand)
    w_sum = w_rows[0]
    for w in w_rows[1:]:
        w_sum = w_sum + w
    gate_ref[...] = jnp.concatenate([w / w_sum * ROUTED_SCALE for w in w_rows], axis=0)
    eidx_ref[...] = jnp.concatenate(idx_rows, axis=0)

    t_row = lax.broadcasted_iota(I32, (tm, tm), 0)
    t_col = lax.broadcasted_iota(I32, (tm, tm), 1)
    earlier = jnp.where(t_row < t_col, 1.0, 0.0).astype(BF16)
    before = _dot(sel.astype(BF16), earlier) + carry_ref[...]
    rank_ref[...] = jnp.concatenate(
        [jnp.sum(jnp.where(ids_e == ix, before, 0.0), axis=0, keepdims=True) for ix in idx_rows],
        axis=0).astype(I32)
    carry_ref[...] = carry_ref[...] + jnp.sum(sel, axis=1, keepdims=True)
    cnt_ref[...] = carry_ref[...]


def _mix_route(x2d, y, att, mod3, wglu, sng, woa, wos, n2g, wrh, wrl, rb, wgu, wds, seq_len, tm):
    n_tok, d = x2d.shape
    tiles_per_seq = seq_len // tm
    row = lambda i: (i, 0)
    col = lambda i: (0, i)
    const = lambda i: (0, 0)
    full = lambda a: pl.BlockSpec(a.shape, const)
    return pl.pallas_call(
        _mix_route_kernel,
        out_shape=(jax.ShapeDtypeStruct((n_tok, d), F32),
                   jax.ShapeDtypeStruct((n_tok * SUBLANES, LANES), F32),
                   jax.ShapeDtypeStruct((TOP_K, n_tok), I32),
                   jax.ShapeDtypeStruct((TOP_K, n_tok), F32),
                   jax.ShapeDtypeStruct((TOP_K, n_tok), I32),
                   jax.ShapeDtypeStruct((N_EXPERTS, 1), F32)),
        grid=(n_tok // tm,),
        in_specs=[pl.BlockSpec((tm, d), row),
                  pl.BlockSpec((tm, SSM_WIDTH), row),
                  pl.BlockSpec((tm, ATT_WIDTH), row),
                  pl.BlockSpec((1, 6, d), lambda i: (i // tiles_per_seq, 0, 0)),
                  full(wglu), full(sng), full(woa), full(wos), full(n2g),
                  full(wrh), full(wrl), full(rb), full(wgu), full(wds)],
        out_specs=(pl.BlockSpec((tm, d), row),
                   pl.BlockSpec((tm * SUBLANES, LANES), row),
                   pl.BlockSpec((TOP_K, tm), col),
                   pl.BlockSpec((TOP_K, tm), col),
                   pl.BlockSpec((TOP_K, tm), col),
                   pl.BlockSpec((N_EXPERTS, 1), const)),
        scratch_shapes=[pltpu.VMEM((N_EXPERTS, 1), F32)],
        compiler_params=pltpu.CompilerParams(dimension_semantics=("arbitrary",),
                                             vmem_limit_bytes=VMEM_LIMIT),
        name="mix_route",
    )(x2d, y, att, mod3, wglu, sng, woa, wos, n2g, wrh, wrl, rb, wgu, wds)


def _row(ref, r):
    return ref.at[pl.ds(pl.multiple_of(r * SUBLANES, SUBLANES), SUBLANES), :]


def _dispatch_kernel(ps_ref, e_ref, r_ref, h2r_ref, xs_ref, sem):
    td = e_ref.shape[1]

    def issue(t, carry):
        for k in range(TOP_K):
            pltpu.make_async_copy(_row(h2r_ref, t), _row(xs_ref, ps_ref[e_ref[k, t]] + r_ref[k, t]),
                                  sem).start(priority=k % 2)
        return carry

    lax.fori_loop(0, td, issue, 0)
    for _ in range(TOP_K):
        pltpu.make_async_copy(h2r_ref, xs_ref.at[pl.ds(0, td * SUBLANES), :], sem).wait()


def _dispatch(pad_starts, eidx, rank, h2r, n_slots, td):
    n_tok = eidx.shape[1]
    smem_block = pl.BlockSpec((TOP_K, td), lambda i, ps: (0, i), memory_space=pltpu.SMEM)
    return pl.pallas_call(
        _dispatch_kernel,
        out_shape=jax.ShapeDtypeStruct((n_slots * SUBLANES, LANES), F32),
        grid_spec=pltpu.PrefetchScalarGridSpec(
            num_scalar_prefetch=1,
            grid=(n_tok // td,),
            in_specs=[smem_block, smem_block,
                      pl.BlockSpec((td * SUBLANES, LANES), lambda i, ps: (i, 0))],
            out_specs=pl.BlockSpec(memory_space=pl.ANY),
            scratch_shapes=[pltpu.SemaphoreType.DMA]),
        compiler_params=pltpu.CompilerParams(dimension_semantics=("arbitrary",),
                                             vmem_limit_bytes=VMEM_LIMIT),
        name="dispatch",
    )(pad_starts, eidx, rank, h2r)


def _expert_kernel(be_ref, nv_ref, last_ref, xs_ref, wg_ref, wu_ref, wd_ref, ys_ref,
                   wg_bf, wu_bf, wd_bf):
    i = pl.program_id(0)
    blk = xs_ref.shape[0] // SUBLANES
    expert = be_ref[i]
    n_valid = nv_ref[i]
    changed = jnp.logical_or(i == 0, expert != be_ref[jnp.maximum(i - 1, 0)])

    @pl.when(jnp.logical_and(n_valid > 0, changed))
    def _():
        wg_bf[...] = wg_ref[0].astype(BF16)
        wu_bf[...] = wu_ref[0].astype(BF16)
        wd_bf[...] = wd_ref[0].astype(BF16)

    @pl.when(n_valid > 0)
    def _():
        x = jnp.concatenate([xs_ref[pl.ds(s, blk, stride=SUBLANES), :] for s in range(SUBLANES)], axis=1)
        rows = lax.broadcasted_iota(I32, (blk, 1), 0)
        x = jnp.where(rows < n_valid, x, 0.0).astype(BF16)
        g = _dot(x, wg_bf[...])
        u = _dot(x, wu_bf[...])
        act = (g * _sigmoid(g) * u).astype(BF16)
        y = _dot(act, wd_bf[...])
        for s in range(SUBLANES):
            ys_ref[pl.ds(s, blk, stride=SUBLANES), :] = y[:, s * LANES:(s + 1) * LANES]


def _experts(block_expert, block_valid, last_block, xs, w_g, w_u, w_d, blk):
    n_blocks = block_expert.shape[0]
    d, de = w_g.shape[1], w_g.shape[2]
    x_map = lambda i, be, nv, last: (jnp.minimum(i, last[0]), 0)
    w_map = lambda i, be, nv, last: (be[i], 0, 0)
    return pl.pallas_call(
        _expert_kernel,
        out_shape=jax.ShapeDtypeStruct(xs.shape, F32),
        grid_spec=pltpu.PrefetchScalarGridSpec(
            num_scalar_prefetch=3,
            grid=(n_blocks,),
            in_specs=[pl.BlockSpec((blk * SUBLANES, LANES), x_map),
                      pl.BlockSpec((1, d, de), w_map),
                      pl.BlockSpec((1, d, de), w_map),
                      pl.BlockSpec((1, de, d), w_map)],
            out_specs=pl.BlockSpec((blk * SUBLANES, LANES), x_map),
            scratch_shapes=[pltpu.VMEM((d, de), BF16),
                            pltpu.VMEM((d, de), BF16),
                            pltpu.VMEM((de, d), BF16)]),
        compiler_params=pltpu.CompilerParams(dimension_semantics=("arbitrary",),
                                             vmem_limit_bytes=VMEM_LIMIT),
        name="experts",
    )(block_expert, block_valid, last_block, xs, w_g, w_u, w_d)


def _combine_kernel(ps_ref, e_ref, r_ref, gate_ref, xs_ref, mod_ref, ys_ref, o_ref, buf, sem):
    tc = e_ref.shape[1]

    def issue(t, carry):
        for k in range(TOP_K):
            pltpu.make_async_copy(_row(ys_ref, ps_ref[e_ref[k, t]] + r_ref[k, t]),
                                  _row(buf, k * tc + t), sem).start(priority=k % 2)
        return carry

    lax.fori_loop(0, tc, issue, 0)
    pltpu.make_async_copy(ys_ref.at[pl.ds(0, TOP_K * tc * SUBLANES), :], buf, sem).wait()

    gate2 = mod_ref[0][5:6]
    acc = [jnp.zeros((tc, LANES), F32) for _ in range(SUBLANES)]
    for k in range(TOP_K):
        gk = jnp.transpose(jnp.broadcast_to(gate_ref[k:k + 1, :], (LANES, tc)))
        for s in range(SUBLANES):
            acc[s] = acc[s] + gk * buf[pl.ds(k * tc * SUBLANES + s, tc, stride=SUBLANES), :]
    o_ref[...] = xs_ref[...] + gate2 * jnp.concatenate(acc, axis=1)


def _combine(pad_starts, eidx, rank, gate, xs, mod3, ys, seq_len, tc):
    n_tok, d = xs.shape
    tiles_per_seq = seq_len // tc
    smem_block = pl.BlockSpec((TOP_K, tc), lambda i, ps: (0, i), memory_space=pltpu.SMEM)
    return pl.pallas_call(
        _combine_kernel,
        out_shape=jax.ShapeDtypeStruct((n_tok, d), F32),
        grid_spec=pltpu.PrefetchScalarGridSpec(
            num_scalar_prefetch=1,
            grid=(n_tok // tc,),
            in_specs=[smem_block, smem_block,
                      pl.BlockSpec((TOP_K, tc), lambda i, ps: (0, i)),
                      pl.BlockSpec((tc, d), lambda i, ps: (i, 0)),
                      pl.BlockSpec((1, 6, d), lambda i, ps: (i // tiles_per_seq, 0, 0)),
                      pl.BlockSpec(memory_space=pl.ANY)],
            out_specs=pl.BlockSpec((tc, d), lambda i, ps: (i, 0)),
            scratch_shapes=[pltpu.VMEM((TOP_K * tc * SUBLANES, LANES), F32),
                            pltpu.SemaphoreType.DMA]),
        compiler_params=pltpu.CompilerParams(dimension_semantics=("arbitrary",),
                                             vmem_limit_bytes=VMEM_LIMIT),
        name="combine",
    )(pad_starts, eidx, rank, gate, xs, mod3, ys)


def _tiles(seq_len):
    pick = lambda pref: min(pref, seq_len)
    return dict(inproj=pick(512), attn=pick(512), ssm=pick(1024), mix=pick(512),
                dispatch=pick(512), combine=pick(256), expert_block=256)


def _rope_tables(seq_len):
    half = ATT_HEAD_DIM // 2
    inv_freq = 1.0 / (ROPE_THETA ** (jnp.arange(0, ATT_HEAD_DIM, 2, dtype=F32) / ATT_HEAD_DIM))
    ang = jnp.arange(seq_len, dtype=F32)[:, None] * inv_freq[None, :]
    cos, sin = jnp.cos(ang), jnp.sin(ang)
    reps = LANES // half
    sign = jnp.tile(jnp.concatenate([-jnp.ones((half,), F32), jnp.ones((half,), F32)]), reps // 2)
    return jnp.tile(cos, (1, reps)), jnp.tile(sin, (1, reps)) * sign[None, :]


def _block_tables(counts, blk, n_blocks):
    padded = (counts + blk - 1) // blk * blk
    pad_ends = jnp.cumsum(padded)
    pad_starts = pad_ends - padded
    used = pad_ends[-1] // blk
    last = jnp.maximum(used - 1, 0)
    starts = jnp.arange(n_blocks, dtype=I32) * blk
    expert = jnp.sum((pad_ends[None, :] <= starts[:, None]).astype(I32), axis=1)
    expert = jnp.minimum(expert, N_EXPERTS - 1)
    expert = jnp.where(starts < pad_ends[-1], expert, expert[last])
    valid = jnp.clip(counts[expert] - (starts - pad_starts[expert]), 0, blk)
    valid = jnp.where(starts < pad_ends[-1], valid, 0).astype(I32)
    return pad_starts.astype(I32), expert, valid, last.reshape(1).astype(I32)


def kernel(x, c, norm1_g, norm2_g, w_ada, b_ada, w_in, q_norm_g, k_norm_g, lambda_q1, lambda_k1, lambda_q2, lambda_k2, subln_g, ssm_a_re, ssm_a_im, ssm_log_dt, ssm_b_re, ssm_b_im, ssm_c_re, ssm_c_im, ssm_d, w_glu, ssm_norm_g, w_out, w_router, router_bias, w_gate_e, w_up_e, w_down_e, w_gate_s, w_up_s, w_down_s):
    bsz, seq_len, d = x.shape
    n_tok = bsz * seq_len
    tiles = _tiles(seq_len)
    blk = tiles['expert_block']
    n_blocks = (n_tok * TOP_K + N_EXPERTS * (blk - 1) + blk - 1) // blk
    cos_t, sin_t = _rope_tables(seq_len)
    seg = jnp.kron(jnp.eye(QK_WIDTH // ATT_HEAD_DIM, dtype=F32),
                   jnp.full((ATT_HEAD_DIM, ATT_HEAD_DIM), 1.0 / ATT_HEAD_DIM, F32)).astype(BF16)
    reps = QK_WIDTH // ATT_HEAD_DIM
    x2d = x.reshape(n_tok, d).astype(F32)
    for layer in range(w_ada.shape[0]):
        lam_init = 0.8 - 0.6 * math.exp(-0.3 * layer)
        mod3 = _adaln(c.astype(F32), w_ada[layer].astype(F32), b_ada[layer].astype(F32)).reshape(bsz, 6, d)
        q, k, v, u = _inproj(
            x2d, mod3, norm1_g[layer].astype(F32).reshape(1, d), w_in[layer].astype(BF16), seg,
            jnp.tile(q_norm_g[layer].astype(F32), reps).reshape(1, QK_WIDTH),
            jnp.tile(k_norm_g[layer].astype(F32), reps).reshape(1, QK_WIDTH),
            cos_t, sin_t, seq_len, tiles['inproj'])
        lam = (jnp.exp(jnp.sum(lambda_q1[layer].astype(F32) * lambda_k1[layer].astype(F32)))
               - jnp.exp(jnp.sum(lambda_q2[layer].astype(F32) * lambda_k2[layer].astype(F32))) + lam_init)
        att = _attention(q, k, v, jnp.full((1, LANES), lam, F32),
                         subln_g[layer].astype(F32).reshape(1, ATT_V_DIM),
                         bsz, seq_len, tiles['attn'], 1.0 - lam_init)
        kmat, bmat, cmat, a8 = _ssm_mats(ssm_a_re[layer], ssm_a_im[layer], ssm_log_dt[layer],
                                         ssm_b_re[layer], ssm_b_im[layer], ssm_c_re[layer],
                                         ssm_c_im[layer], ssm_d[layer])
        y = _ssm(u, kmat, bmat, cmat, a8, bsz, seq_len, tiles['ssm'])
        wr_t = jnp.transpose(w_router[layer].astype(F32))
        wrh, wrl = _split_bf16(wr_t)
        xs, h2r, eidx, gate, rank, counts = _mix_route(
            x2d, y, att, mod3, w_glu[layer].astype(BF16),
            ssm_norm_g[layer].astype(F32).reshape(1, SSM_WIDTH),
            w_out[layer, :ATT_WIDTH].astype(BF16), w_out[layer, ATT_WIDTH:].astype(BF16),
            norm2_g[layer].astype(F32).reshape(1, d), wrh, wrl,
            router_bias[layer].astype(F32).reshape(N_EXPERTS, 1),
            jnp.concatenate([w_gate_s[layer], w_up_s[layer]], axis=1).astype(BF16),
            w_down_s[layer].astype(BF16), seq_len, tiles['mix'])
        pad_starts, block_expert, block_valid, last_block = _block_tables(
            counts.reshape(N_EXPERTS).astype(I32), blk, n_blocks)
        x_slots = _dispatch(pad_starts, eidx, rank, h2r, n_blocks * blk, tiles['dispatch'])
        y_slots = _experts(block_expert, block_valid, last_block, x_slots,
                           w_gate_e[layer], w_up_e[layer], w_down_e[layer], blk)
        x2d = _combine(pad_starts, eidx, rank, gate, xs, mod3, y_slots, seq_len, tiles['combine'])
    return x2d.reshape(bsz, seq_len, d).astype(x.dtype)
```

```python
import functools
import math

import jax
import jax.numpy as jnp
from jax import lax
from jax.experimental import pallas as pl
from jax.experimental.pallas import tpu as pltpu

F32 = jnp.float32
BF16 = jnp.bfloat16
I32 = jnp.int32
U32 = jnp.uint32

LANES = 128
SUBLANES = 8

N_ATT_HEADS = 4
ATT_HEAD_DIM = 64
ATT_V_DIM = 2 * ATT_HEAD_DIM
QK_WIDTH = N_ATT_HEADS * 2 * ATT_HEAD_DIM
ATT_WIDTH = N_ATT_HEADS * ATT_V_DIM
ROPE_THETA = 10000.0
SSM_GROUP = 16
SSM_GROUPS = 32
SSM_STATE = 64
SSM_WIDTH = SSM_GROUPS * SSM_GROUP
SSM_CHUNK = SUBLANES
SSM_LANE_BLOCKS = SSM_WIDTH // LANES
GROUPS_PER_BLOCK = LANES // SSM_GROUP
N_EXPERTS = 256
TOP_K = 8
N_GROUPS = 8
TOPK_GROUPS = 4
EXPERTS_PER_GROUP = N_EXPERTS // N_GROUPS
EXPERT_DIM = 256
SHARED_DIM = 256
ROUTED_SCALE = 2.5
EPS = 1e-6

NT_DIMS = (((1,), (1,)), ((), ()))

VMEM_LIMIT = 48 * 1024 * 1024


def _dot(a, b):
    return jnp.dot(a, b, preferred_element_type=F32)


def _sigmoid(x):
    return 1.0 / (1.0 + jnp.exp(-x))


def _split_bf16(x):
    hi = x.astype(BF16)
    lo = (x - hi.astype(F32)).astype(BF16)
    return hi, lo


def _adaln_kernel(c_ref, w_ref, b_ref, o_ref):
    c = c_ref[...]
    sc = c * _sigmoid(c)
    o_ref[...] = jnp.dot(sc, w_ref[...], preferred_element_type=F32,
                         precision=lax.Precision.HIGHEST) + b_ref[...]


def _adaln(c, w, b):
    bsz, d = c.shape
    n = w.shape[1]
    tn = 1024
    return pl.pallas_call(
        _adaln_kernel,
        out_shape=jax.ShapeDtypeStruct((bsz, n), F32),
        grid=(n // tn,),
        in_specs=[pl.BlockSpec((bsz, d), lambda j: (0, 0)),
                  pl.BlockSpec((d, tn), lambda j: (0, j)),
                  pl.BlockSpec((1, tn), lambda j: (0, j))],
        out_specs=pl.BlockSpec((bsz, tn), lambda j: (0, j)),
        compiler_params=pltpu.CompilerParams(dimension_semantics=("arbitrary",),
                                             vmem_limit_bytes=VMEM_LIMIT),
        name="adaln",
    )(c, w, b.reshape(1, n))


def _inproj_kernel(x_ref, mod_ref, n1g_ref, w_ref, seg_ref, qg_ref, kg_ref, cos_ref, sin_ref,
                   q_ref, k_ref, v_ref, u_ref):
    x = x_ref[...]
    mod = mod_ref[0]
    shift, scale = mod[0:1], mod[1:2]
    ms = jnp.mean(x * x, axis=-1, keepdims=True)
    h = x * lax.rsqrt(ms + EPS) * n1g_ref[...]
    h = h * (1.0 + scale) + shift
    z = _dot(h.astype(BF16), w_ref[...])
    seg = seg_ref[...]
    cos = cos_ref[...]
    sin = sin_ref[...]
    lane = lax.broadcasted_iota(I32, cos.shape, 1)
    first_half = (lane % ATT_HEAD_DIM) < (ATT_HEAD_DIM // 2)

    def norm_rope(t, g_ref, out_scale):
        hi, lo = _split_bf16(t * t)
        msq = _dot(hi, seg) + _dot(lo, seg)
        tn = t * lax.rsqrt(msq + EPS) * g_ref[...]
        outs = []
        for hd in range(N_ATT_HEADS):
            th = tn[:, hd * LANES:(hd + 1) * LANES]
            partner = jnp.where(first_half,
                                pltpu.roll(th, LANES - ATT_HEAD_DIM // 2, 1),
                                pltpu.roll(th, ATT_HEAD_DIM // 2, 1))
            outs.append((th * cos + partner * sin) * out_scale)
        return jnp.concatenate(outs, axis=1)

    q_ref[...] = norm_rope(z[:, :QK_WIDTH], qg_ref, ATT_HEAD_DIM ** -0.5).astype(BF16)
    k_ref[...] = norm_rope(z[:, QK_WIDTH:2 * QK_WIDTH], kg_ref, 1.0).astype(BF16)
    v_ref[...] = z[:, 2 * QK_WIDTH:2 * QK_WIDTH + ATT_WIDTH].astype(BF16)
    u_ref[...] = z[:, 2 * QK_WIDTH + ATT_WIDTH:]


def _inproj(x2d, mod3, n1g, w_in_bf, seg, qg, kg, cos_t, sin_t, seq_len, tm):
    n_tok, d = x2d.shape
    tiles_per_seq = seq_len // tm
    in_width = w_in_bf.shape[1]
    row = lambda i: (i, 0)
    const = lambda i: (0, 0)
    return pl.pallas_call(
        _inproj_kernel,
        out_shape=(jax.ShapeDtypeStruct((n_tok, QK_WIDTH), BF16),
                   jax.ShapeDtypeStruct((n_tok, QK_WIDTH), BF16),
                   jax.ShapeDtypeStruct((n_tok, ATT_WIDTH), BF16),
                   jax.ShapeDtypeStruct((n_tok, SSM_WIDTH), F32)),
        grid=(n_tok // tm,),
        in_specs=[pl.BlockSpec((tm, d), row),
                  pl.BlockSpec((1, 6, d), lambda i: (i // tiles_per_seq, 0, 0)),
                  pl.BlockSpec((1, d), const),
                  pl.BlockSpec((d, in_width), const),
                  pl.BlockSpec((QK_WIDTH, QK_WIDTH), const),
                  pl.BlockSpec((1, QK_WIDTH), const),
                  pl.BlockSpec((1, QK_WIDTH), const),
                  pl.BlockSpec((tm, LANES), lambda i: (i % tiles_per_seq, 0)),
                  pl.BlockSpec((tm, LANES), lambda i: (i % tiles_per_seq, 0))],
        out_specs=(pl.BlockSpec((tm, QK_WIDTH), row),
                   pl.BlockSpec((tm, QK_WIDTH), row),
                   pl.BlockSpec((tm, ATT_WIDTH), row),
                   pl.BlockSpec((tm, SSM_WIDTH), row)),
        compiler_params=pltpu.CompilerParams(dimension_semantics=("arbitrary",),
                                             vmem_limit_bytes=VMEM_LIMIT),
        name="inproj",
    )(x2d, mod3, n1g, w_in_bf, seg, qg, kg, cos_t, sin_t)


def _attn_kernel(qi_ref, kj_ref, q_ref, k_ref, v_ref, lam_ref, sg_ref, o_ref,
                 m_ref, l_ref, acc_ref, *, out_scale):
    p = pl.program_id(2)
    qi = qi_ref[p]
    kj = kj_ref[p]

    @pl.when(kj == 0)
    def _():
        m_ref[...] = jnp.full(m_ref.shape, -jnp.inf, F32)
        l_ref[...] = jnp.zeros(l_ref.shape, F32)
        acc_ref[...] = jnp.zeros(acc_ref.shape, F32)

    def step(masked):
        q = q_ref[...]
        k = k_ref[...]
        v = v_ref[...]
        lane = lax.broadcasted_iota(I32, q.shape, 1)
        zero = jnp.zeros_like(q)
        maps = (jnp.where(lane < ATT_HEAD_DIM, q, zero), jnp.where(lane >= ATT_HEAD_DIM, q, zero))
        for c in range(2):
            s = lax.dot_general(maps[c], k, NT_DIMS, preferred_element_type=F32)
            if masked:
                r = lax.broadcasted_iota(I32, s.shape, 0)
                col = lax.broadcasted_iota(I32, s.shape, 1)
                s = jnp.where(col <= r, s, -jnp.inf)
            m_old = m_ref[c]
            m_new = jnp.maximum(m_old, jnp.max(s, axis=-1, keepdims=True))
            alpha = jnp.exp(m_old - m_new)
            pr = jnp.exp(s - jnp.concatenate([m_new] * (s.shape[1] // LANES), axis=1))
            l_ref[c] = alpha * l_ref[c] + jnp.sum(pr, axis=-1, keepdims=True)
            acc_ref[c] = alpha * acc_ref[c] + _dot(pr.astype(BF16), v)
            m_ref[c] = m_new

    @pl.when(kj < qi)
    def _():
        step(False)

    @pl.when(kj == qi)
    def _():
        step(True)
        o = acc_ref[0] / l_ref[0] - lam_ref[...] * (acc_ref[1] / l_ref[1])
        ms = jnp.mean(o * o, axis=-1, keepdims=True)
        o_ref[...] = (o * lax.rsqrt(ms + EPS) * sg_ref[...] * out_scale).astype(o_ref.dtype)


def _attention(q, k, v, lam_row, subln_g, bsz, seq_len, tq, out_scale):
    nq = seq_len // tq
    pairs = [(i, j) for i in range(nq) for j in range(i + 1)]
    qi = jnp.asarray([p[0] for p in pairs], I32)
    kj = jnp.asarray([p[1] for p in pairs], I32)
    q_map = lambda b, h, p, qi, kj: (b * nq + qi[p], h)
    k_map = lambda b, h, p, qi, kj: (b * nq + kj[p], h)
    const = lambda b, h, p, qi, kj: (0, 0)
    return pl.pallas_call(
        functools.partial(_attn_kernel, out_scale=out_scale),
        out_shape=jax.ShapeDtypeStruct((bsz * seq_len, ATT_WIDTH), BF16),
        grid_spec=pltpu.PrefetchScalarGridSpec(
            num_scalar_prefetch=2,
            grid=(bsz, N_ATT_HEADS, len(pairs)),
            in_specs=[pl.BlockSpec((tq, LANES), q_map),
                      pl.BlockSpec((tq, LANES), k_map),
                      pl.BlockSpec((tq, LANES), k_map),
                      pl.BlockSpec((1, LANES), const),
                      pl.BlockSpec((1, LANES), const)],
            out_specs=pl.BlockSpec((tq, LANES), q_map),
            scratch_shapes=[pltpu.VMEM((2, tq, LANES), F32),
                            pltpu.VMEM((2, tq, LANES), F32),
                            pltpu.VMEM((2, tq, ATT_V_DIM), F32)]),
        compiler_params=pltpu.CompilerParams(
            dimension_semantics=("arbitrary", "arbitrary", "arbitrary"),
            vmem_limit_bytes=VMEM_LIMIT),
        name="attn",
    )(qi, kj, q, k, v, lam_row, subln_g)


def _ssm_mats(a_re, a_im, log_dt, b_re, b_im, c_re, c_im, d_skip):
    a_re, a_im, b_re, b_im, c_re, c_im, d_skip = (
        t.astype(F32) for t in (a_re, a_im, b_re, b_im, c_re, c_im, d_skip))
    dt = jnp.exp(log_dt.astype(F32))[:, None]
    mag = jnp.exp(a_re * dt)
    abar_re = mag * jnp.cos(a_im * dt)
    abar_im = mag * jnp.sin(a_im * dt)
    den = a_re * a_re + a_im * a_im
    nr = abar_re - 1.0
    f_re = ((nr * a_re + abar_im * a_im) / den)[..., None]
    f_im = ((abar_im * a_re - nr * a_im) / den)[..., None]
    bb_re = f_re * b_re - f_im * b_im
    bb_im = f_re * b_im + f_im * b_re
    steps = jnp.arange(SSM_CHUNK + 1, dtype=F32)[:, None, None]
    pmag = jnp.exp(a_re * dt * steps)
    pw_re = pmag * jnp.cos(a_im * dt * steps)
    pw_im = pmag * jnp.sin(a_im * dt * steps)
    ca_re = c_re[None] * pw_re[:, :, None, :] - c_im[None] * pw_im[:, :, None, :]
    ca_im = c_re[None] * pw_im[:, :, None, :] + c_im[None] * pw_re[:, :, None, :]
    hp = lax.Precision.HIGHEST
    lag = (jnp.einsum('mgcp,gpd->mgcd', ca_re[:SSM_CHUNK], bb_re, precision=hp)
           - jnp.einsum('mgcp,gpd->mgcd', ca_im[:SSM_CHUNK], bb_im, precision=hp))
    lag = lag.at[0].add(d_skip[:, :, None] * jnp.eye(SSM_GROUP, dtype=F32)[None])
    nb, gpb = SSM_LANE_BLOCKS, GROUPS_PER_BLOCK
    n_state = gpb * SSM_STATE
    in_group = jnp.arange(LANES) // SSM_GROUP
    state_group = jnp.arange(n_state) // SSM_STATE

    def spread(small, row_group, col_group):
        w = small.shape[-1]
        tiled = jnp.tile(jnp.eye(w, dtype=BF16), (1, col_group.shape[0] // w))
        wide = jnp.einsum('...w,wn->...n', small.astype(BF16), tiled, preferred_element_type=F32)
        return jnp.where(row_group[:, None] == col_group[None, :], wide, 0.0).astype(BF16)

    lag_blocks = spread(jnp.transpose(lag, (0, 1, 3, 2)).reshape(SSM_CHUNK, nb, LANES, SSM_GROUP),
                        in_group, in_group)
    zero_block = jnp.zeros_like(lag_blocks[0])
    kmat = jnp.concatenate(
        [jnp.concatenate([lag_blocks[t - j] if t >= j else zero_block for t in range(SSM_CHUNK)], axis=-1)
         for j in range(SSM_CHUNK)], axis=1)
    rev = SSM_CHUNK - 1 - jnp.arange(SSM_CHUNK)
    w_re = pw_re[rev][..., None] * bb_re[None] - pw_im[rev][..., None] * bb_im[None]
    w_im = pw_re[rev][..., None] * bb_im[None] + pw_im[rev][..., None] * bb_re[None]

    def in_to_state(w):
        small = jnp.transpose(w, (0, 1, 3, 2)).reshape(SSM_CHUNK, nb, LANES, SSM_STATE)
        return spread(small, in_group, state_group)

    bmat = jnp.concatenate([in_to_state(w_re), in_to_state(w_im)], axis=-1)
    bmat = jnp.transpose(bmat, (1, 0, 2, 3)).reshape(nb, SSM_CHUNK * LANES, 2 * n_state)

    def state_to_out(ca):
        small = jnp.transpose(ca[1:], (0, 1, 3, 2)).reshape(SSM_CHUNK, nb, n_state, SSM_GROUP)
        blocks = spread(small, state_group, in_group)
        return jnp.transpose(blocks, (1, 2, 0, 3)).reshape(nb, n_state, SSM_CHUNK * LANES)

    cmat = jnp.concatenate([state_to_out(ca_re), -state_to_out(ca_im)], axis=1)
    a8 = jnp.concatenate([pw_re[SSM_CHUNK].reshape(nb, 1, n_state),
                          pw_im[SSM_CHUNK].reshape(nb, 1, n_state)], axis=-1)
    return kmat, bmat, cmat, a8


def _ssm_kernel(u_ref, km_ref, bm_ref, cm_ref, a8_ref, y_ref, carry_ref, se_ref, sp_ref):
    n_chunks = se_ref.shape[0]
    half = se_ref.shape[1] // 2

    @pl.when(pl.program_id(2) == 0)
    def _():
        carry_ref[...] = jnp.zeros(carry_ref.shape, F32)

    ucat = jnp.concatenate([u_ref[pl.ds(j, n_chunks, stride=SSM_CHUNK), :] for j in range(SSM_CHUNK)],
                           axis=1).astype(BF16)
    se_ref[...] = _dot(ucat, bm_ref[0])
    a8 = a8_ref[0]
    a_re, a_im = a8[:, :half], a8[:, half:]

    def body(m, carry):
        c_re, c_im = carry
        sp_ref[pl.ds(m, 1), :] = jnp.concatenate([c_re, c_im], axis=1)
        row = se_ref[pl.ds(m, 1), :]
        return (a_re * c_re - a_im * c_im + row[:, :half],
                a_re * c_im + a_im * c_re + row[:, half:])

    c_re, c_im = lax.fori_loop(0, n_chunks, body, (carry_ref[:, :half], carry_ref[:, half:]))
    carry_ref[...] = jnp.concatenate([c_re, c_im], axis=1)
    y = _dot(ucat, km_ref[0]) + _dot(sp_ref[...].astype(BF16), cm_ref[0])
    for t in range(SSM_CHUNK):
        y_ref[pl.ds(t, n_chunks, stride=SSM_CHUNK), :] = y[:, t * LANES:(t + 1) * LANES]


def _ssm(u, kmat, bmat, cmat, a8, bsz, seq_len, tt):
    nt = seq_len // tt
    n_chunks = tt // SSM_CHUNK
    width = SSM_CHUNK * LANES
    u_map = lambda g, b, i: (b * nt + i, g)
    w_map = lambda g, b, i: (g, 0, 0)
    return pl.pallas_call(
        _ssm_kernel,
        out_shape=jax.ShapeDtypeStruct((bsz * seq_len, SSM_WIDTH), F32),
        grid=(SSM_LANE_BLOCKS, bsz, nt),
        in_specs=[pl.BlockSpec((tt, LANES), u_map),
                  pl.BlockSpec((1, width, width), w_map),
                  pl.BlockSpec((1, width, width), w_map),
                  pl.BlockSpec((1, width, width), w_map),
                  pl.BlockSpec((1, 1, width), w_map)],
        out_specs=pl.BlockSpec((tt, LANES), u_map),
        scratch_shapes=[pltpu.VMEM((1, width), F32),
                        pltpu.VMEM((n_chunks, width), F32),
                        pltpu.VMEM((n_chunks, width), F32)],
        compiler_params=pltpu.CompilerParams(
            dimension_semantics=("arbitrary", "arbitrary", "arbitrary"),
            vmem_limit_bytes=VMEM_LIMIT),
        name="ssm",
    )(u, kmat, bmat, cmat, a8)


def _first_max(v, ids, n):
    m = jnp.max(v, axis=0, keepdims=True)
    ix = jnp.min(jnp.where(v == m, ids, n), axis=0, keepdims=True)
    return m, ix


def _mix_route_kernel(x_ref, y_ref, att_ref, mod_ref, wglu_ref, sng_ref, woa_ref, wos_ref, n2g_ref,
                      wrh_ref, wrl_ref, rb_ref, wgu_ref, wds_ref,
                      xs_ref, h2p_ref, eidx_ref, gate_ref, rank_ref, cnt_ref, carry_ref):
    tm = x_ref.shape[0]

    @pl.when(pl.program_id(0) == 0)
    def _():
        carry_ref[...] = jnp.zeros(carry_ref.shape, F32)

    mod = mod_ref[0]
    gate1, shift2, scale2, gate2 = mod[2:3], mod[3:4], mod[4:5], mod[5:6]

    y = y_ref[...]
    g = 0.5 * y * (1.0 + jnp.tanh(math.sqrt(2.0 / math.pi) * (y + 0.044715 * (y * y * y))))
    glu = g * _sigmoid(_dot(g.astype(BF16), wglu_ref[...]))
    ssm = glu * lax.rsqrt(jnp.mean(glu * glu, axis=-1, keepdims=True) + EPS) * sng_ref[...]

    mix = _dot(att_ref[...], woa_ref[...]) + _dot(ssm.astype(BF16), wos_ref[...])
    x1 = x_ref[...] + gate1 * mix
    h2 = x1 * lax.rsqrt(jnp.mean(x1 * x1, axis=-1, keepdims=True) + EPS) * n2g_ref[...]
    h2 = h2 * (1.0 + scale2) + shift2
    _store_rows(h2p_ref, _pack_rows(h2))
    hb, h_lo = _split_bf16(h2)

    gu = _dot(hb, wgu_ref[...])
    gs, us = gu[:, :SHARED_DIM], gu[:, SHARED_DIM:]
    act = (gs * _sigmoid(gs) * us).astype(BF16)
    xs_ref[...] = x1 + gate2 * _dot(act, wds_ref[...])

    wrh = wrh_ref[...]
    logits = (lax.dot_general(wrh, hb, NT_DIMS, preferred_element_type=F32)
              + lax.dot_general(wrh, h_lo, NT_DIMS, preferred_element_type=F32)
              + lax.dot_general(wrl_ref[...], hb, NT_DIMS, preferred_element_type=F32))
    score = _sigmoid(logits)
    biased = score + rb_ref[...]
    neg = -jnp.inf

    ids_g = lax.broadcasted_iota(I32, (EXPERTS_PER_GROUP, tm), 0)
    group_rows = []
    for gi in range(N_GROUPS):
        blk = biased[gi * EXPERTS_PER_GROUP:(gi + 1) * EXPERTS_PER_GROUP, :]
        m1, i1 = _first_max(blk, ids_g, EXPERTS_PER_GROUP)
        m2 = jnp.max(jnp.where(ids_g == i1, neg, blk), axis=0, keepdims=True)
        group_rows.append(m1 + m2)
    cur = jnp.concatenate(group_rows, axis=0)
    ids_8 = lax.broadcasted_iota(I32, cur.shape, 0)
    picked = jnp.zeros(cur.shape, F32)
    for _ in range(TOPK_GROUPS):
        _, ix = _first_max(cur, ids_8, N_GROUPS)
        hit = ids_8 == ix
        picked = jnp.where(hit, 1.0, picked)
        cur = jnp.where(hit, neg, cur)
    e_mask = jnp.concatenate(
        [jnp.broadcast_to(picked[gi:gi + 1, :], (EXPERTS_PER_GROUP, tm)) for gi in range(N_GROUPS)], axis=0)
    cand = jnp.where(e_mask > 0.0, biased, neg)

    ids_e = lax.broadcasted_iota(I32, cand.shape, 0)
    sel = jnp.zeros(cand.shape, F32)
    idx_rows, w_rows = [], []
    for _ in range(TOP_K):
        _, ix = _first_max(cand, ids_e, N_EXPERTS)
        hit = ids_e == ix
        idx_rows.append(ix)
        w_rows.append(jnp.sum(jnp.where(hit, score, 0.0), axis=0, keepdims=True))
        sel = jnp.where(hit, 1.0, sel)
        cand = jnp.where(hit, neg, cand)
    w_sum = w_rows[0]
    for w in w_rows[1:]:
        w_sum = w_sum + w
    gate_ref[...] = jnp.concatenate([w / w_sum * ROUTED_SCALE for w in w_rows], axis=0)
    eidx_ref[...] = jnp.concatenate(idx_rows, axis=0)

    t_row = lax.broadcasted_iota(I32, (tm, tm), 0)
    t_col = lax.broadcasted_iota(I32, (tm, tm), 1)
    earlier = jnp.where(t_row < t_col, 1.0, 0.0).astype(BF16)
    before = _dot(sel.astype(BF16), earlier) + carry_ref[...]
    rank_ref[...] = jnp.concatenate(
        [jnp.sum(jnp.where(ids_e == ix, before, 0.0), axis=0, keepdims=True) for ix in idx_rows],
        axis=0).astype(I32)
    carry_ref[...] = carry_ref[...] + jnp.sum(sel, axis=1, keepdims=True)
    cnt_ref[...] = carry_ref[...]


def _mix_route(x2d, y, att, mod3, wglu, sng, woa, wos, n2g, wrh, wrl, rb, wgu, wds, seq_len, tm):
    n_tok, d = x2d.shape
    tiles_per_seq = seq_len // tm
    row = lambda i: (i, 0)
    col = lambda i: (0, i)
    const = lambda i: (0, 0)
    full = lambda a: pl.BlockSpec(a.shape, const)
    return pl.pallas_call(
        _mix_route_kernel,
        out_shape=(jax.ShapeDtypeStruct((n_tok, d), F32),
                   jax.ShapeDtypeStruct((n_tok * ROW_TILE, LANES), U32),
                   jax.ShapeDtypeStruct((TOP_K, n_tok), I32),
                   jax.ShapeDtypeStruct((TOP_K, n_tok), F32),
                   jax.ShapeDtypeStruct((TOP_K, n_tok), I32),
                   jax.ShapeDtypeStruct((N_EXPERTS, 1), F32)),
        grid=(n_tok // tm,),
        in_specs=[pl.BlockSpec((tm, d), row),
                  pl.BlockSpec((tm, SSM_WIDTH), row),
                  pl.BlockSpec((tm, ATT_WIDTH), row),
                  pl.BlockSpec((1, 6, d), lambda i: (i // tiles_per_seq, 0, 0)),
                  full(wglu), full(sng), full(woa), full(wos), full(n2g),
                  full(wrh), full(wrl), full(rb), full(wgu), full(wds)],
        out_specs=(pl.BlockSpec((tm, d), row),
                   pl.BlockSpec((tm * ROW_TILE, LANES), row),
                   pl.BlockSpec((TOP_K, tm), col),
                   pl.BlockSpec((TOP_K, tm), col),
                   pl.BlockSpec((TOP_K, tm), col),
                   pl.BlockSpec((N_EXPERTS, 1), const)),
        scratch_shapes=[pltpu.VMEM((N_EXPERTS, 1), F32)],
        compiler_params=pltpu.CompilerParams(dimension_semantics=("arbitrary",),
                                             vmem_limit_bytes=VMEM_LIMIT),
        name="mix_route",
    )(x2d, y, att, mod3, wglu, sng, woa, wos, n2g, wrh, wrl, rb, wgu, wds)


ROW_TILE = 4


def _pack_rows(v):
    half = v.shape[1] // 2
    return pltpu.pack_elementwise([v[:, :half], v[:, half:]], packed_dtype=BF16)


def _unpack_rows(w, index):
    return pltpu.unpack_elementwise(w, index=index, packed_dtype=BF16, unpacked_dtype=F32)


def _store_rows(ref, packed):
    m = packed.shape[0]
    for s in range(ROW_TILE):
        ref[pl.ds(s, m, stride=ROW_TILE), :] = packed[:, s * LANES:(s + 1) * LANES]


def _load_rows(ref, m):
    return jnp.concatenate([ref[pl.ds(s, m, stride=ROW_TILE), :] for s in range(ROW_TILE)], axis=1)


def _row(ref, r):
    return ref.at[pl.ds(pl.multiple_of(r * ROW_TILE, ROW_TILE), ROW_TILE), :]


def _dispatch_kernel(d_ref, h2p_ref, xs_ref, sem):
    td = d_ref.shape[1]

    def issue(t, carry):
        for k in range(TOP_K):
            pltpu.make_async_copy(_row(h2p_ref, t), _row(xs_ref, d_ref[k, t]), sem).start(priority=k % 2)
        return carry

    lax.fori_loop(0, td, issue, 0)
    for _ in range(TOP_K):
        pltpu.make_async_copy(h2p_ref, xs_ref.at[pl.ds(0, td * ROW_TILE), :], sem).wait()


def _dispatch(dest, h2p, n_slots, td):
    n_tok = dest.shape[1]
    return pl.pallas_call(
        _dispatch_kernel,
        out_shape=jax.ShapeDtypeStruct((n_slots * ROW_TILE, LANES), U32),
        grid=(n_tok // td,),
        in_specs=[pl.BlockSpec((TOP_K, td), lambda i: (0, i), memory_space=pltpu.SMEM),
                  pl.BlockSpec((td * ROW_TILE, LANES), lambda i: (i, 0))],
        out_specs=pl.BlockSpec(memory_space=pl.ANY),
        scratch_shapes=[pltpu.SemaphoreType.DMA],
        compiler_params=pltpu.CompilerParams(dimension_semantics=("arbitrary",),
                                             vmem_limit_bytes=VMEM_LIMIT),
        name="dispatch",
    )(dest, h2p)


def _expert_kernel(be_ref, nv_ref, last_ref, xs_ref, wg_ref, wu_ref, wd_ref, ys_ref,
                   wg_bf, wu_bf, wd_bf):
    i = pl.program_id(0)
    blk = xs_ref.shape[0] // ROW_TILE
    expert = be_ref[i]
    n_valid = nv_ref[i]
    changed = jnp.logical_or(i == 0, expert != be_ref[jnp.maximum(i - 1, 0)])

    @pl.when(jnp.logical_and(n_valid > 0, changed))
    def _():
        wg_bf[...] = wg_ref[0].astype(BF16)
        wu_bf[...] = wu_ref[0].astype(BF16)
        wd_bf[...] = wd_ref[0].astype(BF16)

    @pl.when(n_valid > 0)
    def _():
        packed = _load_rows(xs_ref, blk)
        x = jnp.concatenate([_unpack_rows(packed, 0), _unpack_rows(packed, 1)], axis=1)
        rows = lax.broadcasted_iota(I32, (blk, 1), 0)
        x = jnp.where(rows < n_valid, x, 0.0).astype(BF16)
        g = _dot(x, wg_bf[...])
        u = _dot(x, wu_bf[...])
        act = (g * _sigmoid(g) * u).astype(BF16)
        _store_rows(ys_ref, _pack_rows(_dot(act, wd_bf[...])))


def _experts(block_expert, block_valid, last_block, xs, w_g, w_u, w_d, blk):
    n_blocks = block_expert.shape[0]
    d, de = w_g.shape[1], w_g.shape[2]
    x_map = lambda i, be, nv, last: (jnp.minimum(i, last[0]), 0)
    w_map = lambda i, be, nv, last: (be[i], 0, 0)
    return pl.pallas_call(
        _expert_kernel,
        out_shape=jax.ShapeDtypeStruct(xs.shape, U32),
        grid_spec=pltpu.PrefetchScalarGridSpec(
            num_scalar_prefetch=3,
            grid=(n_blocks,),
            in_specs=[pl.BlockSpec((blk * ROW_TILE, LANES), x_map),
                      pl.BlockSpec((1, d, de), w_map),
                      pl.BlockSpec((1, d, de), w_map),
                      pl.BlockSpec((1, de, d), w_map)],
            out_specs=pl.BlockSpec((blk * ROW_TILE, LANES), x_map),
            scratch_shapes=[pltpu.VMEM((d, de), BF16),
                            pltpu.VMEM((d, de), BF16),
                            pltpu.VMEM((de, d), BF16)]),
        compiler_params=pltpu.CompilerParams(dimension_semantics=("arbitrary",),
                                             vmem_limit_bytes=VMEM_LIMIT),
        name="experts",
    )(block_expert, block_valid, last_block, xs, w_g, w_u, w_d)


def _combine_kernel(d_ref, dn_ref, gate_ref, xs_ref, mod_ref, ys_ref, o_ref, buf, sem):
    tc = d_ref.shape[1]
    i = pl.program_id(0)
    slot = lax.rem(i, 2)

    def gather(dest_ref, into):
        def issue(t, carry):
            for k in range(TOP_K):
                pltpu.make_async_copy(_row(ys_ref, dest_ref[k, t]), _row(buf.at[into], k * tc + t),
                                      sem.at[into]).start(priority=k % 2)
            return carry
        lax.fori_loop(0, tc, issue, 0)

    @pl.when(i == 0)
    def _():
        gather(d_ref, 0)

    @pl.when(i + 1 < pl.num_programs(0))
    def _():
        gather(dn_ref, 1 - slot)

    pltpu.make_async_copy(ys_ref.at[pl.ds(0, TOP_K * tc * ROW_TILE), :], buf.at[slot], sem.at[slot]).wait()

    gate2 = mod_ref[0][5:6]
    rows = buf.at[slot]
    acc = [jnp.zeros((tc, LANES), F32) for _ in range(2 * ROW_TILE)]
    for k in range(TOP_K):
        gk = jnp.transpose(jnp.broadcast_to(gate_ref[k:k + 1, :], (LANES, tc)))
        for s in range(ROW_TILE):
            w = rows[pl.ds(k * tc * ROW_TILE + s, tc, stride=ROW_TILE), :]
            acc[s] = acc[s] + gk * _unpack_rows(w, 0)
            acc[ROW_TILE + s] = acc[ROW_TILE + s] + gk * _unpack_rows(w, 1)
    o_ref[...] = xs_ref[...] + gate2 * jnp.concatenate(acc, axis=1)


def _combine(dest, gate, xs, mod3, ys, seq_len, tc):
    n_tok, d = xs.shape
    tiles_per_seq = seq_len // tc
    n_tiles = n_tok // tc
    return pl.pallas_call(
        _combine_kernel,
        out_shape=jax.ShapeDtypeStruct((n_tok, d), F32),
        grid=(n_tiles,),
        in_specs=[pl.BlockSpec((TOP_K, tc), lambda i: (0, i), memory_space=pltpu.SMEM),
                  pl.BlockSpec((TOP_K, tc), lambda i: (0, jnp.minimum(i + 1, n_tiles - 1)),
                               memory_space=pltpu.SMEM),
                  pl.BlockSpec((TOP_K, tc), lambda i: (0, i)),
                  pl.BlockSpec((tc, d), lambda i: (i, 0)),
                  pl.BlockSpec((1, 6, d), lambda i: (i // tiles_per_seq, 0, 0)),
                  pl.BlockSpec(memory_space=pl.ANY)],
        out_specs=pl.BlockSpec((tc, d), lambda i: (i, 0)),
        scratch_shapes=[pltpu.VMEM((2, TOP_K * tc * ROW_TILE, LANES), U32),
                        pltpu.SemaphoreType.DMA((2,))],
        compiler_params=pltpu.CompilerParams(dimension_semantics=("arbitrary",),
                                             vmem_limit_bytes=VMEM_LIMIT),
        name="combine",
    )(dest, dest, gate, xs, mod3, ys)


def _tiles(seq_len):
    pick = lambda pref: min(pref, seq_len)
    return dict(inproj=pick(512), attn=pick(512), ssm=pick(1024), mix=pick(512),
                dispatch=pick(512), combine=pick(256), expert_block=256)


def _rope_tables(seq_len):
    half = ATT_HEAD_DIM // 2
    inv_freq = 1.0 / (ROPE_THETA ** (jnp.arange(0, ATT_HEAD_DIM, 2, dtype=F32) / ATT_HEAD_DIM))
    ang = jnp.arange(seq_len, dtype=F32)[:, None] * inv_freq[None, :]
    cos, sin = jnp.cos(ang), jnp.sin(ang)
    reps = LANES // half
    sign = jnp.tile(jnp.concatenate([-jnp.ones((half,), F32), jnp.ones((half,), F32)]), reps // 2)
    return jnp.tile(cos, (1, reps)), jnp.tile(sin, (1, reps)) * sign[None, :]


def _block_tables(counts, blk, n_blocks):
    padded = (counts + blk - 1) // blk * blk
    pad_ends = jnp.cumsum(padded)
    pad_starts = pad_ends - padded
    used = pad_ends[-1] // blk
    last = jnp.maximum(used - 1, 0)
    starts = jnp.arange(n_blocks, dtype=I32) * blk
    expert = jnp.sum((pad_ends[None, :] <= starts[:, None]).astype(I32), axis=1)
    expert = jnp.minimum(expert, N_EXPERTS - 1)
    expert = jnp.where(starts < pad_ends[-1], expert, expert[last])
    valid = jnp.clip(counts[expert] - (starts - pad_starts[expert]), 0, blk)
    valid = jnp.where(starts < pad_ends[-1], valid, 0).astype(I32)
    return pad_starts.astype(I32), expert, valid, last.reshape(1).astype(I32)


def kernel(x, c, norm1_g, norm2_g, w_ada, b_ada, w_in, q_norm_g, k_norm_g, lambda_q1, lambda_k1, lambda_q2, lambda_k2, subln_g, ssm_a_re, ssm_a_im, ssm_log_dt, ssm_b_re, ssm_b_im, ssm_c_re, ssm_c_im, ssm_d, w_glu, ssm_norm_g, w_out, w_router, router_bias, w_gate_e, w_up_e, w_down_e, w_gate_s, w_up_s, w_down_s):
    bsz, seq_len, d = x.shape
    n_tok = bsz * seq_len
    tiles = _tiles(seq_len)
    blk = tiles['expert_block']
    n_blocks = (n_tok * TOP_K + N_EXPERTS * (blk - 1) + blk - 1) // blk
    cos_t, sin_t = _rope_tables(seq_len)
    seg = jnp.kron(jnp.eye(QK_WIDTH // ATT_HEAD_DIM, dtype=F32),
                   jnp.full((ATT_HEAD_DIM, ATT_HEAD_DIM), 1.0 / ATT_HEAD_DIM, F32)).astype(BF16)
    reps = QK_WIDTH // ATT_HEAD_DIM
    x2d = x.reshape(n_tok, d).astype(F32)
    for layer in range(w_ada.shape[0]):
        lam_init = 0.8 - 0.6 * math.exp(-0.3 * layer)
        mod3 = _adaln(c.astype(F32), w_ada[layer].astype(F32), b_ada[layer].astype(F32)).reshape(bsz, 6, d)
        q, k, v, u = _inproj(
            x2d, mod3, norm1_g[layer].astype(F32).reshape(1, d), w_in[layer].astype(BF16), seg,
            jnp.tile(q_norm_g[layer].astype(F32), reps).reshape(1, QK_WIDTH),
            jnp.tile(k_norm_g[layer].astype(F32), reps).reshape(1, QK_WIDTH),
            cos_t, sin_t, seq_len, tiles['inproj'])
        lam = (jnp.exp(jnp.sum(lambda_q1[layer].astype(F32) * lambda_k1[layer].astype(F32)))
               - jnp.exp(jnp.sum(lambda_q2[layer].astype(F32) * lambda_k2[layer].astype(F32))) + lam_init)
        att = _attention(q, k, v, jnp.full((1, LANES), lam, F32),
                         subln_g[layer].astype(F32).reshape(1, ATT_V_DIM),
                         bsz, seq_len, tiles['attn'], 1.0 - lam_init)
        kmat, bmat, cmat, a8 = _ssm_mats(ssm_a_re[layer], ssm_a_im[layer], ssm_log_dt[layer],
                                         ssm_b_re[layer], ssm_b_im[layer], ssm_c_re[layer],
                                         ssm_c_im[layer], ssm_d[layer])
        y = _ssm(u, kmat, bmat, cmat, a8, bsz, seq_len, tiles['ssm'])
        wr_t = jnp.transpose(w_router[layer].astype(F32))
        wrh, wrl = _split_bf16(wr_t)
        xs, h2p, eidx, gate, rank, counts = _mix_route(
            x2d, y, att, mod3, w_glu[layer].astype(BF16),
            ssm_norm_g[layer].astype(F32).reshape(1, SSM_WIDTH),
            w_out[layer, :ATT_WIDTH].astype(BF16), w_out[layer, ATT_WIDTH:].astype(BF16),
            norm2_g[layer].astype(F32).reshape(1, d), wrh, wrl,
            router_bias[layer].astype(F32).reshape(N_EXPERTS, 1),
            jnp.concatenate([w_gate_s[layer], w_up_s[layer]], axis=1).astype(BF16),
            w_down_s[layer].astype(BF16), seq_len, tiles['mix'])
        pad_starts, block_expert, block_valid, last_block = _block_tables(
            counts.reshape(N_EXPERTS).astype(I32), blk, n_blocks)
        dest = jnp.take(pad_starts, eidx) + rank
        x_slots = _dispatch(dest, h2p, n_blocks * blk, tiles['dispatch'])
        y_slots = _experts(block_expert, block_valid, last_block, x_slots,
                           w_gate_e[layer], w_up_e[layer], w_down_e[layer], blk)
        x2d = _combine(dest, gate, xs, mod3, y_slots, seq_len, tiles['combine'])
    return x2d.reshape(bsz, seq_len, d).astype(x.dtype)
```

```python
import functools
import math

import jax
import jax.numpy as jnp
from jax import lax
from jax.experimental import pallas as pl
from jax.experimental.pallas import tpu as pltpu

F32 = jnp.float32
BF16 = jnp.bfloat16
I32 = jnp.int32
U32 = jnp.uint32

LANES = 128
SUBLANES = 8

N_ATT_HEADS = 4
ATT_HEAD_DIM = 64
ATT_V_DIM = 2 * ATT_HEAD_DIM
QK_WIDTH = N_ATT_HEADS * 2 * ATT_HEAD_DIM
ATT_WIDTH = N_ATT_HEADS * ATT_V_DIM
ROPE_THETA = 10000.0
SSM_GROUP = 16
SSM_GROUPS = 32
SSM_STATE = 64
SSM_WIDTH = SSM_GROUPS * SSM_GROUP
SSM_CHUNK = SUBLANES
SSM_LANE_BLOCKS = SSM_WIDTH // LANES
GROUPS_PER_BLOCK = LANES // SSM_GROUP
N_EXPERTS = 256
TOP_K = 8
N_GROUPS = 8
TOPK_GROUPS = 4
EXPERTS_PER_GROUP = N_EXPERTS // N_GROUPS
EXPERT_DIM = 256
SHARED_DIM = 256
ROUTED_SCALE = 2.5
EPS = 1e-6

NT_DIMS = (((1,), (1,)), ((), ()))

VMEM_LIMIT = 48 * 1024 * 1024


def _dot(a, b):
    return jnp.dot(a, b, preferred_element_type=F32)


def _sigmoid(x):
    return 1.0 / (1.0 + jnp.exp(-x))


def _split_bf16(x):
    hi = x.astype(BF16)
    lo = (x - hi.astype(F32)).astype(BF16)
    return hi, lo


def _adaln_kernel(c_ref, w_ref, b_ref, o_ref):
    c = c_ref[...]
    sc = c * _sigmoid(c)
    o_ref[...] = jnp.dot(sc, w_ref[...], preferred_element_type=F32,
                         precision=lax.Precision.HIGHEST) + b_ref[...]


def _adaln(c, w, b):
    bsz, d = c.shape
    n = w.shape[1]
    tn = 1024
    return pl.pallas_call(
        _adaln_kernel,
        out_shape=jax.ShapeDtypeStruct((bsz, n), F32),
        grid=(n // tn,),
        in_specs=[pl.BlockSpec((bsz, d), lambda j: (0, 0)),
                  pl.BlockSpec((d, tn), lambda j: (0, j)),
                  pl.BlockSpec((1, tn), lambda j: (0, j))],
        out_specs=pl.BlockSpec((bsz, tn), lambda j: (0, j)),
        compiler_params=pltpu.CompilerParams(dimension_semantics=("arbitrary",),
                                             vmem_limit_bytes=VMEM_LIMIT),
        name="adaln",
    )(c, w, b.reshape(1, n))


def _inproj_kernel(x_ref, mod_ref, n1g_ref, w_ref, seg_ref, qg_ref, kg_ref, cos_ref, sin_ref,
                   q_ref, k_ref, v_ref, u_ref):
    x = x_ref[...]
    mod = mod_ref[0]
    shift, scale = mod[0:1], mod[1:2]
    ms = jnp.mean(x * x, axis=-1, keepdims=True)
    h = x * lax.rsqrt(ms + EPS) * n1g_ref[...]
    h = h * (1.0 + scale) + shift
    z = _dot(h.astype(BF16), w_ref[...])
    seg = seg_ref[...]
    cos = cos_ref[...]
    sin = sin_ref[...]
    lane = lax.broadcasted_iota(I32, cos.shape, 1)
    first_half = (lane % ATT_HEAD_DIM) < (ATT_HEAD_DIM // 2)

    def norm_rope(t, g_ref, out_scale):
        hi, lo = _split_bf16(t * t)
        msq = _dot(hi, seg) + _dot(lo, seg)
        tn = t * lax.rsqrt(msq + EPS) * g_ref[...]
        outs = []
        for hd in range(N_ATT_HEADS):
            th = tn[:, hd * LANES:(hd + 1) * LANES]
            partner = jnp.where(first_half,
                                pltpu.roll(th, LANES - ATT_HEAD_DIM // 2, 1),
                                pltpu.roll(th, ATT_HEAD_DIM // 2, 1))
            outs.append((th * cos + partner * sin) * out_scale)
        return jnp.concatenate(outs, axis=1)

    q_ref[...] = norm_rope(z[:, :QK_WIDTH], qg_ref, ATT_HEAD_DIM ** -0.5).astype(BF16)
    k_ref[...] = norm_rope(z[:, QK_WIDTH:2 * QK_WIDTH], kg_ref, 1.0).astype(BF16)
    v_ref[...] = z[:, 2 * QK_WIDTH:2 * QK_WIDTH + ATT_WIDTH].astype(BF16)
    u_ref[...] = z[:, 2 * QK_WIDTH + ATT_WIDTH:]


def _inproj(x2d, mod3, n1g, w_in_bf, seg, qg, kg, cos_t, sin_t, seq_len, tm):
    n_tok, d = x2d.shape
    tiles_per_seq = seq_len // tm
    in_width = w_in_bf.shape[1]
    row = lambda i: (i, 0)
    const = lambda i: (0, 0)
    return pl.pallas_call(
        _inproj_kernel,
        out_shape=(jax.ShapeDtypeStruct((n_tok, QK_WIDTH), BF16),
                   jax.ShapeDtypeStruct((n_tok, QK_WIDTH), BF16),
                   jax.ShapeDtypeStruct((n_tok, ATT_WIDTH), BF16),
                   jax.ShapeDtypeStruct((n_tok, SSM_WIDTH), F32)),
        grid=(n_tok // tm,),
        in_specs=[pl.BlockSpec((tm, d), row),
                  pl.BlockSpec((1, 6, d), lambda i: (i // tiles_per_seq, 0, 0)),
                  pl.BlockSpec((1, d), const),
                  pl.BlockSpec((d, in_width), const),
                  pl.BlockSpec((QK_WIDTH, QK_WIDTH), const),
                  pl.BlockSpec((1, QK_WIDTH), const),
                  pl.BlockSpec((1, QK_WIDTH), const),
                  pl.BlockSpec((tm, LANES), lambda i: (i % tiles_per_seq, 0)),
                  pl.BlockSpec((tm, LANES), lambda i: (i % tiles_per_seq, 0))],
        out_specs=(pl.BlockSpec((tm, QK_WIDTH), row),
                   pl.BlockSpec((tm, QK_WIDTH), row),
                   pl.BlockSpec((tm, ATT_WIDTH), row),
                   pl.BlockSpec((tm, SSM_WIDTH), row)),
        compiler_params=pltpu.CompilerParams(dimension_semantics=("arbitrary",),
                                             vmem_limit_bytes=VMEM_LIMIT),
        name="inproj",
    )(x2d, mod3, n1g, w_in_bf, seg, qg, kg, cos_t, sin_t)


def _attn_kernel(qi_ref, kj_ref, q_ref, k_ref, v_ref, lam_ref, sg_ref, o_ref,
                 m_ref, l_ref, acc_ref, *, out_scale):
    p = pl.program_id(2)
    qi = qi_ref[p]
    kj = kj_ref[p]

    @pl.when(kj == 0)
    def _():
        m_ref[...] = jnp.full(m_ref.shape, -jnp.inf, F32)
        l_ref[...] = jnp.zeros(l_ref.shape, F32)
        acc_ref[...] = jnp.zeros(acc_ref.shape, F32)

    def step(masked):
        q = q_ref[...]
        k = k_ref[...]
        v = v_ref[...]
        lane = lax.broadcasted_iota(I32, q.shape, 1)
        zero = jnp.zeros_like(q)
        maps = (jnp.where(lane < ATT_HEAD_DIM, q, zero), jnp.where(lane >= ATT_HEAD_DIM, q, zero))
        for c in range(2):
            s = lax.dot_general(maps[c], k, NT_DIMS, preferred_element_type=F32)
            if masked:
                r = lax.broadcasted_iota(I32, s.shape, 0)
                col = lax.broadcasted_iota(I32, s.shape, 1)
                s = jnp.where(col <= r, s, -jnp.inf)
            m_old = m_ref[c]
            m_new = jnp.maximum(m_old, jnp.max(s, axis=-1, keepdims=True))
            alpha = jnp.exp(m_old - m_new)
            pr = jnp.exp(s - jnp.concatenate([m_new] * (s.shape[1] // LANES), axis=1))
            l_ref[c] = alpha * l_ref[c] + jnp.sum(pr, axis=-1, keepdims=True)
            acc_ref[c] = alpha * acc_ref[c] + _dot(pr.astype(BF16), v)
            m_ref[c] = m_new

    @pl.when(kj < qi)
    def _():
        step(False)

    @pl.when(kj == qi)
    def _():
        step(True)
        o = acc_ref[0] / l_ref[0] - lam_ref[...] * (acc_ref[1] / l_ref[1])
        ms = jnp.mean(o * o, axis=-1, keepdims=True)
        o_ref[...] = (o * lax.rsqrt(ms + EPS) * sg_ref[...] * out_scale).astype(o_ref.dtype)


def _attention(q, k, v, lam_row, subln_g, bsz, seq_len, tq, out_scale):
    nq = seq_len // tq
    pairs = [(i, j) for i in range(nq) for j in range(i + 1)]
    qi = jnp.asarray([p[0] for p in pairs], I32)
    kj = jnp.asarray([p[1] for p in pairs], I32)
    q_map = lambda b, h, p, qi, kj: (b * nq + qi[p], h)
    k_map = lambda b, h, p, qi, kj: (b * nq + kj[p], h)
    const = lambda b, h, p, qi, kj: (0, 0)
    return pl.pallas_call(
        functools.partial(_attn_kernel, out_scale=out_scale),
        out_shape=jax.ShapeDtypeStruct((bsz * seq_len, ATT_WIDTH), BF16),
        grid_spec=pltpu.PrefetchScalarGridSpec(
            num_scalar_prefetch=2,
            grid=(bsz, N_ATT_HEADS, len(pairs)),
            in_specs=[pl.BlockSpec((tq, LANES), q_map),
                      pl.BlockSpec((tq, LANES), k_map),
                      pl.BlockSpec((tq, LANES), k_map),
                      pl.BlockSpec((1, LANES), const),
                      pl.BlockSpec((1, LANES), const)],
            out_specs=pl.BlockSpec((tq, LANES), q_map),
            scratch_shapes=[pltpu.VMEM((2, tq, LANES), F32),
                            pltpu.VMEM((2, tq, LANES), F32),
                            pltpu.VMEM((2, tq, ATT_V_DIM), F32)]),
        compiler_params=pltpu.CompilerParams(
            dimension_semantics=("arbitrary", "arbitrary", "arbitrary"),
            vmem_limit_bytes=VMEM_LIMIT),
        name="attn",
    )(qi, kj, q, k, v, lam_row, subln_g)


def _ssm_mats(a_re, a_im, log_dt, b_re, b_im, c_re, c_im, d_skip):
    a_re, a_im, b_re, b_im, c_re, c_im, d_skip = (
        t.astype(F32) for t in (a_re, a_im, b_re, b_im, c_re, c_im, d_skip))
    dt = jnp.exp(log_dt.astype(F32))[:, None]
    mag = jnp.exp(a_re * dt)
    abar_re = mag * jnp.cos(a_im * dt)
    abar_im = mag * jnp.sin(a_im * dt)
    den = a_re * a_re + a_im * a_im
    nr = abar_re - 1.0
    f_re = ((nr * a_re + abar_im * a_im) / den)[..., None]
    f_im = ((abar_im * a_re - nr * a_im) / den)[..., None]
    bb_re = f_re * b_re - f_im * b_im
    bb_im = f_re * b_im + f_im * b_re
    steps = jnp.arange(SSM_CHUNK + 1, dtype=F32)[:, None, None]
    pmag = jnp.exp(a_re * dt * steps)
    pw_re = pmag * jnp.cos(a_im * dt * steps)
    pw_im = pmag * jnp.sin(a_im * dt * steps)
    ca_re = c_re[None] * pw_re[:, :, None, :] - c_im[None] * pw_im[:, :, None, :]
    ca_im = c_re[None] * pw_im[:, :, None, :] + c_im[None] * pw_re[:, :, None, :]
    hp = lax.Precision.HIGHEST
    lag = (jnp.einsum('mgcp,gpd->mgcd', ca_re[:SSM_CHUNK], bb_re, precision=hp)
           - jnp.einsum('mgcp,gpd->mgcd', ca_im[:SSM_CHUNK], bb_im, precision=hp))
    lag = lag.at[0].add(d_skip[:, :, None] * jnp.eye(SSM_GROUP, dtype=F32)[None])
    nb, gpb = SSM_LANE_BLOCKS, GROUPS_PER_BLOCK
    n_state = gpb * SSM_STATE
    in_group = jnp.arange(LANES) // SSM_GROUP
    state_group = jnp.arange(n_state) // SSM_STATE

    def spread(small, row_group, col_group):
        w = small.shape[-1]
        tiled = jnp.tile(jnp.eye(w, dtype=BF16), (1, col_group.shape[0] // w))
        wide = jnp.einsum('...w,wn->...n', small.astype(BF16), tiled, preferred_element_type=F32)
        return jnp.where(row_group[:, None] == col_group[None, :], wide, 0.0).astype(BF16)

    lag_blocks = spread(jnp.transpose(lag, (0, 1, 3, 2)).reshape(SSM_CHUNK, nb, LANES, SSM_GROUP),
                        in_group, in_group)
    zero_block = jnp.zeros_like(lag_blocks[0])
    kmat = jnp.concatenate(
        [jnp.concatenate([lag_blocks[t - j] if t >= j else zero_block for t in range(SSM_CHUNK)], axis=-1)
         for j in range(SSM_CHUNK)], axis=1)
    rev = SSM_CHUNK - 1 - jnp.arange(SSM_CHUNK)
    w_re = pw_re[rev][..., None] * bb_re[None] - pw_im[rev][..., None] * bb_im[None]
    w_im = pw_re[rev][..., None] * bb_im[None] + pw_im[rev][..., None] * bb_re[None]

    def in_to_state(w):
        small = jnp.transpose(w, (0, 1, 3, 2)).reshape(SSM_CHUNK, nb, LANES, SSM_STATE)
        return spread(small, in_group, state_group)

    bmat = jnp.concatenate([in_to_state(w_re), in_to_state(w_im)], axis=-1)
    bmat = jnp.transpose(bmat, (1, 0, 2, 3)).reshape(nb, SSM_CHUNK * LANES, 2 * n_state)

    def state_to_out(ca):
        small = jnp.transpose(ca[1:], (0, 1, 3, 2)).reshape(SSM_CHUNK, nb, n_state, SSM_GROUP)
        blocks = spread(small, state_group, in_group)
        return jnp.transpose(blocks, (1, 2, 0, 3)).reshape(nb, n_state, SSM_CHUNK * LANES)

    cmat = jnp.concatenate([state_to_out(ca_re), -state_to_out(ca_im)], axis=1)
    a8 = jnp.concatenate([pw_re[SSM_CHUNK].reshape(nb, 1, n_state),
                          pw_im[SSM_CHUNK].reshape(nb, 1, n_state)], axis=-1)
    return kmat, bmat, cmat, a8


def _ssm_kernel(u_ref, km_ref, bm_ref, cm_ref, a8_ref, y_ref, carry_ref, se_ref, sp_ref):
    n_chunks = se_ref.shape[0]
    half = se_ref.shape[1] // 2

    @pl.when(pl.program_id(2) == 0)
    def _():
        carry_ref[...] = jnp.zeros(carry_ref.shape, F32)

    ucat = jnp.concatenate([u_ref[pl.ds(j, n_chunks, stride=SSM_CHUNK), :] for j in range(SSM_CHUNK)],
                           axis=1).astype(BF16)
    se_ref[...] = _dot(ucat, bm_ref[0])
    a8 = a8_ref[0]
    a_re, a_im = a8[:, :half], a8[:, half:]

    def body(m, carry):
        c_re, c_im = carry
        sp_ref[pl.ds(m, 1), :] = jnp.concatenate([c_re, c_im], axis=1)
        row = se_ref[pl.ds(m, 1), :]
        return (a_re * c_re - a_im * c_im + row[:, :half],
                a_re * c_im + a_im * c_re + row[:, half:])

    c_re, c_im = lax.fori_loop(0, n_chunks, body, (carry_ref[:, :half], carry_ref[:, half:]))
    carry_ref[...] = jnp.concatenate([c_re, c_im], axis=1)
    y = _dot(ucat, km_ref[0]) + _dot(sp_ref[...].astype(BF16), cm_ref[0])
    for t in range(SSM_CHUNK):
        y_ref[pl.ds(t, n_chunks, stride=SSM_CHUNK), :] = y[:, t * LANES:(t + 1) * LANES]


def _ssm(u, kmat, bmat, cmat, a8, bsz, seq_len, tt):
    nt = seq_len // tt
    n_chunks = tt // SSM_CHUNK
    width = SSM_CHUNK * LANES
    u_map = lambda g, b, i: (b * nt + i, g)
    w_map = lambda g, b, i: (g, 0, 0)
    return pl.pallas_call(
        _ssm_kernel,
        out_shape=jax.ShapeDtypeStruct((bsz * seq_len, SSM_WIDTH), F32),
        grid=(SSM_LANE_BLOCKS, bsz, nt),
        in_specs=[pl.BlockSpec((tt, LANES), u_map),
                  pl.BlockSpec((1, width, width), w_map),
                  pl.BlockSpec((1, width, width), w_map),
                  pl.BlockSpec((1, width, width), w_map),
                  pl.BlockSpec((1, 1, width), w_map)],
        out_specs=pl.BlockSpec((tt, LANES), u_map),
        scratch_shapes=[pltpu.VMEM((1, width), F32),
                        pltpu.VMEM((n_chunks, width), F32),
                        pltpu.VMEM((n_chunks, width), F32)],
        compiler_params=pltpu.CompilerParams(
            dimension_semantics=("arbitrary", "arbitrary", "arbitrary"),
            vmem_limit_bytes=VMEM_LIMIT),
        name="ssm",
    )(u, kmat, bmat, cmat, a8)


def _first_max(v, ids, n):
    m = jnp.max(v, axis=0, keepdims=True)
    ix = jnp.min(jnp.where(v == m, ids, n), axis=0, keepdims=True)
    return m, ix


def _mix_route_kernel(x_ref, y_ref, att_ref, mod_ref, wglu_ref, sng_ref, woa_ref, wos_ref, n2g_ref,
                      wrh_ref, wrl_ref, rb_ref, wgu_ref, wds_ref,
                      xs_ref, h2p_ref, eidx_ref, gate_ref, rank_ref, cnt_ref, carry_ref):
    tm = x_ref.shape[0]

    @pl.when(pl.program_id(0) == 0)
    def _():
        carry_ref[...] = jnp.zeros(carry_ref.shape, F32)

    mod = mod_ref[0]
    gate1, shift2, scale2, gate2 = mod[2:3], mod[3:4], mod[4:5], mod[5:6]

    y = y_ref[...]
    g = 0.5 * y * (1.0 + jnp.tanh(math.sqrt(2.0 / math.pi) * (y + 0.044715 * (y * y * y))))
    glu = g * _sigmoid(_dot(g.astype(BF16), wglu_ref[...]))
    ssm = glu * lax.rsqrt(jnp.mean(glu * glu, axis=-1, keepdims=True) + EPS) * sng_ref[...]

    mix = _dot(att_ref[...], woa_ref[...]) + _dot(ssm.astype(BF16), wos_ref[...])
    x1 = x_ref[...] + gate1 * mix
    h2 = x1 * lax.rsqrt(jnp.mean(x1 * x1, axis=-1, keepdims=True) + EPS) * n2g_ref[...]
    h2 = h2 * (1.0 + scale2) + shift2
    _store_rows(h2p_ref, _pack_rows(h2))
    hb, h_lo = _split_bf16(h2)

    gu = _dot(hb, wgu_ref[...])
    gs, us = gu[:, :SHARED_DIM], gu[:, SHARED_DIM:]
    act = (gs * _sigmoid(gs) * us).astype(BF16)
    xs_ref[...] = x1 + gate2 * _dot(act, wds_ref[...])

    wrh = wrh_ref[...]
    logits = (lax.dot_general(wrh, hb, NT_DIMS, preferred_element_type=F32)
              + lax.dot_general(wrh, h_lo, NT_DIMS, preferred_element_type=F32)
              + lax.dot_general(wrl_ref[...], hb, NT_DIMS, preferred_element_type=F32))
    score = _sigmoid(logits)
    biased = score + rb_ref[...]
    neg = -jnp.inf

    ids_g = lax.broadcasted_iota(I32, (EXPERTS_PER_GROUP, tm), 0)
    group_rows = []
    for gi in range(N_GROUPS):
        blk = biased[gi * EXPERTS_PER_GROUP:(gi + 1) * EXPERTS_PER_GROUP, :]
        m1, i1 = _first_max(blk, ids_g, EXPERTS_PER_GROUP)
        m2 = jnp.max(jnp.where(ids_g == i1, neg, blk), axis=0, keepdims=True)
        group_rows.append(m1 + m2)
    cur = jnp.concatenate(group_rows, axis=0)
    ids_8 = lax.broadcasted_iota(I32, cur.shape, 0)
    picked = jnp.zeros(cur.shape, F32)
    for _ in range(TOPK_GROUPS):
        _, ix = _first_max(cur, ids_8, N_GROUPS)
        hit = ids_8 == ix
        picked = jnp.where(hit, 1.0, picked)
        cur = jnp.where(hit, neg, cur)
    e_mask = jnp.concatenate(
        [jnp.broadcast_to(picked[gi:gi + 1, :], (EXPERTS_PER_GROUP, tm)) for gi in range(N_GROUPS)], axis=0)
    cand = jnp.where(e_mask > 0.0, biased, neg)

    ids_e = lax.broadcasted_iota(I32, cand.shape, 0)
    sel = jnp.zeros(cand.shape, F32)
    idx_rows, w_rows = [], []
    for _ in range(TOP_K):
        _, ix = _first_max(cand, ids_e, N_EXPERTS)
        hit = ids_e == ix
        idx_rows.append(ix)
        w_rows.append(jnp.sum(jnp.where(hit, score, 0.0), axis=0, keepdims=True))
        sel = jnp.where(hit, 1.0, sel)
        cand = jnp.where(hit, neg, cand)
    w_sum = w_rows[0]
    for w in w_rows[1:]:
        w_sum = w_sum + w
    gate_ref[...] = jnp.concatenate([w / w_sum * ROUTED_SCALE for w in w_rows], axis=0)
    eidx_ref[...] = jnp.concatenate(idx_rows, axis=0)

    t_row = lax.broadcasted_iota(I32, (tm, tm), 0)
    t_col = lax.broadcasted_iota(I32, (tm, tm), 1)
    earlier = jnp.where(t_row < t_col, 1.0, 0.0).astype(BF16)
    before = _dot(sel.astype(BF16), earlier) + carry_ref[...]
    rank_ref[...] = jnp.concatenate(
        [jnp.sum(jnp.where(ids_e == ix, before, 0.0), axis=0, keepdims=True) for ix in idx_rows],
        axis=0).astype(I32)
    carry_ref[...] = carry_ref[...] + jnp.sum(sel, axis=1, keepdims=True)
    cnt_ref[...] = carry_ref[...]


def _mix_route(x2d, y, att, mod3, wglu, sng, woa, wos, n2g, wrh, wrl, rb, wgu, wds, seq_len, tm):
    n_tok, d = x2d.shape
    tiles_per_seq = seq_len // tm
    row = lambda i: (i, 0)
    col = lambda i: (0, i)
    const = lambda i: (0, 0)
    full = lambda a: pl.BlockSpec(a.shape, const)
    return pl.pallas_call(
        _mix_route_kernel,
        out_shape=(jax.ShapeDtypeStruct((n_tok, d), F32),
                   jax.ShapeDtypeStruct((n_tok * ROW_TILE, LANES), U32),
                   jax.ShapeDtypeStruct((TOP_K, n_tok), I32),
                   jax.ShapeDtypeStruct((TOP_K, n_tok), F32),
                   jax.ShapeDtypeStruct((TOP_K, n_tok), I32),
                   jax.ShapeDtypeStruct((N_EXPERTS, 1), F32)),
        grid=(n_tok // tm,),
        in_specs=[pl.BlockSpec((tm, d), row),
                  pl.BlockSpec((tm, SSM_WIDTH), row),
                  pl.BlockSpec((tm, ATT_WIDTH), row),
                  pl.BlockSpec((1, 6, d), lambda i: (i // tiles_per_seq, 0, 0)),
                  full(wglu), full(sng), full(woa), full(wos), full(n2g),
                  full(wrh), full(wrl), full(rb), full(wgu), full(wds)],
        out_specs=(pl.BlockSpec((tm, d), row),
                   pl.BlockSpec((tm * ROW_TILE, LANES), row),
                   pl.BlockSpec((TOP_K, tm), col),
                   pl.BlockSpec((TOP_K, tm), col),
                   pl.BlockSpec((TOP_K, tm), col),
                   pl.BlockSpec((N_EXPERTS, 1), const)),
        scratch_shapes=[pltpu.VMEM((N_EXPERTS, 1), F32)],
        compiler_params=pltpu.CompilerParams(dimension_semantics=("arbitrary",),
                                             vmem_limit_bytes=VMEM_LIMIT),
        name="mix_route",
    )(x2d, y, att, mod3, wglu, sng, woa, wos, n2g, wrh, wrl, rb, wgu, wds)


ROW_TILE = 4
EXPERT_SUB = 256


def _pack_rows(v):
    half = v.shape[1] // 2
    return pltpu.pack_elementwise([v[:, :half], v[:, half:]], packed_dtype=BF16)


def _unpack_rows(w, index):
    return pltpu.unpack_elementwise(w, index=index, packed_dtype=BF16, unpacked_dtype=F32)


def _store_rows(ref, packed, first=0):
    m = packed.shape[0]
    for s in range(ROW_TILE):
        ref[pl.ds(first * ROW_TILE + s, m, stride=ROW_TILE), :] = packed[:, s * LANES:(s + 1) * LANES]


def _load_rows(ref, m, first=0):
    return jnp.concatenate([ref[pl.ds(first * ROW_TILE + s, m, stride=ROW_TILE), :] for s in range(ROW_TILE)],
                           axis=1)


def _row(ref, r):
    return ref.at[pl.ds(pl.multiple_of(r * ROW_TILE, ROW_TILE), ROW_TILE), :]


def _slot_kernel(e_ref, r_ref, ps_ref, d_ref):
    e = e_ref[...]
    ids = lax.broadcasted_iota(I32, (N_EXPERTS, e.shape[1]), 0)
    starts = ps_ref[...]
    rows = [jnp.sum(jnp.where(ids == e[k:k + 1, :], starts, 0.0), axis=0, keepdims=True)
            for k in range(TOP_K)]
    d_ref[...] = jnp.concatenate(rows, axis=0).astype(I32) + r_ref[...]


def _slots(eidx, rank, pad_starts, ts):
    n_tok = eidx.shape[1]
    col = lambda i: (0, i)
    return pl.pallas_call(
        _slot_kernel,
        out_shape=jax.ShapeDtypeStruct((TOP_K, n_tok), I32),
        grid=(n_tok // ts,),
        in_specs=[pl.BlockSpec((TOP_K, ts), col),
                  pl.BlockSpec((TOP_K, ts), col),
                  pl.BlockSpec((N_EXPERTS, 1), lambda i: (0, 0))],
        out_specs=pl.BlockSpec((TOP_K, ts), col),
        compiler_params=pltpu.CompilerParams(dimension_semantics=("arbitrary",),
                                             vmem_limit_bytes=VMEM_LIMIT),
        name="slots",
    )(eidx, rank, pad_starts.astype(F32).reshape(N_EXPERTS, 1))


def _dispatch_kernel(d_ref, h2p_ref, xs_ref, sem):
    td = d_ref.shape[1]

    def issue(t, carry):
        for k in range(TOP_K):
            pltpu.make_async_copy(_row(h2p_ref, t), _row(xs_ref, d_ref[k, t]), sem).start(priority=k % 2)
        return carry

    lax.fori_loop(0, td, issue, 0)
    for _ in range(TOP_K):
        pltpu.make_async_copy(h2p_ref, xs_ref.at[pl.ds(0, td * ROW_TILE), :], sem).wait()


def _dispatch(dest, h2p, n_slots, td):
    n_tok = dest.shape[1]
    return pl.pallas_call(
        _dispatch_kernel,
        out_shape=jax.ShapeDtypeStruct((n_slots * ROW_TILE, LANES), U32),
        grid=(n_tok // td,),
        in_specs=[pl.BlockSpec((TOP_K, td), lambda i: (0, i), memory_space=pltpu.SMEM),
                  pl.BlockSpec((td * ROW_TILE, LANES), lambda i: (i, 0))],
        out_specs=pl.BlockSpec(memory_space=pl.ANY),
        scratch_shapes=[pltpu.SemaphoreType.DMA],
        compiler_params=pltpu.CompilerParams(dimension_semantics=("arbitrary",),
                                             vmem_limit_bytes=VMEM_LIMIT),
        name="dispatch",
    )(dest, h2p)


def _expert_kernel(be_ref, nv_ref, last_ref, xs_ref, wg_ref, wu_ref, wd_ref, ys_ref,
                   wg_bf, wu_bf, wd_bf):
    i = pl.program_id(0)
    blk = xs_ref.shape[0] // ROW_TILE
    expert = be_ref[i]
    n_valid = nv_ref[i]
    changed = jnp.logical_or(i == 0, expert != be_ref[jnp.maximum(i - 1, 0)])

    @pl.when(jnp.logical_and(n_valid > 0, changed))
    def _():
        wg_bf[...] = wg_ref[0].astype(BF16)
        wu_bf[...] = wu_ref[0].astype(BF16)
        wd_bf[...] = wd_ref[0].astype(BF16)

    def swiglu_rows(first):
        packed = _load_rows(xs_ref, EXPERT_SUB, first)
        x = jnp.concatenate([_unpack_rows(packed, 0), _unpack_rows(packed, 1)], axis=1)
        rows = first + lax.broadcasted_iota(I32, (EXPERT_SUB, 1), 0)
        x = jnp.where(rows < n_valid, x, 0.0).astype(BF16)
        g = _dot(x, wg_bf[...])
        u = _dot(x, wu_bf[...])
        act = (g * _sigmoid(g) * u).astype(BF16)
        _store_rows(ys_ref, _pack_rows(_dot(act, wd_bf[...])), first)

    for first in range(0, blk, EXPERT_SUB):
        pl.when(n_valid > first)(functools.partial(swiglu_rows, first))


def _experts(block_expert, block_valid, last_block, xs, w_g, w_u, w_d, blk):
    n_blocks = block_expert.shape[0]
    d, de = w_g.shape[1], w_g.shape[2]
    x_map = lambda i, be, nv, last: (jnp.minimum(i, last[0]), 0)
    w_map = lambda i, be, nv, last: (be[i], 0, 0)
    return pl.pallas_call(
        _expert_kernel,
        out_shape=jax.ShapeDtypeStruct(xs.shape, U32),
        grid_spec=pltpu.PrefetchScalarGridSpec(
            num_scalar_prefetch=3,
            grid=(n_blocks,),
            in_specs=[pl.BlockSpec((blk * ROW_TILE, LANES), x_map),
                      pl.BlockSpec((1, d, de), w_map),
                      pl.BlockSpec((1, d, de), w_map),
                      pl.BlockSpec((1, de, d), w_map)],
            out_specs=pl.BlockSpec((blk * ROW_TILE, LANES), x_map),
            scratch_shapes=[pltpu.VMEM((d, de), BF16),
                            pltpu.VMEM((d, de), BF16),
                            pltpu.VMEM((de, d), BF16)]),
        compiler_params=pltpu.CompilerParams(dimension_semantics=("arbitrary",),
                                             vmem_limit_bytes=VMEM_LIMIT),
        name="experts",
    )(block_expert, block_valid, last_block, xs, w_g, w_u, w_d)


def _combine_kernel(d_ref, dn_ref, gate_ref, xs_ref, mod_ref, ys_ref, o_ref, buf, sem):
    tc = d_ref.shape[1]
    i = pl.program_id(0)
    slot = lax.rem(i, 2)

    def gather(dest_ref, into):
        def issue(t, carry):
            for k in range(TOP_K):
                pltpu.make_async_copy(_row(ys_ref, dest_ref[k, t]), _row(buf.at[into], k * tc + t),
                                      sem.at[into]).start(priority=k % 2)
            return carry
        lax.fori_loop(0, tc, issue, 0)

    @pl.when(i == 0)
    def _():
        gather(d_ref, 0)

    @pl.when(i + 1 < pl.num_programs(0))
    def _():
        gather(dn_ref, 1 - slot)

    pltpu.make_async_copy(ys_ref.at[pl.ds(0, TOP_K * tc * ROW_TILE), :], buf.at[slot], sem.at[slot]).wait()

    gate2 = mod_ref[0][5:6]
    rows = buf.at[slot]
    acc = [jnp.zeros((tc, LANES), F32) for _ in range(2 * ROW_TILE)]
    for k in range(TOP_K):
        gk = jnp.transpose(jnp.broadcast_to(gate_ref[k:k + 1, :], (LANES, tc)))
        for s in range(ROW_TILE):
            w = rows[pl.ds(k * tc * ROW_TILE + s, tc, stride=ROW_TILE), :]
            acc[s] = acc[s] + gk * _unpack_rows(w, 0)
            acc[ROW_TILE + s] = acc[ROW_TILE + s] + gk * _unpack_rows(w, 1)
    o_ref[...] = xs_ref[...] + gate2 * jnp.concatenate(acc, axis=1)


def _combine(dest, gate, xs, mod3, ys, seq_len, tc):
    n_tok, d = xs.shape
    tiles_per_seq = seq_len // tc
    n_tiles = n_tok // tc
    return pl.pallas_call(
        _combine_kernel,
        out_shape=jax.ShapeDtypeStruct((n_tok, d), F32),
        grid=(n_tiles,),
        in_specs=[pl.BlockSpec((TOP_K, tc), lambda i: (0, i), memory_space=pltpu.SMEM),
                  pl.BlockSpec((TOP_K, tc), lambda i: (0, jnp.minimum(i + 1, n_tiles - 1)),
                               memory_space=pltpu.SMEM),
                  pl.BlockSpec((TOP_K, tc), lambda i: (0, i)),
                  pl.BlockSpec((tc, d), lambda i: (i, 0)),
                  pl.BlockSpec((1, 6, d), lambda i: (i // tiles_per_seq, 0, 0)),
                  pl.BlockSpec(memory_space=pl.ANY)],
        out_specs=pl.BlockSpec((tc, d), lambda i: (i, 0)),
        scratch_shapes=[pltpu.VMEM((2, TOP_K * tc * ROW_TILE, LANES), U32),
                        pltpu.SemaphoreType.DMA((2,))],
        compiler_params=pltpu.CompilerParams(dimension_semantics=("arbitrary",),
                                             vmem_limit_bytes=VMEM_LIMIT),
        name="combine",
    )(dest, dest, gate, xs, mod3, ys)


def _tiles(seq_len):
    pick = lambda pref: min(pref, seq_len)
    return dict(inproj=pick(512), attn=pick(512), ssm=pick(1024), mix=pick(512),
                dispatch=pick(512), combine=pick(256), slots=pick(2048), expert_block=2 * EXPERT_SUB)


def _rope_tables(seq_len):
    half = ATT_HEAD_DIM // 2
    inv_freq = 1.0 / (ROPE_THETA ** (jnp.arange(0, ATT_HEAD_DIM, 2, dtype=F32) / ATT_HEAD_DIM))
    ang = jnp.arange(seq_len, dtype=F32)[:, None] * inv_freq[None, :]
    cos, sin = jnp.cos(ang), jnp.sin(ang)
    reps = LANES // half
    sign = jnp.tile(jnp.concatenate([-jnp.ones((half,), F32), jnp.ones((half,), F32)]), reps // 2)
    return jnp.tile(cos, (1, reps)), jnp.tile(sin, (1, reps)) * sign[None, :]


def _block_tables(counts, blk, n_blocks):
    padded = (counts + blk - 1) // blk * blk
    pad_ends = jnp.cumsum(padded)
    pad_starts = pad_ends - padded
    used = pad_ends[-1] // blk
    last = jnp.maximum(used - 1, 0)
    starts = jnp.arange(n_blocks, dtype=I32) * blk
    expert = jnp.sum((pad_ends[None, :] <= starts[:, None]).astype(I32), axis=1)
    expert = jnp.minimum(expert, N_EXPERTS - 1)
    expert = jnp.where(starts < pad_ends[-1], expert, expert[last])
    valid = jnp.clip(counts[expert] - (starts - pad_starts[expert]), 0, blk)
    valid = jnp.where(starts < pad_ends[-1], valid, 0).astype(I32)
    return pad_starts.astype(I32), expert, valid, last.reshape(1).astype(I32)


def kernel(x, c, norm1_g, norm2_g, w_ada, b_ada, w_in, q_norm_g, k_norm_g, lambda_q1, lambda_k1, lambda_q2, lambda_k2, subln_g, ssm_a_re, ssm_a_im, ssm_log_dt, ssm_b_re, ssm_b_im, ssm_c_re, ssm_c_im, ssm_d, w_glu, ssm_norm_g, w_out, w_router, router_bias, w_gate_e, w_up_e, w_down_e, w_gate_s, w_up_s, w_down_s):
    bsz, seq_len, d = x.shape
    n_tok = bsz * seq_len
    tiles = _tiles(seq_len)
    blk = tiles['expert_block']
    n_blocks = (n_tok * TOP_K + N_EXPERTS * (blk - 1) + blk - 1) // blk
    cos_t, sin_t = _rope_tables(seq_len)
    seg = jnp.kron(jnp.eye(QK_WIDTH // ATT_HEAD_DIM, dtype=F32),
                   jnp.full((ATT_HEAD_DIM, ATT_HEAD_DIM), 1.0 / ATT_HEAD_DIM, F32)).astype(BF16)
    reps = QK_WIDTH // ATT_HEAD_DIM
    x2d = x.reshape(n_tok, d).astype(F32)
    for layer in range(w_ada.shape[0]):
        lam_init = 0.8 - 0.6 * math.exp(-0.3 * layer)
        mod3 = _adaln(c.astype(F32), w_ada[layer].astype(F32), b_ada[layer].astype(F32)).reshape(bsz, 6, d)
        q, k, v, u = _inproj(
            x2d, mod3, norm1_g[layer].astype(F32).reshape(1, d), w_in[layer].astype(BF16), seg,
            jnp.tile(q_norm_g[layer].astype(F32), reps).reshape(1, QK_WIDTH),
            jnp.tile(k_norm_g[layer].astype(F32), reps).reshape(1, QK_WIDTH),
            cos_t, sin_t, seq_len, tiles['inproj'])
        lam = (jnp.exp(jnp.sum(lambda_q1[layer].astype(F32) * lambda_k1[layer].astype(F32)))
               - jnp.exp(jnp.sum(lambda_q2[layer].astype(F32) * lambda_k2[layer].astype(F32))) + lam_init)
        att = _attention(q, k, v, jnp.full((1, LANES), lam, F32),
                         subln_g[layer].astype(F32).reshape(1, ATT_V_DIM),
                         bsz, seq_len, tiles['attn'], 1.0 - lam_init)
        kmat, bmat, cmat, a8 = _ssm_mats(ssm_a_re[layer], ssm_a_im[layer], ssm_log_dt[layer],
                                         ssm_b_re[layer], ssm_b_im[layer], ssm_c_re[layer],
                                         ssm_c_im[layer], ssm_d[layer])
        y = _ssm(u, kmat, bmat, cmat, a8, bsz, seq_len, tiles['ssm'])
        wr_t = jnp.transpose(w_router[layer].astype(F32))
        wrh, wrl = _split_bf16(wr_t)
        xs, h2p, eidx, gate, rank, counts = _mix_route(
            x2d, y, att, mod3, w_glu[layer].astype(BF16),
            ssm_norm_g[layer].astype(F32).reshape(1, SSM_WIDTH),
            w_out[layer, :ATT_WIDTH].astype(BF16), w_out[layer, ATT_WIDTH:].astype(BF16),
            norm2_g[layer].astype(F32).reshape(1, d), wrh, wrl,
            router_bias[layer].astype(F32).reshape(N_EXPERTS, 1),
            jnp.concatenate([w_gate_s[layer], w_up_s[layer]], axis=1).astype(BF16),
            w_down_s[layer].astype(BF16), seq_len, tiles['mix'])
        pad_starts, block_expert, block_valid, last_block = _block_tables(
            counts.reshape(N_EXPERTS).astype(I32), blk, n_blocks)
        dest = _slots(eidx, rank, pad_starts, tiles['slots'])
        x_slots = _dispatch(dest, h2p, n_blocks * blk, tiles['dispatch'])
        y_slots = _experts(block_expert, block_valid, last_block, x_slots,
                           w_gate_e[layer], w_up_e[layer], w_down_e[layer], blk)
        x2d = _combine(dest, gate, xs, mod3, y_slots, seq_len, tiles['combine'])
    return x2d.reshape(bsz, seq_len, d).astype(x.dtype)
```

```python
import functools
import math

import jax
import jax.numpy as jnp
from jax import lax
from jax.experimental import pallas as pl
from jax.experimental.pallas import tpu as pltpu

F32 = jnp.float32
BF16 = jnp.bfloat16
I32 = jnp.int32
U32 = jnp.uint32

LANES = 128
SUBLANES = 8

N_ATT_HEADS = 4
ATT_HEAD_DIM = 64
ATT_V_DIM = 2 * ATT_HEAD_DIM
QK_WIDTH = N_ATT_HEADS * 2 * ATT_HEAD_DIM
ATT_WIDTH = N_ATT_HEADS * ATT_V_DIM
ROPE_THETA = 10000.0
SSM_GROUP = 16
SSM_GROUPS = 32
SSM_STATE = 64
SSM_WIDTH = SSM_GROUPS * SSM_GROUP
SSM_CHUNK = SUBLANES
SSM_LANE_BLOCKS = SSM_WIDTH // LANES
GROUPS_PER_BLOCK = LANES // SSM_GROUP
N_EXPERTS = 256
TOP_K = 8
N_GROUPS = 8
TOPK_GROUPS = 4
EXPERTS_PER_GROUP = N_EXPERTS // N_GROUPS
EXPERT_DIM = 256
SHARED_DIM = 256
ROUTED_SCALE = 2.5
EPS = 1e-6

NT_DIMS = (((1,), (1,)), ((), ()))

VMEM_LIMIT = 48 * 1024 * 1024


def _dot(a, b):
    return jnp.dot(a, b, preferred_element_type=F32)


def _sigmoid(x):
    return 1.0 / (1.0 + jnp.exp(-x))


def _split_bf16(x):
    hi = x.astype(BF16)
    lo = (x - hi.astype(F32)).astype(BF16)
    return hi, lo


def _adaln_kernel(c_ref, w_ref, b_ref, o_ref):
    c = c_ref[...]
    sc = c * _sigmoid(c)
    o_ref[...] = jnp.dot(sc, w_ref[...], preferred_element_type=F32,
                         precision=lax.Precision.HIGHEST) + b_ref[...]


def _adaln(c, w, b):
    bsz, d = c.shape
    n = w.shape[1]
    tn = 1024
    return pl.pallas_call(
        _adaln_kernel,
        out_shape=jax.ShapeDtypeStruct((bsz, n), F32),
        grid=(n // tn,),
        in_specs=[pl.BlockSpec((bsz, d), lambda j: (0, 0)),
                  pl.BlockSpec((d, tn), lambda j: (0, j)),
                  pl.BlockSpec((1, tn), lambda j: (0, j))],
        out_specs=pl.BlockSpec((bsz, tn), lambda j: (0, j)),
        compiler_params=pltpu.CompilerParams(dimension_semantics=("arbitrary",),
                                             vmem_limit_bytes=VMEM_LIMIT),
        name="adaln",
    )(c, w, b.reshape(1, n))


def _inproj_kernel(x_ref, mod_ref, n1g_ref, w_ref, seg_ref, qg_ref, kg_ref, cos_ref, sin_ref,
                   q_ref, k_ref, v_ref, u_ref):
    x = x_ref[...]
    mod = mod_ref[0]
    shift, scale = mod[0:1], mod[1:2]
    ms = jnp.mean(x * x, axis=-1, keepdims=True)
    h = x * lax.rsqrt(ms + EPS) * n1g_ref[...]
    h = h * (1.0 + scale) + shift
    z = _dot(h.astype(BF16), w_ref[...])
    seg = seg_ref[...]
    cos = cos_ref[...]
    sin = sin_ref[...]
    lane = lax.broadcasted_iota(I32, cos.shape, 1)
    first_half = (lane % ATT_HEAD_DIM) < (ATT_HEAD_DIM // 2)

    def norm_rope(t, g_ref, out_scale):
        hi, lo = _split_bf16(t * t)
        msq = _dot(hi, seg) + _dot(lo, seg)
        tn = t * lax.rsqrt(msq + EPS) * g_ref[...]
        outs = []
        for hd in range(N_ATT_HEADS):
            th = tn[:, hd * LANES:(hd + 1) * LANES]
            partner = jnp.where(first_half,
                                pltpu.roll(th, LANES - ATT_HEAD_DIM // 2, 1),
                                pltpu.roll(th, ATT_HEAD_DIM // 2, 1))
            outs.append((th * cos + partner * sin) * out_scale)
        return jnp.concatenate(outs, axis=1)

    q_ref[...] = norm_rope(z[:, :QK_WIDTH], qg_ref, ATT_HEAD_DIM ** -0.5).astype(BF16)
    k_ref[...] = norm_rope(z[:, QK_WIDTH:2 * QK_WIDTH], kg_ref, 1.0).astype(BF16)
    v_ref[...] = z[:, 2 * QK_WIDTH:2 * QK_WIDTH + ATT_WIDTH].astype(BF16)
    u_ref[...] = z[:, 2 * QK_WIDTH + ATT_WIDTH:]


def _inproj(x2d, mod3, n1g, w_in_bf, seg, qg, kg, cos_t, sin_t, seq_len, tm):
    n_tok, d = x2d.shape
    tiles_per_seq = seq_len // tm
    in_width = w_in_bf.shape[1]
    row = lambda i: (i, 0)
    const = lambda i: (0, 0)
    return pl.pallas_call(
        _inproj_kernel,
        out_shape=(jax.ShapeDtypeStruct((n_tok, QK_WIDTH), BF16),
                   jax.ShapeDtypeStruct((n_tok, QK_WIDTH), BF16),
                   jax.ShapeDtypeStruct((n_tok, ATT_WIDTH), BF16),
                   jax.ShapeDtypeStruct((n_tok, SSM_WIDTH), F32)),
        grid=(n_tok // tm,),
        in_specs=[pl.BlockSpec((tm, d), row),
                  pl.BlockSpec((1, 6, d), lambda i: (i // tiles_per_seq, 0, 0)),
                  pl.BlockSpec((1, d), const),
                  pl.BlockSpec((d, in_width), const),
                  pl.BlockSpec((QK_WIDTH, QK_WIDTH), const),
                  pl.BlockSpec((1, QK_WIDTH), const),
                  pl.BlockSpec((1, QK_WIDTH), const),
                  pl.BlockSpec((tm, LANES), lambda i: (i % tiles_per_seq, 0)),
                  pl.BlockSpec((tm, LANES), lambda i: (i % tiles_per_seq, 0))],
        out_specs=(pl.BlockSpec((tm, QK_WIDTH), row),
                   pl.BlockSpec((tm, QK_WIDTH), row),
                   pl.BlockSpec((tm, ATT_WIDTH), row),
                   pl.BlockSpec((tm, SSM_WIDTH), row)),
        compiler_params=pltpu.CompilerParams(dimension_semantics=("arbitrary",),
                                             vmem_limit_bytes=VMEM_LIMIT),
        name="inproj",
    )(x2d, mod3, n1g, w_in_bf, seg, qg, kg, cos_t, sin_t)


def _attn_kernel(qi_ref, kj_ref, q_ref, k_ref, v_ref, lam_ref, sg_ref, o_ref,
                 m_ref, l_ref, acc_ref, *, out_scale):
    p = pl.program_id(2)
    qi = qi_ref[p]
    kj = kj_ref[p]

    @pl.when(kj == 0)
    def _():
        m_ref[...] = jnp.full(m_ref.shape, -jnp.inf, F32)
        l_ref[...] = jnp.zeros(l_ref.shape, F32)
        acc_ref[...] = jnp.zeros(acc_ref.shape, F32)

    def step(masked):
        q = q_ref[...]
        k = k_ref[...]
        v = v_ref[...]
        lane = lax.broadcasted_iota(I32, q.shape, 1)
        zero = jnp.zeros_like(q)
        maps = (jnp.where(lane < ATT_HEAD_DIM, q, zero), jnp.where(lane >= ATT_HEAD_DIM, q, zero))
        for c in range(2):
            s = lax.dot_general(maps[c], k, NT_DIMS, preferred_element_type=F32)
            if masked:
                r = lax.broadcasted_iota(I32, s.shape, 0)
                col = lax.broadcasted_iota(I32, s.shape, 1)
                s = jnp.where(col <= r, s, -jnp.inf)
            m_old = m_ref[c]
            m_new = jnp.maximum(m_old, jnp.max(s, axis=-1, keepdims=True))
            alpha = jnp.exp(m_old - m_new)
            pr = jnp.exp(s - jnp.concatenate([m_new] * (s.shape[1] // LANES), axis=1))
            l_ref[c] = alpha * l_ref[c] + jnp.sum(pr, axis=-1, keepdims=True)
            acc_ref[c] = alpha * acc_ref[c] + _dot(pr.astype(BF16), v)
            m_ref[c] = m_new

    @pl.when(kj < qi)
    def _():
        step(False)

    @pl.when(kj == qi)
    def _():
        step(True)
        o = acc_ref[0] / l_ref[0] - lam_ref[...] * (acc_ref[1] / l_ref[1])
        ms = jnp.mean(o * o, axis=-1, keepdims=True)
        o_ref[...] = (o * lax.rsqrt(ms + EPS) * sg_ref[...] * out_scale).astype(o_ref.dtype)


def _attention(q, k, v, lam_row, subln_g, bsz, seq_len, tq, out_scale):
    nq = seq_len // tq
    pairs = [(i, j) for i in range(nq) for j in range(i + 1)]
    qi = jnp.asarray([p[0] for p in pairs], I32)
    kj = jnp.asarray([p[1] for p in pairs], I32)
    q_map = lambda b, h, p, qi, kj: (b * nq + qi[p], h)
    k_map = lambda b, h, p, qi, kj: (b * nq + kj[p], h)
    const = lambda b, h, p, qi, kj: (0, 0)
    return pl.pallas_call(
        functools.partial(_attn_kernel, out_scale=out_scale),
        out_shape=jax.ShapeDtypeStruct((bsz * seq_len, ATT_WIDTH), BF16),
        grid_spec=pltpu.PrefetchScalarGridSpec(
            num_scalar_prefetch=2,
            grid=(bsz, N_ATT_HEADS, len(pairs)),
            in_specs=[pl.BlockSpec((tq, LANES), q_map),
                      pl.BlockSpec((tq, LANES), k_map),
                      pl.BlockSpec((tq, LANES), k_map),
                      pl.BlockSpec((1, LANES), const),
                      pl.BlockSpec((1, LANES), const)],
            out_specs=pl.BlockSpec((tq, LANES), q_map),
            scratch_shapes=[pltpu.VMEM((2, tq, LANES), F32),
                            pltpu.VMEM((2, tq, LANES), F32),
                            pltpu.VMEM((2, tq, ATT_V_DIM), F32)]),
        compiler_params=pltpu.CompilerParams(
            dimension_semantics=("arbitrary", "arbitrary", "arbitrary"),
            vmem_limit_bytes=VMEM_LIMIT),
        name="attn",
    )(qi, kj, q, k, v, lam_row, subln_g)


def _ssm_mats(a_re, a_im, log_dt, b_re, b_im, c_re, c_im, d_skip):
    a_re, a_im, b_re, b_im, c_re, c_im, d_skip = (
        t.astype(F32) for t in (a_re, a_im, b_re, b_im, c_re, c_im, d_skip))
    dt = jnp.exp(log_dt.astype(F32))[:, None]
    mag = jnp.exp(a_re * dt)
    abar_re = mag * jnp.cos(a_im * dt)
    abar_im = mag * jnp.sin(a_im * dt)
    den = a_re * a_re + a_im * a_im
    nr = abar_re - 1.0
    f_re = ((nr * a_re + abar_im * a_im) / den)[..., None]
    f_im = ((abar_im * a_re - nr * a_im) / den)[..., None]
    bb_re = f_re * b_re - f_im * b_im
    bb_im = f_re * b_im + f_im * b_re
    steps = jnp.arange(SSM_CHUNK + 1, dtype=F32)[:, None, None]
    pmag = jnp.exp(a_re * dt * steps)
    pw_re = pmag * jnp.cos(a_im * dt * steps)
    pw_im = pmag * jnp.sin(a_im * dt * steps)
    ca_re = c_re[None] * pw_re[:, :, None, :] - c_im[None] * pw_im[:, :, None, :]
    ca_im = c_re[None] * pw_im[:, :, None, :] + c_im[None] * pw_re[:, :, None, :]
    hp = lax.Precision.HIGHEST
    lag = (jnp.einsum('mgcp,gpd->mgcd', ca_re[:SSM_CHUNK], bb_re, precision=hp)
           - jnp.einsum('mgcp,gpd->mgcd', ca_im[:SSM_CHUNK], bb_im, precision=hp))
    lag = lag.at[0].add(d_skip[:, :, None] * jnp.eye(SSM_GROUP, dtype=F32)[None])
    nb, gpb = SSM_LANE_BLOCKS, GROUPS_PER_BLOCK
    n_state = gpb * SSM_STATE
    in_group = jnp.arange(LANES) // SSM_GROUP
    state_group = jnp.arange(n_state) // SSM_STATE

    def spread(small, row_group, col_group):
        w = small.shape[-1]
        tiled = jnp.tile(jnp.eye(w, dtype=BF16), (1, col_group.shape[0] // w))
        wide = jnp.einsum('...w,wn->...n', small.astype(BF16), tiled, preferred_element_type=F32)
        return jnp.where(row_group[:, None] == col_group[None, :], wide, 0.0).astype(BF16)

    lag_blocks = spread(jnp.transpose(lag, (0, 1, 3, 2)).reshape(SSM_CHUNK, nb, LANES, SSM_GROUP),
                        in_group, in_group)
    zero_block = jnp.zeros_like(lag_blocks[0])
    kmat = jnp.concatenate(
        [jnp.concatenate([lag_blocks[t - j] if t >= j else zero_block for t in range(SSM_CHUNK)], axis=-1)
         for j in range(SSM_CHUNK)], axis=1)
    rev = SSM_CHUNK - 1 - jnp.arange(SSM_CHUNK)
    w_re = pw_re[rev][..., None] * bb_re[None] - pw_im[rev][..., None] * bb_im[None]
    w_im = pw_re[rev][..., None] * bb_im[None] + pw_im[rev][..., None] * bb_re[None]

    def in_to_state(w):
        small = jnp.transpose(w, (0, 1, 3, 2)).reshape(SSM_CHUNK, nb, LANES, SSM_STATE)
        return spread(small, in_group, state_group)

    bmat = jnp.concatenate([in_to_state(w_re), in_to_state(w_im)], axis=-1)
    bmat = jnp.transpose(bmat, (1, 0, 2, 3)).reshape(nb, SSM_CHUNK * LANES, 2 * n_state)

    def state_to_out(ca):
        small = jnp.transpose(ca[1:], (0, 1, 3, 2)).reshape(SSM_CHUNK, nb, n_state, SSM_GROUP)
        blocks = spread(small, state_group, in_group)
        return jnp.transpose(blocks, (1, 2, 0, 3)).reshape(nb, n_state, SSM_CHUNK * LANES)

    cmat = jnp.concatenate([state_to_out(ca_re), -state_to_out(ca_im)], axis=1)
    chunk_steps = SSM_CHUNK * jnp.arange(2 * SUBLANES, dtype=F32)[:, None, None]
    cmag = jnp.exp(a_re * dt * chunk_steps)
    to_block = lambda t: jnp.transpose(t.reshape(2 * SUBLANES, nb, n_state), (1, 0, 2))
    bpow = jnp.concatenate([to_block(cmag * jnp.cos(a_im * dt * chunk_steps)),
                            to_block(cmag * jnp.sin(a_im * dt * chunk_steps))], axis=-1)
    return kmat, bmat, cmat, bpow


def _cmul(a_re, a_im, x_re, x_im):
    return a_re * x_re - a_im * x_im, a_re * x_im + a_im * x_re


def _ssm_kernel(u_ref, km_ref, bm_ref, cm_ref, bp_ref, y_ref, carry_ref, sp_ref):
    n_chunks = sp_ref.shape[0]
    half = sp_ref.shape[1] // 2

    @pl.when(pl.program_id(2) == 0)
    def _():
        carry_ref[...] = jnp.zeros(carry_ref.shape, F32)

    ucat = jnp.concatenate([u_ref[pl.ds(j, n_chunks, stride=SSM_CHUNK), :] for j in range(SSM_CHUNK)],
                           axis=1).astype(BF16)
    s_end = _dot(ucat, bm_ref[0])
    p_re, p_im = s_end[:, :half], s_end[:, half:]
    bp = bp_ref[0]
    sub = lax.broadcasted_iota(I32, (n_chunks, half), 0) % SUBLANES

    def shifted(v, d):
        return jnp.where(sub >= d, pltpu.roll(v, d, 0), 0.0)

    for d in (1, 2, 4):
        d_re, d_im = _cmul(bp[d:d + 1, :half], bp[d:d + 1, half:], shifted(p_re, d), shifted(p_im, d))
        p_re, p_im = p_re + d_re, p_im + d_im
    x_re, x_im = shifted(p_re, 1), shifted(p_im, 1)
    t_re, t_im = bp[:SUBLANES, :half], bp[:SUBLANES, half:]
    l_re, l_im = bp[SUBLANES:SUBLANES + 1, :half], bp[SUBLANES:SUBLANES + 1, half:]
    c_re, c_im = carry_ref[:, :half], carry_ref[:, half:]
    for g in range(n_chunks // SUBLANES):
        lo, hi = g * SUBLANES, (g + 1) * SUBLANES
        d_re, d_im = _cmul(t_re, t_im, c_re, c_im)
        sp_ref[lo:hi, :half] = x_re[lo:hi] + d_re
        sp_ref[lo:hi, half:] = x_im[lo:hi] + d_im
        e_re, e_im = _cmul(l_re, l_im, c_re, c_im)
        c_re, c_im = p_re[hi - 1:hi] + e_re, p_im[hi - 1:hi] + e_im
    carry_ref[...] = jnp.concatenate([c_re, c_im], axis=1)
    y = _dot(ucat, km_ref[0]) + _dot(sp_ref[...].astype(BF16), cm_ref[0])
    for t in range(SSM_CHUNK):
        y_ref[pl.ds(t, n_chunks, stride=SSM_CHUNK), :] = y[:, t * LANES:(t + 1) * LANES]


def _ssm(u, kmat, bmat, cmat, bpow, bsz, seq_len, tt):
    nt = seq_len // tt
    n_chunks = tt // SSM_CHUNK
    width = SSM_CHUNK * LANES
    u_map = lambda g, b, i: (b * nt + i, g)
    w_map = lambda g, b, i: (g, 0, 0)
    return pl.pallas_call(
        _ssm_kernel,
        out_shape=jax.ShapeDtypeStruct((bsz * seq_len, SSM_WIDTH), F32),
        grid=(SSM_LANE_BLOCKS, bsz, nt),
        in_specs=[pl.BlockSpec((tt, LANES), u_map),
                  pl.BlockSpec((1, width, width), w_map),
                  pl.BlockSpec((1, width, width), w_map),
                  pl.BlockSpec((1, width, width), w_map),
                  pl.BlockSpec((1, 2 * SUBLANES, width), w_map)],
        out_specs=pl.BlockSpec((tt, LANES), u_map),
        scratch_shapes=[pltpu.VMEM((1, width), F32),
                        pltpu.VMEM((n_chunks, width), F32)],
        compiler_params=pltpu.CompilerParams(
            dimension_semantics=("arbitrary", "arbitrary", "arbitrary"),
            vmem_limit_bytes=VMEM_LIMIT),
        name="ssm",
    )(u, kmat, bmat, cmat, bpow)


def _first_max(v, ids, n):
    m = jnp.max(v, axis=0, keepdims=True)
    ix = jnp.min(jnp.where(v == m, ids, n), axis=0, keepdims=True)
    return m, ix


def _mix_route_kernel(x_ref, y_ref, att_ref, mod_ref, wglu_ref, sng_ref, woa_ref, wos_ref, n2g_ref,
                      wrh_ref, wrl_ref, rb_ref, wgu_ref, wds_ref,
                      xs_ref, h2p_ref, eidx_ref, gate_ref, rank_ref, cnt_ref, carry_ref):
    tm = x_ref.shape[0]

    @pl.when(pl.program_id(0) == 0)
    def _():
        carry_ref[...] = jnp.zeros(carry_ref.shape, F32)

    mod = mod_ref[0]
    gate1, shift2, scale2, gate2 = mod[2:3], mod[3:4], mod[4:5], mod[5:6]

    y = y_ref[...]
    g = 0.5 * y * (1.0 + jnp.tanh(math.sqrt(2.0 / math.pi) * (y + 0.044715 * (y * y * y))))
    glu = g * _sigmoid(_dot(g.astype(BF16), wglu_ref[...]))
    ssm = glu * lax.rsqrt(jnp.mean(glu * glu, axis=-1, keepdims=True) + EPS) * sng_ref[...]

    mix = _dot(att_ref[...], woa_ref[...]) + _dot(ssm.astype(BF16), wos_ref[...])
    x1 = x_ref[...] + gate1 * mix
    h2 = x1 * lax.rsqrt(jnp.mean(x1 * x1, axis=-1, keepdims=True) + EPS) * n2g_ref[...]
    h2 = h2 * (1.0 + scale2) + shift2
    _store_rows(h2p_ref, _pack_rows(h2))
    hb, h_lo = _split_bf16(h2)

    gu = _dot(hb, wgu_ref[...])
    gs, us = gu[:, :SHARED_DIM], gu[:, SHARED_DIM:]
    act = (gs * _sigmoid(gs) * us).astype(BF16)
    xs_ref[...] = x1 + gate2 * _dot(act, wds_ref[...])

    wrh = wrh_ref[...]
    logits = (lax.dot_general(wrh, hb, NT_DIMS, preferred_element_type=F32)
              + lax.dot_general(wrh, h_lo, NT_DIMS, preferred_element_type=F32)
              + lax.dot_general(wrl_ref[...], hb, NT_DIMS, preferred_element_type=F32))
    score = _sigmoid(logits)
    biased = score + rb_ref[...]
    neg = -jnp.inf

    ids_g = lax.broadcasted_iota(I32, (EXPERTS_PER_GROUP, tm), 0)
    group_rows = []
    for gi in range(N_GROUPS):
        blk = biased[gi * EXPERTS_PER_GROUP:(gi + 1) * EXPERTS_PER_GROUP, :]
        m1, i1 = _first_max(blk, ids_g, EXPERTS_PER_GROUP)
        m2 = jnp.max(jnp.where(ids_g == i1, neg, blk), axis=0, keepdims=True)
        group_rows.append(m1 + m2)
    cur = jnp.concatenate(group_rows, axis=0)
    ids_8 = lax.broadcasted_iota(I32, cur.shape, 0)
    picked = jnp.zeros(cur.shape, F32)
    for _ in range(TOPK_GROUPS):
        _, ix = _first_max(cur, ids_8, N_GROUPS)
        hit = ids_8 == ix
        picked = jnp.where(hit, 1.0, picked)
        cur = jnp.where(hit, neg, cur)
    e_mask = jnp.concatenate(
        [jnp.broadcast_to(picked[gi:gi + 1, :], (EXPERTS_PER_GROUP, tm)) for gi in range(N_GROUPS)], axis=0)
    cand = jnp.where(e_mask > 0.0, biased, neg)

    ids_e = lax.broadcasted_iota(I32, cand.shape, 0)
    sel = jnp.zeros(cand.shape, F32)
    idx_rows, w_rows = [], []
    for _ in range(TOP_K):
        _, ix = _first_max(cand, ids_e, N_EXPERTS)
        hit = ids_e == ix
        idx_rows.append(ix)
        w_rows.append(jnp.sum(jnp.where(hit, score, 0.0), axis=0, keepdims=True))
        sel = jnp.where(hit, 1.0, sel)
        cand = jnp.where(hit, neg, cand)
    w_sum = w_rows[0]
    for w in w_rows[1:]:
        w_sum = w_sum + w
    gate_ref[...] = jnp.concatenate([w / w_sum * ROUTED_SCALE for w in w_rows], axis=0)
    eidx_ref[...] = jnp.concatenate(idx_rows, axis=0)

    t_row = lax.broadcasted_iota(I32, (tm, tm), 0)
    t_col = lax.broadcasted_iota(I32, (tm, tm), 1)
    earlier = jnp.where(t_row < t_col, 1.0, 0.0).astype(BF16)
    before = _dot(sel.astype(BF16), earlier) + carry_ref[...]
    rank_ref[...] = jnp.concatenate(
        [jnp.sum(jnp.where(ids_e == ix, before, 0.0), axis=0, keepdims=True) for ix in idx_rows],
        axis=0).astype(I32)
    carry_ref[...] = carry_ref[...] + jnp.sum(sel, axis=1, keepdims=True)
    cnt_ref[...] = carry_ref[...]


def _mix_route(x2d, y, att, mod3, wglu, sng, woa, wos, n2g, wrh, wrl, rb, wgu, wds, seq_len, tm):
    n_tok, d = x2d.shape
    tiles_per_seq = seq_len // tm
    row = lambda i: (i, 0)
    col = lambda i: (0, i)
    const = lambda i: (0, 0)
    full = lambda a: pl.BlockSpec(a.shape, const)
    return pl.pallas_call(
        _mix_route_kernel,
        out_shape=(jax.ShapeDtypeStruct((n_tok, d), F32),
                   jax.ShapeDtypeStruct((n_tok * ROW_TILE, LANES), U32),
                   jax.ShapeDtypeStruct((TOP_K, n_tok), I32),
                   jax.ShapeDtypeStruct((TOP_K, n_tok), F32),
                   jax.ShapeDtypeStruct((TOP_K, n_tok), I32),
                   jax.ShapeDtypeStruct((N_EXPERTS, 1), F32)),
        grid=(n_tok // tm,),
        in_specs=[pl.BlockSpec((tm, d), row),
                  pl.BlockSpec((tm, SSM_WIDTH), row),
                  pl.BlockSpec((tm, ATT_WIDTH), row),
                  pl.BlockSpec((1, 6, d), lambda i: (i // tiles_per_seq, 0, 0)),
                  full(wglu), full(sng), full(woa), full(wos), full(n2g),
                  full(wrh), full(wrl), full(rb), full(wgu), full(wds)],
        out_specs=(pl.BlockSpec((tm, d), row),
                   pl.BlockSpec((tm * ROW_TILE, LANES), row),
                   pl.BlockSpec((TOP_K, tm), col),
                   pl.BlockSpec((TOP_K, tm), col),
                   pl.BlockSpec((TOP_K, tm), col),
                   pl.BlockSpec((N_EXPERTS, 1), const)),
        scratch_shapes=[pltpu.VMEM((N_EXPERTS, 1), F32)],
        compiler_params=pltpu.CompilerParams(dimension_semantics=("arbitrary",),
                                             vmem_limit_bytes=VMEM_LIMIT),
        name="mix_route",
    )(x2d, y, att, mod3, wglu, sng, woa, wos, n2g, wrh, wrl, rb, wgu, wds)


ROW_TILE = 4
EXPERT_SUB = 256


def _pack_rows(v):
    half = v.shape[1] // 2
    return pltpu.pack_elementwise([v[:, :half], v[:, half:]], packed_dtype=BF16)


def _unpack_rows(w, index):
    return pltpu.unpack_elementwise(w, index=index, packed_dtype=BF16, unpacked_dtype=F32)


def _store_rows(ref, packed, first=0):
    m = packed.shape[0]
    for s in range(ROW_TILE):
        ref[pl.ds(first * ROW_TILE + s, m, stride=ROW_TILE), :] = packed[:, s * LANES:(s + 1) * LANES]


def _load_rows(ref, m, first=0):
    return jnp.concatenate([ref[pl.ds(first * ROW_TILE + s, m, stride=ROW_TILE), :] for s in range(ROW_TILE)],
                           axis=1)


def _row(ref, r):
    return ref.at[pl.ds(pl.multiple_of(r * ROW_TILE, ROW_TILE), ROW_TILE), :]


def _slot_kernel(e_ref, r_ref, ps_ref, d_ref):
    e = e_ref[...]
    ids = lax.broadcasted_iota(I32, (N_EXPERTS, e.shape[1]), 0)
    starts = ps_ref[...]
    rows = [jnp.sum(jnp.where(ids == e[k:k + 1, :], starts, 0.0), axis=0, keepdims=True)
            for k in range(TOP_K)]
    d_ref[...] = jnp.concatenate(rows, axis=0).astype(I32) + r_ref[...]


def _slots(eidx, rank, pad_starts, ts):
    n_tok = eidx.shape[1]
    col = lambda i: (0, i)
    return pl.pallas_call(
        _slot_kernel,
        out_shape=jax.ShapeDtypeStruct((TOP_K, n_tok), I32),
        grid=(n_tok // ts,),
        in_specs=[pl.BlockSpec((TOP_K, ts), col),
                  pl.BlockSpec((TOP_K, ts), col),
                  pl.BlockSpec((N_EXPERTS, 1), lambda i: (0, 0))],
        out_specs=pl.BlockSpec((TOP_K, ts), col),
        compiler_params=pltpu.CompilerParams(dimension_semantics=("arbitrary",),
                                             vmem_limit_bytes=VMEM_LIMIT),
        name="slots",
    )(eidx, rank, pad_starts.astype(F32).reshape(N_EXPERTS, 1))


def _dispatch_kernel(d_ref, h2p_ref, xs_ref, sem):
    td = d_ref.shape[1]

    def issue(t, carry):
        for k in range(TOP_K):
            pltpu.make_async_copy(_row(h2p_ref, t), _row(xs_ref, d_ref[k, t]), sem).start(priority=k % 2)
        return carry

    lax.fori_loop(0, td, issue, 0)
    for _ in range(TOP_K):
        pltpu.make_async_copy(h2p_ref, xs_ref.at[pl.ds(0, td * ROW_TILE), :], sem).wait()


def _dispatch(dest, h2p, n_slots, td):
    n_tok = dest.shape[1]
    return pl.pallas_call(
        _dispatch_kernel,
        out_shape=jax.ShapeDtypeStruct((n_slots * ROW_TILE, LANES), U32),
        grid=(n_tok // td,),
        in_specs=[pl.BlockSpec((TOP_K, td), lambda i: (0, i), memory_space=pltpu.SMEM),
                  pl.BlockSpec((td * ROW_TILE, LANES), lambda i: (i, 0))],
        out_specs=pl.BlockSpec(memory_space=pl.ANY),
        scratch_shapes=[pltpu.SemaphoreType.DMA],
        compiler_params=pltpu.CompilerParams(dimension_semantics=("arbitrary",),
                                             vmem_limit_bytes=VMEM_LIMIT),
        name="dispatch",
    )(dest, h2p)


def _expert_kernel(be_ref, nv_ref, nx_ref, last_ref, xs_ref, wg_hbm, wu_hbm, wd_hbm, ys_ref,
                   wg_f32, wu_f32, wd_f32, wg_bf, wu_bf, wd_bf, sem):
    i = pl.program_id(0)
    blk = xs_ref.shape[0] // ROW_TILE
    expert = be_ref[i]
    n_valid = nv_ref[i]
    changed = jnp.logical_or(i == 0, expert != be_ref[jnp.maximum(i - 1, 0)])

    def weight_copies(e):
        return [pltpu.make_async_copy(wg_hbm.at[e], wg_f32, sem.at[0]),
                pltpu.make_async_copy(wu_hbm.at[e], wu_f32, sem.at[1]),
                pltpu.make_async_copy(wd_hbm.at[e], wd_f32, sem.at[2])]

    @pl.when(i == 0)
    def _():
        for cp in weight_copies(expert):
            cp.start()

    @pl.when(jnp.logical_and(n_valid > 0, changed))
    def _():
        for cp in weight_copies(expert):
            cp.wait()
        wg_bf[...] = wg_f32[...].astype(BF16)
        wu_bf[...] = wu_f32[...].astype(BF16)
        wd_bf[...] = wd_f32[...].astype(BF16)
        upcoming = nx_ref[i]

        @pl.when(upcoming >= 0)
        def _():
            for cp in weight_copies(upcoming):
                cp.start()

    def swiglu_rows(first):
        packed = _load_rows(xs_ref, EXPERT_SUB, first)
        x = jnp.concatenate([_unpack_rows(packed, 0), _unpack_rows(packed, 1)], axis=1)
        rows = first + lax.broadcasted_iota(I32, (EXPERT_SUB, 1), 0)
        x = jnp.where(rows < n_valid, x, 0.0).astype(BF16)
        g = _dot(x, wg_bf[...])
        u = _dot(x, wu_bf[...])
        act = (g * _sigmoid(g) * u).astype(BF16)
        _store_rows(ys_ref, _pack_rows(_dot(act, wd_bf[...])), first)

    for first in range(0, blk, EXPERT_SUB):
        pl.when(n_valid > first)(functools.partial(swiglu_rows, first))


def _experts(block_expert, block_valid, next_expert, last_block, xs, w_g, w_u, w_d, blk):
    n_blocks = block_expert.shape[0]
    d, de = w_g.shape[1], w_g.shape[2]
    x_map = lambda i, be, nv, nx, last: (jnp.minimum(i, last[0]), 0)
    hbm = pl.BlockSpec(memory_space=pl.ANY)
    return pl.pallas_call(
        _expert_kernel,
        out_shape=jax.ShapeDtypeStruct(xs.shape, U32),
        grid_spec=pltpu.PrefetchScalarGridSpec(
            num_scalar_prefetch=4,
            grid=(n_blocks,),
            in_specs=[pl.BlockSpec((blk * ROW_TILE, LANES), x_map), hbm, hbm, hbm],
            out_specs=pl.BlockSpec((blk * ROW_TILE, LANES), x_map),
            scratch_shapes=[pltpu.VMEM((d, de), F32),
                            pltpu.VMEM((d, de), F32),
                            pltpu.VMEM((de, d), F32),
                            pltpu.VMEM((d, de), BF16),
                            pltpu.VMEM((d, de), BF16),
                            pltpu.VMEM((de, d), BF16),
                            pltpu.SemaphoreType.DMA((3,))]),
        compiler_params=pltpu.CompilerParams(dimension_semantics=("arbitrary",),
                                             vmem_limit_bytes=VMEM_LIMIT),
        name="experts",
    )(block_expert, block_valid, next_expert, last_block, xs, w_g, w_u, w_d)


def _combine_kernel(d_ref, dn_ref, gate_ref, xs_ref, mod_ref, ys_ref, o_ref, buf, sem):
    tc = d_ref.shape[1]
    i = pl.program_id(0)
    slot = lax.rem(i, 2)

    def gather(dest_ref, into):
        def issue(t, carry):
            for k in range(TOP_K):
                pltpu.make_async_copy(_row(ys_ref, dest_ref[k, t]), _row(buf.at[into], k * tc + t),
                                      sem.at[into]).start(priority=k % 2)
            return carry
        lax.fori_loop(0, tc, issue, 0)

    @pl.when(i == 0)
    def _():
        gather(d_ref, 0)

    @pl.when(i + 1 < pl.num_programs(0))
    def _():
        gather(dn_ref, 1 - slot)

    pltpu.make_async_copy(ys_ref.at[pl.ds(0, TOP_K * tc * ROW_TILE), :], buf.at[slot], sem.at[slot]).wait()

    gate2 = mod_ref[0][5:6]
    rows = buf.at[slot]
    acc = [jnp.zeros((tc, LANES), F32) for _ in range(2 * ROW_TILE)]
    for k in range(TOP_K):
        gk = jnp.transpose(jnp.broadcast_to(gate_ref[k:k + 1, :], (LANES, tc)))
        for s in range(ROW_TILE):
            w = rows[pl.ds(k * tc * ROW_TILE + s, tc, stride=ROW_TILE), :]
            acc[s] = acc[s] + gk * _unpack_rows(w, 0)
            acc[ROW_TILE + s] = acc[ROW_TILE + s] + gk * _unpack_rows(w, 1)
    o_ref[...] = xs_ref[...] + gate2 * jnp.concatenate(acc, axis=1)


def _combine(dest, gate, xs, mod3, ys, seq_len, tc):
    n_tok, d = xs.shape
    tiles_per_seq = seq_len // tc
    n_tiles = n_tok // tc
    return pl.pallas_call(
        _combine_kernel,
        out_shape=jax.ShapeDtypeStruct((n_tok, d), F32),
        grid=(n_tiles,),
        in_specs=[pl.BlockSpec((TOP_K, tc), lambda i: (0, i), memory_space=pltpu.SMEM),
                  pl.BlockSpec((TOP_K, tc), lambda i: (0, jnp.minimum(i + 1, n_tiles - 1)),
                               memory_space=pltpu.SMEM),
                  pl.BlockSpec((TOP_K, tc), lambda i: (0, i)),
                  pl.BlockSpec((tc, d), lambda i: (i, 0)),
                  pl.BlockSpec((1, 6, d), lambda i: (i // tiles_per_seq, 0, 0)),
                  pl.BlockSpec(memory_space=pl.ANY)],
        out_specs=pl.BlockSpec((tc, d), lambda i: (i, 0)),
        scratch_shapes=[pltpu.VMEM((2, TOP_K * tc * ROW_TILE, LANES), U32),
                        pltpu.SemaphoreType.DMA((2,))],
        compiler_params=pltpu.CompilerParams(dimension_semantics=("arbitrary",),
                                             vmem_limit_bytes=VMEM_LIMIT),
        name="combine",
    )(dest, dest, gate, xs, mod3, ys)


def _tiles(seq_len):
    pick = lambda pref: min(pref, seq_len)
    return dict(inproj=pick(512), attn=pick(512), ssm=pick(1024), mix=pick(512),
                dispatch=pick(512), combine=pick(256), slots=pick(2048), expert_block=2 * EXPERT_SUB)


def _rope_tables(seq_len):
    half = ATT_HEAD_DIM // 2
    inv_freq = 1.0 / (ROPE_THETA ** (jnp.arange(0, ATT_HEAD_DIM, 2, dtype=F32) / ATT_HEAD_DIM))
    ang = jnp.arange(seq_len, dtype=F32)[:, None] * inv_freq[None, :]
    cos, sin = jnp.cos(ang), jnp.sin(ang)
    reps = LANES // half
    sign = jnp.tile(jnp.concatenate([-jnp.ones((half,), F32), jnp.ones((half,), F32)]), reps // 2)
    return jnp.tile(cos, (1, reps)), jnp.tile(sin, (1, reps)) * sign[None, :]


def _block_tables(counts, blk, n_blocks):
    padded = (counts + blk - 1) // blk * blk
    pad_ends = jnp.cumsum(padded)
    pad_starts = pad_ends - padded
    used = pad_ends[-1] // blk
    last = jnp.maximum(used - 1, 0)
    starts = jnp.arange(n_blocks, dtype=I32) * blk
    expert = jnp.sum((pad_ends[None, :] <= starts[:, None]).astype(I32), axis=1)
    expert = jnp.minimum(expert, N_EXPERTS - 1)
    expert = jnp.where(starts < pad_ends[-1], expert, expert[last])
    valid = jnp.clip(counts[expert] - (starts - pad_starts[expert]), 0, blk)
    valid = jnp.where(starts < pad_ends[-1], valid, 0).astype(I32)
    ids = jnp.arange(N_EXPERTS, dtype=I32)
    later_used = jnp.logical_and(ids[None, :] > expert[:, None], (counts > 0)[None, :])
    upcoming = jnp.min(jnp.where(later_used, ids[None, :], N_EXPERTS), axis=1)
    upcoming = jnp.where(upcoming < N_EXPERTS, upcoming, -1).astype(I32)
    return pad_starts.astype(I32), expert, valid, upcoming, last.reshape(1).astype(I32)


def kernel(x, c, norm1_g, norm2_g, w_ada, b_ada, w_in, q_norm_g, k_norm_g, lambda_q1, lambda_k1, lambda_q2, lambda_k2, subln_g, ssm_a_re, ssm_a_im, ssm_log_dt, ssm_b_re, ssm_b_im, ssm_c_re, ssm_c_im, ssm_d, w_glu, ssm_norm_g, w_out, w_router, router_bias, w_gate_e, w_up_e, w_down_e, w_gate_s, w_up_s, w_down_s):
    bsz, seq_len, d = x.shape
    n_tok = bsz * seq_len
    tiles = _tiles(seq_len)
    blk = tiles['expert_block']
    n_blocks = (n_tok * TOP_K + N_EXPERTS * (blk - 1) + blk - 1) // blk
    cos_t, sin_t = _rope_tables(seq_len)
    seg = jnp.kron(jnp.eye(QK_WIDTH // ATT_HEAD_DIM, dtype=F32),
                   jnp.full((ATT_HEAD_DIM, ATT_HEAD_DIM), 1.0 / ATT_HEAD_DIM, F32)).astype(BF16)
    reps = QK_WIDTH // ATT_HEAD_DIM
    x2d = x.reshape(n_tok, d).astype(F32)
    for layer in range(w_ada.shape[0]):
        lam_init = 0.8 - 0.6 * math.exp(-0.3 * layer)
        mod3 = _adaln(c.astype(F32), w_ada[layer].astype(F32), b_ada[layer].astype(F32)).reshape(bsz, 6, d)
        q, k, v, u = _inproj(
            x2d, mod3, norm1_g[layer].astype(F32).reshape(1, d), w_in[layer].astype(BF16), seg,
            jnp.tile(q_norm_g[layer].astype(F32), reps).reshape(1, QK_WIDTH),
            jnp.tile(k_norm_g[layer].astype(F32), reps).reshape(1, QK_WIDTH),
            cos_t, sin_t, seq_len, tiles['inproj'])
        lam = (jnp.exp(jnp.sum(lambda_q1[layer].astype(F32) * lambda_k1[layer].astype(F32)))
               - jnp.exp(jnp.sum(lambda_q2[layer].astype(F32) * lambda_k2[layer].astype(F32))) + lam_init)
        att = _attention(q, k, v, jnp.full((1, LANES), lam, F32),
                         subln_g[layer].astype(F32).reshape(1, ATT_V_DIM),
                         bsz, seq_len, tiles['attn'], 1.0 - lam_init)
        kmat, bmat, cmat, bpow = _ssm_mats(ssm_a_re[layer], ssm_a_im[layer], ssm_log_dt[layer],
                                         ssm_b_re[layer], ssm_b_im[layer], ssm_c_re[layer],
                                         ssm_c_im[layer], ssm_d[layer])
        y = _ssm(u, kmat, bmat, cmat, bpow, bsz, seq_len, tiles['ssm'])
        wr_t = jnp.transpose(w_router[layer].astype(F32))
        wrh, wrl = _split_bf16(wr_t)
        xs, h2p, eidx, gate, rank, counts = _mix_route(
            x2d, y, att, mod3, w_glu[layer].astype(BF16),
            ssm_norm_g[layer].astype(F32).reshape(1, SSM_WIDTH),
            w_out[layer, :ATT_WIDTH].astype(BF16), w_out[layer, ATT_WIDTH:].astype(BF16),
            norm2_g[layer].astype(F32).reshape(1, d), wrh, wrl,
            router_bias[layer].astype(F32).reshape(N_EXPERTS, 1),
            jnp.concatenate([w_gate_s[layer], w_up_s[layer]], axis=1).astype(BF16),
            w_down_s[layer].astype(BF16), seq_len, tiles['mix'])
        pad_starts, block_expert, block_valid, next_expert, last_block = _block_tables(
            counts.reshape(N_EXPERTS).astype(I32), blk, n_blocks)
        dest = _slots(eidx, rank, pad_starts, tiles['slots'])
        x_slots = _dispatch(dest, h2p, n_blocks * blk, tiles['dispatch'])
        y_slots = _experts(block_expert, block_valid, next_expert, last_block, x_slots,
                           w_gate_e[layer], w_up_e[layer], w_down_e[layer], blk)
        x2d = _combine(dest, gate, xs, mod3, y_slots, seq_len, tiles['combine'])
    return x2d.reshape(bsz, seq_len, d).astype(x.dtype)
```

```python
import functools
import math

import jax
import jax.numpy as jnp
from jax import lax
from jax.experimental import pallas as pl
from jax.experimental.pallas import tpu as pltpu

F32 = jnp.float32
BF16 = jnp.bfloat16
I32 = jnp.int32
U32 = jnp.uint32

LANES = 128
SUBLANES = 8

N_ATT_HEADS = 4
ATT_HEAD_DIM = 64
ATT_V_DIM = 2 * ATT_HEAD_DIM
QK_WIDTH = N_ATT_HEADS * 2 * ATT_HEAD_DIM
ATT_WIDTH = N_ATT_HEADS * ATT_V_DIM
ROPE_THETA = 10000.0
SSM_GROUP = 16
SSM_GROUPS = 32
SSM_STATE = 64
SSM_WIDTH = SSM_GROUPS * SSM_GROUP
SSM_CHUNK = SUBLANES
SSM_LANE_BLOCKS = SSM_WIDTH // LANES
GROUPS_PER_BLOCK = LANES // SSM_GROUP
N_EXPERTS = 256
TOP_K = 8
N_GROUPS = 8
TOPK_GROUPS = 4
EXPERTS_PER_GROUP = N_EXPERTS // N_GROUPS
EXPERT_DIM = 256
SHARED_DIM = 256
ROUTED_SCALE = 2.5
EPS = 1e-6

NT_DIMS = (((1,), (1,)), ((), ()))

VMEM_LIMIT = 48 * 1024 * 1024


def _dot(a, b):
    return jnp.dot(a, b, preferred_element_type=F32)


def _sigmoid(x):
    return 1.0 / (1.0 + jnp.exp(-x))


def _split_bf16(x):
    hi = x.astype(BF16)
    lo = (x - hi.astype(F32)).astype(BF16)
    return hi, lo


def _adaln_kernel(c_ref, w_ref, b_ref, o_ref):
    c = c_ref[...]
    sc = c * _sigmoid(c)
    o_ref[...] = jnp.dot(sc, w_ref[...], preferred_element_type=F32,
                         precision=lax.Precision.HIGHEST) + b_ref[...]


def _adaln(c, w, b):
    bsz, d = c.shape
    n = w.shape[1]
    tn = 1024
    return pl.pallas_call(
        _adaln_kernel,
        out_shape=jax.ShapeDtypeStruct((bsz, n), F32),
        grid=(n // tn,),
        in_specs=[pl.BlockSpec((bsz, d), lambda j: (0, 0)),
                  pl.BlockSpec((d, tn), lambda j: (0, j)),
                  pl.BlockSpec((1, tn), lambda j: (0, j))],
        out_specs=pl.BlockSpec((bsz, tn), lambda j: (0, j)),
        compiler_params=pltpu.CompilerParams(dimension_semantics=("arbitrary",),
                                             vmem_limit_bytes=VMEM_LIMIT),
        name="adaln",
    )(c, w, b.reshape(1, n))


def _inproj_kernel(x_ref, mod_ref, n1g_ref, w_ref, seg_ref, qg_ref, kg_ref, cos_ref, sin_ref,
                   q_ref, k_ref, vt_ref, u_ref):
    x = x_ref[...]
    mod = mod_ref[0]
    shift, scale = mod[0:1], mod[1:2]
    ms = jnp.mean(x * x, axis=-1, keepdims=True)
    h = x * lax.rsqrt(ms + EPS) * n1g_ref[...]
    h = h * (1.0 + scale) + shift
    z = _dot(h.astype(BF16), w_ref[...])
    seg = seg_ref[...]
    cos = cos_ref[...]
    sin = sin_ref[...]
    lane = lax.broadcasted_iota(I32, cos.shape, 1)
    first_half = (lane % ATT_HEAD_DIM) < (ATT_HEAD_DIM // 2)

    def norm_rope(t, g_ref, out_scale):
        hi, lo = _split_bf16(t * t)
        msq = _dot(hi, seg) + _dot(lo, seg)
        tn = t * lax.rsqrt(msq + EPS) * g_ref[...]
        outs = []
        for hd in range(N_ATT_HEADS):
            th = tn[:, hd * LANES:(hd + 1) * LANES]
            partner = jnp.where(first_half,
                                pltpu.roll(th, LANES - ATT_HEAD_DIM // 2, 1),
                                pltpu.roll(th, ATT_HEAD_DIM // 2, 1))
            outs.append((th * cos + partner * sin) * out_scale)
        return jnp.concatenate(outs, axis=1)

    q_ref[...] = norm_rope(z[:, :QK_WIDTH], qg_ref, ATT_HEAD_DIM ** -0.5).astype(BF16)
    k_ref[...] = norm_rope(z[:, QK_WIDTH:2 * QK_WIDTH], kg_ref, 1.0).astype(BF16)
    vt_ref[...] = jnp.transpose(z[:, 2 * QK_WIDTH:2 * QK_WIDTH + ATT_WIDTH]).astype(BF16)
    u_ref[...] = z[:, 2 * QK_WIDTH + ATT_WIDTH:]


def _inproj(x2d, mod3, n1g, w_in_bf, seg, qg, kg, cos_t, sin_t, seq_len, tm):
    n_tok, d = x2d.shape
    tiles_per_seq = seq_len // tm
    in_width = w_in_bf.shape[1]
    row = lambda i: (i, 0)
    const = lambda i: (0, 0)
    return pl.pallas_call(
        _inproj_kernel,
        out_shape=(jax.ShapeDtypeStruct((n_tok, QK_WIDTH), BF16),
                   jax.ShapeDtypeStruct((n_tok, QK_WIDTH), BF16),
                   jax.ShapeDtypeStruct((ATT_WIDTH, n_tok), BF16),
                   jax.ShapeDtypeStruct((n_tok, SSM_WIDTH), F32)),
        grid=(n_tok // tm,),
        in_specs=[pl.BlockSpec((tm, d), row),
                  pl.BlockSpec((1, 6, d), lambda i: (i // tiles_per_seq, 0, 0)),
                  pl.BlockSpec((1, d), const),
                  pl.BlockSpec((d, in_width), const),
                  pl.BlockSpec((QK_WIDTH, QK_WIDTH), const),
                  pl.BlockSpec((1, QK_WIDTH), const),
                  pl.BlockSpec((1, QK_WIDTH), const),
                  pl.BlockSpec((tm, LANES), lambda i: (i % tiles_per_seq, 0)),
                  pl.BlockSpec((tm, LANES), lambda i: (i % tiles_per_seq, 0))],
        out_specs=(pl.BlockSpec((tm, QK_WIDTH), row),
                   pl.BlockSpec((tm, QK_WIDTH), row),
                   pl.BlockSpec((ATT_WIDTH, tm), lambda i: (0, i)),
                   pl.BlockSpec((tm, SSM_WIDTH), row)),
        compiler_params=pltpu.CompilerParams(dimension_semantics=("arbitrary",),
                                             vmem_limit_bytes=VMEM_LIMIT),
        name="inproj",
    )(x2d, mod3, n1g, w_in_bf, seg, qg, kg, cos_t, sin_t)


def _attn_kernel(qi_ref, kj_ref, q_ref, k_ref, vt_ref, lam_ref, sg_ref, o_ref,
                 m_ref, l_ref, acc_ref, *, out_scale):
    p = pl.program_id(2)
    qi = qi_ref[p]
    kj = kj_ref[p]

    @pl.when(kj == 0)
    def _():
        m_ref[...] = jnp.full(m_ref.shape, -jnp.inf, F32)
        l_ref[...] = jnp.zeros(l_ref.shape, F32)
        acc_ref[...] = jnp.zeros(acc_ref.shape, F32)

    def rows_of(stat, n):
        return jnp.concatenate([stat] * (n // SUBLANES), axis=0)

    def step(masked):
        q = q_ref[...]
        k = k_ref[...]
        vt = vt_ref[...]
        lane = lax.broadcasted_iota(I32, q.shape, 1)
        zero = jnp.zeros_like(q)
        maps = (jnp.where(lane < ATT_HEAD_DIM, q, zero), jnp.where(lane >= ATT_HEAD_DIM, q, zero))
        for c in range(2):
            st = lax.dot_general(k, maps[c], NT_DIMS, preferred_element_type=F32)
            if masked:
                key = lax.broadcasted_iota(I32, st.shape, 0)
                query = lax.broadcasted_iota(I32, st.shape, 1)
                st = jnp.where(key <= query, st, -jnp.inf)
            m_old = m_ref[c]
            m_new = jnp.maximum(m_old, jnp.max(st, axis=0, keepdims=True))
            alpha = jnp.exp(m_old - m_new)
            pt = jnp.exp(st - rows_of(m_new, st.shape[0]))
            l_ref[c] = alpha * l_ref[c] + jnp.sum(pt, axis=0, keepdims=True)
            acc_ref[c] = rows_of(alpha, ATT_V_DIM) * acc_ref[c] + _dot(vt, pt.astype(BF16))
            m_ref[c] = m_new

    @pl.when(kj < qi)
    def _():
        step(False)

    @pl.when(kj == qi)
    def _():
        step(True)
        o = (acc_ref[0] * rows_of(1.0 / l_ref[0], ATT_V_DIM)
             - lam_ref[:, :1] * (acc_ref[1] * rows_of(1.0 / l_ref[1], ATT_V_DIM)))
        ms = jnp.mean(o * o, axis=0, keepdims=True)
        o = o * lax.rsqrt(ms + EPS) * (sg_ref[...] * out_scale)
        o_ref[...] = jnp.transpose(o).astype(o_ref.dtype)


def _attention(q, k, v_t, lam_row, subln_col, bsz, seq_len, tq, out_scale):
    nq = seq_len // tq
    pairs = [(i, j) for i in range(nq) for j in range(i + 1)]
    qi = jnp.asarray([p[0] for p in pairs], I32)
    kj = jnp.asarray([p[1] for p in pairs], I32)
    q_map = lambda b, h, p, qi, kj: (b * nq + qi[p], h)
    k_map = lambda b, h, p, qi, kj: (b * nq + kj[p], h)
    vt_map = lambda b, h, p, qi, kj: (h, b * nq + kj[p])
    const = lambda b, h, p, qi, kj: (0, 0)
    return pl.pallas_call(
        functools.partial(_attn_kernel, out_scale=out_scale),
        out_shape=jax.ShapeDtypeStruct((bsz * seq_len, ATT_WIDTH), BF16),
        grid_spec=pltpu.PrefetchScalarGridSpec(
            num_scalar_prefetch=2,
            grid=(bsz, N_ATT_HEADS, len(pairs)),
            in_specs=[pl.BlockSpec((tq, LANES), q_map),
                      pl.BlockSpec((tq, LANES), k_map),
                      pl.BlockSpec((ATT_V_DIM, tq), vt_map),
                      pl.BlockSpec((1, LANES), const),
                      pl.BlockSpec((ATT_V_DIM, 1), const)],
            out_specs=pl.BlockSpec((tq, LANES), q_map),
            scratch_shapes=[pltpu.VMEM((2, SUBLANES, tq), F32),
                            pltpu.VMEM((2, SUBLANES, tq), F32),
                            pltpu.VMEM((2, ATT_V_DIM, tq), F32)]),
        compiler_params=pltpu.CompilerParams(
            dimension_semantics=("arbitrary", "arbitrary", "arbitrary"),
            vmem_limit_bytes=VMEM_LIMIT),
        name="attn",
    )(qi, kj, q, k, v_t, lam_row, subln_col)


def _ssm_mats(a_re, a_im, log_dt, b_re, b_im, c_re, c_im, d_skip):
    a_re, a_im, b_re, b_im, c_re, c_im, d_skip = (
        t.astype(F32) for t in (a_re, a_im, b_re, b_im, c_re, c_im, d_skip))
    dt = jnp.exp(log_dt.astype(F32))[:, None]
    mag = jnp.exp(a_re * dt)
    abar_re = mag * jnp.cos(a_im * dt)
    abar_im = mag * jnp.sin(a_im * dt)
    den = a_re * a_re + a_im * a_im
    nr = abar_re - 1.0
    f_re = ((nr * a_re + abar_im * a_im) / den)[..., None]
    f_im = ((abar_im * a_re - nr * a_im) / den)[..., None]
    bb_re = f_re * b_re - f_im * b_im
    bb_im = f_re * b_im + f_im * b_re
    steps = jnp.arange(SSM_CHUNK + 1, dtype=F32)[:, None, None]
    pmag = jnp.exp(a_re * dt * steps)
    pw_re = pmag * jnp.cos(a_im * dt * steps)
    pw_im = pmag * jnp.sin(a_im * dt * steps)
    ca_re = c_re[None] * pw_re[:, :, None, :] - c_im[None] * pw_im[:, :, None, :]
    ca_im = c_re[None] * pw_im[:, :, None, :] + c_im[None] * pw_re[:, :, None, :]
    hp = lax.Precision.HIGHEST
    lag = (jnp.einsum('mgcp,gpd->mgcd', ca_re[:SSM_CHUNK], bb_re, precision=hp)
           - jnp.einsum('mgcp,gpd->mgcd', ca_im[:SSM_CHUNK], bb_im, precision=hp))
    lag = lag.at[0].add(d_skip[:, :, None] * jnp.eye(SSM_GROUP, dtype=F32)[None])
    nb, gpb = SSM_LANE_BLOCKS, GROUPS_PER_BLOCK
    n_state = gpb * SSM_STATE
    in_group = jnp.arange(LANES) // SSM_GROUP
    state_group = jnp.arange(n_state) // SSM_STATE

    def spread(small, row_group, col_group):
        w = small.shape[-1]
        tiled = jnp.tile(jnp.eye(w, dtype=BF16), (1, col_group.shape[0] // w))
        wide = jnp.einsum('...w,wn->...n', small.astype(BF16), tiled, preferred_element_type=F32)
        return jnp.where(row_group[:, None] == col_group[None, :], wide, 0.0).astype(BF16)

    lag_blocks = spread(jnp.transpose(lag, (0, 1, 3, 2)).reshape(SSM_CHUNK, nb, LANES, SSM_GROUP),
                        in_group, in_group)
    zero_block = jnp.zeros_like(lag_blocks[0])
    kmat = jnp.concatenate(
        [jnp.concatenate([lag_blocks[t - j] if t >= j else zero_block for t in range(SSM_CHUNK)], axis=-1)
         for j in range(SSM_CHUNK)], axis=1)
    rev = SSM_CHUNK - 1 - jnp.arange(SSM_CHUNK)
    w_re = pw_re[rev][..., None] * bb_re[None] - pw_im[rev][..., None] * bb_im[None]
    w_im = pw_re[rev][..., None] * bb_im[None] + pw_im[rev][..., None] * bb_re[None]

    def in_to_state(w):
        small = jnp.transpose(w, (0, 1, 3, 2)).reshape(SSM_CHUNK, nb, LANES, SSM_STATE)
        return spread(small, in_group, state_group)

    bmat = jnp.concatenate([in_to_state(w_re), in_to_state(w_im)], axis=-1)
    bmat = jnp.transpose(bmat, (1, 0, 2, 3)).reshape(nb, SSM_CHUNK * LANES, 2 * n_state)

    def state_to_out(ca):
        small = jnp.transpose(ca[1:], (0, 1, 3, 2)).reshape(SSM_CHUNK, nb, n_state, SSM_GROUP)
        blocks = spread(small, state_group, in_group)
        return jnp.transpose(blocks, (1, 2, 0, 3)).reshape(nb, n_state, SSM_CHUNK * LANES)

    cmat = jnp.concatenate([state_to_out(ca_re), -state_to_out(ca_im)], axis=1)
    chunk_steps = SSM_CHUNK * jnp.arange(2 * SUBLANES, dtype=F32)[:, None, None]
    cmag = jnp.exp(a_re * dt * chunk_steps)
    to_block = lambda t: jnp.transpose(t.reshape(2 * SUBLANES, nb, n_state), (1, 0, 2))
    bpow = jnp.concatenate([to_block(cmag * jnp.cos(a_im * dt * chunk_steps)),
                            to_block(cmag * jnp.sin(a_im * dt * chunk_steps))], axis=-1)
    return kmat, bmat, cmat, bpow


def _cmul(a_re, a_im, x_re, x_im):
    return a_re * x_re - a_im * x_im, a_re * x_im + a_im * x_re


def _ssm_kernel(u_ref, km_ref, bm_ref, cm_ref, bp_ref, y_ref, carry_ref, sp_ref):
    n_chunks = sp_ref.shape[0]
    half = sp_ref.shape[1] // 2

    @pl.when(pl.program_id(2) == 0)
    def _():
        carry_ref[...] = jnp.zeros(carry_ref.shape, F32)

    ucat = jnp.concatenate([u_ref[pl.ds(j, n_chunks, stride=SSM_CHUNK), :] for j in range(SSM_CHUNK)],
                           axis=1).astype(BF16)
    s_end = _dot(ucat, bm_ref[0])
    p_re, p_im = s_end[:, :half], s_end[:, half:]
    bp = bp_ref[0]
    sub = lax.broadcasted_iota(I32, (n_chunks, half), 0) % SUBLANES

    def shifted(v, d):
        return jnp.where(sub >= d, pltpu.roll(v, d, 0), 0.0)

    for d in (1, 2, 4):
        d_re, d_im = _cmul(bp[d:d + 1, :half], bp[d:d + 1, half:], shifted(p_re, d), shifted(p_im, d))
        p_re, p_im = p_re + d_re, p_im + d_im
    x_re, x_im = shifted(p_re, 1), shifted(p_im, 1)
    t_re, t_im = bp[:SUBLANES, :half], bp[:SUBLANES, half:]
    l_re, l_im = bp[SUBLANES:SUBLANES + 1, :half], bp[SUBLANES:SUBLANES + 1, half:]
    c_re, c_im = carry_ref[:, :half], carry_ref[:, half:]
    for g in range(n_chunks // SUBLANES):
        lo, hi = g * SUBLANES, (g + 1) * SUBLANES
        d_re, d_im = _cmul(t_re, t_im, c_re, c_im)
        sp_ref[lo:hi, :half] = x_re[lo:hi] + d_re
        sp_ref[lo:hi, half:] = x_im[lo:hi] + d_im
        e_re, e_im = _cmul(l_re, l_im, c_re, c_im)
        c_re, c_im = p_re[hi - 1:hi] + e_re, p_im[hi - 1:hi] + e_im
    carry_ref[...] = jnp.concatenate([c_re, c_im], axis=1)
    y = _dot(ucat, km_ref[0]) + _dot(sp_ref[...].astype(BF16), cm_ref[0])
    for t in range(SSM_CHUNK):
        y_ref[pl.ds(t, n_chunks, stride=SSM_CHUNK), :] = y[:, t * LANES:(t + 1) * LANES]


def _ssm(u, kmat, bmat, cmat, bpow, bsz, seq_len, tt):
    nt = seq_len // tt
    n_chunks = tt // SSM_CHUNK
    width = SSM_CHUNK * LANES
    u_map = lambda g, b, i: (b * nt + i, g)
    w_map = lambda g, b, i: (g, 0, 0)
    return pl.pallas_call(
        _ssm_kernel,
        out_shape=jax.ShapeDtypeStruct((bsz * seq_len, SSM_WIDTH), F32),
        grid=(SSM_LANE_BLOCKS, bsz, nt),
        in_specs=[pl.BlockSpec((tt, LANES), u_map),
                  pl.BlockSpec((1, width, width), w_map),
                  pl.BlockSpec((1, width, width), w_map),
                  pl.BlockSpec((1, width, width), w_map),
                  pl.BlockSpec((1, 2 * SUBLANES, width), w_map)],
        out_specs=pl.BlockSpec((tt, LANES), u_map),
        scratch_shapes=[pltpu.VMEM((1, width), F32),
                        pltpu.VMEM((n_chunks, width), F32)],
        compiler_params=pltpu.CompilerParams(
            dimension_semantics=("arbitrary", "arbitrary", "arbitrary"),
            vmem_limit_bytes=VMEM_LIMIT),
        name="ssm",
    )(u, kmat, bmat, cmat, bpow)


def _first_max(v, ids, n):
    m = jnp.max(v, axis=0, keepdims=True)
    ix = jnp.min(jnp.where(v == m, ids, n), axis=0, keepdims=True)
    return m, ix


def _mix_route_kernel(x_ref, y_ref, att_ref, mod_ref, wglu_ref, sng_ref, woa_ref, wos_ref, n2g_ref,
                      wrh_ref, wrl_ref, rb_ref, wgu_ref, wds_ref,
                      xs_ref, h2p_ref, eidx_ref, gate_ref, rank_ref, cnt_ref, carry_ref):
    tm = x_ref.shape[0]

    @pl.when(pl.program_id(0) == 0)
    def _():
        carry_ref[...] = jnp.zeros(carry_ref.shape, F32)

    mod = mod_ref[0]
    gate1, shift2, scale2, gate2 = mod[2:3], mod[3:4], mod[4:5], mod[5:6]

    y = y_ref[...]
    g = 0.5 * y * (1.0 + jnp.tanh(math.sqrt(2.0 / math.pi) * (y + 0.044715 * (y * y * y))))
    glu = g * _sigmoid(_dot(g.astype(BF16), wglu_ref[...]))
    ssm = glu * lax.rsqrt(jnp.mean(glu * glu, axis=-1, keepdims=True) + EPS) * sng_ref[...]

    mix = _dot(att_ref[...], woa_ref[...]) + _dot(ssm.astype(BF16), wos_ref[...])
    x1 = x_ref[...] + gate1 * mix
    h2 = x1 * lax.rsqrt(jnp.mean(x1 * x1, axis=-1, keepdims=True) + EPS) * n2g_ref[...]
    h2 = h2 * (1.0 + scale2) + shift2
    _store_rows(h2p_ref, _pack_rows(h2))
    hb, h_lo = _split_bf16(h2)

    gu = _dot(hb, wgu_ref[...])
    gs, us = gu[:, :SHARED_DIM], gu[:, SHARED_DIM:]
    act = (gs * _sigmoid(gs) * us).astype(BF16)
    xs_ref[...] = x1 + gate2 * _dot(act, wds_ref[...])

    wrh = wrh_ref[...]
    logits = (lax.dot_general(wrh, hb, NT_DIMS, preferred_element_type=F32)
              + lax.dot_general(wrh, h_lo, NT_DIMS, preferred_element_type=F32)
              + lax.dot_general(wrl_ref[...], hb, NT_DIMS, preferred_element_type=F32))
    score = _sigmoid(logits)
    biased = score + rb_ref[...]
    neg = -jnp.inf

    ids_g = lax.broadcasted_iota(I32, (EXPERTS_PER_GROUP, tm), 0)
    group_rows = []
    for gi in range(N_GROUPS):
        blk = biased[gi * EXPERTS_PER_GROUP:(gi + 1) * EXPERTS_PER_GROUP, :]
        m1, i1 = _first_max(blk, ids_g, EXPERTS_PER_GROUP)
        m2 = jnp.max(jnp.where(ids_g == i1, neg, blk), axis=0, keepdims=True)
        group_rows.append(m1 + m2)
    cur = jnp.concatenate(group_rows, axis=0)
    ids_8 = lax.broadcasted_iota(I32, cur.shape, 0)
    picked = jnp.zeros(cur.shape, F32)
    for _ in range(TOPK_GROUPS):
        _, ix = _first_max(cur, ids_8, N_GROUPS)
        hit = ids_8 == ix
        picked = jnp.where(hit, 1.0, picked)
        cur = jnp.where(hit, neg, cur)
    e_mask = jnp.concatenate(
        [jnp.broadcast_to(picked[gi:gi + 1, :], (EXPERTS_PER_GROUP, tm)) for gi in range(N_GROUPS)], axis=0)
    cand = jnp.where(e_mask > 0.0, biased, neg)

    ids_e = lax.broadcasted_iota(I32, cand.shape, 0)
    sel = jnp.zeros(cand.shape, F32)
    idx_rows, w_rows = [], []
    for _ in range(TOP_K):
        _, ix = _first_max(cand, ids_e, N_EXPERTS)
        hit = ids_e == ix
        idx_rows.append(ix)
        w_rows.append(jnp.sum(jnp.where(hit, score, 0.0), axis=0, keepdims=True))
        sel = jnp.where(hit, 1.0, sel)
        cand = jnp.where(hit, neg, cand)
    w_sum = w_rows[0]
    for w in w_rows[1:]:
        w_sum = w_sum + w
    gate_ref[...] = jnp.concatenate([w / w_sum * ROUTED_SCALE for w in w_rows], axis=0)
    eidx_ref[...] = jnp.concatenate(idx_rows, axis=0)

    t_row = lax.broadcasted_iota(I32, (tm, tm), 0)
    t_col = lax.broadcasted_iota(I32, (tm, tm), 1)
    earlier = jnp.where(t_row < t_col, 1.0, 0.0).astype(BF16)
    before = _dot(sel.astype(BF16), earlier) + carry_ref[...]
    rank_ref[...] = jnp.concatenate(
        [jnp.sum(jnp.where(ids_e == ix, before, 0.0), axis=0, keepdims=True) for ix in idx_rows],
        axis=0).astype(I32)
    carry_ref[...] = carry_ref[...] + jnp.sum(sel, axis=1, keepdims=True)
    cnt_ref[...] = carry_ref[...]


def _mix_route(x2d, y, att, mod3, wglu, sng, woa, wos, n2g, wrh, wrl, rb, wgu, wds, seq_len, tm):
    n_tok, d = x2d.shape
    tiles_per_seq = seq_len // tm
    row = lambda i: (i, 0)
    col = lambda i: (0, i)
    const = lambda i: (0, 0)
    full = lambda a: pl.BlockSpec(a.shape, const)
    return pl.pallas_call(
        _mix_route_kernel,
        out_shape=(jax.ShapeDtypeStruct((n_tok, d), F32),
                   jax.ShapeDtypeStruct((n_tok * ROW_TILE, LANES), U32),
                   jax.ShapeDtypeStruct((TOP_K, n_tok), I32),
                   jax.ShapeDtypeStruct((TOP_K, n_tok), F32),
                   jax.ShapeDtypeStruct((TOP_K, n_tok), I32),
                   jax.ShapeDtypeStruct((N_EXPERTS, 1), F32)),
        grid=(n_tok // tm,),
        in_specs=[pl.BlockSpec((tm, d), row),
                  pl.BlockSpec((tm, SSM_WIDTH), row),
                  pl.BlockSpec((tm, ATT_WIDTH), row),
                  pl.BlockSpec((1, 6, d), lambda i: (i // tiles_per_seq, 0, 0)),
                  full(wglu), full(sng), full(woa), full(wos), full(n2g),
                  full(wrh), full(wrl), full(rb), full(wgu), full(wds)],
        out_specs=(pl.BlockSpec((tm, d), row),
                   pl.BlockSpec((tm * ROW_TILE, LANES), row),
                   pl.BlockSpec((TOP_K, tm), col),
                   pl.BlockSpec((TOP_K, tm), col),
                   pl.BlockSpec((TOP_K, tm), col),
                   pl.BlockSpec((N_EXPERTS, 1), const)),
        scratch_shapes=[pltpu.VMEM((N_EXPERTS, 1), F32)],
        compiler_params=pltpu.CompilerParams(dimension_semantics=("arbitrary",),
                                             vmem_limit_bytes=VMEM_LIMIT),
        name="mix_route",
    )(x2d, y, att, mod3, wglu, sng, woa, wos, n2g, wrh, wrl, rb, wgu, wds)


ROW_TILE = 4
EXPERT_SUB = 256


def _pack_rows(v):
    half = v.shape[1] // 2
    return pltpu.pack_elementwise([v[:, :half], v[:, half:]], packed_dtype=BF16)


def _unpack_rows(w, index):
    return pltpu.unpack_elementwise(w, index=index, packed_dtype=BF16, unpacked_dtype=F32)


def _store_rows(ref, packed, first=0):
    m = packed.shape[0]
    for s in range(ROW_TILE):
        ref[pl.ds(first * ROW_TILE + s, m, stride=ROW_TILE), :] = packed[:, s * LANES:(s + 1) * LANES]


def _load_rows(ref, m, first=0):
    return jnp.concatenate([ref[pl.ds(first * ROW_TILE + s, m, stride=ROW_TILE), :] for s in range(ROW_TILE)],
                           axis=1)


def _row(ref, r):
    return ref.at[pl.ds(pl.multiple_of(r * ROW_TILE, ROW_TILE), ROW_TILE), :]


def _slot_kernel(e_ref, r_ref, ps_ref, d_ref):
    e = e_ref[...]
    ids = lax.broadcasted_iota(I32, (N_EXPERTS, e.shape[1]), 0)
    starts = ps_ref[...]
    rows = [jnp.sum(jnp.where(ids == e[k:k + 1, :], starts, 0.0), axis=0, keepdims=True)
            for k in range(TOP_K)]
    d_ref[...] = jnp.concatenate(rows, axis=0).astype(I32) + r_ref[...]


def _slots(eidx, rank, pad_starts, ts):
    n_tok = eidx.shape[1]
    col = lambda i: (0, i)
    return pl.pallas_call(
        _slot_kernel,
        out_shape=jax.ShapeDtypeStruct((TOP_K, n_tok), I32),
        grid=(n_tok // ts,),
        in_specs=[pl.BlockSpec((TOP_K, ts), col),
                  pl.BlockSpec((TOP_K, ts), col),
                  pl.BlockSpec((N_EXPERTS, 1), lambda i: (0, 0))],
        out_specs=pl.BlockSpec((TOP_K, ts), col),
        compiler_params=pltpu.CompilerParams(dimension_semantics=("arbitrary",),
                                             vmem_limit_bytes=VMEM_LIMIT),
        name="slots",
    )(eidx, rank, pad_starts.astype(F32).reshape(N_EXPERTS, 1))


def _dispatch_kernel(d_ref, h2p_ref, xs_ref, sem):
    td = d_ref.shape[1]

    def issue(t, carry):
        for k in range(TOP_K):
            pltpu.make_async_copy(_row(h2p_ref, t), _row(xs_ref, d_ref[k, t]), sem).start(priority=k % 2)
        return carry

    lax.fori_loop(0, td, issue, 0)
    for _ in range(TOP_K):
        pltpu.make_async_copy(h2p_ref, xs_ref.at[pl.ds(0, td * ROW_TILE), :], sem).wait()


def _dispatch(dest, h2p, n_slots, td):
    n_tok = dest.shape[1]
    return pl.pallas_call(
        _dispatch_kernel,
        out_shape=jax.ShapeDtypeStruct((n_slots * ROW_TILE, LANES), U32),
        grid=(n_tok // td,),
        in_specs=[pl.BlockSpec((TOP_K, td), lambda i: (0, i), memory_space=pltpu.SMEM),
                  pl.BlockSpec((td * ROW_TILE, LANES), lambda i: (i, 0))],
        out_specs=pl.BlockSpec(memory_space=pl.ANY),
        scratch_shapes=[pltpu.SemaphoreType.DMA],
        compiler_params=pltpu.CompilerParams(dimension_semantics=("arbitrary",),
                                             vmem_limit_bytes=VMEM_LIMIT),
        name="dispatch",
    )(dest, h2p)


def _expert_kernel(be_ref, nv_ref, nx_ref, last_ref, xs_ref, wg_hbm, wu_hbm, wd_hbm, ys_ref,
                   wg_f32, wu_f32, wd_f32, wg_bf, wu_bf, wd_bf, sem):
    i = pl.program_id(0)
    blk = xs_ref.shape[0] // ROW_TILE
    expert = be_ref[i]
    n_valid = nv_ref[i]
    changed = jnp.logical_or(i == 0, expert != be_ref[jnp.maximum(i - 1, 0)])

    def weight_copies(e):
        return [pltpu.make_async_copy(wg_hbm.at[e], wg_f32, sem.at[0]),
                pltpu.make_async_copy(wu_hbm.at[e], wu_f32, sem.at[1]),
                pltpu.make_async_copy(wd_hbm.at[e], wd_f32, sem.at[2])]

    @pl.when(i == 0)
    def _():
        for cp in weight_copies(expert):
            cp.start()

    @pl.when(jnp.logical_and(n_valid > 0, changed))
    def _():
        for cp in weight_copies(expert):
            cp.wait()
        wg_bf[...] = wg_f32[...].astype(BF16)
        wu_bf[...] = wu_f32[...].astype(BF16)
        wd_bf[...] = wd_f32[...].astype(BF16)
        upcoming = nx_ref[i]

        @pl.when(upcoming >= 0)
        def _():
            for cp in weight_copies(upcoming):
                cp.start()

    def swiglu_rows(first):
        packed = _load_rows(xs_ref, EXPERT_SUB, first)
        x = jnp.concatenate([_unpack_rows(packed, 0), _unpack_rows(packed, 1)], axis=1)
        rows = first + lax.broadcasted_iota(I32, (EXPERT_SUB, 1), 0)
        x = jnp.where(rows < n_valid, x, 0.0).astype(BF16)
        g = _dot(x, wg_bf[...])
        u = _dot(x, wu_bf[...])
        act = (g * _sigmoid(g) * u).astype(BF16)
        _store_rows(ys_ref, _pack_rows(_dot(act, wd_bf[...])), first)

    for first in range(0, blk, EXPERT_SUB):
        pl.when(n_valid > first)(functools.partial(swiglu_rows, first))


def _experts(block_expert, block_valid, next_expert, last_block, xs, w_g, w_u, w_d, blk):
    n_blocks = block_expert.shape[0]
    d, de = w_g.shape[1], w_g.shape[2]
    x_map = lambda i, be, nv, nx, last: (jnp.minimum(i, last[0]), 0)
    hbm = pl.BlockSpec(memory_space=pl.ANY)
    return pl.pallas_call(
        _expert_kernel,
        out_shape=jax.ShapeDtypeStruct(xs.shape, U32),
        grid_spec=pltpu.PrefetchScalarGridSpec(
            num_scalar_prefetch=4,
            grid=(n_blocks,),
            in_specs=[pl.BlockSpec((blk * ROW_TILE, LANES), x_map), hbm, hbm, hbm],
            out_specs=pl.BlockSpec((blk * ROW_TILE, LANES), x_map),
            scratch_shapes=[pltpu.VMEM((d, de), F32),
                            pltpu.VMEM((d, de), F32),
                            pltpu.VMEM((de, d), F32),
                            pltpu.VMEM((d, de), BF16),
                            pltpu.VMEM((d, de), BF16),
                            pltpu.VMEM((de, d), BF16),
                            pltpu.SemaphoreType.DMA((3,))]),
        compiler_params=pltpu.CompilerParams(dimension_semantics=("arbitrary",),
                                             vmem_limit_bytes=VMEM_LIMIT),
        name="experts",
    )(block_expert, block_valid, next_expert, last_block, xs, w_g, w_u, w_d)


def _combine_kernel(d_ref, dn_ref, gate_ref, xs_ref, mod_ref, ys_ref, o_ref, buf, sem):
    tc = d_ref.shape[1]
    i = pl.program_id(0)
    slot = lax.rem(i, 2)

    def gather(dest_ref, into):
        def issue(t, carry):
            for k in range(TOP_K):
                pltpu.make_async_copy(_row(ys_ref, dest_ref[k, t]), _row(buf.at[into], k * tc + t),
                                      sem.at[into]).start(priority=k % 2)
            return carry
        lax.fori_loop(0, tc, issue, 0)

    @pl.when(i == 0)
    def _():
        gather(d_ref, 0)

    @pl.when(i + 1 < pl.num_programs(0))
    def _():
        gather(dn_ref, 1 - slot)

    pltpu.make_async_copy(ys_ref.at[pl.ds(0, TOP_K * tc * ROW_TILE), :], buf.at[slot], sem.at[slot]).wait()

    gate2 = mod_ref[0][5:6]
    rows = buf.at[slot]
    acc = [jnp.zeros((tc, LANES), F32) for _ in range(2 * ROW_TILE)]
    for k in range(TOP_K):
        gk = jnp.transpose(jnp.broadcast_to(gate_ref[k:k + 1, :], (LANES, tc)))
        for s in range(ROW_TILE):
            w = rows[pl.ds(k * tc * ROW_TILE + s, tc, stride=ROW_TILE), :]
            acc[s] = acc[s] + gk * _unpack_rows(w, 0)
            acc[ROW_TILE + s] = acc[ROW_TILE + s] + gk * _unpack_rows(w, 1)
    o_ref[...] = xs_ref[...] + gate2 * jnp.concatenate(acc, axis=1)


def _combine(dest, gate, xs, mod3, ys, seq_len, tc):
    n_tok, d = xs.shape
    tiles_per_seq = seq_len // tc
    n_tiles = n_tok // tc
    return pl.pallas_call(
        _combine_kernel,
        out_shape=jax.ShapeDtypeStruct((n_tok, d), F32),
        grid=(n_tiles,),
        in_specs=[pl.BlockSpec((TOP_K, tc), lambda i: (0, i), memory_space=pltpu.SMEM),
                  pl.BlockSpec((TOP_K, tc), lambda i: (0, jnp.minimum(i + 1, n_tiles - 1)),
                               memory_space=pltpu.SMEM),
                  pl.BlockSpec((TOP_K, tc), lambda i: (0, i)),
                  pl.BlockSpec((tc, d), lambda i: (i, 0)),
                  pl.BlockSpec((1, 6, d), lambda i: (i // tiles_per_seq, 0, 0)),
                  pl.BlockSpec(memory_space=pl.ANY)],
        out_specs=pl.BlockSpec((tc, d), lambda i: (i, 0)),
        scratch_shapes=[pltpu.VMEM((2, TOP_K * tc * ROW_TILE, LANES), U32),
                        pltpu.SemaphoreType.DMA((2,))],
        compiler_params=pltpu.CompilerParams(dimension_semantics=("arbitrary",),
                                             vmem_limit_bytes=VMEM_LIMIT),
        name="combine",
    )(dest, dest, gate, xs, mod3, ys)


def _tiles(seq_len):
    pick = lambda pref: min(pref, seq_len)
    return dict(inproj=pick(512), attn=pick(512), ssm=pick(1024), mix=pick(512),
                dispatch=pick(512), combine=pick(256), slots=pick(2048), expert_block=2 * EXPERT_SUB)


def _rope_tables(seq_len):
    half = ATT_HEAD_DIM // 2
    inv_freq = 1.0 / (ROPE_THETA ** (jnp.arange(0, ATT_HEAD_DIM, 2, dtype=F32) / ATT_HEAD_DIM))
    ang = jnp.arange(seq_len, dtype=F32)[:, None] * inv_freq[None, :]
    cos, sin = jnp.cos(ang), jnp.sin(ang)
    reps = LANES // half
    sign = jnp.tile(jnp.concatenate([-jnp.ones((half,), F32), jnp.ones((half,), F32)]), reps // 2)
    return jnp.tile(cos, (1, reps)), jnp.tile(sin, (1, reps)) * sign[None, :]


def _block_tables(counts, blk, n_blocks):
    padded = (counts + blk - 1) // blk * blk
    pad_ends = jnp.cumsum(padded)
    pad_starts = pad_ends - padded
    used = pad_ends[-1] // blk
    last = jnp.maximum(used - 1, 0)
    starts = jnp.arange(n_blocks, dtype=I32) * blk
    expert = jnp.sum((pad_ends[None, :] <= starts[:, None]).astype(I32), axis=1)
    expert = jnp.minimum(expert, N_EXPERTS - 1)
    expert = jnp.where(starts < pad_ends[-1], expert, expert[last])
    valid = jnp.clip(counts[expert] - (starts - pad_starts[expert]), 0, blk)
    valid = jnp.where(starts < pad_ends[-1], valid, 0).astype(I32)
    ids = jnp.arange(N_EXPERTS, dtype=I32)
    later_used = jnp.logical_and(ids[None, :] > expert[:, None], (counts > 0)[None, :])
    upcoming = jnp.min(jnp.where(later_used, ids[None, :], N_EXPERTS), axis=1)
    upcoming = jnp.where(upcoming < N_EXPERTS, upcoming, -1).astype(I32)
    return pad_starts.astype(I32), expert, valid, upcoming, last.reshape(1).astype(I32)


def kernel(x, c, norm1_g, norm2_g, w_ada, b_ada, w_in, q_norm_g, k_norm_g, lambda_q1, lambda_k1, lambda_q2, lambda_k2, subln_g, ssm_a_re, ssm_a_im, ssm_log_dt, ssm_b_re, ssm_b_im, ssm_c_re, ssm_c_im, ssm_d, w_glu, ssm_norm_g, w_out, w_router, router_bias, w_gate_e, w_up_e, w_down_e, w_gate_s, w_up_s, w_down_s):
    bsz, seq_len, d = x.shape
    n_tok = bsz * seq_len
    tiles = _tiles(seq_len)
    blk = tiles['expert_block']
    n_blocks = (n_tok * TOP_K + N_EXPERTS * (blk - 1) + blk - 1) // blk
    cos_t, sin_t = _rope_tables(seq_len)
    seg = jnp.kron(jnp.eye(QK_WIDTH // ATT_HEAD_DIM, dtype=F32),
                   jnp.full((ATT_HEAD_DIM, ATT_HEAD_DIM), 1.0 / ATT_HEAD_DIM, F32)).astype(BF16)
    reps = QK_WIDTH // ATT_HEAD_DIM
    x2d = x.reshape(n_tok, d).astype(F32)
    for layer in range(w_ada.shape[0]):
        lam_init = 0.8 - 0.6 * math.exp(-0.3 * layer)
        mod3 = _adaln(c.astype(F32), w_ada[layer].astype(F32), b_ada[layer].astype(F32)).reshape(bsz, 6, d)
        q, k, v_t, u = _inproj(
            x2d, mod3, norm1_g[layer].astype(F32).reshape(1, d), w_in[layer].astype(BF16), seg,
            jnp.tile(q_norm_g[layer].astype(F32), reps).reshape(1, QK_WIDTH),
            jnp.tile(k_norm_g[layer].astype(F32), reps).reshape(1, QK_WIDTH),
            cos_t, sin_t, seq_len, tiles['inproj'])
        lam = (jnp.exp(jnp.sum(lambda_q1[layer].astype(F32) * lambda_k1[layer].astype(F32)))
               - jnp.exp(jnp.sum(lambda_q2[layer].astype(F32) * lambda_k2[layer].astype(F32))) + lam_init)
        att = _attention(q, k, v_t, jnp.full((1, LANES), lam, F32),
                         subln_g[layer].astype(F32).reshape(ATT_V_DIM, 1),
                         bsz, seq_len, tiles['attn'], 1.0 - lam_init)
        kmat, bmat, cmat, bpow = _ssm_mats(ssm_a_re[layer], ssm_a_im[layer], ssm_log_dt[layer],
                                         ssm_b_re[layer], ssm_b_im[layer], ssm_c_re[layer],
                                         ssm_c_im[layer], ssm_d[layer])
        y = _ssm(u, kmat, bmat, cmat, bpow, bsz, seq_len, tiles['ssm'])
        wr_t = jnp.transpose(w_router[layer].astype(F32))
        wrh, wrl = _split_bf16(wr_t)
        xs, h2p, eidx, gate, rank, counts = _mix_route(
            x2d, y, att, mod3, w_glu[layer].astype(BF16),
            ssm_norm_g[layer].astype(F32).reshape(1, SSM_WIDTH),
            w_out[layer, :ATT_WIDTH].astype(BF16), w_out[layer, ATT_WIDTH:].astype(BF16),
            norm2_g[layer].astype(F32).reshape(1, d), wrh, wrl,
            router_bias[layer].astype(F32).reshape(N_EXPERTS, 1),
            jnp.concatenate([w_gate_s[layer], w_up_s[layer]], axis=1).astype(BF16),
            w_down_s[layer].astype(BF16), seq_len, tiles['mix'])
        pad_starts, block_expert, block_valid, next_expert, last_block = _block_tables(
            counts.reshape(N_EXPERTS).astype(I32), blk, n_blocks)
        dest = _slots(eidx, rank, pad_starts, tiles['slots'])
        x_slots = _dispatch(dest, h2p, n_blocks * blk, tiles['dispatch'])
        y_slots = _experts(block_expert, block_valid, next_expert, last_block, x_slots,
                           w_gate_e[layer], w_up_e[layer], w_down_e[layer], blk)
        x2d = _combine(dest, gate, xs, mod3, y_slots, seq_len, tiles['combine'])
    return x2d.reshape(bsz, seq_len, d).astype(x.dtype)
```

```python
import functools
import math

import jax
import jax.numpy as jnp
from jax import lax
from jax.experimental import pallas as pl
from jax.experimental.pallas import tpu as pltpu

F32 = jnp.float32
BF16 = jnp.bfloat16
I32 = jnp.int32
U32 = jnp.uint32

LANES = 128
SUBLANES = 8

N_ATT_HEADS = 4
ATT_HEAD_DIM = 64
ATT_V_DIM = 2 * ATT_HEAD_DIM
QK_WIDTH = N_ATT_HEADS * 2 * ATT_HEAD_DIM
ATT_WIDTH = N_ATT_HEADS * ATT_V_DIM
ROPE_THETA = 10000.0
SSM_GROUP = 16
SSM_GROUPS = 32
SSM_STATE = 64
SSM_WIDTH = SSM_GROUPS * SSM_GROUP
SSM_CHUNK = SUBLANES
SSM_LANE_BLOCKS = SSM_WIDTH // LANES
GROUPS_PER_BLOCK = LANES // SSM_GROUP
N_EXPERTS = 256
TOP_K = 8
N_GROUPS = 8
TOPK_GROUPS = 4
EXPERTS_PER_GROUP = N_EXPERTS // N_GROUPS
EXPERT_DIM = 256
SHARED_DIM = 256
ROUTED_SCALE = 2.5
EPS = 1e-6

NT_DIMS = (((1,), (1,)), ((), ()))

VMEM_LIMIT = 48 * 1024 * 1024


def _dot(a, b):
    return jnp.dot(a, b, preferred_element_type=F32)


def _sigmoid(x):
    return 1.0 / (1.0 + jnp.exp(-x))


def _split_bf16(x):
    hi = x.astype(BF16)
    lo = (x - hi.astype(F32)).astype(BF16)
    return hi, lo


def _adaln_kernel(c_ref, w_ref, b_ref, o_ref):
    c = c_ref[...]
    sc = c * _sigmoid(c)
    o_ref[...] = jnp.dot(sc, w_ref[...], preferred_element_type=F32,
                         precision=lax.Precision.HIGHEST) + b_ref[...]


def _adaln(c, w, b):
    bsz, d = c.shape
    n = w.shape[1]
    tn = 1024
    return pl.pallas_call(
        _adaln_kernel,
        out_shape=jax.ShapeDtypeStruct((bsz, n), F32),
        grid=(n // tn,),
        in_specs=[pl.BlockSpec((bsz, d), lambda j: (0, 0)),
                  pl.BlockSpec((d, tn), lambda j: (0, j)),
                  pl.BlockSpec((1, tn), lambda j: (0, j))],
        out_specs=pl.BlockSpec((bsz, tn), lambda j: (0, j)),
        compiler_params=pltpu.CompilerParams(dimension_semantics=("arbitrary",),
                                             vmem_limit_bytes=VMEM_LIMIT),
        name="adaln",
    )(c, w, b.reshape(1, n))


def _inproj_kernel(x_ref, mod_ref, n1g_ref, w_ref, seg_ref, qg_ref, kg_ref, cos_ref, sin_ref,
                   q_ref, k_ref, v_ref, u_ref):
    x = x_ref[...]
    mod = mod_ref[0]
    shift, scale = mod[0:1], mod[1:2]
    ms = jnp.mean(x * x, axis=-1, keepdims=True)
    h = x * lax.rsqrt(ms + EPS) * n1g_ref[...]
    h = h * (1.0 + scale) + shift
    z = _dot(h.astype(BF16), w_ref[...])
    seg = seg_ref[...]
    cos = cos_ref[...]
    sin = sin_ref[...]
    lane = lax.broadcasted_iota(I32, cos.shape, 1)
    first_half = (lane % ATT_HEAD_DIM) < (ATT_HEAD_DIM // 2)

    def norm_rope(t, g_ref, out_scale):
        hi, lo = _split_bf16(t * t)
        msq = _dot(hi, seg) + _dot(lo, seg)
        tn = t * lax.rsqrt(msq + EPS) * g_ref[...]
        outs = []
        for hd in range(N_ATT_HEADS):
            th = tn[:, hd * LANES:(hd + 1) * LANES]
            partner = jnp.where(first_half,
                                pltpu.roll(th, LANES - ATT_HEAD_DIM // 2, 1),
                                pltpu.roll(th, ATT_HEAD_DIM // 2, 1))
            outs.append((th * cos + partner * sin) * out_scale)
        return jnp.concatenate(outs, axis=1)

    q_ref[...] = norm_rope(z[:, :QK_WIDTH], qg_ref, ATT_HEAD_DIM ** -0.5).astype(BF16)
    k_ref[...] = norm_rope(z[:, QK_WIDTH:2 * QK_WIDTH], kg_ref, 1.0).astype(BF16)
    v_ref[...] = z[:, 2 * QK_WIDTH:2 * QK_WIDTH + ATT_WIDTH].astype(BF16)
    u_ref[...] = z[:, 2 * QK_WIDTH + ATT_WIDTH:]


def _inproj(x2d, mod3, n1g, w_in_bf, seg, qg, kg, cos_t, sin_t, seq_len, tm):
    n_tok, d = x2d.shape
    tiles_per_seq = seq_len // tm
    in_width = w_in_bf.shape[1]
    row = lambda i: (i, 0)
    const = lambda i: (0, 0)
    return pl.pallas_call(
        _inproj_kernel,
        out_shape=(jax.ShapeDtypeStruct((n_tok, QK_WIDTH), BF16),
                   jax.ShapeDtypeStruct((n_tok, QK_WIDTH), BF16),
                   jax.ShapeDtypeStruct((n_tok, ATT_WIDTH), BF16),
                   jax.ShapeDtypeStruct((n_tok, SSM_WIDTH), F32)),
        grid=(n_tok // tm,),
        in_specs=[pl.BlockSpec((tm, d), row),
                  pl.BlockSpec((1, 6, d), lambda i: (i // tiles_per_seq, 0, 0)),
                  pl.BlockSpec((1, d), const),
                  pl.BlockSpec((d, in_width), const),
                  pl.BlockSpec((QK_WIDTH, QK_WIDTH), const),
                  pl.BlockSpec((1, QK_WIDTH), const),
                  pl.BlockSpec((1, QK_WIDTH), const),
                  pl.BlockSpec((tm, LANES), lambda i: (i % tiles_per_seq, 0)),
                  pl.BlockSpec((tm, LANES), lambda i: (i % tiles_per_seq, 0))],
        out_specs=(pl.BlockSpec((tm, QK_WIDTH), row),
                   pl.BlockSpec((tm, QK_WIDTH), row),
                   pl.BlockSpec((tm, ATT_WIDTH), row),
                   pl.BlockSpec((tm, SSM_WIDTH), row)),
        compiler_params=pltpu.CompilerParams(dimension_semantics=("arbitrary",),
                                             vmem_limit_bytes=VMEM_LIMIT),
        name="inproj",
    )(x2d, mod3, n1g, w_in_bf, seg, qg, kg, cos_t, sin_t)


def _attn_kernel(qi_ref, kj_ref, q_ref, k_ref, v_ref, lam_ref, sg_ref, o_ref,
                 m_ref, l_ref, acc_ref, *, out_scale):
    p = pl.program_id(2)
    qi = qi_ref[p]
    kj = kj_ref[p]

    @pl.when(kj == 0)
    def _():
        m_ref[...] = jnp.full(m_ref.shape, -jnp.inf, F32)
        l_ref[...] = jnp.zeros(l_ref.shape, F32)
        acc_ref[...] = jnp.zeros(acc_ref.shape, F32)

    tq = q_ref.shape[0]
    tk = k_ref.shape[0]
    first_diag = qi * (tq // tk)

    def step(masked):
        q = q_ref[...]
        lane = lax.broadcasted_iota(I32, q.shape, 1)
        zero = jnp.zeros_like(q)
        qq = jnp.concatenate([jnp.where(lane < ATT_HEAD_DIM, q, zero),
                              jnp.where(lane >= ATT_HEAD_DIM, q, zero)], axis=0)
        s = lax.dot_general(qq, k_ref[...], NT_DIMS, preferred_element_type=F32)
        if masked:
            r = lax.broadcasted_iota(I32, (tq, tk), 0)
            col = lax.broadcasted_iota(I32, (tq, tk), 1) + (kj - first_diag) * tk
            causal = col <= r
            s = jnp.where(jnp.concatenate([causal, causal], axis=0), s, -jnp.inf)
        m_old = m_ref[...]
        m_new = jnp.maximum(m_old, jnp.max(s, axis=-1, keepdims=True))
        alpha = jnp.exp(m_old - m_new)
        pr = jnp.exp(s - jnp.concatenate([m_new] * (s.shape[1] // LANES), axis=1))
        l_ref[...] = alpha * l_ref[...] + jnp.sum(pr, axis=-1, keepdims=True)
        acc_ref[...] = alpha * acc_ref[...] + _dot(pr.astype(BF16), v_ref[...])
        m_ref[...] = m_new

    @pl.when(kj < first_diag)
    def _():
        step(False)

    @pl.when(kj >= first_diag)
    def _():
        step(True)

    @pl.when(kj == first_diag + tq // tk - 1)
    def _():
        o = acc_ref[...] / l_ref[...]
        o = o[:tq] - lam_ref[...] * o[tq:]
        ms = jnp.mean(o * o, axis=-1, keepdims=True)
        o_ref[...] = (o * lax.rsqrt(ms + EPS) * sg_ref[...] * out_scale).astype(o_ref.dtype)


def _attention(q, k, v, lam_row, subln_g, bsz, seq_len, tq, tk, out_scale):
    nq, nk = seq_len // tq, seq_len // tk
    pairs = [(i, j) for i in range(nq) for j in range((i + 1) * (tq // tk))]
    qi = jnp.asarray([p[0] for p in pairs], I32)
    kj = jnp.asarray([p[1] for p in pairs], I32)
    q_map = lambda b, h, p, qi, kj: (b * nq + qi[p], h)
    k_map = lambda b, h, p, qi, kj: (b * nk + kj[p], h)
    const = lambda b, h, p, qi, kj: (0, 0)
    return pl.pallas_call(
        functools.partial(_attn_kernel, out_scale=out_scale),
        out_shape=jax.ShapeDtypeStruct((bsz * seq_len, ATT_WIDTH), BF16),
        grid_spec=pltpu.PrefetchScalarGridSpec(
            num_scalar_prefetch=2,
            grid=(bsz, N_ATT_HEADS, len(pairs)),
            in_specs=[pl.BlockSpec((tq, LANES), q_map),
                      pl.BlockSpec((tk, LANES), k_map),
                      pl.BlockSpec((tk, LANES), k_map),
                      pl.BlockSpec((1, LANES), const),
                      pl.BlockSpec((1, LANES), const)],
            out_specs=pl.BlockSpec((tq, LANES), q_map),
            scratch_shapes=[pltpu.VMEM((2 * tq, LANES), F32),
                            pltpu.VMEM((2 * tq, LANES), F32),
                            pltpu.VMEM((2 * tq, ATT_V_DIM), F32)]),
        compiler_params=pltpu.CompilerParams(
            dimension_semantics=("arbitrary", "arbitrary", "arbitrary"),
            vmem_limit_bytes=VMEM_LIMIT),
        name="attn",
    )(qi, kj, q, k, v, lam_row, subln_g)


def _ssm_mats(a_re, a_im, log_dt, b_re, b_im, c_re, c_im, d_skip):
    a_re, a_im, b_re, b_im, c_re, c_im, d_skip = (
        t.astype(F32) for t in (a_re, a_im, b_re, b_im, c_re, c_im, d_skip))
    dt = jnp.exp(log_dt.astype(F32))[:, None]
    mag = jnp.exp(a_re * dt)
    abar_re = mag * jnp.cos(a_im * dt)
    abar_im = mag * jnp.sin(a_im * dt)
    den = a_re * a_re + a_im * a_im
    nr = abar_re - 1.0
    f_re = ((nr * a_re + abar_im * a_im) / den)[..., None]
    f_im = ((abar_im * a_re - nr * a_im) / den)[..., None]
    bb_re = f_re * b_re - f_im * b_im
    bb_im = f_re * b_im + f_im * b_re
    steps = jnp.arange(SSM_CHUNK + 1, dtype=F32)[:, None, None]
    pmag = jnp.exp(a_re * dt * steps)
    pw_re = pmag * jnp.cos(a_im * dt * steps)
    pw_im = pmag * jnp.sin(a_im * dt * steps)
    ca_re = c_re[None] * pw_re[:, :, None, :] - c_im[None] * pw_im[:, :, None, :]
    ca_im = c_re[None] * pw_im[:, :, None, :] + c_im[None] * pw_re[:, :, None, :]
    hp = lax.Precision.HIGHEST
    lag = (jnp.einsum('mgcp,gpd->mgcd', ca_re[:SSM_CHUNK], bb_re, precision=hp)
           - jnp.einsum('mgcp,gpd->mgcd', ca_im[:SSM_CHUNK], bb_im, precision=hp))
    lag = lag.at[0].add(d_skip[:, :, None] * jnp.eye(SSM_GROUP, dtype=F32)[None])
    nb, gpb = SSM_LANE_BLOCKS, GROUPS_PER_BLOCK
    n_state = gpb * SSM_STATE
    in_group = jnp.arange(LANES) // SSM_GROUP
    state_group = jnp.arange(n_state) // SSM_STATE

    def spread(small, row_group, col_group):
        w = small.shape[-1]
        tiled = jnp.tile(jnp.eye(w, dtype=BF16), (1, col_group.shape[0] // w))
        wide = jnp.einsum('...w,wn->...n', small.astype(BF16), tiled, preferred_element_type=F32)
        return jnp.where(row_group[:, None] == col_group[None, :], wide, 0.0).astype(BF16)

    lag_blocks = spread(jnp.transpose(lag, (0, 1, 3, 2)).reshape(SSM_CHUNK, nb, LANES, SSM_GROUP),
                        in_group, in_group)
    zero_block = jnp.zeros_like(lag_blocks[0])
    kmat = jnp.concatenate(
        [jnp.concatenate([lag_blocks[t - j] if t >= j else zero_block for t in range(SSM_CHUNK)], axis=-1)
         for j in range(SSM_CHUNK)], axis=1)
    rev = SSM_CHUNK - 1 - jnp.arange(SSM_CHUNK)
    w_re = pw_re[rev][..., None] * bb_re[None] - pw_im[rev][..., None] * bb_im[None]
    w_im = pw_re[rev][..., None] * bb_im[None] + pw_im[rev][..., None] * bb_re[None]

    def in_to_state(w):
        small = jnp.transpose(w, (0, 1, 3, 2)).reshape(SSM_CHUNK, nb, LANES, SSM_STATE)
        return spread(small, in_group, state_group)

    bmat = jnp.concatenate([in_to_state(w_re), in_to_state(w_im)], axis=-1)
    bmat = jnp.transpose(bmat, (1, 0, 2, 3)).reshape(nb, SSM_CHUNK * LANES, 2 * n_state)

    def state_to_out(ca):
        small = jnp.transpose(ca[1:], (0, 1, 3, 2)).reshape(SSM_CHUNK, nb, n_state, SSM_GROUP)
        blocks = spread(small, state_group, in_group)
        return jnp.transpose(blocks, (1, 2, 0, 3)).reshape(nb, n_state, SSM_CHUNK * LANES)

    cmat = jnp.concatenate([state_to_out(ca_re), -state_to_out(ca_im)], axis=1)
    chunk_steps = SSM_CHUNK * jnp.arange(2 * SUBLANES, dtype=F32)[:, None, None]
    cmag = jnp.exp(a_re * dt * chunk_steps)
    to_block = lambda t: jnp.transpose(t.reshape(2 * SUBLANES, nb, n_state), (1, 0, 2))
    bpow = jnp.concatenate([to_block(cmag * jnp.cos(a_im * dt * chunk_steps)),
                            to_block(cmag * jnp.sin(a_im * dt * chunk_steps))], axis=-1)
    return kmat, bmat, cmat, bpow


def _cmul(a_re, a_im, x_re, x_im):
    return a_re * x_re - a_im * x_im, a_re * x_im + a_im * x_re


def _ssm_kernel(u_ref, km_ref, bm_ref, cm_ref, bp_ref, y_ref, carry_ref, sp_ref):
    n_chunks = sp_ref.shape[0]
    half = sp_ref.shape[1] // 2

    @pl.when(pl.program_id(2) == 0)
    def _():
        carry_ref[...] = jnp.zeros(carry_ref.shape, F32)

    ucat = jnp.concatenate([u_ref[pl.ds(j, n_chunks, stride=SSM_CHUNK), :] for j in range(SSM_CHUNK)],
                           axis=1).astype(BF16)
    s_end = _dot(ucat, bm_ref[0])
    p_re, p_im = s_end[:, :half], s_end[:, half:]
    bp = bp_ref[0]
    sub = lax.broadcasted_iota(I32, (n_chunks, half), 0) % SUBLANES

    def shifted(v, d):
        return jnp.where(sub >= d, pltpu.roll(v, d, 0), 0.0)

    for d in (1, 2, 4):
        d_re, d_im = _cmul(bp[d:d + 1, :half], bp[d:d + 1, half:], shifted(p_re, d), shifted(p_im, d))
        p_re, p_im = p_re + d_re, p_im + d_im
    x_re, x_im = shifted(p_re, 1), shifted(p_im, 1)
    t_re, t_im = bp[:SUBLANES, :half], bp[:SUBLANES, half:]
    l_re, l_im = bp[SUBLANES:SUBLANES + 1, :half], bp[SUBLANES:SUBLANES + 1, half:]
    c_re, c_im = carry_ref[:, :half], carry_ref[:, half:]
    for g in range(n_chunks // SUBLANES):
        lo, hi = g * SUBLANES, (g + 1) * SUBLANES
        d_re, d_im = _cmul(t_re, t_im, c_re, c_im)
        sp_ref[lo:hi, :half] = x_re[lo:hi] + d_re
        sp_ref[lo:hi, half:] = x_im[lo:hi] + d_im
        e_re, e_im = _cmul(l_re, l_im, c_re, c_im)
        c_re, c_im = p_re[hi - 1:hi] + e_re, p_im[hi - 1:hi] + e_im
    carry_ref[...] = jnp.concatenate([c_re, c_im], axis=1)
    y = _dot(ucat, km_ref[0]) + _dot(sp_ref[...].astype(BF16), cm_ref[0])
    for t in range(SSM_CHUNK):
        y_ref[pl.ds(t, n_chunks, stride=SSM_CHUNK), :] = y[:, t * LANES:(t + 1) * LANES]


def _ssm(u, kmat, bmat, cmat, bpow, bsz, seq_len, tt):
    nt = seq_len // tt
    n_chunks = tt // SSM_CHUNK
    width = SSM_CHUNK * LANES
    u_map = lambda g, b, i: (b * nt + i, g)
    w_map = lambda g, b, i: (g, 0, 0)
    return pl.pallas_call(
        _ssm_kernel,
        out_shape=jax.ShapeDtypeStruct((bsz * seq_len, SSM_WIDTH), F32),
        grid=(SSM_LANE_BLOCKS, bsz, nt),
        in_specs=[pl.BlockSpec((tt, LANES), u_map),
                  pl.BlockSpec((1, width, width), w_map),
                  pl.BlockSpec((1, width, width), w_map),
                  pl.BlockSpec((1, width, width), w_map),
                  pl.BlockSpec((1, 2 * SUBLANES, width), w_map)],
        out_specs=pl.BlockSpec((tt, LANES), u_map),
        scratch_shapes=[pltpu.VMEM((1, width), F32),
                        pltpu.VMEM((n_chunks, width), F32)],
        compiler_params=pltpu.CompilerParams(
            dimension_semantics=("arbitrary", "arbitrary", "arbitrary"),
            vmem_limit_bytes=VMEM_LIMIT),
        name="ssm",
    )(u, kmat, bmat, cmat, bpow)


def _first_max(v, ids, n):
    m = jnp.max(v, axis=0, keepdims=True)
    ix = jnp.min(jnp.where(v == m, ids, n), axis=0, keepdims=True)
    return m, ix


def _mix_route_kernel(x_ref, y_ref, att_ref, mod_ref, wglu_ref, sng_ref, woa_ref, wos_ref, n2g_ref,
                      wrh_ref, wrl_ref, rb_ref, wgu_ref, wds_ref,
                      xs_ref, h2p_ref, eidx_ref, gate_ref, rank_ref, cnt_ref, carry_ref):
    tm = x_ref.shape[0]

    @pl.when(pl.program_id(0) == 0)
    def _():
        carry_ref[...] = jnp.zeros(carry_ref.shape, F32)

    mod = mod_ref[0]
    gate1, shift2, scale2, gate2 = mod[2:3], mod[3:4], mod[4:5], mod[5:6]

    y = y_ref[...]
    g = 0.5 * y * (1.0 + jnp.tanh(math.sqrt(2.0 / math.pi) * (y + 0.044715 * (y * y * y))))
    glu = g * _sigmoid(_dot(g.astype(BF16), wglu_ref[...]))
    ssm = glu * lax.rsqrt(jnp.mean(glu * glu, axis=-1, keepdims=True) + EPS) * sng_ref[...]

    mix = _dot(att_ref[...], woa_ref[...]) + _dot(ssm.astype(BF16), wos_ref[...])
    x1 = x_ref[...] + gate1 * mix
    h2 = x1 * lax.rsqrt(jnp.mean(x1 * x1, axis=-1, keepdims=True) + EPS) * n2g_ref[...]
    h2 = h2 * (1.0 + scale2) + shift2
    _store_rows(h2p_ref, _pack_rows(h2))
    hb, h_lo = _split_bf16(h2)

    gu = _dot(hb, wgu_ref[...])
    gs, us = gu[:, :SHARED_DIM], gu[:, SHARED_DIM:]
    act = (gs * _sigmoid(gs) * us).astype(BF16)
    xs_ref[...] = x1 + gate2 * _dot(act, wds_ref[...])

    wrh = wrh_ref[...]
    logits = (lax.dot_general(wrh, hb, NT_DIMS, preferred_element_type=F32)
              + lax.dot_general(wrh, h_lo, NT_DIMS, preferred_element_type=F32)
              + lax.dot_general(wrl_ref[...], hb, NT_DIMS, preferred_element_type=F32))
    score = _sigmoid(logits)
    biased = score + rb_ref[...]
    neg = -jnp.inf

    ids_g = lax.broadcasted_iota(I32, (EXPERTS_PER_GROUP, tm), 0)
    group_rows = []
    for gi in range(N_GROUPS):
        blk = biased[gi * EXPERTS_PER_GROUP:(gi + 1) * EXPERTS_PER_GROUP, :]
        m1, i1 = _first_max(blk, ids_g, EXPERTS_PER_GROUP)
        m2 = jnp.max(jnp.where(ids_g == i1, neg, blk), axis=0, keepdims=True)
        group_rows.append(m1 + m2)
    cur = jnp.concatenate(group_rows, axis=0)
    ids_8 = lax.broadcasted_iota(I32, cur.shape, 0)
    picked = jnp.zeros(cur.shape, F32)
    for _ in range(TOPK_GROUPS):
        _, ix = _first_max(cur, ids_8, N_GROUPS)
        hit = ids_8 == ix
        picked = jnp.where(hit, 1.0, picked)
        cur = jnp.where(hit, neg, cur)
    e_mask = jnp.concatenate(
        [jnp.broadcast_to(picked[gi:gi + 1, :], (EXPERTS_PER_GROUP, tm)) for gi in range(N_GROUPS)], axis=0)
    cand = jnp.where(e_mask > 0.0, biased, neg)

    ids_e = lax.broadcasted_iota(I32, cand.shape, 0)
    sel = jnp.zeros(cand.shape, F32)
    idx_rows, w_rows = [], []
    for _ in range(TOP_K):
        _, ix = _first_max(cand, ids_e, N_EXPERTS)
        hit = ids_e == ix
        idx_rows.append(ix)
        w_rows.append(jnp.sum(jnp.where(hit, score, 0.0), axis=0, keepdims=True))
        sel = jnp.where(hit, 1.0, sel)
        cand = jnp.where(hit, neg, cand)
    w_sum = w_rows[0]
    for w in w_rows[1:]:
        w_sum = w_sum + w
    gate_ref[...] = jnp.concatenate([w / w_sum * ROUTED_SCALE for w in w_rows], axis=0)
    eidx_ref[...] = jnp.concatenate(idx_rows, axis=0)

    t_row = lax.broadcasted_iota(I32, (tm, tm), 0)
    t_col = lax.broadcasted_iota(I32, (tm, tm), 1)
    earlier = jnp.where(t_row < t_col, 1.0, 0.0).astype(BF16)
    before = _dot(sel.astype(BF16), earlier) + carry_ref[...]
    rank_ref[...] = jnp.concatenate(
        [jnp.sum(jnp.where(ids_e == ix, before, 0.0), axis=0, keepdims=True) for ix in idx_rows],
        axis=0).astype(I32)
    carry_ref[...] = carry_ref[...] + jnp.sum(sel, axis=1, keepdims=True)
    cnt_ref[...] = carry_ref[...]


def _mix_route(x2d, y, att, mod3, wglu, sng, woa, wos, n2g, wrh, wrl, rb, wgu, wds, seq_len, tm):
    n_tok, d = x2d.shape
    tiles_per_seq = seq_len // tm
    row = lambda i: (i, 0)
    col = lambda i: (0, i)
    const = lambda i: (0, 0)
    full = lambda a: pl.BlockSpec(a.shape, const)
    return pl.pallas_call(
        _mix_route_kernel,
        out_shape=(jax.ShapeDtypeStruct((n_tok, d), F32),
                   jax.ShapeDtypeStruct((n_tok * ROW_TILE, LANES), U32),
                   jax.ShapeDtypeStruct((TOP_K, n_tok), I32),
                   jax.ShapeDtypeStruct((TOP_K, n_tok), F32),
                   jax.ShapeDtypeStruct((TOP_K, n_tok), I32),
                   jax.ShapeDtypeStruct((N_EXPERTS, 1), F32)),
        grid=(n_tok // tm,),
        in_specs=[pl.BlockSpec((tm, d), row),
                  pl.BlockSpec((tm, SSM_WIDTH), row),
                  pl.BlockSpec((tm, ATT_WIDTH), row),
                  pl.BlockSpec((1, 6, d), lambda i: (i // tiles_per_seq, 0, 0)),
                  full(wglu), full(sng), full(woa), full(wos), full(n2g),
                  full(wrh), full(wrl), full(rb), full(wgu), full(wds)],
        out_specs=(pl.BlockSpec((tm, d), row),
                   pl.BlockSpec((tm * ROW_TILE, LANES), row),
                   pl.BlockSpec((TOP_K, tm), col),
                   pl.BlockSpec((TOP_K, tm), col),
                   pl.BlockSpec((TOP_K, tm), col),
                   pl.BlockSpec((N_EXPERTS, 1), const)),
        scratch_shapes=[pltpu.VMEM((N_EXPERTS, 1), F32)],
        compiler_params=pltpu.CompilerParams(dimension_semantics=("arbitrary",),
                                             vmem_limit_bytes=VMEM_LIMIT),
        name="mix_route",
    )(x2d, y, att, mod3, wglu, sng, woa, wos, n2g, wrh, wrl, rb, wgu, wds)


ROW_TILE = 4
EXPERT_SUB = 256
COMBINE_PITCH = TOP_K + 1


def _pack_rows(v):
    half = v.shape[1] // 2
    return pltpu.pack_elementwise([v[:, :half], v[:, half:]], packed_dtype=BF16)


def _unpack_rows(w, index):
    return pltpu.unpack_elementwise(w, index=index, packed_dtype=BF16, unpacked_dtype=F32)


def _store_rows(ref, packed, first=0):
    m = packed.shape[0]
    for s in range(ROW_TILE):
        ref[pl.ds(first * ROW_TILE + s, m, stride=ROW_TILE), :] = packed[:, s * LANES:(s + 1) * LANES]


def _load_rows(ref, m, first=0):
    return jnp.concatenate([ref[pl.ds(first * ROW_TILE + s, m, stride=ROW_TILE), :] for s in range(ROW_TILE)],
                           axis=1)


def _row(ref, r):
    return ref.at[pl.ds(pl.multiple_of(r * ROW_TILE, ROW_TILE), ROW_TILE), :]


def _slot_kernel(e_ref, r_ref, ps_ref, d_ref):
    e = e_ref[...]
    ids = lax.broadcasted_iota(I32, (N_EXPERTS, e.shape[1]), 0)
    starts = ps_ref[...]
    rows = [jnp.sum(jnp.where(ids == e[k:k + 1, :], starts, 0.0), axis=0, keepdims=True)
            for k in range(TOP_K)]
    d_ref[...] = jnp.concatenate(rows, axis=0).astype(I32) + r_ref[...]


def _slots(eidx, rank, pad_starts, ts):
    n_tok = eidx.shape[1]
    col = lambda i: (0, i)
    return pl.pallas_call(
        _slot_kernel,
        out_shape=jax.ShapeDtypeStruct((TOP_K, n_tok), I32),
        grid=(n_tok // ts,),
        in_specs=[pl.BlockSpec((TOP_K, ts), col),
                  pl.BlockSpec((TOP_K, ts), col),
                  pl.BlockSpec((N_EXPERTS, 1), lambda i: (0, 0))],
        out_specs=pl.BlockSpec((TOP_K, ts), col),
        compiler_params=pltpu.CompilerParams(dimension_semantics=("arbitrary",),
                                             vmem_limit_bytes=VMEM_LIMIT),
        name="slots",
    )(eidx, rank, pad_starts.astype(F32).reshape(N_EXPERTS, 1))


def _dispatch_kernel(d_ref, h2p_ref, xs_ref, sem):
    td = d_ref.shape[1]

    def issue(t, carry):
        for k in range(TOP_K):
            pltpu.make_async_copy(_row(h2p_ref, t), _row(xs_ref, d_ref[k, t]), sem).start(priority=k % 2)
        return carry

    lax.fori_loop(0, td, issue, 0)
    for _ in range(TOP_K):
        pltpu.make_async_copy(h2p_ref, xs_ref.at[pl.ds(0, td * ROW_TILE), :], sem).wait()


def _dispatch(dest, h2p, n_slots, td):
    n_tok = dest.shape[1]
    return pl.pallas_call(
        _dispatch_kernel,
        out_shape=jax.ShapeDtypeStruct((n_slots * ROW_TILE, LANES), U32),
        grid=(n_tok // td,),
        in_specs=[pl.BlockSpec((TOP_K, td), lambda i: (0, i), memory_space=pltpu.SMEM),
                  pl.BlockSpec((td * ROW_TILE, LANES), lambda i: (i, 0))],
        out_specs=pl.BlockSpec(memory_space=pl.ANY),
        scratch_shapes=[pltpu.SemaphoreType.DMA],
        compiler_params=pltpu.CompilerParams(dimension_semantics=("arbitrary",),
                                             vmem_limit_bytes=VMEM_LIMIT),
        name="dispatch",
    )(dest, h2p)


def _expert_kernel(be_ref, nv_ref, nx_ref, last_ref, xs_ref, wg_hbm, wu_hbm, wd_hbm, ys_ref,
                   wg_f32, wu_f32, wd_f32, wg_bf, wu_bf, wd_bf, sem):
    i = pl.program_id(0)
    blk = xs_ref.shape[0] // ROW_TILE
    expert = be_ref[i]
    n_valid = nv_ref[i]
    changed = jnp.logical_or(i == 0, expert != be_ref[jnp.maximum(i - 1, 0)])

    def weight_copies(e):
        return [pltpu.make_async_copy(wg_hbm.at[e], wg_f32, sem.at[0]),
                pltpu.make_async_copy(wu_hbm.at[e], wu_f32, sem.at[1]),
                pltpu.make_async_copy(wd_hbm.at[e], wd_f32, sem.at[2])]

    @pl.when(i == 0)
    def _():
        for cp in weight_copies(expert):
            cp.start()

    @pl.when(jnp.logical_and(n_valid > 0, changed))
    def _():
        for cp in weight_copies(expert):
            cp.wait()
        wg_bf[...] = wg_f32[...].astype(BF16)
        wu_bf[...] = wu_f32[...].astype(BF16)
        wd_bf[...] = wd_f32[...].astype(BF16)
        upcoming = nx_ref[i]

        @pl.when(upcoming >= 0)
        def _():
            for cp in weight_copies(upcoming):
                cp.start()

    def swiglu_rows(first):
        packed = _load_rows(xs_ref, EXPERT_SUB, first)
        x = jnp.concatenate([_unpack_rows(packed, 0), _unpack_rows(packed, 1)], axis=1)
        rows = first + lax.broadcasted_iota(I32, (EXPERT_SUB, 1), 0)
        x = jnp.where(rows < n_valid, x, 0.0).astype(BF16)
        g = _dot(x, wg_bf[...])
        u = _dot(x, wu_bf[...])
        act = (g * _sigmoid(g) * u).astype(BF16)
        _store_rows(ys_ref, _pack_rows(_dot(act, wd_bf[...])), first)

    for first in range(0, blk, EXPERT_SUB):
        pl.when(n_valid > first)(functools.partial(swiglu_rows, first))


def _experts(block_expert, block_valid, next_expert, last_block, xs, w_g, w_u, w_d, blk):
    n_blocks = block_expert.shape[0]
    d, de = w_g.shape[1], w_g.shape[2]
    x_map = lambda i, be, nv, nx, last: (jnp.minimum(i, last[0]), 0)
    hbm = pl.BlockSpec(memory_space=pl.ANY)
    return pl.pallas_call(
        _expert_kernel,
        out_shape=jax.ShapeDtypeStruct(xs.shape, U32),
        grid_spec=pltpu.PrefetchScalarGridSpec(
            num_scalar_prefetch=4,
            grid=(n_blocks,),
            in_specs=[pl.BlockSpec((blk * ROW_TILE, LANES), x_map), hbm, hbm, hbm],
            out_specs=pl.BlockSpec((blk * ROW_TILE, LANES), x_map),
            scratch_shapes=[pltpu.VMEM((d, de), F32),
                            pltpu.VMEM((d, de), F32),
                            pltpu.VMEM((de, d), F32),
                            pltpu.VMEM((d, de), BF16),
                            pltpu.VMEM((d, de), BF16),
                            pltpu.VMEM((de, d), BF16),
                            pltpu.SemaphoreType.DMA((3,))]),
        compiler_params=pltpu.CompilerParams(dimension_semantics=("arbitrary",),
                                             vmem_limit_bytes=VMEM_LIMIT),
        name="experts",
    )(block_expert, block_valid, next_expert, last_block, xs, w_g, w_u, w_d)


def _combine_kernel(d_ref, dn_ref, gate_ref, xs_ref, mod_ref, ys_ref, o_ref, buf, sem):
    tc = d_ref.shape[1]
    i = pl.program_id(0)
    slot = lax.rem(i, 2)

    def gather(dest_ref, into):
        def issue(t, carry):
            for k in range(TOP_K):
                pltpu.make_async_copy(_row(ys_ref, dest_ref[k, t]), _row(buf.at[into], t * COMBINE_PITCH + k),
                                      sem.at[into]).start(priority=k % 2)
            return carry
        lax.fori_loop(0, tc, issue, 0)

    @pl.when(i == 0)
    def _():
        gather(d_ref, 0)

    @pl.when(i + 1 < pl.num_programs(0))
    def _():
        gather(dn_ref, 1 - slot)

    gathered = TOP_K * tc * ROW_TILE
    pltpu.make_async_copy(ys_ref.at[pl.ds(0, gathered), :], buf.at[slot, pl.ds(0, gathered), :],
                          sem.at[slot]).wait()

    gate2 = mod_ref[0][5:6]
    rows = buf.at[slot]
    acc = [jnp.zeros((tc, LANES), F32) for _ in range(2 * ROW_TILE)]
    for k in range(TOP_K):
        gk = jnp.transpose(jnp.broadcast_to(gate_ref[k:k + 1, :], (LANES, tc)))
        for s in range(ROW_TILE):
            w = rows[pl.ds(k * ROW_TILE + s, tc, stride=COMBINE_PITCH * ROW_TILE), :]
            acc[s] = acc[s] + gk * _unpack_rows(w, 0)
            acc[ROW_TILE + s] = acc[ROW_TILE + s] + gk * _unpack_rows(w, 1)
    o_ref[...] = xs_ref[...] + gate2 * jnp.concatenate(acc, axis=1)


def _combine(dest, gate, xs, mod3, ys, seq_len, tc):
    n_tok, d = xs.shape
    tiles_per_seq = seq_len // tc
    n_tiles = n_tok // tc
    return pl.pallas_call(
        _combine_kernel,
        out_shape=jax.ShapeDtypeStruct((n_tok, d), F32),
        grid=(n_tiles,),
        in_specs=[pl.BlockSpec((TOP_K, tc), lambda i: (0, i), memory_space=pltpu.SMEM),
                  pl.BlockSpec((TOP_K, tc), lambda i: (0, jnp.minimum(i + 1, n_tiles - 1)),
                               memory_space=pltpu.SMEM),
                  pl.BlockSpec((TOP_K, tc), lambda i: (0, i)),
                  pl.BlockSpec((tc, d), lambda i: (i, 0)),
                  pl.BlockSpec((1, 6, d), lambda i: (i // tiles_per_seq, 0, 0)),
                  pl.BlockSpec(memory_space=pl.ANY)],
        out_specs=pl.BlockSpec((tc, d), lambda i: (i, 0)),
        scratch_shapes=[pltpu.VMEM((2, COMBINE_PITCH * tc * ROW_TILE, LANES), U32),
                        pltpu.SemaphoreType.DMA((2,))],
        compiler_params=pltpu.CompilerParams(dimension_semantics=("arbitrary",),
                                             vmem_limit_bytes=VMEM_LIMIT),
        name="combine",
    )(dest, dest, gate, xs, mod3, ys)


def _tiles(seq_len):
    pick = lambda pref: min(pref, seq_len)
    return dict(inproj=pick(512), attn_q=pick(512), attn_k=pick(512), ssm=pick(1024), mix=pick(512),
                dispatch=pick(512), combine=pick(256), slots=pick(2048), expert_block=2 * EXPERT_SUB)


def _rope_tables(seq_len):
    half = ATT_HEAD_DIM // 2
    inv_freq = 1.0 / (ROPE_THETA ** (jnp.arange(0, ATT_HEAD_DIM, 2, dtype=F32) / ATT_HEAD_DIM))
    ang = jnp.arange(seq_len, dtype=F32)[:, None] * inv_freq[None, :]
    cos, sin = jnp.cos(ang), jnp.sin(ang)
    reps = LANES // half
    sign = jnp.tile(jnp.concatenate([-jnp.ones((half,), F32), jnp.ones((half,), F32)]), reps // 2)
    return jnp.tile(cos, (1, reps)), jnp.tile(sin, (1, reps)) * sign[None, :]


def _block_tables(counts, blk, n_blocks):
    padded = (counts + blk - 1) // blk * blk
    pad_ends = jnp.cumsum(padded)
    pad_starts = pad_ends - padded
    used = pad_ends[-1] // blk
    last = jnp.maximum(used - 1, 0)
    starts = jnp.arange(n_blocks, dtype=I32) * blk
    expert = jnp.sum((pad_ends[None, :] <= starts[:, None]).astype(I32), axis=1)
    expert = jnp.minimum(expert, N_EXPERTS - 1)
    expert = jnp.where(starts < pad_ends[-1], expert, expert[last])
    valid = jnp.clip(counts[expert] - (starts - pad_starts[expert]), 0, blk)
    valid = jnp.where(starts < pad_ends[-1], valid, 0).astype(I32)
    ids = jnp.arange(N_EXPERTS, dtype=I32)
    later_used = jnp.logical_and(ids[None, :] > expert[:, None], (counts > 0)[None, :])
    upcoming = jnp.min(jnp.where(later_used, ids[None, :], N_EXPERTS), axis=1)
    upcoming = jnp.where(upcoming < N_EXPERTS, upcoming, -1).astype(I32)
    return pad_starts.astype(I32), expert, valid, upcoming, last.reshape(1).astype(I32)


def kernel(x, c, norm1_g, norm2_g, w_ada, b_ada, w_in, q_norm_g, k_norm_g, lambda_q1, lambda_k1, lambda_q2, lambda_k2, subln_g, ssm_a_re, ssm_a_im, ssm_log_dt, ssm_b_re, ssm_b_im, ssm_c_re, ssm_c_im, ssm_d, w_glu, ssm_norm_g, w_out, w_router, router_bias, w_gate_e, w_up_e, w_down_e, w_gate_s, w_up_s, w_down_s):
    bsz, seq_len, d = x.shape
    n_tok = bsz * seq_len
    tiles = _tiles(seq_len)
    blk = tiles['expert_block']
    n_blocks = (n_tok * TOP_K + N_EXPERTS * (blk - 1) + blk - 1) // blk
    cos_t, sin_t = _rope_tables(seq_len)
    seg = jnp.kron(jnp.eye(QK_WIDTH // ATT_HEAD_DIM, dtype=F32),
                   jnp.full((ATT_HEAD_DIM, ATT_HEAD_DIM), 1.0 / ATT_HEAD_DIM, F32)).astype(BF16)
    reps = QK_WIDTH // ATT_HEAD_DIM
    x2d = x.reshape(n_tok, d).astype(F32)
    for layer in range(w_ada.shape[0]):
        lam_init = 0.8 - 0.6 * math.exp(-0.3 * layer)
        mod3 = _adaln(c.astype(F32), w_ada[layer].astype(F32), b_ada[layer].astype(F32)).reshape(bsz, 6, d)
        q, k, v, u = _inproj(
            x2d, mod3, norm1_g[layer].astype(F32).reshape(1, d), w_in[layer].astype(BF16), seg,
            jnp.tile(q_norm_g[layer].astype(F32), reps).reshape(1, QK_WIDTH),
            jnp.tile(k_norm_g[layer].astype(F32), reps).reshape(1, QK_WIDTH),
            cos_t, sin_t, seq_len, tiles['inproj'])
        lam = (jnp.exp(jnp.sum(lambda_q1[layer].astype(F32) * lambda_k1[layer].astype(F32)))
               - jnp.exp(jnp.sum(lambda_q2[layer].astype(F32) * lambda_k2[layer].astype(F32))) + lam_init)
        att = _attention(q, k, v, jnp.full((1, LANES), lam, F32),
                         subln_g[layer].astype(F32).reshape(1, ATT_V_DIM),
                         bsz, seq_len, tiles['attn_q'], tiles['attn_k'], 1.0 - lam_init)
        kmat, bmat, cmat, bpow = _ssm_mats(ssm_a_re[layer], ssm_a_im[layer], ssm_log_dt[layer],
                                         ssm_b_re[layer], ssm_b_im[layer], ssm_c_re[layer],
                                         ssm_c_im[layer], ssm_d[layer])
        y = _ssm(u, kmat, bmat, cmat, bpow, bsz, seq_len, tiles['ssm'])
        wr_t = jnp.transpose(w_router[layer].astype(F32))
        wrh, wrl = _split_bf16(wr_t)
        xs, h2p, eidx, gate, rank, counts = _mix_route(
            x2d, y, att, mod3, w_glu[layer].astype(BF16),
            ssm_norm_g[layer].astype(F32).reshape(1, SSM_WIDTH),
            w_out[layer, :ATT_WIDTH].astype(BF16), w_out[layer, ATT_WIDTH:].astype(BF16),
            norm2_g[layer].astype(F32).reshape(1, d), wrh, wrl,
            router_bias[layer].astype(F32).reshape(N_EXPERTS, 1),
            jnp.concatenate([w_gate_s[layer], w_up_s[layer]], axis=1).astype(BF16),
            w_down_s[layer].astype(BF16), seq_len, tiles['mix'])
        pad_starts, block_expert, block_valid, next_expert, last_block = _block_tables(
            counts.reshape(N_EXPERTS).astype(I32), blk, n_blocks)
        dest = _slots(eidx, rank, pad_starts, tiles['slots'])
        x_slots = _dispatch(dest, h2p, n_blocks * blk, tiles['dispatch'])
        y_slots = _experts(block_expert, block_valid, next_expert, last_block, x_slots,
                           w_gate_e[layer], w_up_e[layer], w_down_e[layer], blk)
        x2d = _combine(dest, gate, xs, mod3, y_slots, seq_len, tiles['combine'])
    return x2d.reshape(bsz, seq_len, d).astype(x.dtype)
```

```python
import functools
import math

import jax
import jax.numpy as jnp
from jax import lax
from jax.experimental import pallas as pl
from jax.experimental.pallas import tpu as pltpu

F32 = jnp.float32
BF16 = jnp.bfloat16
I32 = jnp.int32
U32 = jnp.uint32

LANES = 128
SUBLANES = 8

N_ATT_HEADS = 4
ATT_HEAD_DIM = 64
ATT_V_DIM = 2 * ATT_HEAD_DIM
QK_WIDTH = N_ATT_HEADS * 2 * ATT_HEAD_DIM
ATT_WIDTH = N_ATT_HEADS * ATT_V_DIM
ROPE_THETA = 10000.0
SSM_GROUP = 16
SSM_GROUPS = 32
SSM_STATE = 64
SSM_WIDTH = SSM_GROUPS * SSM_GROUP
SSM_CHUNK = SUBLANES
SSM_LANE_BLOCKS = SSM_WIDTH // LANES
GROUPS_PER_BLOCK = LANES // SSM_GROUP
N_EXPERTS = 256
TOP_K = 8
N_GROUPS = 8
TOPK_GROUPS = 4
EXPERTS_PER_GROUP = N_EXPERTS // N_GROUPS
EXPERT_DIM = 256
SHARED_DIM = 256
ROUTED_SCALE = 2.5
EPS = 1e-6

NT_DIMS = (((1,), (1,)), ((), ()))

VMEM_LIMIT = 48 * 1024 * 1024


def _dot(a, b):
    return jnp.dot(a, b, preferred_element_type=F32)


def _sigmoid(x):
    return 1.0 / (1.0 + jnp.exp(-x))


def _split_bf16(x):
    hi = x.astype(BF16)
    lo = (x - hi.astype(F32)).astype(BF16)
    return hi, lo


def _adaln_kernel(c_ref, w_ref, b_ref, o_ref):
    c = c_ref[...]
    sc = c * _sigmoid(c)
    o_ref[...] = jnp.dot(sc, w_ref[...], preferred_element_type=F32,
                         precision=lax.Precision.HIGHEST) + b_ref[...]


def _adaln(c, w, b):
    bsz, d = c.shape
    n = w.shape[1]
    tn = 1024
    return pl.pallas_call(
        _adaln_kernel,
        out_shape=jax.ShapeDtypeStruct((bsz, n), F32),
        grid=(n // tn,),
        in_specs=[pl.BlockSpec((bsz, d), lambda j: (0, 0)),
                  pl.BlockSpec((d, tn), lambda j: (0, j)),
                  pl.BlockSpec((1, tn), lambda j: (0, j))],
        out_specs=pl.BlockSpec((bsz, tn), lambda j: (0, j)),
        compiler_params=pltpu.CompilerParams(dimension_semantics=("arbitrary",),
                                             vmem_limit_bytes=VMEM_LIMIT),
        name="adaln",
    )(c, w, b.reshape(1, n))


def _inproj_kernel(x_ref, mod_ref, n1g_ref, w_ref, seg_ref, qg_ref, kg_ref, cos_ref, sin_ref,
                   q_ref, k_ref, v_ref, u_ref):
    x = x_ref[...]
    mod = mod_ref[0]
    shift, scale = mod[0:1], mod[1:2]
    ms = jnp.mean(x * x, axis=-1, keepdims=True)
    h = x * lax.rsqrt(ms + EPS) * n1g_ref[...]
    h = h * (1.0 + scale) + shift
    z = _dot(h.astype(BF16), w_ref[...])
    seg = seg_ref[...]
    cos = cos_ref[...]
    sin = sin_ref[...]
    lane = lax.broadcasted_iota(I32, cos.shape, 1)
    first_half = (lane % ATT_HEAD_DIM) < (ATT_HEAD_DIM // 2)

    def norm_rope(t, g_ref, out_scale):
        hi, lo = _split_bf16(t * t)
        msq = _dot(hi, seg) + _dot(lo, seg)
        tn = t * lax.rsqrt(msq + EPS) * g_ref[...]
        outs = []
        for hd in range(N_ATT_HEADS):
            th = tn[:, hd * LANES:(hd + 1) * LANES]
            partner = jnp.where(first_half,
                                pltpu.roll(th, LANES - ATT_HEAD_DIM // 2, 1),
                                pltpu.roll(th, ATT_HEAD_DIM // 2, 1))
            outs.append((th * cos + partner * sin) * out_scale)
        return jnp.concatenate(outs, axis=1)

    q_ref[...] = norm_rope(z[:, :QK_WIDTH], qg_ref, ATT_HEAD_DIM ** -0.5).astype(BF16)
    k_ref[...] = norm_rope(z[:, QK_WIDTH:2 * QK_WIDTH], kg_ref, 1.0).astype(BF16)
    v_ref[...] = z[:, 2 * QK_WIDTH:2 * QK_WIDTH + ATT_WIDTH].astype(BF16)
    u_ref[...] = z[:, 2 * QK_WIDTH + ATT_WIDTH:]


def _inproj(x2d, mod3, n1g, w_in_bf, seg, qg, kg, cos_t, sin_t, seq_len, tm):
    n_tok, d = x2d.shape
    tiles_per_seq = seq_len // tm
    in_width = w_in_bf.shape[1]
    row = lambda i: (i, 0)
    const = lambda i: (0, 0)
    return pl.pallas_call(
        _inproj_kernel,
        out_shape=(jax.ShapeDtypeStruct((n_tok, QK_WIDTH), BF16),
                   jax.ShapeDtypeStruct((n_tok, QK_WIDTH), BF16),
                   jax.ShapeDtypeStruct((n_tok, ATT_WIDTH), BF16),
                   jax.ShapeDtypeStruct((n_tok, SSM_WIDTH), F32)),
        grid=(n_tok // tm,),
        in_specs=[pl.BlockSpec((tm, d), row),
                  pl.BlockSpec((1, 6, d), lambda i: (i // tiles_per_seq, 0, 0)),
                  pl.BlockSpec((1, d), const),
                  pl.BlockSpec((d, in_width), const),
                  pl.BlockSpec((QK_WIDTH, QK_WIDTH), const),
                  pl.BlockSpec((1, QK_WIDTH), const),
                  pl.BlockSpec((1, QK_WIDTH), const),
                  pl.BlockSpec((tm, LANES), lambda i: (i % tiles_per_seq, 0)),
                  pl.BlockSpec((tm, LANES), lambda i: (i % tiles_per_seq, 0))],
        out_specs=(pl.BlockSpec((tm, QK_WIDTH), row),
                   pl.BlockSpec((tm, QK_WIDTH), row),
                   pl.BlockSpec((tm, ATT_WIDTH), row),
                   pl.BlockSpec((tm, SSM_WIDTH), row)),
        compiler_params=pltpu.CompilerParams(dimension_semantics=("arbitrary",),
                                             vmem_limit_bytes=VMEM_LIMIT),
        name="inproj",
    )(x2d, mod3, n1g, w_in_bf, seg, qg, kg, cos_t, sin_t)


def _attn_kernel(qi_ref, kj_ref, q_ref, k_ref, v_ref, lam_ref, sg_ref, o_ref,
                 m_ref, l_ref, acc_ref, *, out_scale):
    p = pl.program_id(2)
    qi = qi_ref[p]
    kj = kj_ref[p]

    @pl.when(kj == 0)
    def _():
        m_ref[...] = jnp.full(m_ref.shape, -jnp.inf, F32)
        l_ref[...] = jnp.zeros(l_ref.shape, F32)
        acc_ref[...] = jnp.zeros(acc_ref.shape, F32)

    tq = q_ref.shape[0]
    tk = k_ref.shape[0]
    first_diag = qi * (tq // tk)

    def step(masked):
        q = q_ref[...]
        lane = lax.broadcasted_iota(I32, q.shape, 1)
        zero = jnp.zeros_like(q)
        qq = jnp.concatenate([jnp.where(lane < ATT_HEAD_DIM, q, zero),
                              jnp.where(lane >= ATT_HEAD_DIM, q, zero)], axis=0)
        s = lax.dot_general(qq, k_ref[...], NT_DIMS, preferred_element_type=F32)
        if masked:
            r = lax.broadcasted_iota(I32, (tq, tk), 0)
            col = lax.broadcasted_iota(I32, (tq, tk), 1) + (kj - first_diag) * tk
            causal = col <= r
            s = jnp.where(jnp.concatenate([causal, causal], axis=0), s, -jnp.inf)
        m_old = m_ref[...]
        m_new = jnp.maximum(m_old, jnp.max(s, axis=-1, keepdims=True))
        alpha = jnp.exp(m_old - m_new)
        pr = jnp.exp(s - jnp.concatenate([m_new] * (s.shape[1] // LANES), axis=1))
        l_ref[...] = alpha * l_ref[...] + jnp.sum(pr, axis=-1, keepdims=True)
        acc_ref[...] = alpha * acc_ref[...] + _dot(pr.astype(BF16), v_ref[...])
        m_ref[...] = m_new

    @pl.when(kj < first_diag)
    def _():
        step(False)

    @pl.when(kj >= first_diag)
    def _():
        step(True)

    @pl.when(kj == first_diag + tq // tk - 1)
    def _():
        o = acc_ref[...] / l_ref[...]
        o = o[:tq] - lam_ref[...] * o[tq:]
        ms = jnp.mean(o * o, axis=-1, keepdims=True)
        o_ref[...] = (o * lax.rsqrt(ms + EPS) * sg_ref[...] * out_scale).astype(o_ref.dtype)


def _attention(q, k, v, lam_row, subln_g, bsz, seq_len, tq, tk, out_scale):
    nq, nk = seq_len // tq, seq_len // tk
    pairs = [(i, j) for i in range(nq) for j in range((i + 1) * (tq // tk))]
    qi = jnp.asarray([p[0] for p in pairs], I32)
    kj = jnp.asarray([p[1] for p in pairs], I32)
    q_map = lambda b, h, p, qi, kj: (b * nq + qi[p], h)
    k_map = lambda b, h, p, qi, kj: (b * nk + kj[p], h)
    const = lambda b, h, p, qi, kj: (0, 0)
    return pl.pallas_call(
        functools.partial(_attn_kernel, out_scale=out_scale),
        out_shape=jax.ShapeDtypeStruct((bsz * seq_len, ATT_WIDTH), BF16),
        grid_spec=pltpu.PrefetchScalarGridSpec(
            num_scalar_prefetch=2,
            grid=(bsz, N_ATT_HEADS, len(pairs)),
            in_specs=[pl.BlockSpec((tq, LANES), q_map),
                      pl.BlockSpec((tk, LANES), k_map),
                      pl.BlockSpec((tk, LANES), k_map),
                      pl.BlockSpec((1, LANES), const),
                      pl.BlockSpec((1, LANES), const)],
            out_specs=pl.BlockSpec((tq, LANES), q_map),
            scratch_shapes=[pltpu.VMEM((2 * tq, LANES), F32),
                            pltpu.VMEM((2 * tq, LANES), F32),
                            pltpu.VMEM((2 * tq, ATT_V_DIM), F32)]),
        compiler_params=pltpu.CompilerParams(
            dimension_semantics=("arbitrary", "arbitrary", "arbitrary"),
            vmem_limit_bytes=VMEM_LIMIT),
        name="attn",
    )(qi, kj, q, k, v, lam_row, subln_g)


def _ssm_mats(a_re, a_im, log_dt, b_re, b_im, c_re, c_im, d_skip):
    a_re, a_im, b_re, b_im, c_re, c_im, d_skip = (
        t.astype(F32) for t in (a_re, a_im, b_re, b_im, c_re, c_im, d_skip))
    dt = jnp.exp(log_dt.astype(F32))[:, None]
    mag = jnp.exp(a_re * dt)
    abar_re = mag * jnp.cos(a_im * dt)
    abar_im = mag * jnp.sin(a_im * dt)
    den = a_re * a_re + a_im * a_im
    nr = abar_re - 1.0
    f_re = ((nr * a_re + abar_im * a_im) / den)[..., None]
    f_im = ((abar_im * a_re - nr * a_im) / den)[..., None]
    bb_re = f_re * b_re - f_im * b_im
    bb_im = f_re * b_im + f_im * b_re
    steps = jnp.arange(SSM_CHUNK + 1, dtype=F32)[:, None, None]
    pmag = jnp.exp(a_re * dt * steps)
    pw_re = pmag * jnp.cos(a_im * dt * steps)
    pw_im = pmag * jnp.sin(a_im * dt * steps)
    ca_re = c_re[None] * pw_re[:, :, None, :] - c_im[None] * pw_im[:, :, None, :]
    ca_im = c_re[None] * pw_im[:, :, None, :] + c_im[None] * pw_re[:, :, None, :]
    hp = lax.Precision.HIGHEST
    lag = (jnp.einsum('mgcp,gpd->mgcd', ca_re[:SSM_CHUNK], bb_re, precision=hp)
           - jnp.einsum('mgcp,gpd->mgcd', ca_im[:SSM_CHUNK], bb_im, precision=hp))
    lag = lag.at[0].add(d_skip[:, :, None] * jnp.eye(SSM_GROUP, dtype=F32)[None])
    nb, gpb = SSM_LANE_BLOCKS, GROUPS_PER_BLOCK
    n_state = gpb * SSM_STATE
    in_group = jnp.arange(LANES) // SSM_GROUP
    state_group = jnp.arange(n_state) // SSM_STATE

    def spread(small, row_group, col_group):
        w = small.shape[-1]
        tiled = jnp.tile(jnp.eye(w, dtype=BF16), (1, col_group.shape[0] // w))
        wide = jnp.einsum('...w,wn->...n', small.astype(BF16), tiled, preferred_element_type=F32)
        return jnp.where(row_group[:, None] == col_group[None, :], wide, 0.0).astype(BF16)

    lag_blocks = spread(jnp.transpose(lag, (0, 1, 3, 2)).reshape(SSM_CHUNK, nb, LANES, SSM_GROUP),
                        in_group, in_group)
    zero_block = jnp.zeros_like(lag_blocks[0])
    kmat = jnp.concatenate(
        [jnp.concatenate([lag_blocks[t - j] if t >= j else zero_block for t in range(SSM_CHUNK)], axis=-1)
         for j in range(SSM_CHUNK)], axis=1)
    rev = SSM_CHUNK - 1 - jnp.arange(SSM_CHUNK)
    w_re = pw_re[rev][..., None] * bb_re[None] - pw_im[rev][..., None] * bb_im[None]
    w_im = pw_re[rev][..., None] * bb_im[None] + pw_im[rev][..., None] * bb_re[None]

    def in_to_state(w):
        small = jnp.transpose(w, (0, 1, 3, 2)).reshape(SSM_CHUNK, nb, LANES, SSM_STATE)
        return spread(small, in_group, state_group)

    bmat = jnp.concatenate([in_to_state(w_re), in_to_state(w_im)], axis=-1)
    bmat = jnp.transpose(bmat, (1, 0, 2, 3)).reshape(nb, SSM_CHUNK * LANES, 2 * n_state)

    def state_to_out(ca):
        small = jnp.transpose(ca[1:], (0, 1, 3, 2)).reshape(SSM_CHUNK, nb, n_state, SSM_GROUP)
        blocks = spread(small, state_group, in_group)
        return jnp.transpose(blocks, (1, 2, 0, 3)).reshape(nb, n_state, SSM_CHUNK * LANES)

    cmat = jnp.concatenate([state_to_out(ca_re), -state_to_out(ca_im)], axis=1)
    chunk_steps = SSM_CHUNK * jnp.arange(2 * SUBLANES, dtype=F32)[:, None, None]
    cmag = jnp.exp(a_re * dt * chunk_steps)
    to_block = lambda t: jnp.transpose(t.reshape(2 * SUBLANES, nb, n_state), (1, 0, 2))
    bpow = jnp.concatenate([to_block(cmag * jnp.cos(a_im * dt * chunk_steps)),
                            to_block(cmag * jnp.sin(a_im * dt * chunk_steps))], axis=-1)
    return kmat, bmat, cmat, bpow


def _cmul(a_re, a_im, x_re, x_im):
    return a_re * x_re - a_im * x_im, a_re * x_im + a_im * x_re


def _ssm_kernel(u_ref, km_ref, bm_ref, cm_ref, bp_ref, y_ref, carry_ref, sp_ref):
    n_chunks = sp_ref.shape[0]
    half = sp_ref.shape[1] // 2

    @pl.when(pl.program_id(2) == 0)
    def _():
        carry_ref[...] = jnp.zeros(carry_ref.shape, F32)

    ucat = jnp.concatenate([u_ref[pl.ds(j, n_chunks, stride=SSM_CHUNK), :] for j in range(SSM_CHUNK)],
                           axis=1).astype(BF16)
    s_end = _dot(ucat, bm_ref[0])
    p_re, p_im = s_end[:, :half], s_end[:, half:]
    bp = bp_ref[0]
    sub = lax.broadcasted_iota(I32, (n_chunks, half), 0) % SUBLANES

    def shifted(v, d):
        return jnp.where(sub >= d, pltpu.roll(v, d, 0), 0.0)

    for d in (1, 2, 4):
        d_re, d_im = _cmul(bp[d:d + 1, :half], bp[d:d + 1, half:], shifted(p_re, d), shifted(p_im, d))
        p_re, p_im = p_re + d_re, p_im + d_im
    x_re, x_im = shifted(p_re, 1), shifted(p_im, 1)
    t_re, t_im = bp[:SUBLANES, :half], bp[:SUBLANES, half:]
    l_re, l_im = bp[SUBLANES:SUBLANES + 1, :half], bp[SUBLANES:SUBLANES + 1, half:]
    c_re, c_im = carry_ref[:, :half], carry_ref[:, half:]
    for g in range(n_chunks // SUBLANES):
        lo, hi = g * SUBLANES, (g + 1) * SUBLANES
        d_re, d_im = _cmul(t_re, t_im, c_re, c_im)
        sp_ref[lo:hi, :half] = x_re[lo:hi] + d_re
        sp_ref[lo:hi, half:] = x_im[lo:hi] + d_im
        e_re, e_im = _cmul(l_re, l_im, c_re, c_im)
        c_re, c_im = p_re[hi - 1:hi] + e_re, p_im[hi - 1:hi] + e_im
    carry_ref[...] = jnp.concatenate([c_re, c_im], axis=1)
    y = _dot(ucat, km_ref[0]) + _dot(sp_ref[...].astype(BF16), cm_ref[0])
    for t in range(SSM_CHUNK):
        y_ref[pl.ds(t, n_chunks, stride=SSM_CHUNK), :] = y[:, t * LANES:(t + 1) * LANES]


def _ssm(u, kmat, bmat, cmat, bpow, bsz, seq_len, tt):
    nt = seq_len // tt
    n_chunks = tt // SSM_CHUNK
    width = SSM_CHUNK * LANES
    u_map = lambda g, b, i: (b * nt + i, g)
    w_map = lambda g, b, i: (g, 0, 0)
    return pl.pallas_call(
        _ssm_kernel,
        out_shape=jax.ShapeDtypeStruct((bsz * seq_len, SSM_WIDTH), F32),
        grid=(SSM_LANE_BLOCKS, bsz, nt),
        in_specs=[pl.BlockSpec((tt, LANES), u_map),
                  pl.BlockSpec((1, width, width), w_map),
                  pl.BlockSpec((1, width, width), w_map),
                  pl.BlockSpec((1, width, width), w_map),
                  pl.BlockSpec((1, 2 * SUBLANES, width), w_map)],
        out_specs=pl.BlockSpec((tt, LANES), u_map),
        scratch_shapes=[pltpu.VMEM((1, width), F32),
                        pltpu.VMEM((n_chunks, width), F32)],
        compiler_params=pltpu.CompilerParams(
            dimension_semantics=("arbitrary", "arbitrary", "arbitrary"),
            vmem_limit_bytes=VMEM_LIMIT),
        name="ssm",
    )(u, kmat, bmat, cmat, bpow)


def _first_max(v, ids, n):
    m = jnp.max(v, axis=0, keepdims=True)
    ix = jnp.min(jnp.where(v == m, ids, n), axis=0, keepdims=True)
    return m, ix


def _mix_route_kernel(x_ref, y_ref, att_ref, mod_ref, wglu_ref, sng_ref, woa_ref, wos_ref, n2g_ref,
                      wrh_ref, wrl_ref, rb_ref, wgu_ref, wds_ref,
                      xs_ref, h2p_ref, eidx_ref, gate_ref, rank_ref, cnt_ref, carry_ref):
    tm = x_ref.shape[0]

    @pl.when(pl.program_id(0) == 0)
    def _():
        carry_ref[...] = jnp.zeros(carry_ref.shape, F32)

    mod = mod_ref[0]
    gate1, shift2, scale2, gate2 = mod[2:3], mod[3:4], mod[4:5], mod[5:6]

    y = y_ref[...]
    g = 0.5 * y * (1.0 + jnp.tanh(math.sqrt(2.0 / math.pi) * (y + 0.044715 * (y * y * y))))
    glu = g * _sigmoid(_dot(g.astype(BF16), wglu_ref[...]))
    ssm = glu * lax.rsqrt(jnp.mean(glu * glu, axis=-1, keepdims=True) + EPS) * sng_ref[...]

    mix = _dot(att_ref[...], woa_ref[...]) + _dot(ssm.astype(BF16), wos_ref[...])
    x1 = x_ref[...] + gate1 * mix
    h2 = x1 * lax.rsqrt(jnp.mean(x1 * x1, axis=-1, keepdims=True) + EPS) * n2g_ref[...]
    h2 = h2 * (1.0 + scale2) + shift2
    _store_rows(h2p_ref, _pack_rows(h2))
    hb, h_lo = _split_bf16(h2)

    gu = _dot(hb, wgu_ref[...])
    gs, us = gu[:, :SHARED_DIM], gu[:, SHARED_DIM:]
    act = (gs * _sigmoid(gs) * us).astype(BF16)
    xs_ref[...] = x1 + gate2 * _dot(act, wds_ref[...])

    wrh = wrh_ref[...]
    logits = (lax.dot_general(wrh, hb, NT_DIMS, preferred_element_type=F32)
              + lax.dot_general(wrh, h_lo, NT_DIMS, preferred_element_type=F32)
              + lax.dot_general(wrl_ref[...], hb, NT_DIMS, preferred_element_type=F32))
    score = _sigmoid(logits)
    biased = score + rb_ref[...]
    neg = -jnp.inf

    ids_g = lax.broadcasted_iota(I32, (EXPERTS_PER_GROUP, tm), 0)
    group_rows = []
    for gi in range(N_GROUPS):
        blk = biased[gi * EXPERTS_PER_GROUP:(gi + 1) * EXPERTS_PER_GROUP, :]
        m1, i1 = _first_max(blk, ids_g, EXPERTS_PER_GROUP)
        m2 = jnp.max(jnp.where(ids_g == i1, neg, blk), axis=0, keepdims=True)
        group_rows.append(m1 + m2)
    cur = jnp.concatenate(group_rows, axis=0)
    ids_8 = lax.broadcasted_iota(I32, cur.shape, 0)
    picked = jnp.zeros(cur.shape, F32)
    for _ in range(TOPK_GROUPS):
        _, ix = _first_max(cur, ids_8, N_GROUPS)
        hit = ids_8 == ix
        picked = jnp.where(hit, 1.0, picked)
        cur = jnp.where(hit, neg, cur)
    e_mask = jnp.concatenate(
        [jnp.broadcast_to(picked[gi:gi + 1, :], (EXPERTS_PER_GROUP, tm)) for gi in range(N_GROUPS)], axis=0)
    cand = jnp.where(e_mask > 0.0, biased, neg)

    ids_e = lax.broadcasted_iota(I32, cand.shape, 0)
    sel = jnp.zeros(cand.shape, F32)
    idx_rows, w_rows = [], []
    for _ in range(TOP_K):
        _, ix = _first_max(cand, ids_e, N_EXPERTS)
        hit = ids_e == ix
        idx_rows.append(ix)
        w_rows.append(jnp.sum(jnp.where(hit, score, 0.0), axis=0, keepdims=True))
        sel = jnp.where(hit, 1.0, sel)
        cand = jnp.where(hit, neg, cand)
    w_sum = w_rows[0]
    for w in w_rows[1:]:
        w_sum = w_sum + w
    gate_ref[...] = jnp.concatenate([w / w_sum * ROUTED_SCALE for w in w_rows], axis=0)
    eidx_ref[...] = jnp.concatenate(idx_rows, axis=0)

    t_row = lax.broadcasted_iota(I32, (tm, tm), 0)
    t_col = lax.broadcasted_iota(I32, (tm, tm), 1)
    earlier = jnp.where(t_row < t_col, 1.0, 0.0).astype(BF16)
    before = _dot(sel.astype(BF16), earlier) + carry_ref[...]
    rank_ref[...] = jnp.concatenate(
        [jnp.sum(jnp.where(ids_e == ix, before, 0.0), axis=0, keepdims=True) for ix in idx_rows],
        axis=0).astype(I32)
    carry_ref[...] = carry_ref[...] + jnp.sum(sel, axis=1, keepdims=True)
    cnt_ref[...] = carry_ref[...]


def _mix_route(x2d, y, att, mod3, wglu, sng, woa, wos, n2g, wrh, wrl, rb, wgu, wds, seq_len, tm):
    n_tok, d = x2d.shape
    tiles_per_seq = seq_len // tm
    row = lambda i: (i, 0)
    col = lambda i: (0, i)
    const = lambda i: (0, 0)
    full = lambda a: pl.BlockSpec(a.shape, const)
    return pl.pallas_call(
        _mix_route_kernel,
        out_shape=(jax.ShapeDtypeStruct((n_tok, d), F32),
                   jax.ShapeDtypeStruct((n_tok * ROW_TILE, LANES), U32),
                   jax.ShapeDtypeStruct((TOP_K, n_tok), I32),
                   jax.ShapeDtypeStruct((TOP_K, n_tok), F32),
                   jax.ShapeDtypeStruct((TOP_K, n_tok), I32),
                   jax.ShapeDtypeStruct((N_EXPERTS, 1), F32)),
        grid=(n_tok // tm,),
        in_specs=[pl.BlockSpec((tm, d), row),
                  pl.BlockSpec((tm, SSM_WIDTH), row),
                  pl.BlockSpec((tm, ATT_WIDTH), row),
                  pl.BlockSpec((1, 6, d), lambda i: (i // tiles_per_seq, 0, 0)),
                  full(wglu), full(sng), full(woa), full(wos), full(n2g),
                  full(wrh), full(wrl), full(rb), full(wgu), full(wds)],
        out_specs=(pl.BlockSpec((tm, d), row),
                   pl.BlockSpec((tm * ROW_TILE, LANES), row),
                   pl.BlockSpec((TOP_K, tm), col),
                   pl.BlockSpec((TOP_K, tm), col),
                   pl.BlockSpec((TOP_K, tm), col),
                   pl.BlockSpec((N_EXPERTS, 1), const)),
        scratch_shapes=[pltpu.VMEM((N_EXPERTS, 1), F32)],
        compiler_params=pltpu.CompilerParams(dimension_semantics=("arbitrary",),
                                             vmem_limit_bytes=VMEM_LIMIT),
        name="mix_route",
    )(x2d, y, att, mod3, wglu, sng, woa, wos, n2g, wrh, wrl, rb, wgu, wds)


ROW_TILE = 4
EXPERT_SUB = 256
COMBINE_PITCH = TOP_K + 1


def _pack_rows(v):
    half = v.shape[1] // 2
    return pltpu.pack_elementwise([v[:, :half], v[:, half:]], packed_dtype=BF16)


def _unpack_rows(w, index):
    return pltpu.unpack_elementwise(w, index=index, packed_dtype=BF16, unpacked_dtype=F32)


def _store_rows(ref, packed, first=0):
    m = packed.shape[0]
    for s in range(ROW_TILE):
        ref[pl.ds(first * ROW_TILE + s, m, stride=ROW_TILE), :] = packed[:, s * LANES:(s + 1) * LANES]


def _load_rows(ref, m, first=0):
    return jnp.concatenate([ref[pl.ds(first * ROW_TILE + s, m, stride=ROW_TILE), :] for s in range(ROW_TILE)],
                           axis=1)


def _row(ref, r):
    return ref.at[pl.ds(pl.multiple_of(r * ROW_TILE, ROW_TILE), ROW_TILE), :]


def _slot_kernel(e_ref, r_ref, ps_ref, d_ref):
    e = e_ref[...]
    ids = lax.broadcasted_iota(I32, (N_EXPERTS, e.shape[1]), 0)
    starts = ps_ref[...]
    rows = [jnp.sum(jnp.where(ids == e[k:k + 1, :], starts, 0.0), axis=0, keepdims=True)
            for k in range(TOP_K)]
    d_ref[...] = jnp.concatenate(rows, axis=0).astype(I32) + r_ref[...]


def _slots(eidx, rank, pad_starts, ts):
    n_tok = eidx.shape[1]
    col = lambda i: (0, i)
    return pl.pallas_call(
        _slot_kernel,
        out_shape=jax.ShapeDtypeStruct((TOP_K, n_tok), I32),
        grid=(n_tok // ts,),
        in_specs=[pl.BlockSpec((TOP_K, ts), col),
                  pl.BlockSpec((TOP_K, ts), col),
                  pl.BlockSpec((N_EXPERTS, 1), lambda i: (0, 0))],
        out_specs=pl.BlockSpec((TOP_K, ts), col),
        compiler_params=pltpu.CompilerParams(dimension_semantics=("arbitrary",),
                                             vmem_limit_bytes=VMEM_LIMIT),
        name="slots",
    )(eidx, rank, pad_starts.astype(F32).reshape(N_EXPERTS, 1))


def _dispatch_kernel(d_ref, h2p_ref, xs_ref, sem):
    td = d_ref.shape[1]

    def issue(t, carry):
        for k in range(TOP_K):
            pltpu.make_async_copy(_row(h2p_ref, t), _row(xs_ref, d_ref[k, t]), sem).start(priority=k % 2)
        return carry

    lax.fori_loop(0, td, issue, 0)
    for _ in range(TOP_K):
        pltpu.make_async_copy(h2p_ref, xs_ref.at[pl.ds(0, td * ROW_TILE), :], sem).wait()


def _dispatch(dest, h2p, n_slots, td):
    n_tok = dest.shape[1]
    return pl.pallas_call(
        _dispatch_kernel,
        out_shape=jax.ShapeDtypeStruct((n_slots * ROW_TILE, LANES), U32),
        grid=(n_tok // td,),
        in_specs=[pl.BlockSpec((TOP_K, td), lambda i: (0, i), memory_space=pltpu.SMEM),
                  pl.BlockSpec((td * ROW_TILE, LANES), lambda i: (i, 0))],
        out_specs=pl.BlockSpec(memory_space=pl.ANY),
        scratch_shapes=[pltpu.SemaphoreType.DMA],
        compiler_params=pltpu.CompilerParams(dimension_semantics=("arbitrary",),
                                             vmem_limit_bytes=VMEM_LIMIT),
        name="dispatch",
    )(dest, h2p)


def _expert_kernel(be_ref, nv_ref, nx_ref, last_ref, xs_ref, wg_hbm, wu_hbm, wd_hbm, ys_ref,
                   wg_f32, wu_f32, wd_f32, wg_bf, wu_bf, wd_bf, sem):
    i = pl.program_id(0)
    blk = xs_ref.shape[0] // ROW_TILE
    expert = be_ref[i]
    n_valid = nv_ref[i]
    changed = jnp.logical_or(i == 0, expert != be_ref[jnp.maximum(i - 1, 0)])

    def weight_copies(e):
        return [pltpu.make_async_copy(wg_hbm.at[e], wg_f32, sem.at[0]),
                pltpu.make_async_copy(wu_hbm.at[e], wu_f32, sem.at[1]),
                pltpu.make_async_copy(wd_hbm.at[e], wd_f32, sem.at[2])]

    @pl.when(i == 0)
    def _():
        for cp in weight_copies(expert):
            cp.start()

    @pl.when(jnp.logical_and(n_valid > 0, changed))
    def _():
        for cp in weight_copies(expert):
            cp.wait()
        wg_bf[...] = wg_f32[...].astype(BF16)
        wu_bf[...] = wu_f32[...].astype(BF16)
        wd_bf[...] = wd_f32[...].astype(BF16)
        upcoming = nx_ref[i]

        @pl.when(upcoming >= 0)
        def _():
            for cp in weight_copies(upcoming):
                cp.start()

    def swiglu_rows(n_rows):
        packed = _load_rows(xs_ref, n_rows)
        x = jnp.concatenate([_unpack_rows(packed, 0), _unpack_rows(packed, 1)], axis=1)
        rows = lax.broadcasted_iota(I32, (n_rows, 1), 0)
        x = jnp.where(rows < n_valid, x, 0.0).astype(BF16)
        g = _dot(x, wg_bf[...])
        u = _dot(x, wu_bf[...])
        act = (g * _sigmoid(g) * u).astype(BF16)
        _store_rows(ys_ref, _pack_rows(_dot(act, wd_bf[...])))

    for n_rows in range(EXPERT_SUB, blk + 1, EXPERT_SUB):
        covers = jnp.logical_and(n_valid > n_rows - EXPERT_SUB, n_valid <= n_rows)
        pl.when(covers)(functools.partial(swiglu_rows, n_rows))


def _experts(block_expert, block_valid, next_expert, last_block, xs, w_g, w_u, w_d, blk):
    n_blocks = block_expert.shape[0]
    d, de = w_g.shape[1], w_g.shape[2]
    x_map = lambda i, be, nv, nx, last: (jnp.minimum(i, last[0]), 0)
    hbm = pl.BlockSpec(memory_space=pl.ANY)
    return pl.pallas_call(
        _expert_kernel,
        out_shape=jax.ShapeDtypeStruct(xs.shape, U32),
        grid_spec=pltpu.PrefetchScalarGridSpec(
            num_scalar_prefetch=4,
            grid=(n_blocks,),
            in_specs=[pl.BlockSpec((blk * ROW_TILE, LANES), x_map), hbm, hbm, hbm],
            out_specs=pl.BlockSpec((blk * ROW_TILE, LANES), x_map),
            scratch_shapes=[pltpu.VMEM((d, de), F32),
                            pltpu.VMEM((d, de), F32),
                            pltpu.VMEM((de, d), F32),
                            pltpu.VMEM((d, de), BF16),
                            pltpu.VMEM((d, de), BF16),
                            pltpu.VMEM((de, d), BF16),
                            pltpu.SemaphoreType.DMA((3,))]),
        compiler_params=pltpu.CompilerParams(dimension_semantics=("arbitrary",),
                                             vmem_limit_bytes=VMEM_LIMIT),
        name="experts",
    )(block_expert, block_valid, next_expert, last_block, xs, w_g, w_u, w_d)


def _combine_kernel(d_ref, dn_ref, gate_ref, xs_ref, mod_ref, ys_ref, o_ref, buf, sem):
    tc = d_ref.shape[1]
    i = pl.program_id(0)
    slot = lax.rem(i, 2)

    def gather(dest_ref, into):
        def issue(t, carry):
            for k in range(TOP_K):
                pltpu.make_async_copy(_row(ys_ref, dest_ref[k, t]), _row(buf.at[into], t * COMBINE_PITCH + k),
                                      sem.at[into]).start(priority=k % 2)
            return carry
        lax.fori_loop(0, tc, issue, 0)

    @pl.when(i == 0)
    def _():
        gather(d_ref, 0)

    @pl.when(i + 1 < pl.num_programs(0))
    def _():
        gather(dn_ref, 1 - slot)

    gathered = TOP_K * tc * ROW_TILE
    pltpu.make_async_copy(ys_ref.at[pl.ds(0, gathered), :], buf.at[slot, pl.ds(0, gathered), :],
                          sem.at[slot]).wait()

    gate2 = mod_ref[0][5:6]
    rows = buf.at[slot]
    acc = [jnp.zeros((tc, LANES), F32) for _ in range(2 * ROW_TILE)]
    for k in range(TOP_K):
        gk = jnp.transpose(jnp.broadcast_to(gate_ref[k:k + 1, :], (LANES, tc)))
        for s in range(ROW_TILE):
            w = rows[pl.ds(k * ROW_TILE + s, tc, stride=COMBINE_PITCH * ROW_TILE), :]
            acc[s] = acc[s] + gk * _unpack_rows(w, 0)
            acc[ROW_TILE + s] = acc[ROW_TILE + s] + gk * _unpack_rows(w, 1)
    o_ref[...] = xs_ref[...] + gate2 * jnp.concatenate(acc, axis=1)


def _combine(dest, gate, xs, mod3, ys, seq_len, tc):
    n_tok, d = xs.shape
    tiles_per_seq = seq_len // tc
    n_tiles = n_tok // tc
    return pl.pallas_call(
        _combine_kernel,
        out_shape=jax.ShapeDtypeStruct((n_tok, d), F32),
        grid=(n_tiles,),
        in_specs=[pl.BlockSpec((TOP_K, tc), lambda i: (0, i), memory_space=pltpu.SMEM),
                  pl.BlockSpec((TOP_K, tc), lambda i: (0, jnp.minimum(i + 1, n_tiles - 1)),
                               memory_space=pltpu.SMEM),
                  pl.BlockSpec((TOP_K, tc), lambda i: (0, i)),
                  pl.BlockSpec((tc, d), lambda i: (i, 0)),
                  pl.BlockSpec((1, 6, d), lambda i: (i // tiles_per_seq, 0, 0)),
                  pl.BlockSpec(memory_space=pl.ANY)],
        out_specs=pl.BlockSpec((tc, d), lambda i: (i, 0)),
        scratch_shapes=[pltpu.VMEM((2, COMBINE_PITCH * tc * ROW_TILE, LANES), U32),
                        pltpu.SemaphoreType.DMA((2,))],
        compiler_params=pltpu.CompilerParams(dimension_semantics=("arbitrary",),
                                             vmem_limit_bytes=VMEM_LIMIT),
        name="combine",
    )(dest, dest, gate, xs, mod3, ys)


def _tiles(seq_len):
    pick = lambda pref: min(pref, seq_len)
    return dict(inproj=pick(512), attn_q=pick(512), attn_k=pick(512), ssm=pick(2048), mix=pick(512),
                dispatch=pick(512), combine=pick(256), slots=pick(2048), expert_block=2 * EXPERT_SUB)


def _rope_tables(seq_len):
    half = ATT_HEAD_DIM // 2
    inv_freq = 1.0 / (ROPE_THETA ** (jnp.arange(0, ATT_HEAD_DIM, 2, dtype=F32) / ATT_HEAD_DIM))
    ang = jnp.arange(seq_len, dtype=F32)[:, None] * inv_freq[None, :]
    cos, sin = jnp.cos(ang), jnp.sin(ang)
    reps = LANES // half
    sign = jnp.tile(jnp.concatenate([-jnp.ones((half,), F32), jnp.ones((half,), F32)]), reps // 2)
    return jnp.tile(cos, (1, reps)), jnp.tile(sin, (1, reps)) * sign[None, :]


def _block_tables(counts, blk, n_blocks):
    padded = (counts + blk - 1) // blk * blk
    pad_ends = jnp.cumsum(padded)
    pad_starts = pad_ends - padded
    used = pad_ends[-1] // blk
    last = jnp.maximum(used - 1, 0)
    starts = jnp.arange(n_blocks, dtype=I32) * blk
    expert = jnp.sum((pad_ends[None, :] <= starts[:, None]).astype(I32), axis=1)
    expert = jnp.minimum(expert, N_EXPERTS - 1)
    expert = jnp.where(starts < pad_ends[-1], expert, expert[last])
    valid = jnp.clip(counts[expert] - (starts - pad_starts[expert]), 0, blk)
    valid = jnp.where(starts < pad_ends[-1], valid, 0).astype(I32)
    ids = jnp.arange(N_EXPERTS, dtype=I32)
    later_used = jnp.logical_and(ids[None, :] > expert[:, None], (counts > 0)[None, :])
    upcoming = jnp.min(jnp.where(later_used, ids[None, :], N_EXPERTS), axis=1)
    upcoming = jnp.where(upcoming < N_EXPERTS, upcoming, -1).astype(I32)
    return pad_starts.astype(I32), expert, valid, upcoming, last.reshape(1).astype(I32)


def kernel(x, c, norm1_g, norm2_g, w_ada, b_ada, w_in, q_norm_g, k_norm_g, lambda_q1, lambda_k1, lambda_q2, lambda_k2, subln_g, ssm_a_re, ssm_a_im, ssm_log_dt, ssm_b_re, ssm_b_im, ssm_c_re, ssm_c_im, ssm_d, w_glu, ssm_norm_g, w_out, w_router, router_bias, w_gate_e, w_up_e, w_down_e, w_gate_s, w_up_s, w_down_s):
    bsz, seq_len, d = x.shape
    n_tok = bsz * seq_len
    tiles = _tiles(seq_len)
    blk = tiles['expert_block']
    n_blocks = (n_tok * TOP_K + N_EXPERTS * (blk - 1) + blk - 1) // blk
    cos_t, sin_t = _rope_tables(seq_len)
    seg = jnp.kron(jnp.eye(QK_WIDTH // ATT_HEAD_DIM, dtype=F32),
                   jnp.full((ATT_HEAD_DIM, ATT_HEAD_DIM), 1.0 / ATT_HEAD_DIM, F32)).astype(BF16)
    reps = QK_WIDTH // ATT_HEAD_DIM
    x2d = x.reshape(n_tok, d).astype(F32)
    for layer in range(w_ada.shape[0]):
        lam_init = 0.8 - 0.6 * math.exp(-0.3 * layer)
        mod3 = _adaln(c.astype(F32), w_ada[layer].astype(F32), b_ada[layer].astype(F32)).reshape(bsz, 6, d)
        q, k, v, u = _inproj(
            x2d, mod3, norm1_g[layer].astype(F32).reshape(1, d), w_in[layer].astype(BF16), seg,
            jnp.tile(q_norm_g[layer].astype(F32), reps).reshape(1, QK_WIDTH),
            jnp.tile(k_norm_g[layer].astype(F32), reps).reshape(1, QK_WIDTH),
            cos_t, sin_t, seq_len, tiles['inproj'])
        lam = (jnp.exp(jnp.sum(lambda_q1[layer].astype(F32) * lambda_k1[layer].astype(F32)))
               - jnp.exp(jnp.sum(lambda_q2[layer].astype(F32) * lambda_k2[layer].astype(F32))) + lam_init)
        att = _attention(q, k, v, jnp.full((1, LANES), lam, F32),
                         subln_g[layer].astype(F32).reshape(1, ATT_V_DIM),
                         bsz, seq_len, tiles['attn_q'], tiles['attn_k'], 1.0 - lam_init)
        kmat, bmat, cmat, bpow = _ssm_mats(ssm_a_re[layer], ssm_a_im[layer], ssm_log_dt[layer],
                                         ssm_b_re[layer], ssm_b_im[layer], ssm_c_re[layer],
                                         ssm_c_im[layer], ssm_d[layer])
        y = _ssm(u, kmat, bmat, cmat, bpow, bsz, seq_len, tiles['ssm'])
        wr_t = jnp.transpose(w_router[layer].astype(F32))
        wrh, wrl = _split_bf16(wr_t)
        xs, h2p, eidx, gate, rank, counts = _mix_route(
            x2d, y, att, mod3, w_glu[layer].astype(BF16),
            ssm_norm_g[layer].astype(F32).reshape(1, SSM_WIDTH),
            w_out[layer, :ATT_WIDTH].astype(BF16), w_out[layer, ATT_WIDTH:].astype(BF16),
            norm2_g[layer].astype(F32).reshape(1, d), wrh, wrl,
            router_bias[layer].astype(F32).reshape(N_EXPERTS, 1),
            jnp.concatenate([w_gate_s[layer], w_up_s[layer]], axis=1).astype(BF16),
            w_down_s[layer].astype(BF16), seq_len, tiles['mix'])
        pad_starts, block_expert, block_valid, next_expert, last_block = _block_tables(
            counts.reshape(N_EXPERTS).astype(I32), blk, n_blocks)
        dest = _slots(eidx, rank, pad_starts, tiles['slots'])
        x_slots = _dispatch(dest, h2p, n_blocks * blk, tiles['dispatch'])
        y_slots = _experts(block_expert, block_valid, next_expert, last_block, x_slots,
                           w_gate_e[layer], w_up_e[layer], w_down_e[layer], blk)
        x2d = _combine(dest, gate, xs, mod3, y_slots, seq_len, tiles['combine'])
    return x2d.reshape(bsz, seq_len, d).astype(x.dtype)
```

```python
import functools
import math

import jax
import jax.numpy as jnp
from jax import lax
from jax.experimental import pallas as pl
from jax.experimental.pallas import tpu as pltpu

F32 = jnp.float32
BF16 = jnp.bfloat16
I32 = jnp.int32
U32 = jnp.uint32

LANES = 128
SUBLANES = 8

N_ATT_HEADS = 4
ATT_HEAD_DIM = 64
ATT_V_DIM = 2 * ATT_HEAD_DIM
QK_WIDTH = N_ATT_HEADS * 2 * ATT_HEAD_DIM
ATT_WIDTH = N_ATT_HEADS * ATT_V_DIM
ROPE_THETA = 10000.0
SSM_GROUP = 16
SSM_GROUPS = 32
SSM_STATE = 64
SSM_WIDTH = SSM_GROUPS * SSM_GROUP
SSM_CHUNK = SUBLANES
SSM_LANE_BLOCKS = SSM_WIDTH // LANES
GROUPS_PER_BLOCK = LANES // SSM_GROUP
N_EXPERTS = 256
TOP_K = 8
N_GROUPS = 8
TOPK_GROUPS = 4
EXPERTS_PER_GROUP = N_EXPERTS // N_GROUPS
EXPERT_DIM = 256
SHARED_DIM = 256
ROUTED_SCALE = 2.5
EPS = 1e-6

NT_DIMS = (((1,), (1,)), ((), ()))

VMEM_LIMIT = 48 * 1024 * 1024


def _dot(a, b):
    return jnp.dot(a, b, preferred_element_type=F32)


def _sigmoid(x):
    return 1.0 / (1.0 + jnp.exp(-x))


def _split_bf16(x):
    hi = x.astype(BF16)
    lo = (x - hi.astype(F32)).astype(BF16)
    return hi, lo


def _adaln_kernel(c_ref, w_ref, b_ref, o_ref):
    c = c_ref[...]
    sc = c * _sigmoid(c)
    o_ref[...] = jnp.dot(sc, w_ref[...], preferred_element_type=F32,
                         precision=lax.Precision.HIGHEST) + b_ref[...]


def _adaln(c, w, b):
    bsz, d = c.shape
    n = w.shape[1]
    tn = 1024
    return pl.pallas_call(
        _adaln_kernel,
        out_shape=jax.ShapeDtypeStruct((bsz, n), F32),
        grid=(n // tn,),
        in_specs=[pl.BlockSpec((bsz, d), lambda j: (0, 0)),
                  pl.BlockSpec((d, tn), lambda j: (0, j)),
                  pl.BlockSpec((1, tn), lambda j: (0, j))],
        out_specs=pl.BlockSpec((bsz, tn), lambda j: (0, j)),
        compiler_params=pltpu.CompilerParams(dimension_semantics=("arbitrary",),
                                             vmem_limit_bytes=VMEM_LIMIT),
        name="adaln",
    )(c, w, b.reshape(1, n))


def _inproj_kernel(x_ref, mod_ref, n1g_ref, w_ref, seg_ref, qg_ref, kg_ref, cos_ref, sin_ref,
                   q_ref, k_ref, v_ref, u_ref):
    x = x_ref[...]
    mod = mod_ref[0]
    shift, scale = mod[0:1], mod[1:2]
    ms = jnp.mean(x * x, axis=-1, keepdims=True)
    h = x * lax.rsqrt(ms + EPS) * n1g_ref[...]
    h = h * (1.0 + scale) + shift
    z = _dot(h.astype(BF16), w_ref[...])
    seg = seg_ref[...]
    cos = cos_ref[...]
    sin = sin_ref[...]
    lane = lax.broadcasted_iota(I32, cos.shape, 1)
    first_half = (lane % ATT_HEAD_DIM) < (ATT_HEAD_DIM // 2)

    def norm_rope(t, g_ref, out_scale):
        hi, lo = _split_bf16(t * t)
        msq = _dot(hi, seg) + _dot(lo, seg)
        tn = t * lax.rsqrt(msq + EPS) * g_ref[...]
        outs = []
        for hd in range(N_ATT_HEADS):
            th = tn[:, hd * LANES:(hd + 1) * LANES]
            partner = jnp.where(first_half,
                                pltpu.roll(th, LANES - ATT_HEAD_DIM // 2, 1),
                                pltpu.roll(th, ATT_HEAD_DIM // 2, 1))
            outs.append((th * cos + partner * sin) * out_scale)
        return jnp.concatenate(outs, axis=1)

    q_ref[...] = norm_rope(z[:, :QK_WIDTH], qg_ref, ATT_HEAD_DIM ** -0.5).astype(BF16)
    k_ref[...] = norm_rope(z[:, QK_WIDTH:2 * QK_WIDTH], kg_ref, 1.0).astype(BF16)
    v_ref[...] = z[:, 2 * QK_WIDTH:2 * QK_WIDTH + ATT_WIDTH].astype(BF16)
    u_ref[...] = z[:, 2 * QK_WIDTH + ATT_WIDTH:]


def _inproj(x2d, mod3, n1g, w_in_bf, seg, qg, kg, cos_t, sin_t, seq_len, tm):
    n_tok, d = x2d.shape
    tiles_per_seq = seq_len // tm
    in_width = w_in_bf.shape[1]
    row = lambda i: (i, 0)
    const = lambda i: (0, 0)
    return pl.pallas_call(
        _inproj_kernel,
        out_shape=(jax.ShapeDtypeStruct((n_tok, QK_WIDTH), BF16),
                   jax.ShapeDtypeStruct((n_tok, QK_WIDTH), BF16),
                   jax.ShapeDtypeStruct((n_tok, ATT_WIDTH), BF16),
                   jax.ShapeDtypeStruct((n_tok, SSM_WIDTH), F32)),
        grid=(n_tok // tm,),
        in_specs=[pl.BlockSpec((tm, d), row),
                  pl.BlockSpec((1, 6, d), lambda i: (i // tiles_per_seq, 0, 0)),
                  pl.BlockSpec((1, d), const),
                  pl.BlockSpec((d, in_width), const),
                  pl.BlockSpec((QK_WIDTH, QK_WIDTH), const),
                  pl.BlockSpec((1, QK_WIDTH), const),
                  pl.BlockSpec((1, QK_WIDTH), const),
                  pl.BlockSpec((tm, LANES), lambda i: (i % tiles_per_seq, 0)),
                  pl.BlockSpec((tm, LANES), lambda i: (i % tiles_per_seq, 0))],
        out_specs=(pl.BlockSpec((tm, QK_WIDTH), row),
                   pl.BlockSpec((tm, QK_WIDTH), row),
                   pl.BlockSpec((tm, ATT_WIDTH), row),
                   pl.BlockSpec((tm, SSM_WIDTH), row)),
        compiler_params=pltpu.CompilerParams(dimension_semantics=("arbitrary",),
                                             vmem_limit_bytes=VMEM_LIMIT),
        name="inproj",
    )(x2d, mod3, n1g, w_in_bf, seg, qg, kg, cos_t, sin_t)


def _attn_kernel(qi_ref, kj_ref, q_ref, k_ref, v_ref, lam_ref, sg_ref, o_ref,
                 m_ref, l_ref, acc_ref, *, out_scale):
    p = pl.program_id(2)
    qi = qi_ref[p]
    kj = kj_ref[p]

    @pl.when(kj == 0)
    def _():
        m_ref[...] = jnp.full(m_ref.shape, -jnp.inf, F32)
        l_ref[...] = jnp.zeros(l_ref.shape, F32)
        acc_ref[...] = jnp.zeros(acc_ref.shape, F32)

    tq = q_ref.shape[0]
    tk = k_ref.shape[0]
    first_diag = qi * (tq // tk)

    def step(masked):
        q = q_ref[...]
        lane = lax.broadcasted_iota(I32, q.shape, 1)
        zero = jnp.zeros_like(q)
        qq = jnp.concatenate([jnp.where(lane < ATT_HEAD_DIM, q, zero),
                              jnp.where(lane >= ATT_HEAD_DIM, q, zero)], axis=0)
        s = lax.dot_general(qq, k_ref[...], NT_DIMS, preferred_element_type=F32)
        if masked:
            r = lax.broadcasted_iota(I32, (tq, tk), 0)
            col = lax.broadcasted_iota(I32, (tq, tk), 1) + (kj - first_diag) * tk
            causal = col <= r
            s = jnp.where(jnp.concatenate([causal, causal], axis=0), s, -jnp.inf)
        m_old = m_ref[...]
        m_new = jnp.maximum(m_old, jnp.max(s, axis=-1, keepdims=True))
        alpha = jnp.exp(m_old - m_new)
        pr = jnp.exp(s - jnp.concatenate([m_new] * (s.shape[1] // LANES), axis=1))
        l_ref[...] = alpha * l_ref[...] + jnp.sum(pr, axis=-1, keepdims=True)
        acc_ref[...] = alpha * acc_ref[...] + _dot(pr.astype(BF16), v_ref[...])
        m_ref[...] = m_new

    @pl.when(kj < first_diag)
    def _():
        step(False)

    @pl.when(kj >= first_diag)
    def _():
        step(True)

    @pl.when(kj == first_diag + tq // tk - 1)
    def _():
        o = acc_ref[...] / l_ref[...]
        o = o[:tq] - lam_ref[...] * o[tq:]
        ms = jnp.mean(o * o, axis=-1, keepdims=True)
        o_ref[...] = (o * lax.rsqrt(ms + EPS) * sg_ref[...] * out_scale).astype(o_ref.dtype)


def _attention(q, k, v, lam_row, subln_g, bsz, seq_len, tq, tk, out_scale):
    nq, nk = seq_len // tq, seq_len // tk
    pairs = [(i, j) for i in range(nq) for j in range((i + 1) * (tq // tk))]
    qi = jnp.asarray([p[0] for p in pairs], I32)
    kj = jnp.asarray([p[1] for p in pairs], I32)
    q_map = lambda b, h, p, qi, kj: (b * nq + qi[p], h)
    k_map = lambda b, h, p, qi, kj: (b * nk + kj[p], h)
    const = lambda b, h, p, qi, kj: (0, 0)
    return pl.pallas_call(
        functools.partial(_attn_kernel, out_scale=out_scale),
        out_shape=jax.ShapeDtypeStruct((bsz * seq_len, ATT_WIDTH), BF16),
        grid_spec=pltpu.PrefetchScalarGridSpec(
            num_scalar_prefetch=2,
            grid=(bsz, N_ATT_HEADS, len(pairs)),
            in_specs=[pl.BlockSpec((tq, LANES), q_map),
                      pl.BlockSpec((tk, LANES), k_map),
                      pl.BlockSpec((tk, LANES), k_map),
                      pl.BlockSpec((1, LANES), const),
                      pl.BlockSpec((1, LANES), const)],
            out_specs=pl.BlockSpec((tq, LANES), q_map),
            scratch_shapes=[pltpu.VMEM((2 * tq, LANES), F32),
                            pltpu.VMEM((2 * tq, LANES), F32),
                            pltpu.VMEM((2 * tq, ATT_V_DIM), F32)]),
        compiler_params=pltpu.CompilerParams(
            dimension_semantics=("arbitrary", "arbitrary", "arbitrary"),
            vmem_limit_bytes=VMEM_LIMIT),
        name="attn",
    )(qi, kj, q, k, v, lam_row, subln_g)


def _ssm_mats(a_re, a_im, log_dt, b_re, b_im, c_re, c_im, d_skip):
    a_re, a_im, b_re, b_im, c_re, c_im, d_skip = (
        t.astype(F32) for t in (a_re, a_im, b_re, b_im, c_re, c_im, d_skip))
    dt = jnp.exp(log_dt.astype(F32))[:, None]
    mag = jnp.exp(a_re * dt)
    abar_re = mag * jnp.cos(a_im * dt)
    abar_im = mag * jnp.sin(a_im * dt)
    den = a_re * a_re + a_im * a_im
    nr = abar_re - 1.0
    f_re = ((nr * a_re + abar_im * a_im) / den)[..., None]
    f_im = ((abar_im * a_re - nr * a_im) / den)[..., None]
    bb_re = f_re * b_re - f_im * b_im
    bb_im = f_re * b_im + f_im * b_re
    steps = jnp.arange(SSM_CHUNK + 1, dtype=F32)[:, None, None]
    pmag = jnp.exp(a_re * dt * steps)
    pw_re = pmag * jnp.cos(a_im * dt * steps)
    pw_im = pmag * jnp.sin(a_im * dt * steps)
    ca_re = c_re[None] * pw_re[:, :, None, :] - c_im[None] * pw_im[:, :, None, :]
    ca_im = c_re[None] * pw_im[:, :, None, :] + c_im[None] * pw_re[:, :, None, :]
    hp = lax.Precision.HIGHEST
    lag = (jnp.einsum('mgcp,gpd->mgcd', ca_re[:SSM_CHUNK], bb_re, precision=hp)
           - jnp.einsum('mgcp,gpd->mgcd', ca_im[:SSM_CHUNK], bb_im, precision=hp))
    lag = lag.at[0].add(d_skip[:, :, None] * jnp.eye(SSM_GROUP, dtype=F32)[None])
    nb, gpb = SSM_LANE_BLOCKS, GROUPS_PER_BLOCK
    n_state = gpb * SSM_STATE
    in_group = jnp.arange(LANES) // SSM_GROUP
    state_group = jnp.arange(n_state) // SSM_STATE

    def spread(small, row_group, col_group):
        w = small.shape[-1]
        tiled = jnp.tile(jnp.eye(w, dtype=BF16), (1, col_group.shape[0] // w))
        wide = jnp.einsum('...w,wn->...n', small.astype(BF16), tiled, preferred_element_type=F32)
        return jnp.where(row_group[:, None] == col_group[None, :], wide, 0.0).astype(BF16)

    lag_blocks = spread(jnp.transpose(lag, (0, 1, 3, 2)).reshape(SSM_CHUNK, nb, LANES, SSM_GROUP),
                        in_group, in_group)
    zero_block = jnp.zeros_like(lag_blocks[0])
    kmat = jnp.concatenate(
        [jnp.concatenate([lag_blocks[t - j] if t >= j else zero_block for t in range(SSM_CHUNK)], axis=-1)
         for j in range(SSM_CHUNK)], axis=1)
    rev = SSM_CHUNK - 1 - jnp.arange(SSM_CHUNK)
    w_re = pw_re[rev][..., None] * bb_re[None] - pw_im[rev][..., None] * bb_im[None]
    w_im = pw_re[rev][..., None] * bb_im[None] + pw_im[rev][..., None] * bb_re[None]

    def in_to_state(w):
        small = jnp.transpose(w, (0, 1, 3, 2)).reshape(SSM_CHUNK, nb, LANES, SSM_STATE)
        return spread(small, in_group, state_group)

    bmat = jnp.concatenate([in_to_state(w_re), in_to_state(w_im)], axis=-1)
    bmat = jnp.transpose(bmat, (1, 0, 2, 3)).reshape(nb, SSM_CHUNK * LANES, 2 * n_state)

    def state_to_out(ca):
        small = jnp.transpose(ca[1:], (0, 1, 3, 2)).reshape(SSM_CHUNK, nb, n_state, SSM_GROUP)
        blocks = spread(small, state_group, in_group)
        return jnp.transpose(blocks, (1, 2, 0, 3)).reshape(nb, n_state, SSM_CHUNK * LANES)

    cmat = jnp.concatenate([state_to_out(ca_re), -state_to_out(ca_im)], axis=1)
    chunk_steps = SSM_CHUNK * jnp.arange(2 * SUBLANES, dtype=F32)[:, None, None]
    cmag = jnp.exp(a_re * dt * chunk_steps)
    to_block = lambda t: jnp.transpose(t.reshape(2 * SUBLANES, nb, n_state), (1, 0, 2))
    bpow = jnp.concatenate([to_block(cmag * jnp.cos(a_im * dt * chunk_steps)),
                            to_block(cmag * jnp.sin(a_im * dt * chunk_steps))], axis=-1)
    return kmat, bmat, cmat, bpow


def _cmul(a_re, a_im, x_re, x_im):
    return a_re * x_re - a_im * x_im, a_re * x_im + a_im * x_re


def _ssm_kernel(u_ref, km_ref, bm_ref, cm_ref, bp_ref, y_ref, carry_ref, sp_ref):
    n_chunks = sp_ref.shape[0]
    half = sp_ref.shape[1] // 2

    @pl.when(pl.program_id(2) == 0)
    def _():
        carry_ref[...] = jnp.zeros(carry_ref.shape, F32)

    ucat = jnp.concatenate([u_ref[pl.ds(j, n_chunks, stride=SSM_CHUNK), :] for j in range(SSM_CHUNK)],
                           axis=1).astype(BF16)
    s_end = _dot(ucat, bm_ref[0])
    p_re, p_im = s_end[:, :half], s_end[:, half:]
    bp = bp_ref[0]
    sub = lax.broadcasted_iota(I32, (n_chunks, half), 0) % SUBLANES

    def shifted(v, d):
        return jnp.where(sub >= d, pltpu.roll(v, d, 0), 0.0)

    for d in (1, 2, 4):
        d_re, d_im = _cmul(bp[d:d + 1, :half], bp[d:d + 1, half:], shifted(p_re, d), shifted(p_im, d))
        p_re, p_im = p_re + d_re, p_im + d_im
    x_re, x_im = shifted(p_re, 1), shifted(p_im, 1)
    t_re, t_im = bp[:SUBLANES, :half], bp[:SUBLANES, half:]
    l_re, l_im = bp[SUBLANES:SUBLANES + 1, :half], bp[SUBLANES:SUBLANES + 1, half:]
    c_re, c_im = carry_ref[:, :half], carry_ref[:, half:]
    for g in range(n_chunks // SUBLANES):
        lo, hi = g * SUBLANES, (g + 1) * SUBLANES
        d_re, d_im = _cmul(t_re, t_im, c_re, c_im)
        sp_ref[lo:hi, :half] = x_re[lo:hi] + d_re
        sp_ref[lo:hi, half:] = x_im[lo:hi] + d_im
        e_re, e_im = _cmul(l_re, l_im, c_re, c_im)
        c_re, c_im = p_re[hi - 1:hi] + e_re, p_im[hi - 1:hi] + e_im
    carry_ref[...] = jnp.concatenate([c_re, c_im], axis=1)
    y = _dot(ucat, km_ref[0]) + _dot(sp_ref[...].astype(BF16), cm_ref[0])
    for t in range(SSM_CHUNK):
        y_ref[pl.ds(t, n_chunks, stride=SSM_CHUNK), :] = y[:, t * LANES:(t + 1) * LANES]


def _ssm(u, kmat, bmat, cmat, bpow, bsz, seq_len, tt):
    nt = seq_len // tt
    n_chunks = tt // SSM_CHUNK
    width = SSM_CHUNK * LANES
    u_map = lambda g, b, i: (b * nt + i, g)
    w_map = lambda g, b, i: (g, 0, 0)
    return pl.pallas_call(
        _ssm_kernel,
        out_shape=jax.ShapeDtypeStruct((bsz * seq_len, SSM_WIDTH), F32),
        grid=(SSM_LANE_BLOCKS, bsz, nt),
        in_specs=[pl.BlockSpec((tt, LANES), u_map),
                  pl.BlockSpec((1, width, width), w_map),
                  pl.BlockSpec((1, width, width), w_map),
                  pl.BlockSpec((1, width, width), w_map),
                  pl.BlockSpec((1, 2 * SUBLANES, width), w_map)],
        out_specs=pl.BlockSpec((tt, LANES), u_map),
        scratch_shapes=[pltpu.VMEM((1, width), F32),
                        pltpu.VMEM((n_chunks, width), F32)],
        compiler_params=pltpu.CompilerParams(
            dimension_semantics=("arbitrary", "arbitrary", "arbitrary"),
            vmem_limit_bytes=VMEM_LIMIT),
        name="ssm",
    )(u, kmat, bmat, cmat, bpow)


def _first_max(v, ids, n):
    m = jnp.max(v, axis=0, keepdims=True)
    ix = jnp.min(jnp.where(v == m, ids, n), axis=0, keepdims=True)
    return m, ix


def _mix_route_kernel(x_ref, y_ref, att_ref, mod_ref, wglu_ref, sng_ref, woa_ref, wos_ref, n2g_ref,
                      wrh_ref, wrl_ref, rb_ref, wgu_ref, wds_ref,
                      xs_ref, h2p_ref, eidx_ref, gate_ref, rank_ref, cnt_ref, carry_ref):
    tm = x_ref.shape[0]

    @pl.when(pl.program_id(0) == 0)
    def _():
        carry_ref[...] = jnp.zeros(carry_ref.shape, F32)

    mod = mod_ref[0]
    gate1, shift2, scale2, gate2 = mod[2:3], mod[3:4], mod[4:5], mod[5:6]

    y = y_ref[...]
    g = 0.5 * y * (1.0 + jnp.tanh(math.sqrt(2.0 / math.pi) * (y + 0.044715 * (y * y * y))))
    glu = g * _sigmoid(_dot(g.astype(BF16), wglu_ref[...]))
    ssm = glu * lax.rsqrt(jnp.mean(glu * glu, axis=-1, keepdims=True) + EPS) * sng_ref[...]

    mix = _dot(att_ref[...], woa_ref[...]) + _dot(ssm.astype(BF16), wos_ref[...])
    x1 = x_ref[...] + gate1 * mix
    h2 = x1 * lax.rsqrt(jnp.mean(x1 * x1, axis=-1, keepdims=True) + EPS) * n2g_ref[...]
    h2 = h2 * (1.0 + scale2) + shift2
    _store_rows(h2p_ref, _pack_rows(h2))
    hb, h_lo = _split_bf16(h2)

    gu = _dot(hb, wgu_ref[...])
    gs, us = gu[:, :SHARED_DIM], gu[:, SHARED_DIM:]
    act = (gs * _sigmoid(gs) * us).astype(BF16)
    xs_ref[...] = x1 + gate2 * _dot(act, wds_ref[...])

    wrh = wrh_ref[...]
    logits = (lax.dot_general(wrh, hb, NT_DIMS, preferred_element_type=F32)
              + lax.dot_general(wrh, h_lo, NT_DIMS, preferred_element_type=F32)
              + lax.dot_general(wrl_ref[...], hb, NT_DIMS, preferred_element_type=F32))
    score = _sigmoid(logits)
    biased = score + rb_ref[...]
    neg = -jnp.inf

    ids_g = lax.broadcasted_iota(I32, (EXPERTS_PER_GROUP, tm), 0)
    group_rows = []
    for gi in range(N_GROUPS):
        blk = biased[gi * EXPERTS_PER_GROUP:(gi + 1) * EXPERTS_PER_GROUP, :]
        m1, i1 = _first_max(blk, ids_g, EXPERTS_PER_GROUP)
        m2 = jnp.max(jnp.where(ids_g == i1, neg, blk), axis=0, keepdims=True)
        group_rows.append(m1 + m2)
    cur = jnp.concatenate(group_rows, axis=0)
    ids_8 = lax.broadcasted_iota(I32, cur.shape, 0)
    picked = jnp.zeros(cur.shape, F32)
    for _ in range(TOPK_GROUPS):
        _, ix = _first_max(cur, ids_8, N_GROUPS)
        hit = ids_8 == ix
        picked = jnp.where(hit, 1.0, picked)
        cur = jnp.where(hit, neg, cur)
    e_mask = jnp.concatenate(
        [jnp.broadcast_to(picked[gi:gi + 1, :], (EXPERTS_PER_GROUP, tm)) for gi in range(N_GROUPS)], axis=0)
    cand = jnp.where(e_mask > 0.0, biased, neg)

    ids_e = lax.broadcasted_iota(I32, cand.shape, 0)
    sel = jnp.zeros(cand.shape, F32)
    idx_rows, w_rows = [], []
    for _ in range(TOP_K):
        _, ix = _first_max(cand, ids_e, N_EXPERTS)
        hit = ids_e == ix
        idx_rows.append(ix)
        w_rows.append(jnp.sum(jnp.where(hit, score, 0.0), axis=0, keepdims=True))
        sel = jnp.where(hit, 1.0, sel)
        cand = jnp.where(hit, neg, cand)
    w_sum = w_rows[0]
    for w in w_rows[1:]:
        w_sum = w_sum + w
    gate_ref[...] = jnp.concatenate([w / w_sum * ROUTED_SCALE for w in w_rows], axis=0)
    eidx_ref[...] = jnp.concatenate(idx_rows, axis=0)

    t_row = lax.broadcasted_iota(I32, (tm, tm), 0)
    t_col = lax.broadcasted_iota(I32, (tm, tm), 1)
    earlier = jnp.where(t_row < t_col, 1.0, 0.0).astype(BF16)
    before = _dot(sel.astype(BF16), earlier) + carry_ref[...]
    rank_ref[...] = jnp.concatenate(
        [jnp.sum(jnp.where(ids_e == ix, before, 0.0), axis=0, keepdims=True) for ix in idx_rows],
        axis=0).astype(I32)
    carry_ref[...] = carry_ref[...] + jnp.sum(sel, axis=1, keepdims=True)
    cnt_ref[...] = carry_ref[...]


def _mix_route(x2d, y, att, mod3, wglu, sng, woa, wos, n2g, wrh, wrl, rb, wgu, wds, seq_len, tm):
    n_tok, d = x2d.shape
    tiles_per_seq = seq_len // tm
    row = lambda i: (i, 0)
    col = lambda i: (0, i)
    const = lambda i: (0, 0)
    full = lambda a: pl.BlockSpec(a.shape, const)
    return pl.pallas_call(
        _mix_route_kernel,
        out_shape=(jax.ShapeDtypeStruct((n_tok, d), F32),
                   jax.ShapeDtypeStruct((n_tok * ROW_TILE, LANES), U32),
                   jax.ShapeDtypeStruct((TOP_K, n_tok), I32),
                   jax.ShapeDtypeStruct((TOP_K, n_tok), F32),
                   jax.ShapeDtypeStruct((TOP_K, n_tok), I32),
                   jax.ShapeDtypeStruct((N_EXPERTS, 1), F32)),
        grid=(n_tok // tm,),
        in_specs=[pl.BlockSpec((tm, d), row),
                  pl.BlockSpec((tm, SSM_WIDTH), row),
                  pl.BlockSpec((tm, ATT_WIDTH), row),
                  pl.BlockSpec((1, 6, d), lambda i: (i // tiles_per_seq, 0, 0)),
                  full(wglu), full(sng), full(woa), full(wos), full(n2g),
                  full(wrh), full(wrl), full(rb), full(wgu), full(wds)],
        out_specs=(pl.BlockSpec((tm, d), row),
                   pl.BlockSpec((tm * ROW_TILE, LANES), row),
                   pl.BlockSpec((TOP_K, tm), col),
                   pl.BlockSpec((TOP_K, tm), col),
                   pl.BlockSpec((TOP_K, tm), col),
                   pl.BlockSpec((N_EXPERTS, 1), const)),
        scratch_shapes=[pltpu.VMEM((N_EXPERTS, 1), F32)],
        compiler_params=pltpu.CompilerParams(dimension_semantics=("arbitrary",),
                                             vmem_limit_bytes=VMEM_LIMIT),
        name="mix_route",
    )(x2d, y, att, mod3, wglu, sng, woa, wos, n2g, wrh, wrl, rb, wgu, wds)


ROW_TILE = 4
EXPERT_SUB = 256
COMBINE_PITCH = TOP_K + 1


def _pack_rows(v):
    half = v.shape[1] // 2
    return pltpu.pack_elementwise([v[:, :half], v[:, half:]], packed_dtype=BF16)


def _unpack_rows(w, index):
    return pltpu.unpack_elementwise(w, index=index, packed_dtype=BF16, unpacked_dtype=F32)


def _store_rows(ref, packed, first=0):
    m = packed.shape[0]
    for s in range(ROW_TILE):
        ref[pl.ds(first * ROW_TILE + s, m, stride=ROW_TILE), :] = packed[:, s * LANES:(s + 1) * LANES]


def _load_rows(ref, m, first=0):
    return jnp.concatenate([ref[pl.ds(first * ROW_TILE + s, m, stride=ROW_TILE), :] for s in range(ROW_TILE)],
                           axis=1)


def _row(ref, r):
    return ref.at[pl.ds(pl.multiple_of(r * ROW_TILE, ROW_TILE), ROW_TILE), :]


def _slot_kernel(e_ref, r_ref, ps_ref, d_ref):
    e = e_ref[...]
    ids = lax.broadcasted_iota(I32, (N_EXPERTS, e.shape[1]), 0)
    starts = ps_ref[...]
    rows = [jnp.sum(jnp.where(ids == e[k:k + 1, :], starts, 0.0), axis=0, keepdims=True)
            for k in range(TOP_K)]
    d_ref[...] = jnp.concatenate(rows, axis=0).astype(I32) + r_ref[...]


def _slots(eidx, rank, pad_starts, ts):
    n_tok = eidx.shape[1]
    col = lambda i: (0, i)
    return pl.pallas_call(
        _slot_kernel,
        out_shape=jax.ShapeDtypeStruct((TOP_K, n_tok), I32),
        grid=(n_tok // ts,),
        in_specs=[pl.BlockSpec((TOP_K, ts), col),
                  pl.BlockSpec((TOP_K, ts), col),
                  pl.BlockSpec((N_EXPERTS, 1), lambda i: (0, 0))],
        out_specs=pl.BlockSpec((TOP_K, ts), col),
        compiler_params=pltpu.CompilerParams(dimension_semantics=("arbitrary",),
                                             vmem_limit_bytes=VMEM_LIMIT),
        name="slots",
    )(eidx, rank, pad_starts.astype(F32).reshape(N_EXPERTS, 1))


def _dispatch_kernel(d_ref, h2p_ref, xs_ref, sem):
    td = d_ref.shape[1]

    def issue(t, carry):
        for k in range(TOP_K):
            pltpu.make_async_copy(_row(h2p_ref, t), _row(xs_ref, d_ref[k, t]), sem).start(priority=k % 2)
        return carry

    lax.fori_loop(0, td, issue, 0)
    for _ in range(TOP_K):
        pltpu.make_async_copy(h2p_ref, xs_ref.at[pl.ds(0, td * ROW_TILE), :], sem).wait()


def _dispatch(dest, h2p, n_slots, td):
    n_tok = dest.shape[1]
    return pl.pallas_call(
        _dispatch_kernel,
        out_shape=jax.ShapeDtypeStruct((n_slots * ROW_TILE, LANES), U32),
        grid=(n_tok // td,),
        in_specs=[pl.BlockSpec((TOP_K, td), lambda i: (0, i), memory_space=pltpu.SMEM),
                  pl.BlockSpec((td * ROW_TILE, LANES), lambda i: (i, 0))],
        out_specs=pl.BlockSpec(memory_space=pl.ANY),
        scratch_shapes=[pltpu.SemaphoreType.DMA],
        compiler_params=pltpu.CompilerParams(dimension_semantics=("arbitrary",),
                                             vmem_limit_bytes=VMEM_LIMIT),
        name="dispatch",
    )(dest, h2p)


def _expert_kernel(be_ref, nv_ref, nx_ref, last_ref, xs_ref, wg_hbm, wu_hbm, wd_hbm, ys_ref,
                   wg_f32, wu_f32, wd_f32, wg_bf, wu_bf, wd_bf, sem):
    i = pl.program_id(0)
    blk = xs_ref.shape[0] // ROW_TILE
    expert = be_ref[i]
    n_valid = nv_ref[i]
    changed = jnp.logical_or(i == 0, expert != be_ref[jnp.maximum(i - 1, 0)])

    def weight_copies(e):
        return [pltpu.make_async_copy(wg_hbm.at[e], wg_f32, sem.at[0]),
                pltpu.make_async_copy(wu_hbm.at[e], wu_f32, sem.at[1]),
                pltpu.make_async_copy(wd_hbm.at[e], wd_f32, sem.at[2])]

    @pl.when(i == 0)
    def _():
        for cp in weight_copies(expert):
            cp.start()

    @pl.when(jnp.logical_and(n_valid > 0, changed))
    def _():
        for cp in weight_copies(expert):
            cp.wait()
        wg_bf[...] = wg_f32[...].astype(BF16)
        wu_bf[...] = wu_f32[...].astype(BF16)
        wd_bf[...] = wd_f32[...].astype(BF16)
        upcoming = nx_ref[i]

        @pl.when(upcoming >= 0)
        def _():
            for cp in weight_copies(upcoming):
                cp.start()

    def swiglu_rows(n_rows):
        packed = _load_rows(xs_ref, n_rows)
        x = jnp.concatenate([_unpack_rows(packed, 0), _unpack_rows(packed, 1)], axis=1)
        rows = lax.broadcasted_iota(I32, (n_rows, 1), 0)
        x = jnp.where(rows < n_valid, x, 0.0).astype(BF16)
        g = _dot(x, wg_bf[...])
        u = _dot(x, wu_bf[...])
        act = (g * _sigmoid(g) * u).astype(BF16)
        _store_rows(ys_ref, _pack_rows(_dot(act, wd_bf[...])))

    for n_rows in range(EXPERT_SUB, blk + 1, EXPERT_SUB):
        covers = jnp.logical_and(n_valid > n_rows - EXPERT_SUB, n_valid <= n_rows)
        pl.when(covers)(functools.partial(swiglu_rows, n_rows))


def _experts(block_expert, block_valid, next_expert, last_block, xs, w_g, w_u, w_d, blk):
    n_blocks = block_expert.shape[0]
    d, de = w_g.shape[1], w_g.shape[2]
    x_map = lambda i, be, nv, nx, last: (jnp.minimum(i, last[0]), 0)
    hbm = pl.BlockSpec(memory_space=pl.ANY)
    return pl.pallas_call(
        _expert_kernel,
        out_shape=jax.ShapeDtypeStruct(xs.shape, U32),
        grid_spec=pltpu.PrefetchScalarGridSpec(
            num_scalar_prefetch=4,
            grid=(n_blocks,),
            in_specs=[pl.BlockSpec((blk * ROW_TILE, LANES), x_map), hbm, hbm, hbm],
            out_specs=pl.BlockSpec((blk * ROW_TILE, LANES), x_map),
            scratch_shapes=[pltpu.VMEM((d, de), F32),
                            pltpu.VMEM((d, de), F32),
                            pltpu.VMEM((de, d), F32),
                            pltpu.VMEM((d, de), BF16),
                            pltpu.VMEM((d, de), BF16),
                            pltpu.VMEM((de, d), BF16),
                            pltpu.SemaphoreType.DMA((3,))]),
        compiler_params=pltpu.CompilerParams(dimension_semantics=("arbitrary",),
                                             vmem_limit_bytes=VMEM_LIMIT),
        name="experts",
    )(block_expert, block_valid, next_expert, last_block, xs, w_g, w_u, w_d)


def _combine_kernel(d_ref, dn_ref, gate_ref, xs_ref, mod_ref, ys_ref, o_ref, buf, sem):
    tc = d_ref.shape[1]
    i = pl.program_id(0)
    slot = lax.rem(i, 2)

    def gather(dest_ref, into):
        def issue(t, carry):
            for k in range(TOP_K):
                pltpu.make_async_copy(_row(ys_ref, dest_ref[k, t]), _row(buf.at[into], t * COMBINE_PITCH + k),
                                      sem.at[into]).start(priority=k % 2)
            return carry
        lax.fori_loop(0, tc, issue, 0)

    @pl.when(i == 0)
    def _():
        gather(d_ref, 0)

    @pl.when(i + 1 < pl.num_programs(0))
    def _():
        gather(dn_ref, 1 - slot)

    gathered = TOP_K * tc * ROW_TILE
    pltpu.make_async_copy(ys_ref.at[pl.ds(0, gathered), :], buf.at[slot, pl.ds(0, gathered), :],
                          sem.at[slot]).wait()

    gate2 = mod_ref[0][5:6]
    rows = buf.at[slot]
    acc = [jnp.zeros((tc, LANES), F32) for _ in range(2 * ROW_TILE)]
    for k in range(TOP_K):
        gk = jnp.transpose(jnp.broadcast_to(gate_ref[k:k + 1, :], (LANES, tc)))
        for s in range(ROW_TILE):
            w = rows[pl.ds(k * ROW_TILE + s, tc, stride=COMBINE_PITCH * ROW_TILE), :]
            acc[s] = acc[s] + gk * _unpack_rows(w, 0)
            acc[ROW_TILE + s] = acc[ROW_TILE + s] + gk * _unpack_rows(w, 1)
    o_ref[...] = xs_ref[...] + gate2 * jnp.concatenate(acc, axis=1)


def _combine(dest, gate, xs, mod3, ys, seq_len, tc):
    n_tok, d = xs.shape
    tiles_per_seq = seq_len // tc
    n_tiles = n_tok // tc
    return pl.pallas_call(
        _combine_kernel,
        out_shape=jax.ShapeDtypeStruct((n_tok, d), F32),
        grid=(n_tiles,),
        in_specs=[pl.BlockSpec((TOP_K, tc), lambda i: (0, i), memory_space=pltpu.SMEM),
                  pl.BlockSpec((TOP_K, tc), lambda i: (0, jnp.minimum(i + 1, n_tiles - 1)),
                               memory_space=pltpu.SMEM),
                  pl.BlockSpec((TOP_K, tc), lambda i: (0, i)),
                  pl.BlockSpec((tc, d), lambda i: (i, 0)),
                  pl.BlockSpec((1, 6, d), lambda i: (i // tiles_per_seq, 0, 0)),
                  pl.BlockSpec(memory_space=pl.ANY)],
        out_specs=pl.BlockSpec((tc, d), lambda i: (i, 0)),
        scratch_shapes=[pltpu.VMEM((2, COMBINE_PITCH * tc * ROW_TILE, LANES), U32),
                        pltpu.SemaphoreType.DMA((2,))],
        compiler_params=pltpu.CompilerParams(dimension_semantics=("arbitrary",),
                                             vmem_limit_bytes=VMEM_LIMIT),
        name="combine",
    )(dest, dest, gate, xs, mod3, ys)


def _tiles(seq_len):
    pick = lambda pref: min(pref, seq_len)
    return dict(inproj=pick(512), attn_q=pick(512), attn_k=pick(512), ssm=pick(2048), mix=pick(512),
                dispatch=pick(512), combine=pick(256), slots=pick(2048), expert_block=4 * EXPERT_SUB)


def _rope_tables(seq_len):
    half = ATT_HEAD_DIM // 2
    inv_freq = 1.0 / (ROPE_THETA ** (jnp.arange(0, ATT_HEAD_DIM, 2, dtype=F32) / ATT_HEAD_DIM))
    ang = jnp.arange(seq_len, dtype=F32)[:, None] * inv_freq[None, :]
    cos, sin = jnp.cos(ang), jnp.sin(ang)
    reps = LANES // half
    sign = jnp.tile(jnp.concatenate([-jnp.ones((half,), F32), jnp.ones((half,), F32)]), reps // 2)
    return jnp.tile(cos, (1, reps)), jnp.tile(sin, (1, reps)) * sign[None, :]


def _block_tables(counts, blk, n_blocks):
    padded = (counts + blk - 1) // blk * blk
    pad_ends = jnp.cumsum(padded)
    pad_starts = pad_ends - padded
    used = pad_ends[-1] // blk
    last = jnp.maximum(used - 1, 0)
    starts = jnp.arange(n_blocks, dtype=I32) * blk
    expert = jnp.sum((pad_ends[None, :] <= starts[:, None]).astype(I32), axis=1)
    expert = jnp.minimum(expert, N_EXPERTS - 1)
    expert = jnp.where(starts < pad_ends[-1], expert, expert[last])
    valid = jnp.clip(counts[expert] - (starts - pad_starts[expert]), 0, blk)
    valid = jnp.where(starts < pad_ends[-1], valid, 0).astype(I32)
    ids = jnp.arange(N_EXPERTS, dtype=I32)
    later_used = jnp.logical_and(ids[None, :] > expert[:, None], (counts > 0)[None, :])
    upcoming = jnp.min(jnp.where(later_used, ids[None, :], N_EXPERTS), axis=1)
    upcoming = jnp.where(upcoming < N_EXPERTS, upcoming, -1).astype(I32)
    return pad_starts.astype(I32), expert, valid, upcoming, last.reshape(1).astype(I32)


def kernel(x, c, norm1_g, norm2_g, w_ada, b_ada, w_in, q_norm_g, k_norm_g, lambda_q1, lambda_k1, lambda_q2, lambda_k2, subln_g, ssm_a_re, ssm_a_im, ssm_log_dt, ssm_b_re, ssm_b_im, ssm_c_re, ssm_c_im, ssm_d, w_glu, ssm_norm_g, w_out, w_router, router_bias, w_gate_e, w_up_e, w_down_e, w_gate_s, w_up_s, w_down_s):
    bsz, seq_len, d = x.shape
    n_tok = bsz * seq_len
    tiles = _tiles(seq_len)
    blk = tiles['expert_block']
    n_blocks = (n_tok * TOP_K + N_EXPERTS * (blk - 1) + blk - 1) // blk
    cos_t, sin_t = _rope_tables(seq_len)
    seg = jnp.kron(jnp.eye(QK_WIDTH // ATT_HEAD_DIM, dtype=F32),
                   jnp.full((ATT_HEAD_DIM, ATT_HEAD_DIM), 1.0 / ATT_HEAD_DIM, F32)).astype(BF16)
    reps = QK_WIDTH // ATT_HEAD_DIM
    x2d = x.reshape(n_tok, d).astype(F32)
    for layer in range(w_ada.shape[0]):
        lam_init = 0.8 - 0.6 * math.exp(-0.3 * layer)
        mod3 = _adaln(c.astype(F32), w_ada[layer].astype(F32), b_ada[layer].astype(F32)).reshape(bsz, 6, d)
        q, k, v, u = _inproj(
            x2d, mod3, norm1_g[layer].astype(F32).reshape(1, d), w_in[layer].astype(BF16), seg,
            jnp.tile(q_norm_g[layer].astype(F32), reps).reshape(1, QK_WIDTH),
            jnp.tile(k_norm_g[layer].astype(F32), reps).reshape(1, QK_WIDTH),
            cos_t, sin_t, seq_len, tiles['inproj'])
        lam = (jnp.exp(jnp.sum(lambda_q1[layer].astype(F32) * lambda_k1[layer].astype(F32)))
               - jnp.exp(jnp.sum(lambda_q2[layer].astype(F32) * lambda_k2[layer].astype(F32))) + lam_init)
        att = _attention(q, k, v, jnp.full((1, LANES), lam, F32),
                         subln_g[layer].astype(F32).reshape(1, ATT_V_DIM),
                         bsz, seq_len, tiles['attn_q'], tiles['attn_k'], 1.0 - lam_init)
        kmat, bmat, cmat, bpow = _ssm_mats(ssm_a_re[layer], ssm_a_im[layer], ssm_log_dt[layer],
                                         ssm_b_re[layer], ssm_b_im[layer], ssm_c_re[layer],
                                         ssm_c_im[layer], ssm_d[layer])
        y = _ssm(u, kmat, bmat, cmat, bpow, bsz, seq_len, tiles['ssm'])
        wr_t = jnp.transpose(w_router[layer].astype(F32))
        wrh, wrl = _split_bf16(wr_t)
        xs, h2p, eidx, gate, rank, counts = _mix_route(
            x2d, y, att, mod3, w_glu[layer].astype(BF16),
            ssm_norm_g[layer].astype(F32).reshape(1, SSM_WIDTH),
            w_out[layer, :ATT_WIDTH].astype(BF16), w_out[layer, ATT_WIDTH:].astype(BF16),
            norm2_g[layer].astype(F32).reshape(1, d), wrh, wrl,
            router_bias[layer].astype(F32).reshape(N_EXPERTS, 1),
            jnp.concatenate([w_gate_s[layer], w_up_s[layer]], axis=1).astype(BF16),
            w_down_s[layer].astype(BF16), seq_len, tiles['mix'])
        pad_starts, block_expert, block_valid, next_expert, last_block = _block_tables(
            counts.reshape(N_EXPERTS).astype(I32), blk, n_blocks)
        dest = _slots(eidx, rank, pad_starts, tiles['slots'])
        x_slots = _dispatch(dest, h2p, n_blocks * blk, tiles['dispatch'])
        y_slots = _experts(block_expert, block_valid, next_expert, last_block, x_slots,
                           w_gate_e[layer], w_up_e[layer], w_down_e[layer], blk)
        x2d = _combine(dest, gate, xs, mod3, y_slots, seq_len, tiles['combine'])
    return x2d.reshape(bsz, seq_len, d).astype(x.dtype)
```

```python
import functools
import math

import jax
import jax.numpy as jnp
from jax import lax
from jax.experimental import pallas as pl
from jax.experimental.pallas import tpu as pltpu

F32 = jnp.float32
BF16 = jnp.bfloat16
I32 = jnp.int32
U32 = jnp.uint32

LANES = 128
SUBLANES = 8

N_ATT_HEADS = 4
ATT_HEAD_DIM = 64
ATT_V_DIM = 2 * ATT_HEAD_DIM
QK_WIDTH = N_ATT_HEADS * 2 * ATT_HEAD_DIM
ATT_WIDTH = N_ATT_HEADS * ATT_V_DIM
ROPE_THETA = 10000.0
SSM_GROUP = 16
SSM_GROUPS = 32
SSM_STATE = 64
SSM_WIDTH = SSM_GROUPS * SSM_GROUP
SSM_CHUNK = SUBLANES
SSM_LANE_BLOCKS = SSM_WIDTH // LANES
GROUPS_PER_BLOCK = LANES // SSM_GROUP
N_EXPERTS = 256
TOP_K = 8
N_GROUPS = 8
TOPK_GROUPS = 4
EXPERTS_PER_GROUP = N_EXPERTS // N_GROUPS
EXPERT_DIM = 256
SHARED_DIM = 256
ROUTED_SCALE = 2.5
EPS = 1e-6

NT_DIMS = (((1,), (1,)), ((), ()))

VMEM_LIMIT = 48 * 1024 * 1024


def _dot(a, b):
    return jnp.dot(a, b, preferred_element_type=F32)


def _sigmoid(x):
    return 1.0 / (1.0 + jnp.exp(-x))


def _split_bf16(x):
    hi = x.astype(BF16)
    lo = (x - hi.astype(F32)).astype(BF16)
    return hi, lo


def _adaln_kernel(c_ref, w_ref, b_ref, o_ref):
    c = c_ref[...]
    sc = c * _sigmoid(c)
    o_ref[...] = jnp.dot(sc, w_ref[...], preferred_element_type=F32,
                         precision=lax.Precision.HIGHEST) + b_ref[...]


def _adaln(c, w, b):
    bsz, d = c.shape
    n = w.shape[1]
    tn = 1024
    return pl.pallas_call(
        _adaln_kernel,
        out_shape=jax.ShapeDtypeStruct((bsz, n), F32),
        grid=(n // tn,),
        in_specs=[pl.BlockSpec((bsz, d), lambda j: (0, 0)),
                  pl.BlockSpec((d, tn), lambda j: (0, j)),
                  pl.BlockSpec((1, tn), lambda j: (0, j))],
        out_specs=pl.BlockSpec((bsz, tn), lambda j: (0, j)),
        compiler_params=pltpu.CompilerParams(dimension_semantics=("arbitrary",),
                                             vmem_limit_bytes=VMEM_LIMIT),
        name="adaln",
    )(c, w, b.reshape(1, n))


def _inproj_kernel(x_ref, mod_ref, n1g_ref, w_ref, seg_ref, qg_ref, kg_ref, cos_ref, sin_ref,
                   q_ref, k_ref, v_ref, u_ref):
    x = x_ref[...]
    mod = mod_ref[0]
    shift, scale = mod[0:1], mod[1:2]
    ms = jnp.mean(x * x, axis=-1, keepdims=True)
    h = x * lax.rsqrt(ms + EPS) * n1g_ref[...]
    h = h * (1.0 + scale) + shift
    z = _dot(h.astype(BF16), w_ref[...])
    seg = seg_ref[...]
    cos = cos_ref[...]
    sin = sin_ref[...]
    lane = lax.broadcasted_iota(I32, cos.shape, 1)
    first_half = (lane % ATT_HEAD_DIM) < (ATT_HEAD_DIM // 2)

    def norm_rope(t, g_ref, out_scale):
        hi, lo = _split_bf16(t * t)
        msq = _dot(hi, seg) + _dot(lo, seg)
        tn = t * lax.rsqrt(msq + EPS) * g_ref[...]
        outs = []
        for hd in range(N_ATT_HEADS):
            th = tn[:, hd * LANES:(hd + 1) * LANES]
            partner = jnp.where(first_half,
                                pltpu.roll(th, LANES - ATT_HEAD_DIM // 2, 1),
                                pltpu.roll(th, ATT_HEAD_DIM // 2, 1))
            outs.append((th * cos + partner * sin) * out_scale)
        return jnp.concatenate(outs, axis=1)

    q_ref[...] = norm_rope(z[:, :QK_WIDTH], qg_ref, ATT_HEAD_DIM ** -0.5).astype(BF16)
    k_ref[...] = norm_rope(z[:, QK_WIDTH:2 * QK_WIDTH], kg_ref, 1.0).astype(BF16)
    v_ref[...] = z[:, 2 * QK_WIDTH:2 * QK_WIDTH + ATT_WIDTH].astype(BF16)
    u_ref[...] = z[:, 2 * QK_WIDTH + ATT_WIDTH:]


def _inproj(x2d, mod3, n1g, w_in_bf, seg, qg, kg, cos_t, sin_t, seq_len, tm):
    n_tok, d = x2d.shape
    tiles_per_seq = seq_len // tm
    in_width = w_in_bf.shape[1]
    row = lambda i: (i, 0)
    const = lambda i: (0, 0)
    return pl.pallas_call(
        _inproj_kernel,
        out_shape=(jax.ShapeDtypeStruct((n_tok, QK_WIDTH), BF16),
                   jax.ShapeDtypeStruct((n_tok, QK_WIDTH), BF16),
                   jax.ShapeDtypeStruct((n_tok, ATT_WIDTH), BF16),
                   jax.ShapeDtypeStruct((n_tok, SSM_WIDTH), F32)),
        grid=(n_tok // tm,),
        in_specs=[pl.BlockSpec((tm, d), row),
                  pl.BlockSpec((1, 6, d), lambda i: (i // tiles_per_seq, 0, 0)),
                  pl.BlockSpec((1, d), const),
                  pl.BlockSpec((d, in_width), const),
                  pl.BlockSpec((QK_WIDTH, QK_WIDTH), const),
                  pl.BlockSpec((1, QK_WIDTH), const),
                  pl.BlockSpec((1, QK_WIDTH), const),
                  pl.BlockSpec((tm, LANES), lambda i: (i % tiles_per_seq, 0)),
                  pl.BlockSpec((tm, LANES), lambda i: (i % tiles_per_seq, 0))],
        out_specs=(pl.BlockSpec((tm, QK_WIDTH), row),
                   pl.BlockSpec((tm, QK_WIDTH), row),
                   pl.BlockSpec((tm, ATT_WIDTH), row),
                   pl.BlockSpec((tm, SSM_WIDTH), row)),
        compiler_params=pltpu.CompilerParams(dimension_semantics=("arbitrary",),
                                             vmem_limit_bytes=VMEM_LIMIT),
        name="inproj",
    )(x2d, mod3, n1g, w_in_bf, seg, qg, kg, cos_t, sin_t)


def _attn_kernel(qi_ref, kj_ref, q_ref, k_ref, v_ref, lam_ref, sg_ref, o_ref,
                 m_ref, l_ref, acc_ref, *, out_scale):
    p = pl.program_id(2)
    qi = qi_ref[p]
    kj = kj_ref[p]

    @pl.when(kj == 0)
    def _():
        m_ref[...] = jnp.full(m_ref.shape, -jnp.inf, F32)
        l_ref[...] = jnp.zeros(l_ref.shape, F32)
        acc_ref[...] = jnp.zeros(acc_ref.shape, F32)

    tq = q_ref.shape[0]
    tk = k_ref.shape[0]
    first_diag = qi * (tq // tk)

    def step(masked):
        q = q_ref[...]
        lane = lax.broadcasted_iota(I32, q.shape, 1)
        zero = jnp.zeros_like(q)
        qq = jnp.concatenate([jnp.where(lane < ATT_HEAD_DIM, q, zero),
                              jnp.where(lane >= ATT_HEAD_DIM, q, zero)], axis=0)
        s = lax.dot_general(qq, k_ref[...], NT_DIMS, preferred_element_type=F32)
        if masked:
            r = lax.broadcasted_iota(I32, (tq, tk), 0)
            col = lax.broadcasted_iota(I32, (tq, tk), 1) + (kj - first_diag) * tk
            causal = col <= r
            s = jnp.where(jnp.concatenate([causal, causal], axis=0), s, -jnp.inf)
        m_old = m_ref[...]
        m_new = jnp.maximum(m_old, jnp.max(s, axis=-1, keepdims=True))
        alpha = jnp.exp(m_old - m_new)
        pr = jnp.exp(s - jnp.concatenate([m_new] * (s.shape[1] // LANES), axis=1))
        l_ref[...] = alpha * l_ref[...] + jnp.sum(pr, axis=-1, keepdims=True)
        acc_ref[...] = alpha * acc_ref[...] + _dot(pr.astype(BF16), v_ref[...])
        m_ref[...] = m_new

    @pl.when(kj < first_diag)
    def _():
        step(False)

    @pl.when(kj >= first_diag)
    def _():
        step(True)

    @pl.when(kj == first_diag + tq // tk - 1)
    def _():
        o = acc_ref[...] / l_ref[...]
        o = o[:tq] - lam_ref[...] * o[tq:]
        ms = jnp.mean(o * o, axis=-1, keepdims=True)
        o_ref[...] = (o * lax.rsqrt(ms + EPS) * sg_ref[...] * out_scale).astype(o_ref.dtype)


def _attention(q, k, v, lam_row, subln_g, bsz, seq_len, tq, tk, out_scale):
    nq, nk = seq_len // tq, seq_len // tk
    pairs = [(i, j) for i in range(nq) for j in range((i + 1) * (tq // tk))]
    qi = jnp.asarray([p[0] for p in pairs], I32)
    kj = jnp.asarray([p[1] for p in pairs], I32)
    q_map = lambda b, h, p, qi, kj: (b * nq + qi[p], h)
    k_map = lambda b, h, p, qi, kj: (b * nk + kj[p], h)
    const = lambda b, h, p, qi, kj: (0, 0)
    return pl.pallas_call(
        functools.partial(_attn_kernel, out_scale=out_scale),
        out_shape=jax.ShapeDtypeStruct((bsz * seq_len, ATT_WIDTH), BF16),
        grid_spec=pltpu.PrefetchScalarGridSpec(
            num_scalar_prefetch=2,
            grid=(bsz, N_ATT_HEADS, len(pairs)),
            in_specs=[pl.BlockSpec((tq, LANES), q_map),
                      pl.BlockSpec((tk, LANES), k_map),
                      pl.BlockSpec((tk, LANES), k_map),
                      pl.BlockSpec((1, LANES), const),
                      pl.BlockSpec((1, LANES), const)],
            out_specs=pl.BlockSpec((tq, LANES), q_map),
            scratch_shapes=[pltpu.VMEM((2 * tq, LANES), F32),
                            pltpu.VMEM((2 * tq, LANES), F32),
                            pltpu.VMEM((2 * tq, ATT_V_DIM), F32)]),
        compiler_params=pltpu.CompilerParams(
            dimension_semantics=("arbitrary", "arbitrary", "arbitrary"),
            vmem_limit_bytes=VMEM_LIMIT),
        name="attn",
    )(qi, kj, q, k, v, lam_row, subln_g)


def _ssm_mats(a_re, a_im, log_dt, b_re, b_im, c_re, c_im, d_skip):
    a_re, a_im, b_re, b_im, c_re, c_im, d_skip = (
        t.astype(F32) for t in (a_re, a_im, b_re, b_im, c_re, c_im, d_skip))
    dt = jnp.exp(log_dt.astype(F32))[:, None]
    mag = jnp.exp(a_re * dt)
    abar_re = mag * jnp.cos(a_im * dt)
    abar_im = mag * jnp.sin(a_im * dt)
    den = a_re * a_re + a_im * a_im
    nr = abar_re - 1.0
    f_re = ((nr * a_re + abar_im * a_im) / den)[..., None]
    f_im = ((abar_im * a_re - nr * a_im) / den)[..., None]
    bb_re = f_re * b_re - f_im * b_im
    bb_im = f_re * b_im + f_im * b_re
    steps = jnp.arange(SSM_CHUNK + 1, dtype=F32)[:, None, None]
    pmag = jnp.exp(a_re * dt * steps)
    pw_re = pmag * jnp.cos(a_im * dt * steps)
    pw_im = pmag * jnp.sin(a_im * dt * steps)
    ca_re = c_re[None] * pw_re[:, :, None, :] - c_im[None] * pw_im[:, :, None, :]
    ca_im = c_re[None] * pw_im[:, :, None, :] + c_im[None] * pw_re[:, :, None, :]
    hp = lax.Precision.HIGHEST
    lag = (jnp.einsum('mgcp,gpd->mgcd', ca_re[:SSM_CHUNK], bb_re, precision=hp)
           - jnp.einsum('mgcp,gpd->mgcd', ca_im[:SSM_CHUNK], bb_im, precision=hp))
    lag = lag.at[0].add(d_skip[:, :, None] * jnp.eye(SSM_GROUP, dtype=F32)[None])
    nb, gpb = SSM_LANE_BLOCKS, GROUPS_PER_BLOCK
    n_state = gpb * SSM_STATE
    in_group = jnp.arange(LANES) // SSM_GROUP
    state_group = jnp.arange(n_state) // SSM_STATE

    def spread(small, row_group, col_group):
        w = small.shape[-1]
        tiled = jnp.tile(jnp.eye(w, dtype=BF16), (1, col_group.shape[0] // w))
        wide = jnp.einsum('...w,wn->...n', small.astype(BF16), tiled, preferred_element_type=F32)
        return jnp.where(row_group[:, None] == col_group[None, :], wide, 0.0).astype(BF16)

    lag_blocks = spread(jnp.transpose(lag, (0, 1, 3, 2)).reshape(SSM_CHUNK, nb, LANES, SSM_GROUP),
                        in_group, in_group)
    zero_block = jnp.zeros_like(lag_blocks[0])
    kmat = jnp.concatenate(
        [jnp.concatenate([lag_blocks[t - j] if t >= j else zero_block for t in range(SSM_CHUNK)], axis=-1)
         for j in range(SSM_CHUNK)], axis=1)
    rev = SSM_CHUNK - 1 - jnp.arange(SSM_CHUNK)
    w_re = pw_re[rev][..., None] * bb_re[None] - pw_im[rev][..., None] * bb_im[None]
    w_im = pw_re[rev][..., None] * bb_im[None] + pw_im[rev][..., None] * bb_re[None]

    def in_to_state(w):
        small = jnp.transpose(w, (0, 1, 3, 2)).reshape(SSM_CHUNK, nb, LANES, SSM_STATE)
        return spread(small, in_group, state_group)

    bmat = jnp.concatenate([in_to_state(w_re), in_to_state(w_im)], axis=-1)
    bmat = jnp.transpose(bmat, (1, 0, 2, 3)).reshape(nb, SSM_CHUNK * LANES, 2 * n_state)

    def state_to_out(ca):
        small = jnp.transpose(ca[1:], (0, 1, 3, 2)).reshape(SSM_CHUNK, nb, n_state, SSM_GROUP)
        blocks = spread(small, state_group, in_group)
        return jnp.transpose(blocks, (1, 2, 0, 3)).reshape(nb, n_state, SSM_CHUNK * LANES)

    cmat = jnp.concatenate([state_to_out(ca_re), -state_to_out(ca_im)], axis=1)
    chunk_steps = SSM_CHUNK * jnp.arange(2 * SUBLANES, dtype=F32)[:, None, None]
    cmag = jnp.exp(a_re * dt * chunk_steps)
    to_block = lambda t: jnp.transpose(t.reshape(2 * SUBLANES, nb, n_state), (1, 0, 2))
    bpow = jnp.concatenate([to_block(cmag * jnp.cos(a_im * dt * chunk_steps)),
                            to_block(cmag * jnp.sin(a_im * dt * chunk_steps))], axis=-1)
    return kmat, bmat, cmat, bpow


def _cmul(a_re, a_im, x_re, x_im):
    return a_re * x_re - a_im * x_im, a_re * x_im + a_im * x_re


def _ssm_kernel(u_ref, km_ref, bm_ref, cm_ref, bp_ref, y_ref, carry_ref, sp_ref):
    n_chunks = sp_ref.shape[0]
    half = sp_ref.shape[1] // 2

    @pl.when(pl.program_id(2) == 0)
    def _():
        carry_ref[...] = jnp.zeros(carry_ref.shape, F32)

    ucat = jnp.concatenate([u_ref[pl.ds(j, n_chunks, stride=SSM_CHUNK), :] for j in range(SSM_CHUNK)],
                           axis=1).astype(BF16)
    s_end = _dot(ucat, bm_ref[0])
    p_re, p_im = s_end[:, :half], s_end[:, half:]
    bp = bp_ref[0]
    sub = lax.broadcasted_iota(I32, (n_chunks, half), 0) % SUBLANES

    def shifted(v, d):
        return jnp.where(sub >= d, pltpu.roll(v, d, 0), 0.0)

    for d in (1, 2, 4):
        d_re, d_im = _cmul(bp[d:d + 1, :half], bp[d:d + 1, half:], shifted(p_re, d), shifted(p_im, d))
        p_re, p_im = p_re + d_re, p_im + d_im
    x_re, x_im = shifted(p_re, 1), shifted(p_im, 1)
    t_re, t_im = bp[:SUBLANES, :half], bp[:SUBLANES, half:]
    l_re, l_im = bp[SUBLANES:SUBLANES + 1, :half], bp[SUBLANES:SUBLANES + 1, half:]
    c_re, c_im = carry_ref[:, :half], carry_ref[:, half:]
    for g in range(n_chunks // SUBLANES):
        lo, hi = g * SUBLANES, (g + 1) * SUBLANES
        d_re, d_im = _cmul(t_re, t_im, c_re, c_im)
        sp_ref[lo:hi, :half] = x_re[lo:hi] + d_re
        sp_ref[lo:hi, half:] = x_im[lo:hi] + d_im
        e_re, e_im = _cmul(l_re, l_im, c_re, c_im)
        c_re, c_im = p_re[hi - 1:hi] + e_re, p_im[hi - 1:hi] + e_im
    carry_ref[...] = jnp.concatenate([c_re, c_im], axis=1)
    y = _dot(ucat, km_ref[0]) + _dot(sp_ref[...].astype(BF16), cm_ref[0])
    for t in range(SSM_CHUNK):
        y_ref[pl.ds(t, n_chunks, stride=SSM_CHUNK), :] = y[:, t * LANES:(t + 1) * LANES]


def _ssm(u, kmat, bmat, cmat, bpow, bsz, seq_len, tt):
    nt = seq_len // tt
    n_chunks = tt // SSM_CHUNK
    width = SSM_CHUNK * LANES
    u_map = lambda g, b, i: (b * nt + i, g)
    w_map = lambda g, b, i: (g, 0, 0)
    return pl.pallas_call(
        _ssm_kernel,
        out_shape=jax.ShapeDtypeStruct((bsz * seq_len, SSM_WIDTH), F32),
        grid=(SSM_LANE_BLOCKS, bsz, nt),
        in_specs=[pl.BlockSpec((tt, LANES), u_map),
                  pl.BlockSpec((1, width, width), w_map),
                  pl.BlockSpec((1, width, width), w_map),
                  pl.BlockSpec((1, width, width), w_map),
                  pl.BlockSpec((1, 2 * SUBLANES, width), w_map)],
        out_specs=pl.BlockSpec((tt, LANES), u_map),
        scratch_shapes=[pltpu.VMEM((1, width), F32),
                        pltpu.VMEM((n_chunks, width), F32)],
        compiler_params=pltpu.CompilerParams(
            dimension_semantics=("arbitrary", "arbitrary", "arbitrary"),
            vmem_limit_bytes=VMEM_LIMIT),
        name="ssm",
    )(u, kmat, bmat, cmat, bpow)


def _first_max(v, ids, n):
    m = jnp.max(v, axis=0, keepdims=True)
    ix = jnp.min(jnp.where(v == m, ids, n), axis=0, keepdims=True)
    return m, ix


def _mix_route_kernel(x_ref, y_ref, att_ref, mod_ref, wglu_ref, sng_ref, woa_ref, wos_ref, n2g_ref,
                      wrh_ref, wrl_ref, rb_ref, wgu_ref, wds_ref,
                      xs_ref, h2p_ref, eidx_ref, gate_ref, rank_ref, cnt_ref, carry_ref):
    tm = x_ref.shape[0]

    @pl.when(pl.program_id(0) == 0)
    def _():
        carry_ref[...] = jnp.zeros(carry_ref.shape, F32)

    mod = mod_ref[0]
    gate1, shift2, scale2, gate2 = mod[2:3], mod[3:4], mod[4:5], mod[5:6]

    y = y_ref[...]
    g = 0.5 * y * (1.0 + jnp.tanh(math.sqrt(2.0 / math.pi) * (y + 0.044715 * (y * y * y))))
    glu = g * _sigmoid(_dot(g.astype(BF16), wglu_ref[...]))
    ssm = glu * lax.rsqrt(jnp.mean(glu * glu, axis=-1, keepdims=True) + EPS) * sng_ref[...]

    mix = _dot(att_ref[...], woa_ref[...]) + _dot(ssm.astype(BF16), wos_ref[...])
    x1 = x_ref[...] + gate1 * mix
    h2 = x1 * lax.rsqrt(jnp.mean(x1 * x1, axis=-1, keepdims=True) + EPS) * n2g_ref[...]
    h2 = h2 * (1.0 + scale2) + shift2
    _store_rows(h2p_ref, _pack_rows(h2))
    hb, h_lo = _split_bf16(h2)

    gu = _dot(hb, wgu_ref[...])
    gs, us = gu[:, :SHARED_DIM], gu[:, SHARED_DIM:]
    act = (gs * _sigmoid(gs) * us).astype(BF16)
    xs_ref[...] = x1 + gate2 * _dot(act, wds_ref[...])

    wrh = wrh_ref[...]
    logits = (lax.dot_general(wrh, hb, NT_DIMS, preferred_element_type=F32)
              + lax.dot_general(wrh, h_lo, NT_DIMS, preferred_element_type=F32)
              + lax.dot_general(wrl_ref[...], hb, NT_DIMS, preferred_element_type=F32))
    score = _sigmoid(logits)
    biased = score + rb_ref[...]
    neg = -jnp.inf

    ids_g = lax.broadcasted_iota(I32, (EXPERTS_PER_GROUP, tm), 0)
    group_rows = []
    for gi in range(N_GROUPS):
        blk = biased[gi * EXPERTS_PER_GROUP:(gi + 1) * EXPERTS_PER_GROUP, :]
        m1, i1 = _first_max(blk, ids_g, EXPERTS_PER_GROUP)
        m2 = jnp.max(jnp.where(ids_g == i1, neg, blk), axis=0, keepdims=True)
        group_rows.append(m1 + m2)
    cur = jnp.concatenate(group_rows, axis=0)
    ids_8 = lax.broadcasted_iota(I32, cur.shape, 0)
    picked = jnp.zeros(cur.shape, F32)
    for _ in range(TOPK_GROUPS):
        _, ix = _first_max(cur, ids_8, N_GROUPS)
        hit = ids_8 == ix
        picked = jnp.where(hit, 1.0, picked)
        cur = jnp.where(hit, neg, cur)
    e_mask = jnp.concatenate(
        [jnp.broadcast_to(picked[gi:gi + 1, :], (EXPERTS_PER_GROUP, tm)) for gi in range(N_GROUPS)], axis=0)
    cand = jnp.where(e_mask > 0.0, biased, neg)

    ids_e = lax.broadcasted_iota(I32, cand.shape, 0)
    sel = jnp.zeros(cand.shape, F32)
    idx_rows, w_rows = [], []
    for _ in range(TOP_K):
        _, ix = _first_max(cand, ids_e, N_EXPERTS)
        hit = ids_e == ix
        idx_rows.append(ix)
        w_rows.append(jnp.sum(jnp.where(hit, score, 0.0), axis=0, keepdims=True))
        sel = jnp.where(hit, 1.0, sel)
        cand = jnp.where(hit, neg, cand)
    w_sum = w_rows[0]
    for w in w_rows[1:]:
        w_sum = w_sum + w
    gate_ref[...] = jnp.concatenate([w / w_sum * ROUTED_SCALE for w in w_rows], axis=0)
    eidx_ref[...] = jnp.concatenate(idx_rows, axis=0)

    t_row = lax.broadcasted_iota(I32, (tm, tm), 0)
    t_col = lax.broadcasted_iota(I32, (tm, tm), 1)
    earlier = jnp.where(t_row < t_col, 1.0, 0.0).astype(BF16)
    before = _dot(sel.astype(BF16), earlier) + carry_ref[...]
    rank_ref[...] = jnp.concatenate(
        [jnp.sum(jnp.where(ids_e == ix, before, 0.0), axis=0, keepdims=True) for ix in idx_rows],
        axis=0).astype(I32)
    carry_ref[...] = carry_ref[...] + jnp.sum(sel, axis=1, keepdims=True)
    cnt_ref[...] = carry_ref[...]


def _mix_route(x2d, y, att, mod3, wglu, sng, woa, wos, n2g, wrh, wrl, rb, wgu, wds, seq_len, tm):
    n_tok, d = x2d.shape
    tiles_per_seq = seq_len // tm
    row = lambda i: (i, 0)
    col = lambda i: (0, i)
    const = lambda i: (0, 0)
    full = lambda a: pl.BlockSpec(a.shape, const)
    return pl.pallas_call(
        _mix_route_kernel,
        out_shape=(jax.ShapeDtypeStruct((n_tok, d), F32),
                   jax.ShapeDtypeStruct((n_tok * ROW_TILE, LANES), U32),
                   jax.ShapeDtypeStruct((TOP_K, n_tok), I32),
                   jax.ShapeDtypeStruct((TOP_K, n_tok), F32),
                   jax.ShapeDtypeStruct((TOP_K, n_tok), I32),
                   jax.ShapeDtypeStruct((N_EXPERTS, 1), F32)),
        grid=(n_tok // tm,),
        in_specs=[pl.BlockSpec((tm, d), row),
                  pl.BlockSpec((tm, SSM_WIDTH), row),
                  pl.BlockSpec((tm, ATT_WIDTH), row),
                  pl.BlockSpec((1, 6, d), lambda i: (i // tiles_per_seq, 0, 0)),
                  full(wglu), full(sng), full(woa), full(wos), full(n2g),
                  full(wrh), full(wrl), full(rb), full(wgu), full(wds)],
        out_specs=(pl.BlockSpec((tm, d), row),
                   pl.BlockSpec((tm * ROW_TILE, LANES), row),
                   pl.BlockSpec((TOP_K, tm), col),
                   pl.BlockSpec((TOP_K, tm), col),
                   pl.BlockSpec((TOP_K, tm), col),
                   pl.BlockSpec((N_EXPERTS, 1), const)),
        scratch_shapes=[pltpu.VMEM((N_EXPERTS, 1), F32)],
        compiler_params=pltpu.CompilerParams(dimension_semantics=("arbitrary",),
                                             vmem_limit_bytes=VMEM_LIMIT),
        name="mix_route",
    )(x2d, y, att, mod3, wglu, sng, woa, wos, n2g, wrh, wrl, rb, wgu, wds)


ROW_TILE = 4
EXPERT_SUB = 256
COMBINE_PITCH = TOP_K + 1


def _pack_rows(v):
    half = v.shape[1] // 2
    return pltpu.pack_elementwise([v[:, :half], v[:, half:]], packed_dtype=BF16)


def _unpack_rows(w, index):
    return pltpu.unpack_elementwise(w, index=index, packed_dtype=BF16, unpacked_dtype=F32)


def _store_rows(ref, packed, first=0):
    m = packed.shape[0]
    for s in range(ROW_TILE):
        ref[pl.ds(first * ROW_TILE + s, m, stride=ROW_TILE), :] = packed[:, s * LANES:(s + 1) * LANES]


def _load_rows(ref, m, first=0):
    return jnp.concatenate([ref[pl.ds(first * ROW_TILE + s, m, stride=ROW_TILE), :] for s in range(ROW_TILE)],
                           axis=1)


def _row(ref, r):
    return ref.at[pl.ds(pl.multiple_of(r * ROW_TILE, ROW_TILE), ROW_TILE), :]


def _slot_kernel(e_ref, r_ref, ps_ref, d_ref):
    e = e_ref[...]
    ids = lax.broadcasted_iota(I32, (N_EXPERTS, e.shape[1]), 0)
    starts = ps_ref[...]
    rows = [jnp.sum(jnp.where(ids == e[k:k + 1, :], starts, 0.0), axis=0, keepdims=True)
            for k in range(TOP_K)]
    d_ref[...] = jnp.concatenate(rows, axis=0).astype(I32) + r_ref[...]


def _slots(eidx, rank, pad_starts, ts):
    n_tok = eidx.shape[1]
    col = lambda i: (0, i)
    return pl.pallas_call(
        _slot_kernel,
        out_shape=jax.ShapeDtypeStruct((TOP_K, n_tok), I32),
        grid=(n_tok // ts,),
        in_specs=[pl.BlockSpec((TOP_K, ts), col),
                  pl.BlockSpec((TOP_K, ts), col),
                  pl.BlockSpec((N_EXPERTS, 1), lambda i: (0, 0))],
        out_specs=pl.BlockSpec((TOP_K, ts), col),
        compiler_params=pltpu.CompilerParams(dimension_semantics=("arbitrary",),
                                             vmem_limit_bytes=VMEM_LIMIT),
        name="slots",
    )(eidx, rank, pad_starts.astype(F32).reshape(N_EXPERTS, 1))


def _dispatch_kernel(d_ref, h2p_ref, xs_ref, sem):
    td = d_ref.shape[1]

    def issue(t, carry):
        for k in range(TOP_K):
            pltpu.make_async_copy(_row(h2p_ref, t), _row(xs_ref, d_ref[k, t]), sem).start(priority=k % 2)
        return carry

    lax.fori_loop(0, td, issue, 0)
    for _ in range(TOP_K):
        pltpu.make_async_copy(h2p_ref, xs_ref.at[pl.ds(0, td * ROW_TILE), :], sem).wait()


def _dispatch(dest, h2p, n_slots, td):
    n_tok = dest.shape[1]
    return pl.pallas_call(
        _dispatch_kernel,
        out_shape=jax.ShapeDtypeStruct((n_slots * ROW_TILE, LANES), U32),
        grid=(n_tok // td,),
        in_specs=[pl.BlockSpec((TOP_K, td), lambda i: (0, i), memory_space=pltpu.SMEM),
                  pl.BlockSpec((td * ROW_TILE, LANES), lambda i: (i, 0))],
        out_specs=pl.BlockSpec(memory_space=pl.ANY),
        scratch_shapes=[pltpu.SemaphoreType.DMA],
        compiler_params=pltpu.CompilerParams(dimension_semantics=("arbitrary",),
                                             vmem_limit_bytes=VMEM_LIMIT),
        name="dispatch",
    )(dest, h2p)


def _expert_kernel(be_ref, nv_ref, nr_ref, nx_ref, last_ref, xs_hbm, wg_hbm, wu_hbm, wd_hbm, ys_hbm,
                   x_buf, y_buf, wg_f32, wu_f32, wd_f32, wg_bf, wu_bf, wd_bf, sem, x_sem, y_sem):
    i = pl.program_id(0)
    blk = x_buf.shape[1] // ROW_TILE
    sizes = range(EXPERT_SUB, blk + 1, EXPERT_SUB)
    expert = be_ref[i]
    n_valid = nv_ref[i]
    last = last_ref[0]
    slot = lax.rem(i, 2)
    changed = jnp.logical_or(i == 0, expert != be_ref[jnp.maximum(i - 1, 0)])

    def block_rows(ref, block, n_rows):
        return ref.at[pl.ds(pl.multiple_of(block * (blk * ROW_TILE), blk * ROW_TILE), n_rows * ROW_TILE), :]

    def x_copy(block, into, n_rows):
        return pltpu.make_async_copy(block_rows(xs_hbm, block, n_rows),
                                     x_buf.at[into, pl.ds(0, n_rows * ROW_TILE), :], x_sem.at[into])

    def y_copy(block, out_of, n_rows):
        return pltpu.make_async_copy(y_buf.at[out_of, pl.ds(0, n_rows * ROW_TILE), :],
                                     block_rows(ys_hbm, block, n_rows), y_sem.at[out_of])

    def for_rows(n, fn):
        for n_rows in sizes:
            pl.when(n == n_rows)(functools.partial(fn, n_rows))

    @pl.when(i == 0)
    def _():
        for_rows(nr_ref[0], lambda n_rows: x_copy(0, 0, n_rows).start())

    @pl.when(jnp.logical_and(i >= 2, i <= last))
    def _():
        for_rows(nr_ref[jnp.maximum(i - 2, 0)], lambda n_rows: y_copy(i - 2, slot, n_rows).wait())

    def weight_copies(e):
        return [pltpu.make_async_copy(wg_hbm.at[e], wg_f32, sem.at[0]),
                pltpu.make_async_copy(wu_hbm.at[e], wu_f32, sem.at[1]),
                pltpu.make_async_copy(wd_hbm.at[e], wd_f32, sem.at[2])]

    @pl.when(i == 0)
    def _():
        for cp in weight_copies(expert):
            cp.start()

    @pl.when(jnp.logical_and(n_valid > 0, changed))
    def _():
        for cp in weight_copies(expert):
            cp.wait()
        wg_bf[...] = wg_f32[...].astype(BF16)
        wu_bf[...] = wu_f32[...].astype(BF16)
        wd_bf[...] = wd_f32[...].astype(BF16)
        upcoming = nx_ref[i]

        @pl.when(upcoming >= 0)
        def _():
            for cp in weight_copies(upcoming):
                cp.start()

    def swiglu_rows(n_rows):
        x_copy(i, slot, n_rows).wait()
        upcoming_rows = nr_ref[jnp.minimum(i + 1, pl.num_programs(0) - 1)]
        for_rows(jnp.where(i < last, upcoming_rows, 0), lambda n_next: x_copy(i + 1, 1 - slot, n_next).start())
        packed = _load_rows(x_buf.at[slot], n_rows)
        x = jnp.concatenate([_unpack_rows(packed, 0), _unpack_rows(packed, 1)], axis=1)
        rows = lax.broadcasted_iota(I32, (n_rows, 1), 0)
        x = jnp.where(rows < n_valid, x, 0.0).astype(BF16)
        g = _dot(x, wg_bf[...])
        u = _dot(x, wu_bf[...])
        act = (g * _sigmoid(g) * u).astype(BF16)
        _store_rows(y_buf.at[slot], _pack_rows(_dot(act, wd_bf[...])))
        y_copy(i, slot, n_rows).start()

    for_rows(nr_ref[i], swiglu_rows)

    @pl.when(i == last)
    def _():
        for_rows(nr_ref[i], lambda n_rows: y_copy(i, slot, n_rows).wait())

        @pl.when(i >= 1)
        def _():
            for_rows(nr_ref[jnp.maximum(i - 1, 0)], lambda n_rows: y_copy(i - 1, 1 - slot, n_rows).wait())


def _experts(block_expert, block_valid, next_expert, last_block, xs, w_g, w_u, w_d, blk):
    n_blocks = block_expert.shape[0]
    d, de = w_g.shape[1], w_g.shape[2]
    block_rows = (block_valid + EXPERT_SUB - 1) // EXPERT_SUB * EXPERT_SUB
    hbm = pl.BlockSpec(memory_space=pl.ANY)
    return pl.pallas_call(
        _expert_kernel,
        out_shape=jax.ShapeDtypeStruct(xs.shape, U32),
        grid_spec=pltpu.PrefetchScalarGridSpec(
            num_scalar_prefetch=5,
            grid=(n_blocks,),
            in_specs=[hbm, hbm, hbm, hbm],
            out_specs=hbm,
            scratch_shapes=[pltpu.VMEM((2, blk * ROW_TILE, LANES), U32),
                            pltpu.VMEM((2, blk * ROW_TILE, LANES), U32),
                            pltpu.VMEM((d, de), F32),
                            pltpu.VMEM((d, de), F32),
                            pltpu.VMEM((de, d), F32),
                            pltpu.VMEM((d, de), BF16),
                            pltpu.VMEM((d, de), BF16),
                            pltpu.VMEM((de, d), BF16),
                            pltpu.SemaphoreType.DMA((3,)),
                            pltpu.SemaphoreType.DMA((2,)),
                            pltpu.SemaphoreType.DMA((2,))]),
        compiler_params=pltpu.CompilerParams(dimension_semantics=("arbitrary",),
                                             vmem_limit_bytes=VMEM_LIMIT),
        name="experts",
    )(block_expert, block_valid, block_rows, next_expert, last_block, xs, w_g, w_u, w_d)


def _combine_kernel(d_ref, dn_ref, gate_ref, xs_ref, mod_ref, ys_ref, o_ref, buf, sem):
    tc = d_ref.shape[1]
    i = pl.program_id(0)
    slot = lax.rem(i, 2)

    def gather(dest_ref, into):
        def issue(t, carry):
            for k in range(TOP_K):
                pltpu.make_async_copy(_row(ys_ref, dest_ref[k, t]), _row(buf.at[into], t * COMBINE_PITCH + k),
                                      sem.at[into]).start(priority=k % 2)
            return carry
        lax.fori_loop(0, tc, issue, 0)

    @pl.when(i == 0)
    def _():
        gather(d_ref, 0)

    @pl.when(i + 1 < pl.num_programs(0))
    def _():
        gather(dn_ref, 1 - slot)

    gathered = TOP_K * tc * ROW_TILE
    pltpu.make_async_copy(ys_ref.at[pl.ds(0, gathered), :], buf.at[slot, pl.ds(0, gathered), :],
                          sem.at[slot]).wait()

    gate2 = mod_ref[0][5:6]
    rows = buf.at[slot]
    acc = [jnp.zeros((tc, LANES), F32) for _ in range(2 * ROW_TILE)]
    for k in range(TOP_K):
        gk = jnp.transpose(jnp.broadcast_to(gate_ref[k:k + 1, :], (LANES, tc)))
        for s in range(ROW_TILE):
            w = rows[pl.ds(k * ROW_TILE + s, tc, stride=COMBINE_PITCH * ROW_TILE), :]
            acc[s] = acc[s] + gk * _unpack_rows(w, 0)
            acc[ROW_TILE + s] = acc[ROW_TILE + s] + gk * _unpack_rows(w, 1)
    o_ref[...] = xs_ref[...] + gate2 * jnp.concatenate(acc, axis=1)


def _combine(dest, gate, xs, mod3, ys, seq_len, tc):
    n_tok, d = xs.shape
    tiles_per_seq = seq_len // tc
    n_tiles = n_tok // tc
    return pl.pallas_call(
        _combine_kernel,
        out_shape=jax.ShapeDtypeStruct((n_tok, d), F32),
        grid=(n_tiles,),
        in_specs=[pl.BlockSpec((TOP_K, tc), lambda i: (0, i), memory_space=pltpu.SMEM),
                  pl.BlockSpec((TOP_K, tc), lambda i: (0, jnp.minimum(i + 1, n_tiles - 1)),
                               memory_space=pltpu.SMEM),
                  pl.BlockSpec((TOP_K, tc), lambda i: (0, i)),
                  pl.BlockSpec((tc, d), lambda i: (i, 0)),
                  pl.BlockSpec((1, 6, d), lambda i: (i // tiles_per_seq, 0, 0)),
                  pl.BlockSpec(memory_space=pl.ANY)],
        out_specs=pl.BlockSpec((tc, d), lambda i: (i, 0)),
        scratch_shapes=[pltpu.VMEM((2, COMBINE_PITCH * tc * ROW_TILE, LANES), U32),
                        pltpu.SemaphoreType.DMA((2,))],
        compiler_params=pltpu.CompilerParams(dimension_semantics=("arbitrary",),
                                             vmem_limit_bytes=VMEM_LIMIT),
        name="combine",
    )(dest, dest, gate, xs, mod3, ys)


def _tiles(seq_len):
    pick = lambda pref: min(pref, seq_len)
    return dict(inproj=pick(512), attn_q=pick(512), attn_k=pick(512), ssm=pick(2048), mix=pick(512),
                dispatch=pick(512), combine=pick(256), slots=pick(2048), expert_block=4 * EXPERT_SUB)


def _rope_tables(seq_len):
    half = ATT_HEAD_DIM // 2
    inv_freq = 1.0 / (ROPE_THETA ** (jnp.arange(0, ATT_HEAD_DIM, 2, dtype=F32) / ATT_HEAD_DIM))
    ang = jnp.arange(seq_len, dtype=F32)[:, None] * inv_freq[None, :]
    cos, sin = jnp.cos(ang), jnp.sin(ang)
    reps = LANES // half
    sign = jnp.tile(jnp.concatenate([-jnp.ones((half,), F32), jnp.ones((half,), F32)]), reps // 2)
    return jnp.tile(cos, (1, reps)), jnp.tile(sin, (1, reps)) * sign[None, :]


def _block_tables(counts, blk, n_blocks):
    padded = (counts + blk - 1) // blk * blk
    pad_ends = jnp.cumsum(padded)
    pad_starts = pad_ends - padded
    used = pad_ends[-1] // blk
    last = jnp.maximum(used - 1, 0)
    starts = jnp.arange(n_blocks, dtype=I32) * blk
    expert = jnp.sum((pad_ends[None, :] <= starts[:, None]).astype(I32), axis=1)
    expert = jnp.minimum(expert, N_EXPERTS - 1)
    expert = jnp.where(starts < pad_ends[-1], expert, expert[last])
    valid = jnp.clip(counts[expert] - (starts - pad_starts[expert]), 0, blk)
    valid = jnp.where(starts < pad_ends[-1], valid, 0).astype(I32)
    ids = jnp.arange(N_EXPERTS, dtype=I32)
    later_used = jnp.logical_and(ids[None, :] > expert[:, None], (counts > 0)[None, :])
    upcoming = jnp.min(jnp.where(later_used, ids[None, :], N_EXPERTS), axis=1)
    upcoming = jnp.where(upcoming < N_EXPERTS, upcoming, -1).astype(I32)
    return pad_starts.astype(I32), expert, valid, upcoming, last.reshape(1).astype(I32)


def kernel(x, c, norm1_g, norm2_g, w_ada, b_ada, w_in, q_norm_g, k_norm_g, lambda_q1, lambda_k1, lambda_q2, lambda_k2, subln_g, ssm_a_re, ssm_a_im, ssm_log_dt, ssm_b_re, ssm_b_im, ssm_c_re, ssm_c_im, ssm_d, w_glu, ssm_norm_g, w_out, w_router, router_bias, w_gate_e, w_up_e, w_down_e, w_gate_s, w_up_s, w_down_s):
    bsz, seq_len, d = x.shape
    n_tok = bsz * seq_len
    tiles = _tiles(seq_len)
    blk = tiles['expert_block']
    n_blocks = (n_tok * TOP_K + N_EXPERTS * (blk - 1) + blk - 1) // blk
    cos_t, sin_t = _rope_tables(seq_len)
    seg = jnp.kron(jnp.eye(QK_WIDTH // ATT_HEAD_DIM, dtype=F32),
                   jnp.full((ATT_HEAD_DIM, ATT_HEAD_DIM), 1.0 / ATT_HEAD_DIM, F32)).astype(BF16)
    reps = QK_WIDTH // ATT_HEAD_DIM
    x2d = x.reshape(n_tok, d).astype(F32)
    for layer in range(w_ada.shape[0]):
        lam_init = 0.8 - 0.6 * math.exp(-0.3 * layer)
        mod3 = _adaln(c.astype(F32), w_ada[layer].astype(F32), b_ada[layer].astype(F32)).reshape(bsz, 6, d)
        q, k, v, u = _inproj(
            x2d, mod3, norm1_g[layer].astype(F32).reshape(1, d), w_in[layer].astype(BF16), seg,
            jnp.tile(q_norm_g[layer].astype(F32), reps).reshape(1, QK_WIDTH),
            jnp.tile(k_norm_g[layer].astype(F32), reps).reshape(1, QK_WIDTH),
            cos_t, sin_t, seq_len, tiles['inproj'])
        lam = (jnp.exp(jnp.sum(lambda_q1[layer].astype(F32) * lambda_k1[layer].astype(F32)))
               - jnp.exp(jnp.sum(lambda_q2[layer].astype(F32) * lambda_k2[layer].astype(F32))) + lam_init)
        att = _attention(q, k, v, jnp.full((1, LANES), lam, F32),
                         subln_g[layer].astype(F32).reshape(1, ATT_V_DIM),
                         bsz, seq_len, tiles['attn_q'], tiles['attn_k'], 1.0 - lam_init)
        kmat, bmat, cmat, bpow = _ssm_mats(ssm_a_re[layer], ssm_a_im[layer], ssm_log_dt[layer],
                                         ssm_b_re[layer], ssm_b_im[layer], ssm_c_re[layer],
                                         ssm_c_im[layer], ssm_d[layer])
        y = _ssm(u, kmat, bmat, cmat, bpow, bsz, seq_len, tiles['ssm'])
        wr_t = jnp.transpose(w_router[layer].astype(F32))
        wrh, wrl = _split_bf16(wr_t)
        xs, h2p, eidx, gate, rank, counts = _mix_route(
            x2d, y, att, mod3, w_glu[layer].astype(BF16),
            ssm_norm_g[layer].astype(F32).reshape(1, SSM_WIDTH),
            w_out[layer, :ATT_WIDTH].astype(BF16), w_out[layer, ATT_WIDTH:].astype(BF16),
            norm2_g[layer].astype(F32).reshape(1, d), wrh, wrl,
            router_bias[layer].astype(F32).reshape(N_EXPERTS, 1),
            jnp.concatenate([w_gate_s[layer], w_up_s[layer]], axis=1).astype(BF16),
            w_down_s[layer].astype(BF16), seq_len, tiles['mix'])
        pad_starts, block_expert, block_valid, next_expert, last_block = _block_tables(
            counts.reshape(N_EXPERTS).astype(I32), blk, n_blocks)
        dest = _slots(eidx, rank, pad_starts, tiles['slots'])
        x_slots = _dispatch(dest, h2p, n_blocks * blk, tiles['dispatch'])
        y_slots = _experts(block_expert, block_valid, next_expert, last_block, x_slots,
                           w_gate_e[layer], w_up_e[layer], w_down_e[layer], blk)
        x2d = _combine(dest, gate, xs, mod3, y_slots, seq_len, tiles['combine'])
    return x2d.reshape(bsz, seq_len, d).astype(x.dtype)
```

```python
import functools
import math

import jax
import jax.numpy as jnp
from jax import lax
from jax.experimental import pallas as pl
from jax.experimental.pallas import tpu as pltpu

F32 = jnp.float32
BF16 = jnp.bfloat16
I32 = jnp.int32
U32 = jnp.uint32

LANES = 128
SUBLANES = 8
V7X_VMEM_BYTES = 64 * 1024 * 1024

N_ATT_HEADS = 4
ATT_HEAD_DIM = 64
ATT_V_DIM = 2 * ATT_HEAD_DIM
QK_WIDTH = N_ATT_HEADS * 2 * ATT_HEAD_DIM
ATT_WIDTH = N_ATT_HEADS * ATT_V_DIM
ROPE_THETA = 10000.0
SSM_GROUP = 16
SSM_GROUPS = 32
SSM_STATE = 64
SSM_WIDTH = SSM_GROUPS * SSM_GROUP
SSM_CHUNK = SUBLANES
SSM_LANE_BLOCKS = SSM_WIDTH // LANES
GROUPS_PER_BLOCK = LANES // SSM_GROUP
N_EXPERTS = 256
TOP_K = 8
N_GROUPS = 8
TOPK_GROUPS = 4
EXPERTS_PER_GROUP = N_EXPERTS // N_GROUPS
SHARED_DIM = 256
ROUTED_SCALE = 2.5
EPS = 1e-6

NT_DIMS = (((1,), (1,)), ((), ()))

VMEM_LIMIT = V7X_VMEM_BYTES * 3 // 4


def _dot(a, b):
    return jnp.dot(a, b, preferred_element_type=F32)


def _sigmoid(x):
    return 1.0 / (1.0 + jnp.exp(-x))


def _split_bf16(x):
    hi = x.astype(BF16)
    lo = (x - hi.astype(F32)).astype(BF16)
    return hi, lo


def _adaln_kernel(c_ref, w_ref, b_ref, o_ref):
    c = c_ref[...]
    sc = c * _sigmoid(c)
    o_ref[...] = jnp.dot(sc, w_ref[...], preferred_element_type=F32,
                         precision=lax.Precision.HIGHEST) + b_ref[...]


def _adaln(c, w, b, tn):
    bsz, d = c.shape
    n = w.shape[1]
    return pl.pallas_call(
        _adaln_kernel,
        out_shape=jax.ShapeDtypeStruct((bsz, n), F32),
        grid=(n // tn,),
        in_specs=[pl.BlockSpec((bsz, d), lambda j: (0, 0)),
                  pl.BlockSpec((d, tn), lambda j: (0, j)),
                  pl.BlockSpec((1, tn), lambda j: (0, j))],
        out_specs=pl.BlockSpec((bsz, tn), lambda j: (0, j)),
        compiler_params=pltpu.CompilerParams(dimension_semantics=("arbitrary",),
                                             vmem_limit_bytes=VMEM_LIMIT),
        name="adaln",
    )(c, w, b.reshape(1, n))


def _inproj_kernel(x_ref, mod_ref, n1g_ref, w_ref, seg_ref, qg_ref, kg_ref, cos_ref, sin_ref,
                   q_ref, k_ref, v_ref, u_ref):
    x = x_ref[...]
    mod = mod_ref[0]
    shift, scale = mod[0:1], mod[1:2]
    ms = jnp.mean(x * x, axis=-1, keepdims=True)
    h = x * lax.rsqrt(ms + EPS) * n1g_ref[...]
    h = h * (1.0 + scale) + shift
    z = _dot(h.astype(BF16), w_ref[...])
    seg = seg_ref[...]
    cos = cos_ref[...]
    sin = sin_ref[...]
    lane = lax.broadcasted_iota(I32, cos.shape, 1)
    first_half = (lane % ATT_HEAD_DIM) < (ATT_HEAD_DIM // 2)

    def norm_rope(t, g_ref, out_scale):
        hi, lo = _split_bf16(t * t)
        msq = _dot(hi, seg) + _dot(lo, seg)
        tn = t * lax.rsqrt(msq + EPS) * g_ref[...]
        outs = []
        for hd in range(N_ATT_HEADS):
            th = tn[:, hd * LANES:(hd + 1) * LANES]
            partner = jnp.where(first_half,
                                pltpu.roll(th, LANES - ATT_HEAD_DIM // 2, 1),
                                pltpu.roll(th, ATT_HEAD_DIM // 2, 1))
            outs.append((th * cos + partner * sin) * out_scale)
        return jnp.concatenate(outs, axis=1)

    q_ref[...] = norm_rope(z[:, :QK_WIDTH], qg_ref, ATT_HEAD_DIM ** -0.5).astype(BF16)
    k_ref[...] = norm_rope(z[:, QK_WIDTH:2 * QK_WIDTH], kg_ref, 1.0).astype(BF16)
    v_ref[...] = z[:, 2 * QK_WIDTH:2 * QK_WIDTH + ATT_WIDTH].astype(BF16)
    u_ref[...] = z[:, 2 * QK_WIDTH + ATT_WIDTH:]


def _inproj(x2d, mod3, n1g, w_in_bf, seg, qg, kg, cos_t, sin_t, seq_len, tm):
    n_tok, d = x2d.shape
    tiles_per_seq = seq_len // tm
    in_width = w_in_bf.shape[1]
    row = lambda i: (i, 0)
    const = lambda i: (0, 0)
    return pl.pallas_call(
        _inproj_kernel,
        out_shape=(jax.ShapeDtypeStruct((n_tok, QK_WIDTH), BF16),
                   jax.ShapeDtypeStruct((n_tok, QK_WIDTH), BF16),
                   jax.ShapeDtypeStruct((n_tok, ATT_WIDTH), BF16),
                   jax.ShapeDtypeStruct((n_tok, SSM_WIDTH), F32)),
        grid=(n_tok // tm,),
        in_specs=[pl.BlockSpec((tm, d), row),
                  pl.BlockSpec((1, 6, d), lambda i: (i // tiles_per_seq, 0, 0)),
                  pl.BlockSpec((1, d), const),
                  pl.BlockSpec((d, in_width), const),
                  pl.BlockSpec((QK_WIDTH, QK_WIDTH), const),
                  pl.BlockSpec((1, QK_WIDTH), const),
                  pl.BlockSpec((1, QK_WIDTH), const),
                  pl.BlockSpec((tm, LANES), lambda i: (i % tiles_per_seq, 0)),
                  pl.BlockSpec((tm, LANES), lambda i: (i % tiles_per_seq, 0))],
        out_specs=(pl.BlockSpec((tm, QK_WIDTH), row),
                   pl.BlockSpec((tm, QK_WIDTH), row),
                   pl.BlockSpec((tm, ATT_WIDTH), row),
                   pl.BlockSpec((tm, SSM_WIDTH), row)),
        compiler_params=pltpu.CompilerParams(dimension_semantics=("arbitrary",),
                                             vmem_limit_bytes=VMEM_LIMIT),
        name="inproj",
    )(x2d, mod3, n1g, w_in_bf, seg, qg, kg, cos_t, sin_t)


def _attn_kernel(qi_ref, kj_ref, q_ref, k_ref, v_ref, lam_ref, sg_ref, o_ref,
                 m_ref, l_ref, acc_ref, *, out_scale):
    p = pl.program_id(2)
    qi = qi_ref[p]
    kj = kj_ref[p]

    @pl.when(kj == 0)
    def _():
        m_ref[...] = jnp.full(m_ref.shape, -jnp.inf, F32)
        l_ref[...] = jnp.zeros(l_ref.shape, F32)
        acc_ref[...] = jnp.zeros(acc_ref.shape, F32)

    tq = q_ref.shape[0]
    tk = k_ref.shape[0]
    first_diag = qi * (tq // tk)

    def step(masked):
        q = q_ref[...]
        lane = lax.broadcasted_iota(I32, q.shape, 1)
        zero = jnp.zeros_like(q)
        qq = jnp.concatenate([jnp.where(lane < ATT_HEAD_DIM, q, zero),
                              jnp.where(lane >= ATT_HEAD_DIM, q, zero)], axis=0)
        s = lax.dot_general(qq, k_ref[...], NT_DIMS, preferred_element_type=F32)
        if masked:
            r = lax.broadcasted_iota(I32, (tq, tk), 0)
            col = lax.broadcasted_iota(I32, (tq, tk), 1) + (kj - first_diag) * tk
            causal = col <= r
            s = jnp.where(jnp.concatenate([causal, causal], axis=0), s, -jnp.inf)
        m_old = m_ref[...]
        m_new = jnp.maximum(m_old, jnp.max(s, axis=-1, keepdims=True))
        alpha = jnp.exp(m_old - m_new)
        pr = jnp.exp(s - jnp.concatenate([m_new] * (s.shape[1] // LANES), axis=1))
        l_ref[...] = alpha * l_ref[...] + jnp.sum(pr, axis=-1, keepdims=True)
        acc_ref[...] = alpha * acc_ref[...] + _dot(pr.astype(BF16), v_ref[...])
        m_ref[...] = m_new

    @pl.when(kj < first_diag)
    def _():
        step(False)

    @pl.when(kj >= first_diag)
    def _():
        step(True)

    @pl.when(kj == first_diag + tq // tk - 1)
    def _():
        o = acc_ref[...] / l_ref[...]
        o = o[:tq] - lam_ref[...] * o[tq:]
        ms = jnp.mean(o * o, axis=-1, keepdims=True)
        o_ref[...] = (o * lax.rsqrt(ms + EPS) * sg_ref[...] * out_scale).astype(o_ref.dtype)


def _attention(q, k, v, lam_row, subln_g, bsz, seq_len, tq, tk, out_scale):
    nq, nk = seq_len // tq, seq_len // tk
    pairs = [(i, j) for i in range(nq) for j in range((i + 1) * (tq // tk))]
    qi = jnp.asarray([p[0] for p in pairs], I32)
    kj = jnp.asarray([p[1] for p in pairs], I32)
    q_map = lambda b, h, p, qi, kj: (b * nq + qi[p], h)
    k_map = lambda b, h, p, qi, kj: (b * nk + kj[p], h)
    const = lambda b, h, p, qi, kj: (0, 0)
    return pl.pallas_call(
        functools.partial(_attn_kernel, out_scale=out_scale),
        out_shape=jax.ShapeDtypeStruct((bsz * seq_len, ATT_WIDTH), BF16),
        grid_spec=pltpu.PrefetchScalarGridSpec(
            num_scalar_prefetch=2,
            grid=(bsz, N_ATT_HEADS, len(pairs)),
            in_specs=[pl.BlockSpec((tq, LANES), q_map),
                      pl.BlockSpec((tk, LANES), k_map),
                      pl.BlockSpec((tk, LANES), k_map),
                      pl.BlockSpec((1, LANES), const),
                      pl.BlockSpec((1, LANES), const)],
            out_specs=pl.BlockSpec((tq, LANES), q_map),
            scratch_shapes=[pltpu.VMEM((2 * tq, LANES), F32),
                            pltpu.VMEM((2 * tq, LANES), F32),
                            pltpu.VMEM((2 * tq, ATT_V_DIM), F32)]),
        compiler_params=pltpu.CompilerParams(
            dimension_semantics=("arbitrary", "arbitrary", "arbitrary"),
            vmem_limit_bytes=VMEM_LIMIT),
        name="attn",
    )(qi, kj, q, k, v, lam_row, subln_g)


def _ssm_mats(a_re, a_im, log_dt, b_re, b_im, c_re, c_im, d_skip):
    a_re, a_im, b_re, b_im, c_re, c_im, d_skip = (
        t.astype(F32) for t in (a_re, a_im, b_re, b_im, c_re, c_im, d_skip))
    dt = jnp.exp(log_dt.astype(F32))[:, None]
    mag = jnp.exp(a_re * dt)
    abar_re = mag * jnp.cos(a_im * dt)
    abar_im = mag * jnp.sin(a_im * dt)
    den = a_re * a_re + a_im * a_im
    nr = abar_re - 1.0
    f_re = ((nr * a_re + abar_im * a_im) / den)[..., None]
    f_im = ((abar_im * a_re - nr * a_im) / den)[..., None]
    bb_re = f_re * b_re - f_im * b_im
    bb_im = f_re * b_im + f_im * b_re
    steps = jnp.arange(SSM_CHUNK + 1, dtype=F32)[:, None, None]
    pmag = jnp.exp(a_re * dt * steps)
    pw_re = pmag * jnp.cos(a_im * dt * steps)
    pw_im = pmag * jnp.sin(a_im * dt * steps)
    ca_re = c_re[None] * pw_re[:, :, None, :] - c_im[None] * pw_im[:, :, None, :]
    ca_im = c_re[None] * pw_im[:, :, None, :] + c_im[None] * pw_re[:, :, None, :]
    hp = lax.Precision.HIGHEST
    lag = (jnp.einsum('mgcp,gpd->mgcd', ca_re[:SSM_CHUNK], bb_re, precision=hp)
           - jnp.einsum('mgcp,gpd->mgcd', ca_im[:SSM_CHUNK], bb_im, precision=hp))
    lag = lag.at[0].add(d_skip[:, :, None] * jnp.eye(SSM_GROUP, dtype=F32)[None])
    nb, gpb = SSM_LANE_BLOCKS, GROUPS_PER_BLOCK
    n_state = gpb * SSM_STATE
    in_group = jnp.arange(LANES) // SSM_GROUP
    state_group = jnp.arange(n_state) // SSM_STATE

    def spread(small, row_group, col_group):
        w = small.shape[-1]
        tiled = jnp.tile(jnp.eye(w, dtype=BF16), (1, col_group.shape[0] // w))
        wide = jnp.einsum('...w,wn->...n', small.astype(BF16), tiled, preferred_element_type=F32)
        return jnp.where(row_group[:, None] == col_group[None, :], wide, 0.0).astype(BF16)

    lag_blocks = spread(jnp.transpose(lag, (0, 1, 3, 2)).reshape(SSM_CHUNK, nb, LANES, SSM_GROUP),
                        in_group, in_group)
    zero_block = jnp.zeros_like(lag_blocks[0])
    kmat = jnp.concatenate(
        [jnp.concatenate([lag_blocks[t - j] if t >= j else zero_block for t in range(SSM_CHUNK)], axis=-1)
         for j in range(SSM_CHUNK)], axis=1)
    rev = SSM_CHUNK - 1 - jnp.arange(SSM_CHUNK)
    w_re = pw_re[rev][..., None] * bb_re[None] - pw_im[rev][..., None] * bb_im[None]
    w_im = pw_re[rev][..., None] * bb_im[None] + pw_im[rev][..., None] * bb_re[None]

    def in_to_state(w):
        small = jnp.transpose(w, (0, 1, 3, 2)).reshape(SSM_CHUNK, nb, LANES, SSM_STATE)
        return spread(small, in_group, state_group)

    bmat = jnp.concatenate([in_to_state(w_re), in_to_state(w_im)], axis=-1)
    bmat = jnp.transpose(bmat, (1, 0, 2, 3)).reshape(nb, SSM_CHUNK * LANES, 2 * n_state)

    def state_to_out(ca):
        small = jnp.transpose(ca[1:], (0, 1, 3, 2)).reshape(SSM_CHUNK, nb, n_state, SSM_GROUP)
        blocks = spread(small, state_group, in_group)
        return jnp.transpose(blocks, (1, 2, 0, 3)).reshape(nb, n_state, SSM_CHUNK * LANES)

    cmat = jnp.concatenate([state_to_out(ca_re), -state_to_out(ca_im)], axis=1)
    chunk_steps = SSM_CHUNK * jnp.arange(2 * SUBLANES, dtype=F32)[:, None, None]
    cmag = jnp.exp(a_re * dt * chunk_steps)
    to_block = lambda t: jnp.transpose(t.reshape(2 * SUBLANES, nb, n_state), (1, 0, 2))
    bpow = jnp.concatenate([to_block(cmag * jnp.cos(a_im * dt * chunk_steps)),
                            to_block(cmag * jnp.sin(a_im * dt * chunk_steps))], axis=-1)
    return kmat, bmat, cmat, bpow


def _cmul(a_re, a_im, x_re, x_im):
    return a_re * x_re - a_im * x_im, a_re * x_im + a_im * x_re


def _ssm_kernel(u_ref, km_ref, bm_ref, cm_ref, bp_ref, y_ref, carry_ref, sp_ref):
    n_chunks = sp_ref.shape[0]
    half = sp_ref.shape[1] // 2

    @pl.when(pl.program_id(2) == 0)
    def _():
        carry_ref[...] = jnp.zeros(carry_ref.shape, F32)

    ucat = jnp.concatenate([u_ref[pl.ds(j, n_chunks, stride=SSM_CHUNK), :] for j in range(SSM_CHUNK)],
                           axis=1).astype(BF16)
    s_end = _dot(ucat, bm_ref[0])
    p_re, p_im = s_end[:, :half], s_end[:, half:]
    bp = bp_ref[0]
    sub = lax.broadcasted_iota(I32, (n_chunks, half), 0) % SUBLANES

    def shifted(v, d):
        return jnp.where(sub >= d, pltpu.roll(v, d, 0), 0.0)

    for d in (1, 2, 4):
        d_re, d_im = _cmul(bp[d:d + 1, :half], bp[d:d + 1, half:], shifted(p_re, d), shifted(p_im, d))
        p_re, p_im = p_re + d_re, p_im + d_im
    x_re, x_im = shifted(p_re, 1), shifted(p_im, 1)
    t_re, t_im = bp[:SUBLANES, :half], bp[:SUBLANES, half:]
    l_re, l_im = bp[SUBLANES:SUBLANES + 1, :half], bp[SUBLANES:SUBLANES + 1, half:]
    c_re, c_im = carry_ref[:, :half], carry_ref[:, half:]
    for g in range(n_chunks // SUBLANES):
        lo, hi = g * SUBLANES, (g + 1) * SUBLANES
        d_re, d_im = _cmul(t_re, t_im, c_re, c_im)
        sp_ref[lo:hi, :half] = x_re[lo:hi] + d_re
        sp_ref[lo:hi, half:] = x_im[lo:hi] + d_im
        e_re, e_im = _cmul(l_re, l_im, c_re, c_im)
        c_re, c_im = p_re[hi - 1:hi] + e_re, p_im[hi - 1:hi] + e_im
    carry_ref[...] = jnp.concatenate([c_re, c_im], axis=1)
    y = _dot(ucat, km_ref[0]) + _dot(sp_ref[...].astype(BF16), cm_ref[0])
    for t in range(SSM_CHUNK):
        y_ref[pl.ds(t, n_chunks, stride=SSM_CHUNK), :] = y[:, t * LANES:(t + 1) * LANES]


def _ssm(u, kmat, bmat, cmat, bpow, bsz, seq_len, tt):
    nt = seq_len // tt
    n_chunks = tt // SSM_CHUNK
    width = SSM_CHUNK * LANES
    u_map = lambda g, b, i: (b * nt + i, g)
    w_map = lambda g, b, i: (g, 0, 0)
    return pl.pallas_call(
        _ssm_kernel,
        out_shape=jax.ShapeDtypeStruct((bsz * seq_len, SSM_WIDTH), F32),
        grid=(SSM_LANE_BLOCKS, bsz, nt),
        in_specs=[pl.BlockSpec((tt, LANES), u_map),
                  pl.BlockSpec((1, width, width), w_map),
                  pl.BlockSpec((1, width, width), w_map),
                  pl.BlockSpec((1, width, width), w_map),
                  pl.BlockSpec((1, 2 * SUBLANES, width), w_map)],
        out_specs=pl.BlockSpec((tt, LANES), u_map),
        scratch_shapes=[pltpu.VMEM((1, width), F32),
                        pltpu.VMEM((n_chunks, width), F32)],
        compiler_params=pltpu.CompilerParams(
            dimension_semantics=("arbitrary", "arbitrary", "arbitrary"),
            vmem_limit_bytes=VMEM_LIMIT),
        name="ssm",
    )(u, kmat, bmat, cmat, bpow)


def _first_max(v, ids, n):
    m = jnp.max(v, axis=0, keepdims=True)
    ix = jnp.min(jnp.where(v == m, ids, n), axis=0, keepdims=True)
    return m, ix


def _mix_route_kernel(x_ref, y_ref, att_ref, mod_ref, wglu_ref, sng_ref, woa_ref, wos_ref, n2g_ref,
                      wrh_ref, wrl_ref, rb_ref, wgu_ref, wds_ref,
                      xs_ref, h2p_ref, eidx_ref, gate_ref, rank_ref, cnt_ref, carry_ref):
    tm = x_ref.shape[0]

    @pl.when(pl.program_id(0) == 0)
    def _():
        carry_ref[...] = jnp.zeros(carry_ref.shape, F32)

    mod = mod_ref[0]
    gate1, shift2, scale2, gate2 = mod[2:3], mod[3:4], mod[4:5], mod[5:6]

    y = y_ref[...]
    g = 0.5 * y * (1.0 + jnp.tanh(math.sqrt(2.0 / math.pi) * (y + 0.044715 * (y * y * y))))
    glu = g * _sigmoid(_dot(g.astype(BF16), wglu_ref[...]))
    ssm = glu * lax.rsqrt(jnp.mean(glu * glu, axis=-1, keepdims=True) + EPS) * sng_ref[...]

    mix = _dot(att_ref[...], woa_ref[...]) + _dot(ssm.astype(BF16), wos_ref[...])
    x1 = x_ref[...] + gate1 * mix
    h2 = x1 * lax.rsqrt(jnp.mean(x1 * x1, axis=-1, keepdims=True) + EPS) * n2g_ref[...]
    h2 = h2 * (1.0 + scale2) + shift2
    _store_rows(h2p_ref, _pack_rows(h2))
    hb, h_lo = _split_bf16(h2)

    gu = _dot(hb, wgu_ref[...])
    gs, us = gu[:, :SHARED_DIM], gu[:, SHARED_DIM:]
    act = (gs * _sigmoid(gs) * us).astype(BF16)
    xs_ref[...] = x1 + gate2 * _dot(act, wds_ref[...])

    wrh = wrh_ref[...]
    logits = (lax.dot_general(wrh, hb, NT_DIMS, preferred_element_type=F32)
              + lax.dot_general(wrh, h_lo, NT_DIMS, preferred_element_type=F32)
              + lax.dot_general(wrl_ref[...], hb, NT_DIMS, preferred_element_type=F32))
    score = _sigmoid(logits)
    biased = score + rb_ref[...]
    neg = -jnp.inf

    ids_g = lax.broadcasted_iota(I32, (EXPERTS_PER_GROUP, tm), 0)
    group_rows = []
    for gi in range(N_GROUPS):
        blk = biased[gi * EXPERTS_PER_GROUP:(gi + 1) * EXPERTS_PER_GROUP, :]
        m1, i1 = _first_max(blk, ids_g, EXPERTS_PER_GROUP)
        m2 = jnp.max(jnp.where(ids_g == i1, neg, blk), axis=0, keepdims=True)
        group_rows.append(m1 + m2)
    cur = jnp.concatenate(group_rows, axis=0)
    ids_8 = lax.broadcasted_iota(I32, cur.shape, 0)
    picked = jnp.zeros(cur.shape, F32)
    for _ in range(TOPK_GROUPS):
        _, ix = _first_max(cur, ids_8, N_GROUPS)
        hit = ids_8 == ix
        picked = jnp.where(hit, 1.0, picked)
        cur = jnp.where(hit, neg, cur)
    e_mask = jnp.concatenate(
        [jnp.broadcast_to(picked[gi:gi + 1, :], (EXPERTS_PER_GROUP, tm)) for gi in range(N_GROUPS)], axis=0)
    cand = jnp.where(e_mask > 0.0, biased, neg)

    ids_e = lax.broadcasted_iota(I32, cand.shape, 0)
    sel = jnp.zeros(cand.shape, F32)
    idx_rows, w_rows = [], []
    for _ in range(TOP_K):
        _, ix = _first_max(cand, ids_e, N_EXPERTS)
        hit = ids_e == ix
        idx_rows.append(ix)
        w_rows.append(jnp.sum(jnp.where(hit, score, 0.0), axis=0, keepdims=True))
        sel = jnp.where(hit, 1.0, sel)
        cand = jnp.where(hit, neg, cand)
    w_sum = w_rows[0]
    for w in w_rows[1:]:
        w_sum = w_sum + w
    gate_ref[...] = jnp.concatenate([w / w_sum * ROUTED_SCALE for w in w_rows], axis=0)
    eidx_ref[...] = jnp.concatenate(idx_rows, axis=0)

    t_row = lax.broadcasted_iota(I32, (tm, tm), 0)
    t_col = lax.broadcasted_iota(I32, (tm, tm), 1)
    earlier = jnp.where(t_row < t_col, 1.0, 0.0).astype(BF16)
    before = _dot(sel.astype(BF16), earlier) + carry_ref[...]
    rank_ref[...] = jnp.concatenate(
        [jnp.sum(jnp.where(ids_e == ix, before, 0.0), axis=0, keepdims=True) for ix in idx_rows],
        axis=0).astype(I32)
    carry_ref[...] = carry_ref[...] + jnp.sum(sel, axis=1, keepdims=True)
    cnt_ref[...] = carry_ref[...]


def _mix_route(x2d, y, att, mod3, wglu, sng, woa, wos, n2g, wrh, wrl, rb, wgu, wds, seq_len, tm):
    n_tok, d = x2d.shape
    tiles_per_seq = seq_len // tm
    row = lambda i: (i, 0)
    col = lambda i: (0, i)
    const = lambda i: (0, 0)
    full = lambda a: pl.BlockSpec(a.shape, const)
    return pl.pallas_call(
        _mix_route_kernel,
        out_shape=(jax.ShapeDtypeStruct((n_tok, d), F32),
                   jax.ShapeDtypeStruct((n_tok * ROW_TILE, LANES), U32),
                   jax.ShapeDtypeStruct((TOP_K, n_tok), I32),
                   jax.ShapeDtypeStruct((TOP_K, n_tok), F32),
                   jax.ShapeDtypeStruct((TOP_K, n_tok), I32),
                   jax.ShapeDtypeStruct((N_EXPERTS, 1), F32)),
        grid=(n_tok // tm,),
        in_specs=[pl.BlockSpec((tm, d), row),
                  pl.BlockSpec((tm, SSM_WIDTH), row),
                  pl.BlockSpec((tm, ATT_WIDTH), row),
                  pl.BlockSpec((1, 6, d), lambda i: (i // tiles_per_seq, 0, 0)),
                  full(wglu), full(sng), full(woa), full(wos), full(n2g),
                  full(wrh), full(wrl), full(rb), full(wgu), full(wds)],
        out_specs=(pl.BlockSpec((tm, d), row),
                   pl.BlockSpec((tm * ROW_TILE, LANES), row),
                   pl.BlockSpec((TOP_K, tm), col),
                   pl.BlockSpec((TOP_K, tm), col),
                   pl.BlockSpec((TOP_K, tm), col),
                   pl.BlockSpec((N_EXPERTS, 1), const)),
        scratch_shapes=[pltpu.VMEM((N_EXPERTS, 1), F32)],
        compiler_params=pltpu.CompilerParams(dimension_semantics=("arbitrary",),
                                             vmem_limit_bytes=VMEM_LIMIT),
        name="mix_route",
    )(x2d, y, att, mod3, wglu, sng, woa, wos, n2g, wrh, wrl, rb, wgu, wds)


ROW_TILE = 4
EXPERT_SUB = 256
COMBINE_PITCH = TOP_K + 1


def _pack_rows(v):
    half = v.shape[1] // 2
    return pltpu.pack_elementwise([v[:, :half], v[:, half:]], packed_dtype=BF16)


def _unpack_rows(w, index):
    return pltpu.unpack_elementwise(w, index=index, packed_dtype=BF16, unpacked_dtype=F32)


def _store_rows(ref, packed, first=0):
    m = packed.shape[0]
    for s in range(ROW_TILE):
        ref[pl.ds(first * ROW_TILE + s, m, stride=ROW_TILE), :] = packed[:, s * LANES:(s + 1) * LANES]


def _load_rows(ref, m, first=0):
    return jnp.concatenate([ref[pl.ds(first * ROW_TILE + s, m, stride=ROW_TILE), :] for s in range(ROW_TILE)],
                           axis=1)


def _row(ref, r):
    return ref.at[pl.ds(pl.multiple_of(r * ROW_TILE, ROW_TILE), ROW_TILE), :]


def _slot_kernel(e_ref, r_ref, ps_ref, d_ref):
    e = e_ref[...]
    ids = lax.broadcasted_iota(I32, (N_EXPERTS, e.shape[1]), 0)
    starts = ps_ref[...]
    rows = [jnp.sum(jnp.where(ids == e[k:k + 1, :], starts, 0.0), axis=0, keepdims=True)
            for k in range(TOP_K)]
    d_ref[...] = jnp.concatenate(rows, axis=0).astype(I32) + r_ref[...]


def _slots(eidx, rank, pad_starts, ts):
    n_tok = eidx.shape[1]
    col = lambda i: (0, i)
    return pl.pallas_call(
        _slot_kernel,
        out_shape=jax.ShapeDtypeStruct((TOP_K, n_tok), I32),
        grid=(n_tok // ts,),
        in_specs=[pl.BlockSpec((TOP_K, ts), col),
                  pl.BlockSpec((TOP_K, ts), col),
                  pl.BlockSpec((N_EXPERTS, 1), lambda i: (0, 0))],
        out_specs=pl.BlockSpec((TOP_K, ts), col),
        compiler_params=pltpu.CompilerParams(dimension_semantics=("arbitrary",),
                                             vmem_limit_bytes=VMEM_LIMIT),
        name="slots",
    )(eidx, rank, pad_starts.astype(F32).reshape(N_EXPERTS, 1))


def _dispatch_kernel(d_ref, h2p_ref, xs_ref, sem):
    td = d_ref.shape[1]

    def issue(t, carry):
        for k in range(TOP_K):
            pltpu.make_async_copy(_row(h2p_ref, t), _row(xs_ref, d_ref[k, t]), sem).start(priority=k % 2)
        return carry

    lax.fori_loop(0, td, issue, 0)
    for _ in range(TOP_K):
        pltpu.make_async_copy(h2p_ref, xs_ref.at[pl.ds(0, td * ROW_TILE), :], sem).wait()


def _dispatch(dest, h2p, n_slots, td):
    n_tok = dest.shape[1]
    return pl.pallas_call(
        _dispatch_kernel,
        out_shape=jax.ShapeDtypeStruct((n_slots * ROW_TILE, LANES), U32),
        grid=(n_tok // td,),
        in_specs=[pl.BlockSpec((TOP_K, td), lambda i: (0, i), memory_space=pltpu.SMEM),
                  pl.BlockSpec((td * ROW_TILE, LANES), lambda i: (i, 0))],
        out_specs=pl.BlockSpec(memory_space=pl.ANY),
        scratch_shapes=[pltpu.SemaphoreType.DMA],
        compiler_params=pltpu.CompilerParams(dimension_semantics=("arbitrary",),
                                             vmem_limit_bytes=VMEM_LIMIT),
        name="dispatch",
    )(dest, h2p)


def _expert_kernel(be_ref, nv_ref, nr_ref, nx_ref, last_ref, xs_hbm, wg_hbm, wu_hbm, wd_hbm, ys_hbm,
                   x_buf, y_buf, wg_f32, wu_f32, wd_f32, wg_bf, wu_bf, wd_bf, sem, x_sem, y_sem):
    i = pl.program_id(0)
    blk = x_buf.shape[1] // ROW_TILE
    sizes = range(EXPERT_SUB, blk + 1, EXPERT_SUB)
    expert = be_ref[i]
    n_valid = nv_ref[i]
    last = last_ref[0]
    slot = lax.rem(i, 2)
    changed = jnp.logical_or(i == 0, expert != be_ref[jnp.maximum(i - 1, 0)])

    def block_rows(ref, block, n_rows):
        return ref.at[pl.ds(pl.multiple_of(block * (blk * ROW_TILE), blk * ROW_TILE), n_rows * ROW_TILE), :]

    def x_copy(block, into, n_rows):
        return pltpu.make_async_copy(block_rows(xs_hbm, block, n_rows),
                                     x_buf.at[into, pl.ds(0, n_rows * ROW_TILE), :], x_sem.at[into])

    def y_copy(block, out_of, n_rows):
        return pltpu.make_async_copy(y_buf.at[out_of, pl.ds(0, n_rows * ROW_TILE), :],
                                     block_rows(ys_hbm, block, n_rows), y_sem.at[out_of])

    def for_rows(n, fn):
        for n_rows in sizes:
            pl.when(n == n_rows)(functools.partial(fn, n_rows))

    @pl.when(i == 0)
    def _():
        for_rows(nr_ref[0], lambda n_rows: x_copy(0, 0, n_rows).start())

    @pl.when(jnp.logical_and(i >= 2, i <= last))
    def _():
        for_rows(nr_ref[jnp.maximum(i - 2, 0)], lambda n_rows: y_copy(i - 2, slot, n_rows).wait())

    def weight_copies(e):
        return [pltpu.make_async_copy(wg_hbm.at[e], wg_f32, sem.at[0]),
                pltpu.make_async_copy(wu_hbm.at[e], wu_f32, sem.at[1]),
                pltpu.make_async_copy(wd_hbm.at[e], wd_f32, sem.at[2])]

    @pl.when(i == 0)
    def _():
        for cp in weight_copies(expert):
            cp.start()

    @pl.when(jnp.logical_and(n_valid > 0, changed))
    def _():
        for cp in weight_copies(expert):
            cp.wait()
        wg_bf[...] = wg_f32[...].astype(BF16)
        wu_bf[...] = wu_f32[...].astype(BF16)
        wd_bf[...] = wd_f32[...].astype(BF16)
        upcoming = nx_ref[i]

        @pl.when(upcoming >= 0)
        def _():
            for cp in weight_copies(upcoming):
                cp.start()

    def swiglu_rows(n_rows):
        x_copy(i, slot, n_rows).wait()
        upcoming_rows = nr_ref[jnp.minimum(i + 1, pl.num_programs(0) - 1)]
        for_rows(jnp.where(i < last, upcoming_rows, 0), lambda n_next: x_copy(i + 1, 1 - slot, n_next).start())
        packed = _load_rows(x_buf.at[slot], n_rows)
        x = jnp.concatenate([_unpack_rows(packed, 0), _unpack_rows(packed, 1)], axis=1)
        rows = lax.broadcasted_iota(I32, (n_rows, 1), 0)
        x = jnp.where(rows < n_valid, x, 0.0).astype(BF16)
        g = _dot(x, wg_bf[...])
        u = _dot(x, wu_bf[...])
        act = (g * _sigmoid(g) * u).astype(BF16)
        _store_rows(y_buf.at[slot], _pack_rows(_dot(act, wd_bf[...])))
        y_copy(i, slot, n_rows).start()

    for_rows(nr_ref[i], swiglu_rows)

    @pl.when(i == last)
    def _():
        for_rows(nr_ref[i], lambda n_rows: y_copy(i, slot, n_rows).wait())

        @pl.when(i >= 1)
        def _():
            for_rows(nr_ref[jnp.maximum(i - 1, 0)], lambda n_rows: y_copy(i - 1, 1 - slot, n_rows).wait())


def _experts(block_expert, block_valid, next_expert, last_block, xs, w_g, w_u, w_d, blk):
    n_blocks = block_expert.shape[0]
    d, de = w_g.shape[1], w_g.shape[2]
    block_rows = (block_valid + EXPERT_SUB - 1) // EXPERT_SUB * EXPERT_SUB
    hbm = pl.BlockSpec(memory_space=pl.ANY)
    return pl.pallas_call(
        _expert_kernel,
        out_shape=jax.ShapeDtypeStruct(xs.shape, U32),
        grid_spec=pltpu.PrefetchScalarGridSpec(
            num_scalar_prefetch=5,
            grid=(n_blocks,),
            in_specs=[hbm, hbm, hbm, hbm],
            out_specs=hbm,
            scratch_shapes=[pltpu.VMEM((2, blk * ROW_TILE, LANES), U32),
                            pltpu.VMEM((2, blk * ROW_TILE, LANES), U32),
                            pltpu.VMEM((d, de), F32),
                            pltpu.VMEM((d, de), F32),
                            pltpu.VMEM((de, d), F32),
                            pltpu.VMEM((d, de), BF16),
                            pltpu.VMEM((d, de), BF16),
                            pltpu.VMEM((de, d), BF16),
                            pltpu.SemaphoreType.DMA((3,)),
                            pltpu.SemaphoreType.DMA((2,)),
                            pltpu.SemaphoreType.DMA((2,))]),
        compiler_params=pltpu.CompilerParams(dimension_semantics=("arbitrary",),
                                             vmem_limit_bytes=VMEM_LIMIT),
        name="experts",
    )(block_expert, block_valid, block_rows, next_expert, last_block, xs, w_g, w_u, w_d)


def _combine_kernel(d_ref, dn_ref, gate_ref, xs_ref, mod_ref, ys_ref, o_ref, buf, sem):
    tc = d_ref.shape[1]
    i = pl.program_id(0)
    slot = lax.rem(i, 2)

    def gather(dest_ref, into):
        def issue(t, carry):
            for k in range(TOP_K):
                pltpu.make_async_copy(_row(ys_ref, dest_ref[k, t]), _row(buf.at[into], t * COMBINE_PITCH + k),
                                      sem.at[into]).start(priority=k % 2)
            return carry
        lax.fori_loop(0, tc, issue, 0)

    @pl.when(i == 0)
    def _():
        gather(d_ref, 0)

    @pl.when(i + 1 < pl.num_programs(0))
    def _():
        gather(dn_ref, 1 - slot)

    gathered = TOP_K * tc * ROW_TILE
    pltpu.make_async_copy(ys_ref.at[pl.ds(0, gathered), :], buf.at[slot, pl.ds(0, gathered), :],
                          sem.at[slot]).wait()

    gate2 = mod_ref[0][5:6]
    rows = buf.at[slot]
    acc = [jnp.zeros((tc, LANES), F32) for _ in range(2 * ROW_TILE)]
    for k in range(TOP_K):
        gk = jnp.transpose(jnp.broadcast_to(gate_ref[k:k + 1, :], (LANES, tc)))
        for s in range(ROW_TILE):
            w = rows[pl.ds(k * ROW_TILE + s, tc, stride=COMBINE_PITCH * ROW_TILE), :]
            acc[s] = acc[s] + gk * _unpack_rows(w, 0)
            acc[ROW_TILE + s] = acc[ROW_TILE + s] + gk * _unpack_rows(w, 1)
    o_ref[...] = xs_ref[...] + gate2 * jnp.concatenate(acc, axis=1)


def _combine(dest, gate, xs, mod3, ys, seq_len, tc):
    n_tok, d = xs.shape
    tiles_per_seq = seq_len // tc
    n_tiles = n_tok // tc
    return pl.pallas_call(
        _combine_kernel,
        out_shape=jax.ShapeDtypeStruct((n_tok, d), F32),
        grid=(n_tiles,),
        in_specs=[pl.BlockSpec((TOP_K, tc), lambda i: (0, i), memory_space=pltpu.SMEM),
                  pl.BlockSpec((TOP_K, tc), lambda i: (0, jnp.minimum(i + 1, n_tiles - 1)),
                               memory_space=pltpu.SMEM),
                  pl.BlockSpec((TOP_K, tc), lambda i: (0, i)),
                  pl.BlockSpec((tc, d), lambda i: (i, 0)),
                  pl.BlockSpec((1, 6, d), lambda i: (i // tiles_per_seq, 0, 0)),
                  pl.BlockSpec(memory_space=pl.ANY)],
        out_specs=pl.BlockSpec((tc, d), lambda i: (i, 0)),
        scratch_shapes=[pltpu.VMEM((2, COMBINE_PITCH * tc * ROW_TILE, LANES), U32),
                        pltpu.SemaphoreType.DMA((2,))],
        compiler_params=pltpu.CompilerParams(dimension_semantics=("arbitrary",),
                                             vmem_limit_bytes=VMEM_LIMIT),
        name="combine",
    )(dest, dest, gate, xs, mod3, ys)


def _tiles(seq_len):
    pick = lambda pref: min(pref, seq_len)
    return dict(adaln=1024, inproj=pick(512), attn_q=pick(512), attn_k=pick(512), ssm=pick(2048), mix=pick(512),
                dispatch=pick(512), combine=pick(256), slots=pick(2048), expert_block=4 * EXPERT_SUB)


def _rope_tables(seq_len):
    half = ATT_HEAD_DIM // 2
    inv_freq = 1.0 / (ROPE_THETA ** (jnp.arange(0, ATT_HEAD_DIM, 2, dtype=F32) / ATT_HEAD_DIM))
    ang = jnp.arange(seq_len, dtype=F32)[:, None] * inv_freq[None, :]
    cos, sin = jnp.cos(ang), jnp.sin(ang)
    reps = LANES // half
    sign = jnp.tile(jnp.concatenate([-jnp.ones((half,), F32), jnp.ones((half,), F32)]), reps // 2)
    return jnp.tile(cos, (1, reps)), jnp.tile(sin, (1, reps)) * sign[None, :]


def _block_tables(counts, blk, n_blocks):
    padded = (counts + blk - 1) // blk * blk
    pad_ends = jnp.cumsum(padded)
    pad_starts = pad_ends - padded
    used = pad_ends[-1] // blk
    last = jnp.maximum(used - 1, 0)
    starts = jnp.arange(n_blocks, dtype=I32) * blk
    expert = jnp.sum((pad_ends[None, :] <= starts[:, None]).astype(I32), axis=1)
    expert = jnp.minimum(expert, N_EXPERTS - 1)
    expert = jnp.where(starts < pad_ends[-1], expert, expert[last])
    valid = jnp.clip(counts[expert] - (starts - pad_starts[expert]), 0, blk)
    valid = jnp.where(starts < pad_ends[-1], valid, 0).astype(I32)
    ids = jnp.arange(N_EXPERTS, dtype=I32)
    later_used = jnp.logical_and(ids[None, :] > expert[:, None], (counts > 0)[None, :])
    upcoming = jnp.min(jnp.where(later_used, ids[None, :], N_EXPERTS), axis=1)
    upcoming = jnp.where(upcoming < N_EXPERTS, upcoming, -1).astype(I32)
    return pad_starts.astype(I32), expert, valid, upcoming, last.reshape(1).astype(I32)


def kernel(x, c, norm1_g, norm2_g, w_ada, b_ada, w_in, q_norm_g, k_norm_g, lambda_q1, lambda_k1, lambda_q2, lambda_k2, subln_g, ssm_a_re, ssm_a_im, ssm_log_dt, ssm_b_re, ssm_b_im, ssm_c_re, ssm_c_im, ssm_d, w_glu, ssm_norm_g, w_out, w_router, router_bias, w_gate_e, w_up_e, w_down_e, w_gate_s, w_up_s, w_down_s):
    bsz, seq_len, d = x.shape
    n_tok = bsz * seq_len
    tiles = _tiles(seq_len)
    assert d == 2 * ROW_TILE * LANES, "packed token rows are ROW_TILE x 128 words of two bf16 each"
    assert w_in.shape[1:] == (d, 2 * QK_WIDTH + ATT_WIDTH + SSM_WIDTH) and w_ada.shape[2] % tiles['adaln'] == 0
    assert w_router.shape[2] == N_EXPERTS and w_gate_e.shape[1] == N_EXPERTS
    assert all(seq_len % tiles[t] == 0 for t in ('inproj', 'attn_q', 'attn_k', 'ssm', 'mix', 'dispatch',
                                                 'combine', 'slots')), "sequence length must be tile aligned"
    blk = tiles['expert_block']
    n_blocks = (n_tok * TOP_K + N_EXPERTS * (blk - 1) + blk - 1) // blk
    cos_t, sin_t = _rope_tables(seq_len)
    seg = jnp.kron(jnp.eye(QK_WIDTH // ATT_HEAD_DIM, dtype=F32),
                   jnp.full((ATT_HEAD_DIM, ATT_HEAD_DIM), 1.0 / ATT_HEAD_DIM, F32)).astype(BF16)
    reps = QK_WIDTH // ATT_HEAD_DIM
    x2d = x.reshape(n_tok, d).astype(F32)
    for layer in range(w_ada.shape[0]):
        lam_init = 0.8 - 0.6 * math.exp(-0.3 * layer)
        mod3 = _adaln(c.astype(F32), w_ada[layer].astype(F32), b_ada[layer].astype(F32),
                      tiles['adaln']).reshape(bsz, 6, d)
        q, k, v, u = _inproj(
            x2d, mod3, norm1_g[layer].astype(F32).reshape(1, d), w_in[layer].astype(BF16), seg,
            jnp.tile(q_norm_g[layer].astype(F32), reps).reshape(1, QK_WIDTH),
            jnp.tile(k_norm_g[layer].astype(F32), reps).reshape(1, QK_WIDTH),
            cos_t, sin_t, seq_len, tiles['inproj'])
        lam = (jnp.exp(jnp.sum(lambda_q1[layer].astype(F32) * lambda_k1[layer].astype(F32)))
               - jnp.exp(jnp.sum(lambda_q2[layer].astype(F32) * lambda_k2[layer].astype(F32))) + lam_init)
        att = _attention(q, k, v, jnp.full((1, LANES), lam, F32),
                         subln_g[layer].astype(F32).reshape(1, ATT_V_DIM),
                         bsz, seq_len, tiles['attn_q'], tiles['attn_k'], 1.0 - lam_init)
        kmat, bmat, cmat, bpow = _ssm_mats(ssm_a_re[layer], ssm_a_im[layer], ssm_log_dt[layer],
                                         ssm_b_re[layer], ssm_b_im[layer], ssm_c_re[layer],
                                         ssm_c_im[layer], ssm_d[layer])
        y = _ssm(u, kmat, bmat, cmat, bpow, bsz, seq_len, tiles['ssm'])
        wr_t = jnp.transpose(w_router[layer].astype(F32))
        wrh, wrl = _split_bf16(wr_t)
        xs, h2p, eidx, gate, rank, counts = _mix_route(
            x2d, y, att, mod3, w_glu[layer].astype(BF16),
            ssm_norm_g[layer].astype(F32).reshape(1, SSM_WIDTH),
            w_out[layer, :ATT_WIDTH].astype(BF16), w_out[layer, ATT_WIDTH:].astype(BF16),
            norm2_g[layer].astype(F32).reshape(1, d), wrh, wrl,
            router_bias[layer].astype(F32).reshape(N_EXPERTS, 1),
            jnp.concatenate([w_gate_s[layer], w_up_s[layer]], axis=1).astype(BF16),
            w_down_s[layer].astype(BF16), seq_len, tiles['mix'])
        pad_starts, block_expert, block_valid, next_expert, last_block = _block_tables(
            counts.reshape(N_EXPERTS).astype(I32), blk, n_blocks)
        dest = _slots(eidx, rank, pad_starts, tiles['slots'])
        x_slots = _dispatch(dest, h2p, n_blocks * blk, tiles['dispatch'])
        y_slots = _experts(block_expert, block_valid, next_expert, last_block, x_slots,
                           w_gate_e[layer], w_up_e[layer], w_down_e[layer], blk)
        x2d = _combine(dest, gate, xs, mod3, y_slots, seq_len, tiles['combine'])
    return x2d.reshape(bsz, seq_len, d).astype(x.dtype)
```

```python
import functools
import math

import jax
import jax.numpy as jnp
from jax import lax
from jax.experimental import pallas as pl
from jax.experimental.pallas import tpu as pltpu

F32 = jnp.float32
BF16 = jnp.bfloat16
I32 = jnp.int32
U32 = jnp.uint32

LANES = 128
SUBLANES = 8
V7X_VMEM_BYTES = 64 * 1024 * 1024

N_ATT_HEADS = 4
ATT_HEAD_DIM = 64
ATT_V_DIM = 2 * ATT_HEAD_DIM
QK_WIDTH = N_ATT_HEADS * 2 * ATT_HEAD_DIM
ATT_WIDTH = N_ATT_HEADS * ATT_V_DIM
ROPE_THETA = 10000.0
SSM_GROUP = 16
SSM_GROUPS = 32
SSM_STATE = 64
SSM_WIDTH = SSM_GROUPS * SSM_GROUP
SSM_CHUNK = SUBLANES
SSM_LANE_BLOCKS = SSM_WIDTH // LANES
GROUPS_PER_BLOCK = LANES // SSM_GROUP
N_EXPERTS = 256
TOP_K = 8
N_GROUPS = 8
TOPK_GROUPS = 4
EXPERTS_PER_GROUP = N_EXPERTS // N_GROUPS
SHARED_DIM = 256
ROUTED_SCALE = 2.5
EPS = 1e-6

NT_DIMS = (((1,), (1,)), ((), ()))

VMEM_LIMIT = V7X_VMEM_BYTES * 3 // 4


def _dot(a, b):
    return jnp.dot(a, b, preferred_element_type=F32)


def _sigmoid(x):
    return 1.0 / (1.0 + jnp.exp(-x))


def _split_bf16(x):
    hi = x.astype(BF16)
    lo = (x - hi.astype(F32)).astype(BF16)
    return hi, lo


def _adaln_kernel(c_ref, w_ref, b_ref, o_ref):
    c = c_ref[...]
    sc = c * _sigmoid(c)
    o_ref[...] = jnp.dot(sc, w_ref[...], preferred_element_type=F32,
                         precision=lax.Precision.HIGHEST) + b_ref[...]


def _adaln(c, w, b, tn):
    bsz, d = c.shape
    n = w.shape[1]
    return pl.pallas_call(
        _adaln_kernel,
        out_shape=jax.ShapeDtypeStruct((bsz, n), F32),
        grid=(n // tn,),
        in_specs=[pl.BlockSpec((bsz, d), lambda j: (0, 0)),
                  pl.BlockSpec((d, tn), lambda j: (0, j)),
                  pl.BlockSpec((1, tn), lambda j: (0, j))],
        out_specs=pl.BlockSpec((bsz, tn), lambda j: (0, j)),
        compiler_params=pltpu.CompilerParams(dimension_semantics=("arbitrary",),
                                             vmem_limit_bytes=VMEM_LIMIT),
        name="adaln",
    )(c, w, b.reshape(1, n))


def _inproj_kernel(x_ref, mod_ref, n1g_ref, w_ref, seg_ref, qg_ref, kg_ref, cos_ref, sin_ref,
                   q_ref, k_ref, v_ref, u_ref):
    x = x_ref[...]
    mod = mod_ref[0]
    shift, scale = mod[0:1], mod[1:2]
    ms = jnp.mean(x * x, axis=-1, keepdims=True)
    h = x * lax.rsqrt(ms + EPS) * n1g_ref[...]
    h = h * (1.0 + scale) + shift
    z = _dot(h.astype(BF16), w_ref[...])
    seg = seg_ref[...]
    cos = cos_ref[...]
    sin = sin_ref[...]
    lane = lax.broadcasted_iota(I32, cos.shape, 1)
    first_half = (lane % ATT_HEAD_DIM) < (ATT_HEAD_DIM // 2)

    def norm_rope(t, g_ref, out_scale):
        hi, lo = _split_bf16(t * t)
        msq = _dot(hi, seg) + _dot(lo, seg)
        tn = t * lax.rsqrt(msq + EPS) * g_ref[...]
        outs = []
        for hd in range(N_ATT_HEADS):
            th = tn[:, hd * LANES:(hd + 1) * LANES]
            partner = jnp.where(first_half,
                                pltpu.roll(th, LANES - ATT_HEAD_DIM // 2, 1),
                                pltpu.roll(th, ATT_HEAD_DIM // 2, 1))
            outs.append((th * cos + partner * sin) * out_scale)
        return jnp.concatenate(outs, axis=1)

    q_ref[...] = norm_rope(z[:, :QK_WIDTH], qg_ref, ATT_HEAD_DIM ** -0.5).astype(BF16)
    k_ref[...] = norm_rope(z[:, QK_WIDTH:2 * QK_WIDTH], kg_ref, 1.0).astype(BF16)
    v_ref[...] = z[:, 2 * QK_WIDTH:2 * QK_WIDTH + ATT_WIDTH].astype(BF16)
    u_ref[...] = z[:, 2 * QK_WIDTH + ATT_WIDTH:]


def _inproj(x2d, mod3, n1g, w_in_bf, seg, qg, kg, cos_t, sin_t, seq_len, tm):
    n_tok, d = x2d.shape
    tiles_per_seq = seq_len // tm
    in_width = w_in_bf.shape[1]
    row = lambda i: (i, 0)
    const = lambda i: (0, 0)
    return pl.pallas_call(
        _inproj_kernel,
        out_shape=(jax.ShapeDtypeStruct((n_tok, QK_WIDTH), BF16),
                   jax.ShapeDtypeStruct((n_tok, QK_WIDTH), BF16),
                   jax.ShapeDtypeStruct((n_tok, ATT_WIDTH), BF16),
                   jax.ShapeDtypeStruct((n_tok, SSM_WIDTH), F32)),
        grid=(n_tok // tm,),
        in_specs=[pl.BlockSpec((tm, d), row),
                  pl.BlockSpec((1, 6, d), lambda i: (i // tiles_per_seq, 0, 0)),
                  pl.BlockSpec((1, d), const),
                  pl.BlockSpec((d, in_width), const),
                  pl.BlockSpec((QK_WIDTH, QK_WIDTH), const),
                  pl.BlockSpec((1, QK_WIDTH), const),
                  pl.BlockSpec((1, QK_WIDTH), const),
                  pl.BlockSpec((tm, LANES), lambda i: (i % tiles_per_seq, 0)),
                  pl.BlockSpec((tm, LANES), lambda i: (i % tiles_per_seq, 0))],
        out_specs=(pl.BlockSpec((tm, QK_WIDTH), row),
                   pl.BlockSpec((tm, QK_WIDTH), row),
                   pl.BlockSpec((tm, ATT_WIDTH), row),
                   pl.BlockSpec((tm, SSM_WIDTH), row)),
        compiler_params=pltpu.CompilerParams(dimension_semantics=("arbitrary",),
                                             vmem_limit_bytes=VMEM_LIMIT),
        name="inproj",
    )(x2d, mod3, n1g, w_in_bf, seg, qg, kg, cos_t, sin_t)


def _attn_kernel(qi_ref, kj_ref, q_ref, k_ref, v_ref, lam_ref, sg_ref, o_ref,
                 m_ref, l_ref, acc_ref, *, out_scale):
    p = pl.program_id(2)
    qi = qi_ref[p]
    kj = kj_ref[p]

    @pl.when(kj == 0)
    def _():
        m_ref[...] = jnp.full(m_ref.shape, -jnp.inf, F32)
        l_ref[...] = jnp.zeros(l_ref.shape, F32)
        acc_ref[...] = jnp.zeros(acc_ref.shape, F32)

    tq = q_ref.shape[0]
    tk = k_ref.shape[0]
    first_diag = qi * (tq // tk)

    def step(masked):
        q = q_ref[...]
        lane = lax.broadcasted_iota(I32, q.shape, 1)
        zero = jnp.zeros_like(q)
        qq = jnp.concatenate([jnp.where(lane < ATT_HEAD_DIM, q, zero),
                              jnp.where(lane >= ATT_HEAD_DIM, q, zero)], axis=0)
        s = lax.dot_general(qq, k_ref[...], NT_DIMS, preferred_element_type=F32)
        if masked:
            r = lax.broadcasted_iota(I32, (tq, tk), 0)
            col = lax.broadcasted_iota(I32, (tq, tk), 1) + (kj - first_diag) * tk
            causal = col <= r
            s = jnp.where(jnp.concatenate([causal, causal], axis=0), s, -jnp.inf)
        m_old = m_ref[...]
        m_new = jnp.maximum(m_old, jnp.max(s, axis=-1, keepdims=True))
        alpha = jnp.exp(m_old - m_new)
        pr = jnp.exp(s - jnp.concatenate([m_new] * (s.shape[1] // LANES), axis=1))
        l_ref[...] = alpha * l_ref[...] + jnp.sum(pr, axis=-1, keepdims=True)
        acc_ref[...] = alpha * acc_ref[...] + _dot(pr.astype(BF16), v_ref[...])
        m_ref[...] = m_new

    @pl.when(kj < first_diag)
    def _():
        step(False)

    @pl.when(kj >= first_diag)
    def _():
        step(True)

    @pl.when(kj == first_diag + tq // tk - 1)
    def _():
        o = acc_ref[...] / l_ref[...]
        o = o[:tq] - lam_ref[...] * o[tq:]
        ms = jnp.mean(o * o, axis=-1, keepdims=True)
        o_ref[...] = (o * lax.rsqrt(ms + EPS) * sg_ref[...] * out_scale).astype(o_ref.dtype)


def _attention(q, k, v, lam_row, subln_g, bsz, seq_len, tq, tk, out_scale):
    nq, nk = seq_len // tq, seq_len // tk
    pairs = [(i, j) for i in range(nq) for j in range((i + 1) * (tq // tk))]
    qi = jnp.asarray([p[0] for p in pairs], I32)
    kj = jnp.asarray([p[1] for p in pairs], I32)
    q_map = lambda b, h, p, qi, kj: (b * nq + qi[p], h)
    k_map = lambda b, h, p, qi, kj: (b * nk + kj[p], h)
    const = lambda b, h, p, qi, kj: (0, 0)
    return pl.pallas_call(
        functools.partial(_attn_kernel, out_scale=out_scale),
        out_shape=jax.ShapeDtypeStruct((bsz * seq_len, ATT_WIDTH), BF16),
        grid_spec=pltpu.PrefetchScalarGridSpec(
            num_scalar_prefetch=2,
            grid=(bsz, N_ATT_HEADS, len(pairs)),
            in_specs=[pl.BlockSpec((tq, LANES), q_map),
                      pl.BlockSpec((tk, LANES), k_map),
                      pl.BlockSpec((tk, LANES), k_map),
                      pl.BlockSpec((1, LANES), const),
                      pl.BlockSpec((1, LANES), const)],
            out_specs=pl.BlockSpec((tq, LANES), q_map),
            scratch_shapes=[pltpu.VMEM((2 * tq, LANES), F32),
                            pltpu.VMEM((2 * tq, LANES), F32),
                            pltpu.VMEM((2 * tq, ATT_V_DIM), F32)]),
        compiler_params=pltpu.CompilerParams(
            dimension_semantics=("arbitrary", "arbitrary", "arbitrary"),
            vmem_limit_bytes=VMEM_LIMIT),
        name="attn",
    )(qi, kj, q, k, v, lam_row, subln_g)


def _ssm_mats(a_re, a_im, log_dt, b_re, b_im, c_re, c_im, d_skip):
    a_re, a_im, b_re, b_im, c_re, c_im, d_skip = (
        t.astype(F32) for t in (a_re, a_im, b_re, b_im, c_re, c_im, d_skip))
    dt = jnp.exp(log_dt.astype(F32))[:, None]
    mag = jnp.exp(a_re * dt)
    abar_re = mag * jnp.cos(a_im * dt)
    abar_im = mag * jnp.sin(a_im * dt)
    den = a_re * a_re + a_im * a_im
    nr = abar_re - 1.0
    f_re = ((nr * a_re + abar_im * a_im) / den)[..., None]
    f_im = ((abar_im * a_re - nr * a_im) / den)[..., None]
    bb_re = f_re * b_re - f_im * b_im
    bb_im = f_re * b_im + f_im * b_re
    steps = jnp.arange(SSM_CHUNK + 1, dtype=F32)[:, None, None]
    pmag = jnp.exp(a_re * dt * steps)
    pw_re = pmag * jnp.cos(a_im * dt * steps)
    pw_im = pmag * jnp.sin(a_im * dt * steps)
    ca_re = c_re[None] * pw_re[:, :, None, :] - c_im[None] * pw_im[:, :, None, :]
    ca_im = c_re[None] * pw_im[:, :, None, :] + c_im[None] * pw_re[:, :, None, :]
    bbt_re = jnp.transpose(bb_re, (0, 2, 1))[None, :, None]
    bbt_im = jnp.transpose(bb_im, (0, 2, 1))[None, :, None]
    lag = jnp.sum(ca_re[:SSM_CHUNK, :, :, None, :] * bbt_re
                  - ca_im[:SSM_CHUNK, :, :, None, :] * bbt_im, axis=-1)
    lag = lag.at[0].add(d_skip[:, :, None] * jnp.eye(SSM_GROUP, dtype=F32)[None])
    nb, gpb = SSM_LANE_BLOCKS, GROUPS_PER_BLOCK
    n_state = gpb * SSM_STATE
    in_group = jnp.arange(LANES) // SSM_GROUP
    state_group = jnp.arange(n_state) // SSM_STATE

    def spread(small, row_group, col_group):
        w = small.shape[-1]
        tiled = jnp.tile(jnp.eye(w, dtype=BF16), (1, col_group.shape[0] // w))
        wide = jnp.einsum('...w,wn->...n', small.astype(BF16), tiled, preferred_element_type=F32)
        return jnp.where(row_group[:, None] == col_group[None, :], wide, 0.0).astype(BF16)

    lag_blocks = spread(jnp.transpose(lag, (0, 1, 3, 2)).reshape(SSM_CHUNK, nb, LANES, SSM_GROUP),
                        in_group, in_group)
    zero_block = jnp.zeros_like(lag_blocks[0])
    kmat = jnp.concatenate(
        [jnp.concatenate([lag_blocks[t - j] if t >= j else zero_block for t in range(SSM_CHUNK)], axis=-1)
         for j in range(SSM_CHUNK)], axis=1)
    rev = SSM_CHUNK - 1 - jnp.arange(SSM_CHUNK)
    w_re = pw_re[rev][..., None] * bb_re[None] - pw_im[rev][..., None] * bb_im[None]
    w_im = pw_re[rev][..., None] * bb_im[None] + pw_im[rev][..., None] * bb_re[None]

    def in_to_state(w):
        small = jnp.transpose(w, (0, 1, 3, 2)).reshape(SSM_CHUNK, nb, LANES, SSM_STATE)
        return spread(small, in_group, state_group)

    bmat = jnp.concatenate([in_to_state(w_re), in_to_state(w_im)], axis=-1)
    bmat = jnp.transpose(bmat, (1, 0, 2, 3)).reshape(nb, SSM_CHUNK * LANES, 2 * n_state)

    def state_to_out(ca):
        small = jnp.transpose(ca[1:], (0, 1, 3, 2)).reshape(SSM_CHUNK, nb, n_state, SSM_GROUP)
        blocks = spread(small, state_group, in_group)
        return jnp.transpose(blocks, (1, 2, 0, 3)).reshape(nb, n_state, SSM_CHUNK * LANES)

    cmat = jnp.concatenate([state_to_out(ca_re), -state_to_out(ca_im)], axis=1)
    chunk_steps = SSM_CHUNK * jnp.arange(2 * SUBLANES, dtype=F32)[:, None, None]
    cmag = jnp.exp(a_re * dt * chunk_steps)
    to_block = lambda t: jnp.transpose(t.reshape(2 * SUBLANES, nb, n_state), (1, 0, 2))
    bpow = jnp.concatenate([to_block(cmag * jnp.cos(a_im * dt * chunk_steps)),
                            to_block(cmag * jnp.sin(a_im * dt * chunk_steps))], axis=-1)
    return kmat, bmat, cmat, bpow


def _cmul(a_re, a_im, x_re, x_im):
    return a_re * x_re - a_im * x_im, a_re * x_im + a_im * x_re


def _ssm_kernel(u_ref, km_ref, bm_ref, cm_ref, bp_ref, y_ref, carry_ref, sp_ref):
    n_chunks = sp_ref.shape[0]
    half = sp_ref.shape[1] // 2

    @pl.when(pl.program_id(2) == 0)
    def _():
        carry_ref[...] = jnp.zeros(carry_ref.shape, F32)

    ucat = jnp.concatenate([u_ref[pl.ds(j, n_chunks, stride=SSM_CHUNK), :] for j in range(SSM_CHUNK)],
                           axis=1).astype(BF16)
    s_end = _dot(ucat, bm_ref[0])
    p_re, p_im = s_end[:, :half], s_end[:, half:]
    bp = bp_ref[0]
    sub = lax.broadcasted_iota(I32, (n_chunks, half), 0) % SUBLANES

    def shifted(v, d):
        return jnp.where(sub >= d, pltpu.roll(v, d, 0), 0.0)

    for d in (1, 2, 4):
        d_re, d_im = _cmul(bp[d:d + 1, :half], bp[d:d + 1, half:], shifted(p_re, d), shifted(p_im, d))
        p_re, p_im = p_re + d_re, p_im + d_im
    x_re, x_im = shifted(p_re, 1), shifted(p_im, 1)
    t_re, t_im = bp[:SUBLANES, :half], bp[:SUBLANES, half:]
    l_re, l_im = bp[SUBLANES:SUBLANES + 1, :half], bp[SUBLANES:SUBLANES + 1, half:]
    c_re, c_im = carry_ref[:, :half], carry_ref[:, half:]
    for g in range(n_chunks // SUBLANES):
        lo, hi = g * SUBLANES, (g + 1) * SUBLANES
        d_re, d_im = _cmul(t_re, t_im, c_re, c_im)
        sp_ref[lo:hi, :half] = x_re[lo:hi] + d_re
        sp_ref[lo:hi, half:] = x_im[lo:hi] + d_im
        e_re, e_im = _cmul(l_re, l_im, c_re, c_im)
        c_re, c_im = p_re[hi - 1:hi] + e_re, p_im[hi - 1:hi] + e_im
    carry_ref[...] = jnp.concatenate([c_re, c_im], axis=1)
    y = _dot(ucat, km_ref[0]) + _dot(sp_ref[...].astype(BF16), cm_ref[0])
    for t in range(SSM_CHUNK):
        y_ref[pl.ds(t, n_chunks, stride=SSM_CHUNK), :] = y[:, t * LANES:(t + 1) * LANES]


def _ssm(u, kmat, bmat, cmat, bpow, bsz, seq_len, tt):
    nt = seq_len // tt
    n_chunks = tt // SSM_CHUNK
    width = SSM_CHUNK * LANES
    u_map = lambda g, b, i: (b * nt + i, g)
    w_map = lambda g, b, i: (g, 0, 0)
    return pl.pallas_call(
        _ssm_kernel,
        out_shape=jax.ShapeDtypeStruct((bsz * seq_len, SSM_WIDTH), F32),
        grid=(SSM_LANE_BLOCKS, bsz, nt),
        in_specs=[pl.BlockSpec((tt, LANES), u_map),
                  pl.BlockSpec((1, width, width), w_map),
                  pl.BlockSpec((1, width, width), w_map),
                  pl.BlockSpec((1, width, width), w_map),
                  pl.BlockSpec((1, 2 * SUBLANES, width), w_map)],
        out_specs=pl.BlockSpec((tt, LANES), u_map),
        scratch_shapes=[pltpu.VMEM((1, width), F32),
                        pltpu.VMEM((n_chunks, width), F32)],
        compiler_params=pltpu.CompilerParams(
            dimension_semantics=("arbitrary", "arbitrary", "arbitrary"),
            vmem_limit_bytes=VMEM_LIMIT),
        name="ssm",
    )(u, kmat, bmat, cmat, bpow)


def _first_max(v, ids, n):
    m = jnp.max(v, axis=0, keepdims=True)
    ix = jnp.min(jnp.where(v == m, ids, n), axis=0, keepdims=True)
    return m, ix


def _mix_route_kernel(x_ref, y_ref, att_ref, mod_ref, wglu_ref, sng_ref, woa_ref, wos_ref, n2g_ref,
                      wrh_ref, wrl_ref, rb_ref, wgu_ref, wds_ref,
                      xs_ref, h2p_ref, eidx_ref, gate_ref, rank_ref, cnt_ref, carry_ref):
    tm = x_ref.shape[0]

    @pl.when(pl.program_id(0) == 0)
    def _():
        carry_ref[...] = jnp.zeros(carry_ref.shape, F32)

    mod = mod_ref[0]
    gate1, shift2, scale2, gate2 = mod[2:3], mod[3:4], mod[4:5], mod[5:6]

    y = y_ref[...]
    g = 0.5 * y * (1.0 + jnp.tanh(math.sqrt(2.0 / math.pi) * (y + 0.044715 * (y * y * y))))
    glu = g * _sigmoid(_dot(g.astype(BF16), wglu_ref[...]))
    ssm = glu * lax.rsqrt(jnp.mean(glu * glu, axis=-1, keepdims=True) + EPS) * sng_ref[...]

    mix = _dot(att_ref[...], woa_ref[...]) + _dot(ssm.astype(BF16), wos_ref[...])
    x1 = x_ref[...] + gate1 * mix
    h2 = x1 * lax.rsqrt(jnp.mean(x1 * x1, axis=-1, keepdims=True) + EPS) * n2g_ref[...]
    h2 = h2 * (1.0 + scale2) + shift2
    _store_rows(h2p_ref, _pack_rows(h2))
    hb, h_lo = _split_bf16(h2)

    gu = _dot(hb, wgu_ref[...])
    gs, us = gu[:, :SHARED_DIM], gu[:, SHARED_DIM:]
    act = (gs * _sigmoid(gs) * us).astype(BF16)
    xs_ref[...] = x1 + gate2 * _dot(act, wds_ref[...])

    wrh = wrh_ref[...]
    logits = (lax.dot_general(wrh, hb, NT_DIMS, preferred_element_type=F32)
              + lax.dot_general(wrh, h_lo, NT_DIMS, preferred_element_type=F32)
              + lax.dot_general(wrl_ref[...], hb, NT_DIMS, preferred_element_type=F32))
    score = _sigmoid(logits)
    biased = score + rb_ref[...]
    neg = -jnp.inf

    ids_g = lax.broadcasted_iota(I32, (EXPERTS_PER_GROUP, tm), 0)
    group_rows = []
    for gi in range(N_GROUPS):
        blk = biased[gi * EXPERTS_PER_GROUP:(gi + 1) * EXPERTS_PER_GROUP, :]
        m1, i1 = _first_max(blk, ids_g, EXPERTS_PER_GROUP)
        m2 = jnp.max(jnp.where(ids_g == i1, neg, blk), axis=0, keepdims=True)
        group_rows.append(m1 + m2)
    cur = jnp.concatenate(group_rows, axis=0)
    ids_8 = lax.broadcasted_iota(I32, cur.shape, 0)
    picked = jnp.zeros(cur.shape, F32)
    for _ in range(TOPK_GROUPS):
        _, ix = _first_max(cur, ids_8, N_GROUPS)
        hit = ids_8 == ix
        picked = jnp.where(hit, 1.0, picked)
        cur = jnp.where(hit, neg, cur)
    e_mask = jnp.concatenate(
        [jnp.broadcast_to(picked[gi:gi + 1, :], (EXPERTS_PER_GROUP, tm)) for gi in range(N_GROUPS)], axis=0)
    cand = jnp.where(e_mask > 0.0, biased, neg)

    ids_e = lax.broadcasted_iota(I32, cand.shape, 0)
    sel = jnp.zeros(cand.shape, F32)
    idx_rows, w_rows = [], []
    for _ in range(TOP_K):
        _, ix = _first_max(cand, ids_e, N_EXPERTS)
        hit = ids_e == ix
        idx_rows.append(ix)
        w_rows.append(jnp.sum(jnp.where(hit, score, 0.0), axis=0, keepdims=True))
        sel = jnp.where(hit, 1.0, sel)
        cand = jnp.where(hit, neg, cand)
    w_sum = w_rows[0]
    for w in w_rows[1:]:
        w_sum = w_sum + w
    gate_ref[...] = jnp.concatenate([w / w_sum * ROUTED_SCALE for w in w_rows], axis=0)
    eidx_ref[...] = jnp.concatenate(idx_rows, axis=0)

    t_row = lax.broadcasted_iota(I32, (tm, tm), 0)
    t_col = lax.broadcasted_iota(I32, (tm, tm), 1)
    earlier = jnp.where(t_row < t_col, 1.0, 0.0).astype(BF16)
    before = _dot(sel.astype(BF16), earlier) + carry_ref[...]
    rank_ref[...] = jnp.concatenate(
        [jnp.sum(jnp.where(ids_e == ix, before, 0.0), axis=0, keepdims=True) for ix in idx_rows],
        axis=0).astype(I32)
    carry_ref[...] = carry_ref[...] + jnp.sum(sel, axis=1, keepdims=True)
    cnt_ref[...] = carry_ref[...]


def _mix_route(x2d, y, att, mod3, wglu, sng, woa, wos, n2g, wrh, wrl, rb, wgu, wds, seq_len, tm):
    n_tok, d = x2d.shape
    tiles_per_seq = seq_len // tm
    row = lambda i: (i, 0)
    col = lambda i: (0, i)
    const = lambda i: (0, 0)
    full = lambda a: pl.BlockSpec(a.shape, const)
    return pl.pallas_call(
        _mix_route_kernel,
        out_shape=(jax.ShapeDtypeStruct((n_tok, d), F32),
                   jax.ShapeDtypeStruct((n_tok * ROW_TILE, LANES), U32),
                   jax.ShapeDtypeStruct((TOP_K, n_tok), I32),
                   jax.ShapeDtypeStruct((TOP_K, n_tok), F32),
                   jax.ShapeDtypeStruct((TOP_K, n_tok), I32),
                   jax.ShapeDtypeStruct((N_EXPERTS, 1), F32)),
        grid=(n_tok // tm,),
        in_specs=[pl.BlockSpec((tm, d), row),
                  pl.BlockSpec((tm, SSM_WIDTH), row),
                  pl.BlockSpec((tm, ATT_WIDTH), row),
                  pl.BlockSpec((1, 6, d), lambda i: (i // tiles_per_seq, 0, 0)),
                  full(wglu), full(sng), full(woa), full(wos), full(n2g),
                  full(wrh), full(wrl), full(rb), full(wgu), full(wds)],
        out_specs=(pl.BlockSpec((tm, d), row),
                   pl.BlockSpec((tm * ROW_TILE, LANES), row),
                   pl.BlockSpec((TOP_K, tm), col),
                   pl.BlockSpec((TOP_K, tm), col),
                   pl.BlockSpec((TOP_K, tm), col),
                   pl.BlockSpec((N_EXPERTS, 1), const)),
        scratch_shapes=[pltpu.VMEM((N_EXPERTS, 1), F32)],
        compiler_params=pltpu.CompilerParams(dimension_semantics=("arbitrary",),
                                             vmem_limit_bytes=VMEM_LIMIT),
        name="mix_route",
    )(x2d, y, att, mod3, wglu, sng, woa, wos, n2g, wrh, wrl, rb, wgu, wds)


ROW_TILE = 4
EXPERT_SUB = 256
COMBINE_PITCH = TOP_K + 1


def _pack_rows(v):
    half = v.shape[1] // 2
    return pltpu.pack_elementwise([v[:, :half], v[:, half:]], packed_dtype=BF16)


def _unpack_rows(w, index):
    return pltpu.unpack_elementwise(w, index=index, packed_dtype=BF16, unpacked_dtype=F32)


def _store_rows(ref, packed, first=0):
    m = packed.shape[0]
    for s in range(ROW_TILE):
        ref[pl.ds(first * ROW_TILE + s, m, stride=ROW_TILE), :] = packed[:, s * LANES:(s + 1) * LANES]


def _load_rows(ref, m, first=0):
    return jnp.concatenate([ref[pl.ds(first * ROW_TILE + s, m, stride=ROW_TILE), :] for s in range(ROW_TILE)],
                           axis=1)


def _row(ref, r):
    return ref.at[pl.ds(pl.multiple_of(r * ROW_TILE, ROW_TILE), ROW_TILE), :]


def _slot_kernel(e_ref, r_ref, ps_ref, d_ref):
    e = e_ref[...]
    ids = lax.broadcasted_iota(I32, (N_EXPERTS, e.shape[1]), 0)
    starts = ps_ref[...]
    rows = [jnp.sum(jnp.where(ids == e[k:k + 1, :], starts, 0.0), axis=0, keepdims=True)
            for k in range(TOP_K)]
    d_ref[...] = jnp.concatenate(rows, axis=0).astype(I32) + r_ref[...]


def _slots(eidx, rank, pad_starts, ts):
    n_tok = eidx.shape[1]
    col = lambda i: (0, i)
    return pl.pallas_call(
        _slot_kernel,
        out_shape=jax.ShapeDtypeStruct((TOP_K, n_tok), I32),
        grid=(n_tok // ts,),
        in_specs=[pl.BlockSpec((TOP_K, ts), col),
                  pl.BlockSpec((TOP_K, ts), col),
                  pl.BlockSpec((N_EXPERTS, 1), lambda i: (0, 0))],
        out_specs=pl.BlockSpec((TOP_K, ts), col),
        compiler_params=pltpu.CompilerParams(dimension_semantics=("arbitrary",),
                                             vmem_limit_bytes=VMEM_LIMIT),
        name="slots",
    )(eidx, rank, pad_starts.astype(F32).reshape(N_EXPERTS, 1))


def _dispatch_kernel(d_ref, h2p_ref, xs_ref, sem):
    td = d_ref.shape[0] // TOP_K

    def issue(t, carry):
        for k in range(TOP_K):
            pltpu.make_async_copy(_row(h2p_ref, t), _row(xs_ref, d_ref[t * TOP_K + k]),
                                  sem).start(priority=k % 2)
        return carry

    lax.fori_loop(0, td, issue, 0)
    for _ in range(TOP_K):
        pltpu.make_async_copy(h2p_ref, xs_ref.at[pl.ds(0, td * ROW_TILE), :], sem).wait()


def _dispatch(dest, h2p, n_slots, td):
    n_tok = dest.shape[0] // TOP_K
    return pl.pallas_call(
        _dispatch_kernel,
        out_shape=jax.ShapeDtypeStruct((n_slots * ROW_TILE, LANES), U32),
        grid=(n_tok // td,),
        in_specs=[pl.BlockSpec((td * TOP_K,), lambda i: (i,), memory_space=pltpu.SMEM),
                  pl.BlockSpec((td * ROW_TILE, LANES), lambda i: (i, 0))],
        out_specs=pl.BlockSpec(memory_space=pl.ANY),
        scratch_shapes=[pltpu.SemaphoreType.DMA],
        compiler_params=pltpu.CompilerParams(dimension_semantics=("arbitrary",),
                                             vmem_limit_bytes=VMEM_LIMIT),
        name="dispatch",
    )(dest, h2p)


def _expert_kernel(be_ref, nv_ref, nr_ref, nx_ref, last_ref, xs_hbm, wg_hbm, wu_hbm, wd_hbm, ys_hbm,
                   x_buf, y_buf, wg_f32, wu_f32, wd_f32, wg_bf, wu_bf, wd_bf, sem, x_sem, y_sem):
    i = pl.program_id(0)
    blk = x_buf.shape[1] // ROW_TILE
    sizes = range(EXPERT_SUB, blk + 1, EXPERT_SUB)
    expert = be_ref[i]
    n_valid = nv_ref[i]
    last = last_ref[0]
    slot = lax.rem(i, 2)
    changed = jnp.logical_or(i == 0, expert != be_ref[jnp.maximum(i - 1, 0)])

    def block_rows(ref, block, n_rows):
        return ref.at[pl.ds(pl.multiple_of(block * (blk * ROW_TILE), blk * ROW_TILE), n_rows * ROW_TILE), :]

    def x_copy(block, into, n_rows):
        return pltpu.make_async_copy(block_rows(xs_hbm, block, n_rows),
                                     x_buf.at[into, pl.ds(0, n_rows * ROW_TILE), :], x_sem.at[into])

    def y_copy(block, out_of, n_rows):
        return pltpu.make_async_copy(y_buf.at[out_of, pl.ds(0, n_rows * ROW_TILE), :],
                                     block_rows(ys_hbm, block, n_rows), y_sem.at[out_of])

    def for_rows(n, fn):
        for n_rows in sizes:
            pl.when(n == n_rows)(functools.partial(fn, n_rows))

    @pl.when(i == 0)
    def _():
        for_rows(nr_ref[0], lambda n_rows: x_copy(0, 0, n_rows).start())

    @pl.when(jnp.logical_and(i >= 2, i <= last))
    def _():
        for_rows(nr_ref[jnp.maximum(i - 2, 0)], lambda n_rows: y_copy(i - 2, slot, n_rows).wait())

    def weight_copies(e):
        return [pltpu.make_async_copy(wg_hbm.at[e], wg_f32, sem.at[0]),
                pltpu.make_async_copy(wu_hbm.at[e], wu_f32, sem.at[1]),
                pltpu.make_async_copy(wd_hbm.at[e], wd_f32, sem.at[2])]

    @pl.when(i == 0)
    def _():
        for cp in weight_copies(expert):
            cp.start()

    @pl.when(jnp.logical_and(n_valid > 0, changed))
    def _():
        for cp in weight_copies(expert):
            cp.wait()
        wg_bf[...] = wg_f32[...].astype(BF16)
        wu_bf[...] = wu_f32[...].astype(BF16)
        wd_bf[...] = wd_f32[...].astype(BF16)
        upcoming = nx_ref[i]

        @pl.when(upcoming >= 0)
        def _():
            for cp in weight_copies(upcoming):
                cp.start()

    def swiglu_rows(n_rows):
        x_copy(i, slot, n_rows).wait()
        upcoming_rows = nr_ref[jnp.minimum(i + 1, pl.num_programs(0) - 1)]
        for_rows(jnp.where(i < last, upcoming_rows, 0), lambda n_next: x_copy(i + 1, 1 - slot, n_next).start())
        packed = _load_rows(x_buf.at[slot], n_rows)
        x = jnp.concatenate([_unpack_rows(packed, 0), _unpack_rows(packed, 1)], axis=1)
        rows = lax.broadcasted_iota(I32, (n_rows, 1), 0)
        x = jnp.where(rows < n_valid, x, 0.0).astype(BF16)
        g = _dot(x, wg_bf[...])
        u = _dot(x, wu_bf[...])
        act = (g * _sigmoid(g) * u).astype(BF16)
        _store_rows(y_buf.at[slot], _pack_rows(_dot(act, wd_bf[...])))
        y_copy(i, slot, n_rows).start()

    for_rows(nr_ref[i], swiglu_rows)

    @pl.when(i == last)
    def _():
        for_rows(nr_ref[i], lambda n_rows: y_copy(i, slot, n_rows).wait())

        @pl.when(i >= 1)
        def _():
            for_rows(nr_ref[jnp.maximum(i - 1, 0)], lambda n_rows: y_copy(i - 1, 1 - slot, n_rows).wait())


def _experts(block_expert, block_valid, next_expert, last_block, xs, w_g, w_u, w_d, blk):
    n_blocks = block_expert.shape[0]
    d, de = w_g.shape[1], w_g.shape[2]
    block_rows = (block_valid + EXPERT_SUB - 1) // EXPERT_SUB * EXPERT_SUB
    hbm = pl.BlockSpec(memory_space=pl.ANY)
    return pl.pallas_call(
        _expert_kernel,
        out_shape=jax.ShapeDtypeStruct(xs.shape, U32),
        grid_spec=pltpu.PrefetchScalarGridSpec(
            num_scalar_prefetch=5,
            grid=(n_blocks,),
            in_specs=[hbm, hbm, hbm, hbm],
            out_specs=hbm,
            scratch_shapes=[pltpu.VMEM((2, blk * ROW_TILE, LANES), U32),
                            pltpu.VMEM((2, blk * ROW_TILE, LANES), U32),
                            pltpu.VMEM((d, de), F32),
                            pltpu.VMEM((d, de), F32),
                            pltpu.VMEM((de, d), F32),
                            pltpu.VMEM((d, de), BF16),
                            pltpu.VMEM((d, de), BF16),
                            pltpu.VMEM((de, d), BF16),
                            pltpu.SemaphoreType.DMA((3,)),
                            pltpu.SemaphoreType.DMA((2,)),
                            pltpu.SemaphoreType.DMA((2,))]),
        compiler_params=pltpu.CompilerParams(dimension_semantics=("arbitrary",),
                                             vmem_limit_bytes=VMEM_LIMIT),
        name="experts",
    )(block_expert, block_valid, block_rows, next_expert, last_block, xs, w_g, w_u, w_d)


def _combine_kernel(d_ref, dn_ref, gate_ref, xs_ref, mod_ref, ys_ref, o_ref, buf, sem):
    tc = d_ref.shape[1]
    i = pl.program_id(0)
    slot = lax.rem(i, 2)

    def gather(dest_ref, into):
        def issue(t, carry):
            for k in range(TOP_K):
                pltpu.make_async_copy(_row(ys_ref, dest_ref[k, t]), _row(buf.at[into], t * COMBINE_PITCH + k),
                                      sem.at[into]).start(priority=k % 2)
            return carry
        lax.fori_loop(0, tc, issue, 0)

    @pl.when(i == 0)
    def _():
        gather(d_ref, 0)

    @pl.when(i + 1 < pl.num_programs(0))
    def _():
        gather(dn_ref, 1 - slot)

    gathered = TOP_K * tc * ROW_TILE
    pltpu.make_async_copy(ys_ref.at[pl.ds(0, gathered), :], buf.at[slot, pl.ds(0, gathered), :],
                          sem.at[slot]).wait()

    gate2 = mod_ref[0][5:6]
    rows = buf.at[slot]
    acc = [jnp.zeros((tc, LANES), F32) for _ in range(2 * ROW_TILE)]
    for k in range(TOP_K):
        gk = jnp.transpose(jnp.broadcast_to(gate_ref[k:k + 1, :], (LANES, tc)))
        for s in range(ROW_TILE):
            w = rows[pl.ds(k * ROW_TILE + s, tc, stride=COMBINE_PITCH * ROW_TILE), :]
            acc[s] = acc[s] + gk * _unpack_rows(w, 0)
            acc[ROW_TILE + s] = acc[ROW_TILE + s] + gk * _unpack_rows(w, 1)
    o_ref[...] = xs_ref[...] + gate2 * jnp.concatenate(acc, axis=1)


def _combine(dest, gate, xs, mod3, ys, seq_len, tc):
    n_tok, d = xs.shape
    tiles_per_seq = seq_len // tc
    n_tiles = n_tok // tc
    return pl.pallas_call(
        _combine_kernel,
        out_shape=jax.ShapeDtypeStruct((n_tok, d), F32),
        grid=(n_tiles,),
        in_specs=[pl.BlockSpec((TOP_K, tc), lambda i: (0, i), memory_space=pltpu.SMEM),
                  pl.BlockSpec((TOP_K, tc), lambda i: (0, jnp.minimum(i + 1, n_tiles - 1)),
                               memory_space=pltpu.SMEM),
                  pl.BlockSpec((TOP_K, tc), lambda i: (0, i)),
                  pl.BlockSpec((tc, d), lambda i: (i, 0)),
                  pl.BlockSpec((1, 6, d), lambda i: (i // tiles_per_seq, 0, 0)),
                  pl.BlockSpec(memory_space=pl.ANY)],
        out_specs=pl.BlockSpec((tc, d), lambda i: (i, 0)),
        scratch_shapes=[pltpu.VMEM((2, COMBINE_PITCH * tc * ROW_TILE, LANES), U32),
                        pltpu.SemaphoreType.DMA((2,))],
        compiler_params=pltpu.CompilerParams(dimension_semantics=("arbitrary",),
                                             vmem_limit_bytes=VMEM_LIMIT),
        name="combine",
    )(dest, dest, gate, xs, mod3, ys)


def _tiles(seq_len):
    pick = lambda pref: min(pref, seq_len)
    return dict(adaln=1024, inproj=pick(512), attn_q=pick(512), attn_k=pick(512), ssm=pick(2048), mix=pick(512),
                dispatch=pick(512), combine=pick(256), slots=pick(2048), expert_block=4 * EXPERT_SUB)


def _rope_tables(seq_len):
    half = ATT_HEAD_DIM // 2
    inv_freq = 1.0 / (ROPE_THETA ** (jnp.arange(0, ATT_HEAD_DIM, 2, dtype=F32) / ATT_HEAD_DIM))
    ang = jnp.arange(seq_len, dtype=F32)[:, None] * inv_freq[None, :]
    cos, sin = jnp.cos(ang), jnp.sin(ang)
    reps = LANES // half
    sign = jnp.tile(jnp.concatenate([-jnp.ones((half,), F32), jnp.ones((half,), F32)]), reps // 2)
    return jnp.tile(cos, (1, reps)), jnp.tile(sin, (1, reps)) * sign[None, :]


def _block_tables(counts, blk, n_blocks):
    padded = (counts + blk - 1) // blk * blk
    pad_ends = jnp.cumsum(padded)
    pad_starts = pad_ends - padded
    used = pad_ends[-1] // blk
    last = jnp.maximum(used - 1, 0)
    starts = jnp.arange(n_blocks, dtype=I32) * blk
    expert = jnp.sum((pad_ends[None, :] <= starts[:, None]).astype(I32), axis=1)
    expert = jnp.minimum(expert, N_EXPERTS - 1)
    expert = jnp.where(starts < pad_ends[-1], expert, expert[last])
    valid = jnp.clip(counts[expert] - (starts - pad_starts[expert]), 0, blk)
    valid = jnp.where(starts < pad_ends[-1], valid, 0).astype(I32)
    ids = jnp.arange(N_EXPERTS, dtype=I32)
    later_used = jnp.logical_and(ids[None, :] > expert[:, None], (counts > 0)[None, :])
    upcoming = jnp.min(jnp.where(later_used, ids[None, :], N_EXPERTS), axis=1)
    upcoming = jnp.where(upcoming < N_EXPERTS, upcoming, -1).astype(I32)
    return pad_starts.astype(I32), expert, valid, upcoming, last.reshape(1).astype(I32)


def kernel(x, c, norm1_g, norm2_g, w_ada, b_ada, w_in, q_norm_g, k_norm_g, lambda_q1, lambda_k1, lambda_q2, lambda_k2, subln_g, ssm_a_re, ssm_a_im, ssm_log_dt, ssm_b_re, ssm_b_im, ssm_c_re, ssm_c_im, ssm_d, w_glu, ssm_norm_g, w_out, w_router, router_bias, w_gate_e, w_up_e, w_down_e, w_gate_s, w_up_s, w_down_s):
    bsz, seq_len, d = x.shape
    n_tok = bsz * seq_len
    tiles = _tiles(seq_len)
    assert d == 2 * ROW_TILE * LANES, "packed token rows are ROW_TILE x 128 words of two bf16 each"
    assert w_in.shape[1:] == (d, 2 * QK_WIDTH + ATT_WIDTH + SSM_WIDTH) and w_ada.shape[2] % tiles['adaln'] == 0
    assert w_router.shape[2] == N_EXPERTS and w_gate_e.shape[1] == N_EXPERTS
    assert all(seq_len % tiles[t] == 0 for t in ('inproj', 'attn_q', 'attn_k', 'ssm', 'mix', 'dispatch',
                                                 'combine', 'slots')), "sequence length must be tile aligned"
    blk = tiles['expert_block']
    n_blocks = (n_tok * TOP_K + N_EXPERTS * (blk - 1) + blk - 1) // blk
    cos_t, sin_t = _rope_tables(seq_len)
    seg = jnp.kron(jnp.eye(QK_WIDTH // ATT_HEAD_DIM, dtype=F32),
                   jnp.full((ATT_HEAD_DIM, ATT_HEAD_DIM), 1.0 / ATT_HEAD_DIM, F32)).astype(BF16)
    reps = QK_WIDTH // ATT_HEAD_DIM
    x2d = x.reshape(n_tok, d).astype(F32)
    for layer in range(w_ada.shape[0]):
        lam_init = 0.8 - 0.6 * math.exp(-0.3 * layer)
        mod3 = _adaln(c.astype(F32), w_ada[layer].astype(F32), b_ada[layer].astype(F32),
                      tiles['adaln']).reshape(bsz, 6, d)
        q, k, v, u = _inproj(
            x2d, mod3, norm1_g[layer].astype(F32).reshape(1, d), w_in[layer].astype(BF16), seg,
            jnp.tile(q_norm_g[layer].astype(F32), reps).reshape(1, QK_WIDTH),
            jnp.tile(k_norm_g[layer].astype(F32), reps).reshape(1, QK_WIDTH),
            cos_t, sin_t, seq_len, tiles['inproj'])
        lam = (jnp.exp(jnp.sum(lambda_q1[layer].astype(F32) * lambda_k1[layer].astype(F32)))
               - jnp.exp(jnp.sum(lambda_q2[layer].astype(F32) * lambda_k2[layer].astype(F32))) + lam_init)
        att = _attention(q, k, v, jnp.full((1, LANES), lam, F32),
                         subln_g[layer].astype(F32).reshape(1, ATT_V_DIM),
                         bsz, seq_len, tiles['attn_q'], tiles['attn_k'], 1.0 - lam_init)
        kmat, bmat, cmat, bpow = _ssm_mats(ssm_a_re[layer], ssm_a_im[layer], ssm_log_dt[layer],
                                         ssm_b_re[layer], ssm_b_im[layer], ssm_c_re[layer],
                                         ssm_c_im[layer], ssm_d[layer])
        y = _ssm(u, kmat, bmat, cmat, bpow, bsz, seq_len, tiles['ssm'])
        wr_t = jnp.transpose(w_router[layer].astype(F32))
        wrh, wrl = _split_bf16(wr_t)
        xs, h2p, eidx, gate, rank, counts = _mix_route(
            x2d, y, att, mod3, w_glu[layer].astype(BF16),
            ssm_norm_g[layer].astype(F32).reshape(1, SSM_WIDTH),
            w_out[layer, :ATT_WIDTH].astype(BF16), w_out[layer, ATT_WIDTH:].astype(BF16),
            norm2_g[layer].astype(F32).reshape(1, d), wrh, wrl,
            router_bias[layer].astype(F32).reshape(N_EXPERTS, 1),
            jnp.concatenate([w_gate_s[layer], w_up_s[layer]], axis=1).astype(BF16),
            w_down_s[layer].astype(BF16), seq_len, tiles['mix'])
        pad_starts, block_expert, block_valid, next_expert, last_block = _block_tables(
            counts.reshape(N_EXPERTS).astype(I32), blk, n_blocks)
        dest = _slots(eidx, rank, pad_starts, tiles['slots'])
        dest_by_token = jnp.transpose(dest).reshape(n_tok * TOP_K)
        x_slots = _dispatch(dest_by_token, h2p, n_blocks * blk, tiles['dispatch'])
        y_slots = _experts(block_expert, block_valid, next_expert, last_block, x_slots,
                           w_gate_e[layer], w_up_e[layer], w_down_e[layer], blk)
        x2d = _combine(dest, gate, xs, mod3, y_slots, seq_len, tiles['combine'])
    return x2d.reshape(bsz, seq_len, d).astype(x.dtype)
```

```python
import functools
import math

import jax
import jax.numpy as jnp
from jax import lax
from jax.experimental import pallas as pl
from jax.experimental.pallas import tpu as pltpu

F32 = jnp.float32
BF16 = jnp.bfloat16
I32 = jnp.int32
U32 = jnp.uint32

LANES = 128
SUBLANES = 8
V7X_VMEM_BYTES = 64 * 1024 * 1024

N_ATT_HEADS = 4
ATT_HEAD_DIM = 64
ATT_V_DIM = 2 * ATT_HEAD_DIM
QK_WIDTH = N_ATT_HEADS * 2 * ATT_HEAD_DIM
ATT_WIDTH = N_ATT_HEADS * ATT_V_DIM
ROPE_THETA = 10000.0
SSM_GROUP = 16
SSM_GROUPS = 32
SSM_STATE = 64
SSM_WIDTH = SSM_GROUPS * SSM_GROUP
SSM_CHUNK = SUBLANES
SSM_LANE_BLOCKS = SSM_WIDTH // LANES
GROUPS_PER_BLOCK = LANES // SSM_GROUP
N_EXPERTS = 256
TOP_K = 8
N_GROUPS = 8
TOPK_GROUPS = 4
EXPERTS_PER_GROUP = N_EXPERTS // N_GROUPS
SHARED_DIM = 256
ROUTED_SCALE = 2.5
EPS = 1e-6

NT_DIMS = (((1,), (1,)), ((), ()))

VMEM_LIMIT = V7X_VMEM_BYTES * 3 // 4


def _dot(a, b):
    return jnp.dot(a, b, preferred_element_type=F32)


def _sigmoid(x):
    return 1.0 / (1.0 + jnp.exp(-x))


def _split_bf16(x):
    hi = x.astype(BF16)
    lo = (x - hi.astype(F32)).astype(BF16)
    return hi, lo


def _adaln_kernel(c_ref, w_ref, b_ref, o_ref):
    c = c_ref[...]
    sc = c * _sigmoid(c)
    o_ref[...] = jnp.dot(sc, w_ref[...], preferred_element_type=F32,
                         precision=lax.Precision.HIGHEST) + b_ref[...]


def _adaln(c, w, b, tn):
    bsz, d = c.shape
    n = w.shape[1]
    return pl.pallas_call(
        _adaln_kernel,
        out_shape=jax.ShapeDtypeStruct((bsz, n), F32),
        grid=(n // tn,),
        in_specs=[pl.BlockSpec((bsz, d), lambda j: (0, 0)),
                  pl.BlockSpec((d, tn), lambda j: (0, j)),
                  pl.BlockSpec((1, tn), lambda j: (0, j))],
        out_specs=pl.BlockSpec((bsz, tn), lambda j: (0, j)),
        compiler_params=pltpu.CompilerParams(dimension_semantics=("arbitrary",),
                                             vmem_limit_bytes=VMEM_LIMIT),
        name="adaln",
    )(c, w, b.reshape(1, n))


def _inproj_kernel(x_ref, mod_ref, n1g_ref, w_ref, seg_ref, qg_ref, kg_ref, cos_ref, sin_ref,
                   q_ref, k_ref, v_ref, u_ref):
    x = x_ref[...]
    mod = mod_ref[0]
    shift, scale = mod[0:1], mod[1:2]
    ms = jnp.mean(x * x, axis=-1, keepdims=True)
    h = x * lax.rsqrt(ms + EPS) * n1g_ref[...]
    h = h * (1.0 + scale) + shift
    z = _dot(h.astype(BF16), w_ref[...])
    seg = seg_ref[...]
    cos = cos_ref[...]
    sin = sin_ref[...]
    lane = lax.broadcasted_iota(I32, cos.shape, 1)
    first_half = (lane % ATT_HEAD_DIM) < (ATT_HEAD_DIM // 2)

    def norm_rope(t, g_ref, out_scale):
        hi, lo = _split_bf16(t * t)
        msq = _dot(hi, seg) + _dot(lo, seg)
        tn = t * lax.rsqrt(msq + EPS) * g_ref[...]
        outs = []
        for hd in range(N_ATT_HEADS):
            th = tn[:, hd * LANES:(hd + 1) * LANES]
            partner = jnp.where(first_half,
                                pltpu.roll(th, LANES - ATT_HEAD_DIM // 2, 1),
                                pltpu.roll(th, ATT_HEAD_DIM // 2, 1))
            outs.append((th * cos + partner * sin) * out_scale)
        return jnp.concatenate(outs, axis=1)

    q_ref[...] = norm_rope(z[:, :QK_WIDTH], qg_ref, ATT_HEAD_DIM ** -0.5).astype(BF16)
    k_ref[...] = norm_rope(z[:, QK_WIDTH:2 * QK_WIDTH], kg_ref, 1.0).astype(BF16)
    v_ref[...] = z[:, 2 * QK_WIDTH:2 * QK_WIDTH + ATT_WIDTH].astype(BF16)
    u_ref[...] = z[:, 2 * QK_WIDTH + ATT_WIDTH:]


def _inproj(x2d, mod3, n1g, w_in_bf, seg, qg, kg, cos_t, sin_t, seq_len, tm):
    n_tok, d = x2d.shape
    tiles_per_seq = seq_len // tm
    in_width = w_in_bf.shape[1]
    row = lambda i: (i, 0)
    const = lambda i: (0, 0)
    return pl.pallas_call(
        _inproj_kernel,
        out_shape=(jax.ShapeDtypeStruct((n_tok, QK_WIDTH), BF16),
                   jax.ShapeDtypeStruct((n_tok, QK_WIDTH), BF16),
                   jax.ShapeDtypeStruct((n_tok, ATT_WIDTH), BF16),
                   jax.ShapeDtypeStruct((n_tok, SSM_WIDTH), F32)),
        grid=(n_tok // tm,),
        in_specs=[pl.BlockSpec((tm, d), row),
                  pl.BlockSpec((1, 6, d), lambda i: (i // tiles_per_seq, 0, 0)),
                  pl.BlockSpec((1, d), const),
                  pl.BlockSpec((d, in_width), const),
                  pl.BlockSpec((QK_WIDTH, QK_WIDTH), const),
                  pl.BlockSpec((1, QK_WIDTH), const),
                  pl.BlockSpec((1, QK_WIDTH), const),
                  pl.BlockSpec((tm, LANES), lambda i: (i % tiles_per_seq, 0)),
                  pl.BlockSpec((tm, LANES), lambda i: (i % tiles_per_seq, 0))],
        out_specs=(pl.BlockSpec((tm, QK_WIDTH), row),
                   pl.BlockSpec((tm, QK_WIDTH), row),
                   pl.BlockSpec((tm, ATT_WIDTH), row),
                   pl.BlockSpec((tm, SSM_WIDTH), row)),
        compiler_params=pltpu.CompilerParams(dimension_semantics=("arbitrary",),
                                             vmem_limit_bytes=VMEM_LIMIT),
        name="inproj",
    )(x2d, mod3, n1g, w_in_bf, seg, qg, kg, cos_t, sin_t)


def _attn_kernel(qi_ref, kj_ref, q_ref, k_ref, v_ref, lam_ref, sg_ref, o_ref,
                 m_ref, l_ref, acc_ref, *, out_scale):
    p = pl.program_id(2)
    qi = qi_ref[p]
    kj = kj_ref[p]

    @pl.when(kj == 0)
    def _():
        m_ref[...] = jnp.full(m_ref.shape, -jnp.inf, F32)
        l_ref[...] = jnp.zeros(l_ref.shape, F32)
        acc_ref[...] = jnp.zeros(acc_ref.shape, F32)

    t = q_ref.shape[0]
    h = t // 2

    def stacked_queries():
        q = q_ref[...]
        lane = lax.broadcasted_iota(I32, q.shape, 1)
        zero = jnp.zeros_like(q)
        q0 = jnp.where(lane < ATT_HEAD_DIM, q, zero)
        q1 = jnp.where(lane >= ATT_HEAD_DIM, q, zero)
        return jnp.concatenate([q0[:h], q1[:h], q0[h:], q1[h:]], axis=0)

    def update(lo, hi, s, v):
        m_old = m_ref[lo:hi]
        m_new = jnp.maximum(m_old, jnp.max(s, axis=-1, keepdims=True))
        alpha = jnp.exp(m_old - m_new)
        pr = jnp.exp(s - jnp.concatenate([m_new] * (s.shape[1] // LANES), axis=1))
        l_ref[lo:hi] = alpha * l_ref[lo:hi] + jnp.sum(pr, axis=-1, keepdims=True)
        acc_ref[lo:hi] = alpha * acc_ref[lo:hi] + _dot(pr.astype(BF16), v)
        m_ref[lo:hi] = m_new

    @pl.when(kj < qi)
    def _():
        s = lax.dot_general(stacked_queries(), k_ref[...], NT_DIMS, preferred_element_type=F32)
        update(0, 2 * t, s, v_ref[...])

    @pl.when(kj == qi)
    def _():
        qq = stacked_queries()
        k = k_ref[...]
        v = v_ref[...]
        r = lax.broadcasted_iota(I32, (h, t), 0)
        col = lax.broadcasted_iota(I32, (h, t), 1)
        early_ok = (col <= r)[:, :h]
        late_ok = col <= r + h
        s = lax.dot_general(qq[:t], k[:h], NT_DIMS, preferred_element_type=F32)
        update(0, t, jnp.where(jnp.concatenate([early_ok, early_ok], axis=0), s, -jnp.inf), v[:h])
        s = lax.dot_general(qq[t:], k, NT_DIMS, preferred_element_type=F32)
        update(t, 2 * t, jnp.where(jnp.concatenate([late_ok, late_ok], axis=0), s, -jnp.inf), v)

        o = acc_ref[...] / l_ref[...]
        o = (jnp.concatenate([o[:h], o[t:t + h]], axis=0)
             - lam_ref[...] * jnp.concatenate([o[h:t], o[t + h:]], axis=0))
        ms = jnp.mean(o * o, axis=-1, keepdims=True)
        o_ref[...] = (o * lax.rsqrt(ms + EPS) * sg_ref[...] * out_scale).astype(o_ref.dtype)


def _attention(q, k, v, lam_row, subln_g, bsz, seq_len, tq, out_scale):
    nq = seq_len // tq
    pairs = [(i, j) for i in range(nq) for j in range(i + 1)]
    qi = jnp.asarray([p[0] for p in pairs], I32)
    kj = jnp.asarray([p[1] for p in pairs], I32)
    q_map = lambda b, h, p, qi, kj: (b * nq + qi[p], h)
    k_map = lambda b, h, p, qi, kj: (b * nq + kj[p], h)
    const = lambda b, h, p, qi, kj: (0, 0)
    return pl.pallas_call(
        functools.partial(_attn_kernel, out_scale=out_scale),
        out_shape=jax.ShapeDtypeStruct((bsz * seq_len, ATT_WIDTH), BF16),
        grid_spec=pltpu.PrefetchScalarGridSpec(
            num_scalar_prefetch=2,
            grid=(bsz, N_ATT_HEADS, len(pairs)),
            in_specs=[pl.BlockSpec((tq, LANES), q_map),
                      pl.BlockSpec((tq, LANES), k_map),
                      pl.BlockSpec((tq, LANES), k_map),
                      pl.BlockSpec((1, LANES), const),
                      pl.BlockSpec((1, LANES), const)],
            out_specs=pl.BlockSpec((tq, LANES), q_map),
            scratch_shapes=[pltpu.VMEM((2 * tq, LANES), F32),
                            pltpu.VMEM((2 * tq, LANES), F32),
                            pltpu.VMEM((2 * tq, ATT_V_DIM), F32)]),
        compiler_params=pltpu.CompilerParams(
            dimension_semantics=("arbitrary", "arbitrary", "arbitrary"),
            vmem_limit_bytes=VMEM_LIMIT),
        name="attn",
    )(qi, kj, q, k, v, lam_row, subln_g)


def _ssm_mats(a_re, a_im, log_dt, b_re, b_im, c_re, c_im, d_skip):
    a_re, a_im, b_re, b_im, c_re, c_im, d_skip = (
        t.astype(F32) for t in (a_re, a_im, b_re, b_im, c_re, c_im, d_skip))
    dt = jnp.exp(log_dt.astype(F32))[:, None]
    mag = jnp.exp(a_re * dt)
    abar_re = mag * jnp.cos(a_im * dt)
    abar_im = mag * jnp.sin(a_im * dt)
    den = a_re * a_re + a_im * a_im
    nr = abar_re - 1.0
    f_re = ((nr * a_re + abar_im * a_im) / den)[..., None]
    f_im = ((abar_im * a_re - nr * a_im) / den)[..., None]
    bb_re = f_re * b_re - f_im * b_im
    bb_im = f_re * b_im + f_im * b_re
    steps = jnp.arange(SSM_CHUNK + 1, dtype=F32)[:, None, None]
    pmag = jnp.exp(a_re * dt * steps)
    pw_re = pmag * jnp.cos(a_im * dt * steps)
    pw_im = pmag * jnp.sin(a_im * dt * steps)
    ca_re = c_re[None] * pw_re[:, :, None, :] - c_im[None] * pw_im[:, :, None, :]
    ca_im = c_re[None] * pw_im[:, :, None, :] + c_im[None] * pw_re[:, :, None, :]
    bbt_re = jnp.transpose(bb_re, (0, 2, 1))[None, :, None]
    bbt_im = jnp.transpose(bb_im, (0, 2, 1))[None, :, None]
    lag = jnp.sum(ca_re[:SSM_CHUNK, :, :, None, :] * bbt_re
                  - ca_im[:SSM_CHUNK, :, :, None, :] * bbt_im, axis=-1)
    lag = lag.at[0].add(d_skip[:, :, None] * jnp.eye(SSM_GROUP, dtype=F32)[None])
    nb, gpb = SSM_LANE_BLOCKS, GROUPS_PER_BLOCK
    n_state = gpb * SSM_STATE
    in_group = jnp.arange(LANES) // SSM_GROUP
    state_group = jnp.arange(n_state) // SSM_STATE

    def spread(small, row_group, col_group):
        w = small.shape[-1]
        tiled = jnp.tile(jnp.eye(w, dtype=BF16), (1, col_group.shape[0] // w))
        wide = jnp.einsum('...w,wn->...n', small.astype(BF16), tiled, preferred_element_type=F32)
        return jnp.where(row_group[:, None] == col_group[None, :], wide, 0.0).astype(BF16)

    lag_blocks = spread(jnp.transpose(lag, (0, 1, 3, 2)).reshape(SSM_CHUNK, nb, LANES, SSM_GROUP),
                        in_group, in_group)
    zero_block = jnp.zeros_like(lag_blocks[0])
    kmat = jnp.concatenate(
        [jnp.concatenate([lag_blocks[t - j] if t >= j else zero_block for t in range(SSM_CHUNK)], axis=-1)
         for j in range(SSM_CHUNK)], axis=1)
    rev = SSM_CHUNK - 1 - jnp.arange(SSM_CHUNK)
    w_re = pw_re[rev][..., None] * bb_re[None] - pw_im[rev][..., None] * bb_im[None]
    w_im = pw_re[rev][..., None] * bb_im[None] + pw_im[rev][..., None] * bb_re[None]

    def in_to_state(w):
        small = jnp.transpose(w, (0, 1, 3, 2)).reshape(SSM_CHUNK, nb, LANES, SSM_STATE)
        return spread(small, in_group, state_group)

    bmat = jnp.concatenate([in_to_state(w_re), in_to_state(w_im)], axis=-1)
    bmat = jnp.transpose(bmat, (1, 0, 2, 3)).reshape(nb, SSM_CHUNK * LANES, 2 * n_state)

    def state_to_out(ca):
        small = jnp.transpose(ca[1:], (0, 1, 3, 2)).reshape(SSM_CHUNK, nb, n_state, SSM_GROUP)
        blocks = spread(small, state_group, in_group)
        return jnp.transpose(blocks, (1, 2, 0, 3)).reshape(nb, n_state, SSM_CHUNK * LANES)

    cmat = jnp.concatenate([state_to_out(ca_re), -state_to_out(ca_im)], axis=1)
    chunk_steps = SSM_CHUNK * jnp.arange(2 * SUBLANES, dtype=F32)[:, None, None]
    cmag = jnp.exp(a_re * dt * chunk_steps)
    to_block = lambda t: jnp.transpose(t.reshape(2 * SUBLANES, nb, n_state), (1, 0, 2))
    bpow = jnp.concatenate([to_block(cmag * jnp.cos(a_im * dt * chunk_steps)),
                            to_block(cmag * jnp.sin(a_im * dt * chunk_steps))], axis=-1)
    return kmat, bmat, cmat, bpow


def _cmul(a_re, a_im, x_re, x_im):
    return a_re * x_re - a_im * x_im, a_re * x_im + a_im * x_re


def _ssm_kernel(u_ref, km_ref, bm_ref, cm_ref, bp_ref, y_ref, carry_ref, sp_ref):
    n_chunks = sp_ref.shape[0]
    half = sp_ref.shape[1] // 2

    @pl.when(pl.program_id(2) == 0)
    def _():
        carry_ref[...] = jnp.zeros(carry_ref.shape, F32)

    ucat = jnp.concatenate([u_ref[pl.ds(j, n_chunks, stride=SSM_CHUNK), :] for j in range(SSM_CHUNK)],
                           axis=1).astype(BF16)
    s_end = _dot(ucat, bm_ref[0])
    p_re, p_im = s_end[:, :half], s_end[:, half:]
    bp = bp_ref[0]
    sub = lax.broadcasted_iota(I32, (n_chunks, half), 0) % SUBLANES

    def shifted(v, d):
        return jnp.where(sub >= d, pltpu.roll(v, d, 0), 0.0)

    for d in (1, 2, 4):
        d_re, d_im = _cmul(bp[d:d + 1, :half], bp[d:d + 1, half:], shifted(p_re, d), shifted(p_im, d))
        p_re, p_im = p_re + d_re, p_im + d_im
    x_re, x_im = shifted(p_re, 1), shifted(p_im, 1)
    t_re, t_im = bp[:SUBLANES, :half], bp[:SUBLANES, half:]
    l_re, l_im = bp[SUBLANES:SUBLANES + 1, :half], bp[SUBLANES:SUBLANES + 1, half:]
    c_re, c_im = carry_ref[:, :half], carry_ref[:, half:]
    for g in range(n_chunks // SUBLANES):
        lo, hi = g * SUBLANES, (g + 1) * SUBLANES
        d_re, d_im = _cmul(t_re, t_im, c_re, c_im)
        sp_ref[lo:hi, :half] = x_re[lo:hi] + d_re
        sp_ref[lo:hi, half:] = x_im[lo:hi] + d_im
        e_re, e_im = _cmul(l_re, l_im, c_re, c_im)
        c_re, c_im = p_re[hi - 1:hi] + e_re, p_im[hi - 1:hi] + e_im
    carry_ref[...] = jnp.concatenate([c_re, c_im], axis=1)
    y = _dot(ucat, km_ref[0]) + _dot(sp_ref[...].astype(BF16), cm_ref[0])
    for t in range(SSM_CHUNK):
        y_ref[pl.ds(t, n_chunks, stride=SSM_CHUNK), :] = y[:, t * LANES:(t + 1) * LANES]


def _ssm(u, kmat, bmat, cmat, bpow, bsz, seq_len, tt):
    nt = seq_len // tt
    n_chunks = tt // SSM_CHUNK
    width = SSM_CHUNK * LANES
    u_map = lambda g, b, i: (b * nt + i, g)
    w_map = lambda g, b, i: (g, 0, 0)
    return pl.pallas_call(
        _ssm_kernel,
        out_shape=jax.ShapeDtypeStruct((bsz * seq_len, SSM_WIDTH), F32),
        grid=(SSM_LANE_BLOCKS, bsz, nt),
        in_specs=[pl.BlockSpec((tt, LANES), u_map),
                  pl.BlockSpec((1, width, width), w_map),
                  pl.BlockSpec((1, width, width), w_map),
                  pl.BlockSpec((1, width, width), w_map),
                  pl.BlockSpec((1, 2 * SUBLANES, width), w_map)],
        out_specs=pl.BlockSpec((tt, LANES), u_map),
        scratch_shapes=[pltpu.VMEM((1, width), F32),
                        pltpu.VMEM((n_chunks, width), F32)],
        compiler_params=pltpu.CompilerParams(
            dimension_semantics=("arbitrary", "arbitrary", "arbitrary"),
            vmem_limit_bytes=VMEM_LIMIT),
        name="ssm",
    )(u, kmat, bmat, cmat, bpow)


def _first_max(v, ids, n):
    m = jnp.max(v, axis=0, keepdims=True)
    ix = jnp.min(jnp.where(v == m, ids, n), axis=0, keepdims=True)
    return m, ix


def _mix_route_kernel(x_ref, y_ref, att_ref, mod_ref, wglu_ref, sng_ref, woa_ref, wos_ref, n2g_ref,
                      wrh_ref, wrl_ref, rb_ref, wgu_ref, wds_ref,
                      xs_ref, h2p_ref, eidx_ref, gate_ref, rank_ref, cnt_ref, carry_ref):
    tm = x_ref.shape[0]

    @pl.when(pl.program_id(0) == 0)
    def _():
        carry_ref[...] = jnp.zeros(carry_ref.shape, F32)

    mod = mod_ref[0]
    gate1, shift2, scale2, gate2 = mod[2:3], mod[3:4], mod[4:5], mod[5:6]

    y = y_ref[...]
    g = 0.5 * y * (1.0 + jnp.tanh(math.sqrt(2.0 / math.pi) * (y + 0.044715 * (y * y * y))))
    glu = g * _sigmoid(_dot(g.astype(BF16), wglu_ref[...]))
    ssm = glu * lax.rsqrt(jnp.mean(glu * glu, axis=-1, keepdims=True) + EPS) * sng_ref[...]

    mix = _dot(att_ref[...], woa_ref[...]) + _dot(ssm.astype(BF16), wos_ref[...])
    x1 = x_ref[...] + gate1 * mix
    h2 = x1 * lax.rsqrt(jnp.mean(x1 * x1, axis=-1, keepdims=True) + EPS) * n2g_ref[...]
    h2 = h2 * (1.0 + scale2) + shift2
    _store_rows(h2p_ref, _pack_rows(h2))
    hb, h_lo = _split_bf16(h2)

    gu = _dot(hb, wgu_ref[...])
    gs, us = gu[:, :SHARED_DIM], gu[:, SHARED_DIM:]
    act = (gs * _sigmoid(gs) * us).astype(BF16)
    xs_ref[...] = x1 + gate2 * _dot(act, wds_ref[...])

    wrh = wrh_ref[...]
    logits = (lax.dot_general(wrh, hb, NT_DIMS, preferred_element_type=F32)
              + lax.dot_general(wrh, h_lo, NT_DIMS, preferred_element_type=F32)
              + lax.dot_general(wrl_ref[...], hb, NT_DIMS, preferred_element_type=F32))
    score = _sigmoid(logits)
    biased = score + rb_ref[...]
    neg = -jnp.inf

    ids_g = lax.broadcasted_iota(I32, (EXPERTS_PER_GROUP, tm), 0)
    group_rows = []
    for gi in range(N_GROUPS):
        blk = biased[gi * EXPERTS_PER_GROUP:(gi + 1) * EXPERTS_PER_GROUP, :]
        m1, i1 = _first_max(blk, ids_g, EXPERTS_PER_GROUP)
        m2 = jnp.max(jnp.where(ids_g == i1, neg, blk), axis=0, keepdims=True)
        group_rows.append(m1 + m2)
    cur = jnp.concatenate(group_rows, axis=0)
    ids_8 = lax.broadcasted_iota(I32, cur.shape, 0)
    picked = jnp.zeros(cur.shape, F32)
    for _ in range(TOPK_GROUPS):
        _, ix = _first_max(cur, ids_8, N_GROUPS)
        hit = ids_8 == ix
        picked = jnp.where(hit, 1.0, picked)
        cur = jnp.where(hit, neg, cur)
    e_mask = jnp.concatenate(
        [jnp.broadcast_to(picked[gi:gi + 1, :], (EXPERTS_PER_GROUP, tm)) for gi in range(N_GROUPS)], axis=0)
    cand = jnp.where(e_mask > 0.0, biased, neg)

    ids_e = lax.broadcasted_iota(I32, cand.shape, 0)
    sel = jnp.zeros(cand.shape, F32)
    idx_rows, w_rows = [], []
    for _ in range(TOP_K):
        _, ix = _first_max(cand, ids_e, N_EXPERTS)
        hit = ids_e == ix
        idx_rows.append(ix)
        w_rows.append(jnp.sum(jnp.where(hit, score, 0.0), axis=0, keepdims=True))
        sel = jnp.where(hit, 1.0, sel)
        cand = jnp.where(hit, neg, cand)
    w_sum = w_rows[0]
    for w in w_rows[1:]:
        w_sum = w_sum + w
    gate_ref[...] = jnp.concatenate([w / w_sum * ROUTED_SCALE for w in w_rows], axis=0)
    eidx_ref[...] = jnp.concatenate(idx_rows, axis=0)

    t_row = lax.broadcasted_iota(I32, (tm, tm), 0)
    t_col = lax.broadcasted_iota(I32, (tm, tm), 1)
    earlier = jnp.where(t_row < t_col, 1.0, 0.0).astype(BF16)
    before = _dot(sel.astype(BF16), earlier) + carry_ref[...]
    rank_ref[...] = jnp.concatenate(
        [jnp.sum(jnp.where(ids_e == ix, before, 0.0), axis=0, keepdims=True) for ix in idx_rows],
        axis=0).astype(I32)
    carry_ref[...] = carry_ref[...] + jnp.sum(sel, axis=1, keepdims=True)
    cnt_ref[...] = carry_ref[...]


def _mix_route(x2d, y, att, mod3, wglu, sng, woa, wos, n2g, wrh, wrl, rb, wgu, wds, seq_len, tm):
    n_tok, d = x2d.shape
    tiles_per_seq = seq_len // tm
    row = lambda i: (i, 0)
    col = lambda i: (0, i)
    const = lambda i: (0, 0)
    full = lambda a: pl.BlockSpec(a.shape, const)
    return pl.pallas_call(
        _mix_route_kernel,
        out_shape=(jax.ShapeDtypeStruct((n_tok, d), F32),
                   jax.ShapeDtypeStruct((n_tok * ROW_TILE, LANES), U32),
                   jax.ShapeDtypeStruct((TOP_K, n_tok), I32),
                   jax.ShapeDtypeStruct((TOP_K, n_tok), F32),
                   jax.ShapeDtypeStruct((TOP_K, n_tok), I32),
                   jax.ShapeDtypeStruct((N_EXPERTS, 1), F32)),
        grid=(n_tok // tm,),
        in_specs=[pl.BlockSpec((tm, d), row),
                  pl.BlockSpec((tm, SSM_WIDTH), row),
                  pl.BlockSpec((tm, ATT_WIDTH), row),
                  pl.BlockSpec((1, 6, d), lambda i: (i // tiles_per_seq, 0, 0)),
                  full(wglu), full(sng), full(woa), full(wos), full(n2g),
                  full(wrh), full(wrl), full(rb), full(wgu), full(wds)],
        out_specs=(pl.BlockSpec((tm, d), row),
                   pl.BlockSpec((tm * ROW_TILE, LANES), row),
                   pl.BlockSpec((TOP_K, tm), col),
                   pl.BlockSpec((TOP_K, tm), col),
                   pl.BlockSpec((TOP_K, tm), col),
                   pl.BlockSpec((N_EXPERTS, 1), const)),
        scratch_shapes=[pltpu.VMEM((N_EXPERTS, 1), F32)],
        compiler_params=pltpu.CompilerParams(dimension_semantics=("arbitrary",),
                                             vmem_limit_bytes=VMEM_LIMIT),
        name="mix_route",
    )(x2d, y, att, mod3, wglu, sng, woa, wos, n2g, wrh, wrl, rb, wgu, wds)


ROW_TILE = 4
EXPERT_SUB = 256
COMBINE_PITCH = TOP_K + 1


def _pack_rows(v):
    half = v.shape[1] // 2
    return pltpu.pack_elementwise([v[:, :half], v[:, half:]], packed_dtype=BF16)


def _unpack_rows(w, index):
    return pltpu.unpack_elementwise(w, index=index, packed_dtype=BF16, unpacked_dtype=F32)


def _store_rows(ref, packed, first=0):
    m = packed.shape[0]
    for s in range(ROW_TILE):
        ref[pl.ds(first * ROW_TILE + s, m, stride=ROW_TILE), :] = packed[:, s * LANES:(s + 1) * LANES]


def _load_rows(ref, m, first=0):
    return jnp.concatenate([ref[pl.ds(first * ROW_TILE + s, m, stride=ROW_TILE), :] for s in range(ROW_TILE)],
                           axis=1)


def _row(ref, r):
    return ref.at[pl.ds(pl.multiple_of(r * ROW_TILE, ROW_TILE), ROW_TILE), :]


def _slot_kernel(e_ref, r_ref, ps_ref, d_ref):
    e = e_ref[...]
    ids = lax.broadcasted_iota(I32, (N_EXPERTS, e.shape[1]), 0)
    starts = ps_ref[...]
    rows = [jnp.sum(jnp.where(ids == e[k:k + 1, :], starts, 0.0), axis=0, keepdims=True)
            for k in range(TOP_K)]
    d_ref[...] = jnp.concatenate(rows, axis=0).astype(I32) + r_ref[...]


def _slots(eidx, rank, pad_starts, ts):
    n_tok = eidx.shape[1]
    col = lambda i: (0, i)
    return pl.pallas_call(
        _slot_kernel,
        out_shape=jax.ShapeDtypeStruct((TOP_K, n_tok), I32),
        grid=(n_tok // ts,),
        in_specs=[pl.BlockSpec((TOP_K, ts), col),
                  pl.BlockSpec((TOP_K, ts), col),
                  pl.BlockSpec((N_EXPERTS, 1), lambda i: (0, 0))],
        out_specs=pl.BlockSpec((TOP_K, ts), col),
        compiler_params=pltpu.CompilerParams(dimension_semantics=("arbitrary",),
                                             vmem_limit_bytes=VMEM_LIMIT),
        name="slots",
    )(eidx, rank, pad_starts.astype(F32).reshape(N_EXPERTS, 1))


def _dispatch_kernel(d_ref, h2p_ref, xs_ref, sem):
    td = d_ref.shape[1]

    def issue(t, carry):
        for k in range(TOP_K):
            pltpu.make_async_copy(_row(h2p_ref, t), _row(xs_ref, d_ref[k, t]), sem).start(priority=k % 2)
        return carry

    lax.fori_loop(0, td, issue, 0)
    for _ in range(TOP_K):
        pltpu.make_async_copy(h2p_ref, xs_ref.at[pl.ds(0, td * ROW_TILE), :], sem).wait()


def _dispatch(dest, h2p, n_slots, td):
    n_tok = dest.shape[1]
    return pl.pallas_call(
        _dispatch_kernel,
        out_shape=jax.ShapeDtypeStruct((n_slots * ROW_TILE, LANES), U32),
        grid=(n_tok // td,),
        in_specs=[pl.BlockSpec((TOP_K, td), lambda i: (0, i), memory_space=pltpu.SMEM),
                  pl.BlockSpec((td * ROW_TILE, LANES), lambda i: (i, 0))],
        out_specs=pl.BlockSpec(memory_space=pl.ANY),
        scratch_shapes=[pltpu.SemaphoreType.DMA],
        compiler_params=pltpu.CompilerParams(dimension_semantics=("arbitrary",),
                                             vmem_limit_bytes=VMEM_LIMIT),
        name="dispatch",
    )(dest, h2p)


def _expert_kernel(be_ref, nv_ref, nr_ref, nx_ref, last_ref, xs_hbm, wg_hbm, wu_hbm, wd_hbm, ys_hbm,
                   x_buf, y_buf, wg_f32, wu_f32, wd_f32, wg_bf, wu_bf, wd_bf, sem, x_sem, y_sem):
    i = pl.program_id(0)
    blk = x_buf.shape[1] // ROW_TILE
    sizes = range(EXPERT_SUB, blk + 1, EXPERT_SUB)
    expert = be_ref[i]
    n_valid = nv_ref[i]
    last = last_ref[0]
    slot = lax.rem(i, 2)
    changed = jnp.logical_or(i == 0, expert != be_ref[jnp.maximum(i - 1, 0)])

    def block_rows(ref, block, n_rows):
        return ref.at[pl.ds(pl.multiple_of(block * (blk * ROW_TILE), blk * ROW_TILE), n_rows * ROW_TILE), :]

    def x_copy(block, into, n_rows):
        return pltpu.make_async_copy(block_rows(xs_hbm, block, n_rows),
                                     x_buf.at[into, pl.ds(0, n_rows * ROW_TILE), :], x_sem.at[into])

    def y_copy(block, out_of, n_rows):
        return pltpu.make_async_copy(y_buf.at[out_of, pl.ds(0, n_rows * ROW_TILE), :],
                                     block_rows(ys_hbm, block, n_rows), y_sem.at[out_of])

    def for_rows(n, fn):
        for n_rows in sizes:
            pl.when(n == n_rows)(functools.partial(fn, n_rows))

    @pl.when(i == 0)
    def _():
        for_rows(nr_ref[0], lambda n_rows: x_copy(0, 0, n_rows).start())

    @pl.when(jnp.logical_and(i >= 2, i <= last))
    def _():
        for_rows(nr_ref[jnp.maximum(i - 2, 0)], lambda n_rows: y_copy(i - 2, slot, n_rows).wait())

    def weight_copies(e):
        return [pltpu.make_async_copy(wg_hbm.at[e], wg_f32, sem.at[0]),
                pltpu.make_async_copy(wu_hbm.at[e], wu_f32, sem.at[1]),
                pltpu.make_async_copy(wd_hbm.at[e], wd_f32, sem.at[2])]

    @pl.when(i == 0)
    def _():
        for cp in weight_copies(expert):
            cp.start()

    @pl.when(jnp.logical_and(n_valid > 0, changed))
    def _():
        for cp in weight_copies(expert):
            cp.wait()
        wg_bf[...] = wg_f32[...].astype(BF16)
        wu_bf[...] = wu_f32[...].astype(BF16)
        wd_bf[...] = wd_f32[...].astype(BF16)
        upcoming = nx_ref[i]

        @pl.when(upcoming >= 0)
        def _():
            for cp in weight_copies(upcoming):
                cp.start()

    def swiglu_rows(n_rows):
        x_copy(i, slot, n_rows).wait()
        upcoming_rows = nr_ref[jnp.minimum(i + 1, pl.num_programs(0) - 1)]
        for_rows(jnp.where(i < last, upcoming_rows, 0), lambda n_next: x_copy(i + 1, 1 - slot, n_next).start())
        packed = _load_rows(x_buf.at[slot], n_rows)
        x = jnp.concatenate([_unpack_rows(packed, 0), _unpack_rows(packed, 1)], axis=1)
        rows = lax.broadcasted_iota(I32, (n_rows, 1), 0)
        x = jnp.where(rows < n_valid, x, 0.0).astype(BF16)
        g = _dot(x, wg_bf[...])
        u = _dot(x, wu_bf[...])
        act = (g * _sigmoid(g) * u).astype(BF16)
        _store_rows(y_buf.at[slot], _pack_rows(_dot(act, wd_bf[...])))
        y_copy(i, slot, n_rows).start()

    for_rows(nr_ref[i], swiglu_rows)

    @pl.when(i == last)
    def _():
        for_rows(nr_ref[i], lambda n_rows: y_copy(i, slot, n_rows).wait())

        @pl.when(i >= 1)
        def _():
            for_rows(nr_ref[jnp.maximum(i - 1, 0)], lambda n_rows: y_copy(i - 1, 1 - slot, n_rows).wait())


def _experts(block_expert, block_valid, next_expert, last_block, xs, w_g, w_u, w_d, blk):
    n_blocks = block_expert.shape[0]
    d, de = w_g.shape[1], w_g.shape[2]
    block_rows = (block_valid + EXPERT_SUB - 1) // EXPERT_SUB * EXPERT_SUB
    hbm = pl.BlockSpec(memory_space=pl.ANY)
    return pl.pallas_call(
        _expert_kernel,
        out_shape=jax.ShapeDtypeStruct(xs.shape, U32),
        grid_spec=pltpu.PrefetchScalarGridSpec(
            num_scalar_prefetch=5,
            grid=(n_blocks,),
            in_specs=[hbm, hbm, hbm, hbm],
            out_specs=hbm,
            scratch_shapes=[pltpu.VMEM((2, blk * ROW_TILE, LANES), U32),
                            pltpu.VMEM((2, blk * ROW_TILE, LANES), U32),
                            pltpu.VMEM((d, de), F32),
                            pltpu.VMEM((d, de), F32),
                            pltpu.VMEM((de, d), F32),
                            pltpu.VMEM((d, de), BF16),
                            pltpu.VMEM((d, de), BF16),
                            pltpu.VMEM((de, d), BF16),
                            pltpu.SemaphoreType.DMA((3,)),
                            pltpu.SemaphoreType.DMA((2,)),
                            pltpu.SemaphoreType.DMA((2,))]),
        compiler_params=pltpu.CompilerParams(dimension_semantics=("arbitrary",),
                                             vmem_limit_bytes=VMEM_LIMIT),
        name="experts",
    )(block_expert, block_valid, block_rows, next_expert, last_block, xs, w_g, w_u, w_d)


def _combine_kernel(d_ref, dn_ref, gate_ref, xs_ref, mod_ref, ys_ref, o_ref, buf, sem):
    tc = d_ref.shape[1]
    i = pl.program_id(0)
    slot = lax.rem(i, 2)

    def gather(dest_ref, into):
        def issue(t, carry):
            for k in range(TOP_K):
                pltpu.make_async_copy(_row(ys_ref, dest_ref[k, t]), _row(buf.at[into], t * COMBINE_PITCH + k),
                                      sem.at[into]).start(priority=k % 2)
            return carry
        lax.fori_loop(0, tc, issue, 0)

    @pl.when(i == 0)
    def _():
        gather(d_ref, 0)

    @pl.when(i + 1 < pl.num_programs(0))
    def _():
        gather(dn_ref, 1 - slot)

    gathered = TOP_K * tc * ROW_TILE
    pltpu.make_async_copy(ys_ref.at[pl.ds(0, gathered), :], buf.at[slot, pl.ds(0, gathered), :],
                          sem.at[slot]).wait()

    gate2 = mod_ref[0][5:6]
    rows = buf.at[slot]
    acc = [jnp.zeros((tc, LANES), F32) for _ in range(2 * ROW_TILE)]
    for k in range(TOP_K):
        gk = jnp.transpose(jnp.broadcast_to(gate_ref[k:k + 1, :], (LANES, tc)))
        for s in range(ROW_TILE):
            w = rows[pl.ds(k * ROW_TILE + s, tc, stride=COMBINE_PITCH * ROW_TILE), :]
            acc[s] = acc[s] + gk * _unpack_rows(w, 0)
            acc[ROW_TILE + s] = acc[ROW_TILE + s] + gk * _unpack_rows(w, 1)
    o_ref[...] = xs_ref[...] + gate2 * jnp.concatenate(acc, axis=1)


def _combine(dest, gate, xs, mod3, ys, seq_len, tc):
    n_tok, d = xs.shape
    tiles_per_seq = seq_len // tc
    n_tiles = n_tok // tc
    return pl.pallas_call(
        _combine_kernel,
        out_shape=jax.ShapeDtypeStruct((n_tok, d), F32),
        grid=(n_tiles,),
        in_specs=[pl.BlockSpec((TOP_K, tc), lambda i: (0, i), memory_space=pltpu.SMEM),
                  pl.BlockSpec((TOP_K, tc), lambda i: (0, jnp.minimum(i + 1, n_tiles - 1)),
                               memory_space=pltpu.SMEM),
                  pl.BlockSpec((TOP_K, tc), lambda i: (0, i)),
                  pl.BlockSpec((tc, d), lambda i: (i, 0)),
                  pl.BlockSpec((1, 6, d), lambda i: (i // tiles_per_seq, 0, 0)),
                  pl.BlockSpec(memory_space=pl.ANY)],
        out_specs=pl.BlockSpec((tc, d), lambda i: (i, 0)),
        scratch_shapes=[pltpu.VMEM((2, COMBINE_PITCH * tc * ROW_TILE, LANES), U32),
                        pltpu.SemaphoreType.DMA((2,))],
        compiler_params=pltpu.CompilerParams(dimension_semantics=("arbitrary",),
                                             vmem_limit_bytes=VMEM_LIMIT),
        name="combine",
    )(dest, dest, gate, xs, mod3, ys)


def _tiles(seq_len):
    pick = lambda pref: min(pref, seq_len)
    return dict(adaln=1024, inproj=pick(512), attn=pick(512), ssm=pick(2048), mix=pick(512),
                dispatch=pick(512), combine=pick(256), slots=pick(2048), expert_block=4 * EXPERT_SUB)


def _rope_tables(seq_len):
    half = ATT_HEAD_DIM // 2
    inv_freq = 1.0 / (ROPE_THETA ** (jnp.arange(0, ATT_HEAD_DIM, 2, dtype=F32) / ATT_HEAD_DIM))
    ang = jnp.arange(seq_len, dtype=F32)[:, None] * inv_freq[None, :]
    cos, sin = jnp.cos(ang), jnp.sin(ang)
    reps = LANES // half
    sign = jnp.tile(jnp.concatenate([-jnp.ones((half,), F32), jnp.ones((half,), F32)]), reps // 2)
    return jnp.tile(cos, (1, reps)), jnp.tile(sin, (1, reps)) * sign[None, :]


def _block_tables(counts, blk, n_blocks):
    padded = (counts + blk - 1) // blk * blk
    pad_ends = jnp.cumsum(padded)
    pad_starts = pad_ends - padded
    used = pad_ends[-1] // blk
    last = jnp.maximum(used - 1, 0)
    starts = jnp.arange(n_blocks, dtype=I32) * blk
    expert = jnp.sum((pad_ends[None, :] <= starts[:, None]).astype(I32), axis=1)
    expert = jnp.minimum(expert, N_EXPERTS - 1)
    expert = jnp.where(starts < pad_ends[-1], expert, expert[last])
    valid = jnp.clip(counts[expert] - (starts - pad_starts[expert]), 0, blk)
    valid = jnp.where(starts < pad_ends[-1], valid, 0).astype(I32)
    ids = jnp.arange(N_EXPERTS, dtype=I32)
    later_used = jnp.logical_and(ids[None, :] > expert[:, None], (counts > 0)[None, :])
    upcoming = jnp.min(jnp.where(later_used, ids[None, :], N_EXPERTS), axis=1)
    upcoming = jnp.where(upcoming < N_EXPERTS, upcoming, -1).astype(I32)
    return pad_starts.astype(I32), expert, valid, upcoming, last.reshape(1).astype(I32)


def kernel(x, c, norm1_g, norm2_g, w_ada, b_ada, w_in, q_norm_g, k_norm_g, lambda_q1, lambda_k1, lambda_q2, lambda_k2, subln_g, ssm_a_re, ssm_a_im, ssm_log_dt, ssm_b_re, ssm_b_im, ssm_c_re, ssm_c_im, ssm_d, w_glu, ssm_norm_g, w_out, w_router, router_bias, w_gate_e, w_up_e, w_down_e, w_gate_s, w_up_s, w_down_s):
    bsz, seq_len, d = x.shape
    n_tok = bsz * seq_len
    tiles = _tiles(seq_len)
    assert d == 2 * ROW_TILE * LANES, "packed token rows are ROW_TILE x 128 words of two bf16 each"
    assert w_in.shape[1:] == (d, 2 * QK_WIDTH + ATT_WIDTH + SSM_WIDTH) and w_ada.shape[2] % tiles['adaln'] == 0
    assert w_router.shape[2] == N_EXPERTS and w_gate_e.shape[1] == N_EXPERTS
    assert all(seq_len % tiles[t] == 0 for t in ('inproj', 'attn', 'ssm', 'mix', 'dispatch',
                                                 'combine', 'slots')), "sequence length must be tile aligned"
    blk = tiles['expert_block']
    n_blocks = (n_tok * TOP_K + N_EXPERTS * (blk - 1) + blk - 1) // blk
    cos_t, sin_t = _rope_tables(seq_len)
    seg = jnp.kron(jnp.eye(QK_WIDTH // ATT_HEAD_DIM, dtype=F32),
                   jnp.full((ATT_HEAD_DIM, ATT_HEAD_DIM), 1.0 / ATT_HEAD_DIM, F32)).astype(BF16)
    reps = QK_WIDTH // ATT_HEAD_DIM
    x2d = x.reshape(n_tok, d).astype(F32)
    for layer in range(w_ada.shape[0]):
        lam_init = 0.8 - 0.6 * math.exp(-0.3 * layer)
        mod3 = _adaln(c.astype(F32), w_ada[layer].astype(F32), b_ada[layer].astype(F32),
                      tiles['adaln']).reshape(bsz, 6, d)
        q, k, v, u = _inproj(
            x2d, mod3, norm1_g[layer].astype(F32).reshape(1, d), w_in[layer].astype(BF16), seg,
            jnp.tile(q_norm_g[layer].astype(F32), reps).reshape(1, QK_WIDTH),
            jnp.tile(k_norm_g[layer].astype(F32), reps).reshape(1, QK_WIDTH),
            cos_t, sin_t, seq_len, tiles['inproj'])
        lam = (jnp.exp(jnp.sum(lambda_q1[layer].astype(F32) * lambda_k1[layer].astype(F32)))
               - jnp.exp(jnp.sum(lambda_q2[layer].astype(F32) * lambda_k2[layer].astype(F32))) + lam_init)
        att = _attention(q, k, v, jnp.full((1, LANES), lam, F32),
                         subln_g[layer].astype(F32).reshape(1, ATT_V_DIM),
                         bsz, seq_len, tiles['attn'], 1.0 - lam_init)
        kmat, bmat, cmat, bpow = _ssm_mats(ssm_a_re[layer], ssm_a_im[layer], ssm_log_dt[layer],
                                         ssm_b_re[layer], ssm_b_im[layer], ssm_c_re[layer],
                                         ssm_c_im[layer], ssm_d[layer])
        y = _ssm(u, kmat, bmat, cmat, bpow, bsz, seq_len, tiles['ssm'])
        wr_t = jnp.transpose(w_router[layer].astype(F32))
        wrh, wrl = _split_bf16(wr_t)
        xs, h2p, eidx, gate, rank, counts = _mix_route(
            x2d, y, att, mod3, w_glu[layer].astype(BF16),
            ssm_norm_g[layer].astype(F32).reshape(1, SSM_WIDTH),
            w_out[layer, :ATT_WIDTH].astype(BF16), w_out[layer, ATT_WIDTH:].astype(BF16),
            norm2_g[layer].astype(F32).reshape(1, d), wrh, wrl,
            router_bias[layer].astype(F32).reshape(N_EXPERTS, 1),
            jnp.concatenate([w_gate_s[layer], w_up_s[layer]], axis=1).astype(BF16),
            w_down_s[layer].astype(BF16), seq_len, tiles['mix'])
        pad_starts, block_expert, block_valid, next_expert, last_block = _block_tables(
            counts.reshape(N_EXPERTS).astype(I32), blk, n_blocks)
        dest = _slots(eidx, rank, pad_starts, tiles['slots'])
        x_slots = _dispatch(dest, h2p, n_blocks * blk, tiles['dispatch'])
        y_slots = _experts(block_expert, block_valid, next_expert, last_block, x_slots,
                           w_gate_e[layer], w_up_e[layer], w_down_e[layer], blk)
        x2d = _combine(dest, gate, xs, mod3, y_slots, seq_len, tiles['combine'])
    return x2d.reshape(bsz, seq_len, d).astype(x.dtype)
```

```python
import functools
import math

import jax
import jax.numpy as jnp
from jax import lax
from jax.experimental import pallas as pl
from jax.experimental.pallas import tpu as pltpu

F32 = jnp.float32
BF16 = jnp.bfloat16
I32 = jnp.int32
U32 = jnp.uint32

LANES = 128
SUBLANES = 8
V7X_VMEM_BYTES = 64 * 1024 * 1024

N_ATT_HEADS = 4
ATT_HEAD_DIM = 64
ATT_V_DIM = 2 * ATT_HEAD_DIM
QK_WIDTH = N_ATT_HEADS * 2 * ATT_HEAD_DIM
ATT_WIDTH = N_ATT_HEADS * ATT_V_DIM
ROPE_THETA = 10000.0
SSM_GROUP = 16
SSM_GROUPS = 32
SSM_STATE = 64
SSM_WIDTH = SSM_GROUPS * SSM_GROUP
SSM_CHUNK = SUBLANES
SSM_LANE_BLOCKS = SSM_WIDTH // LANES
GROUPS_PER_BLOCK = LANES // SSM_GROUP
N_EXPERTS = 256
TOP_K = 8
N_GROUPS = 8
TOPK_GROUPS = 4
EXPERTS_PER_GROUP = N_EXPERTS // N_GROUPS
SHARED_DIM = 256
ROUTED_SCALE = 2.5
EPS = 1e-6

NT_DIMS = (((1,), (1,)), ((), ()))

VMEM_LIMIT = V7X_VMEM_BYTES * 3 // 4


def _dot(a, b):
    return jnp.dot(a, b, preferred_element_type=F32)


def _sigmoid(x):
    return 1.0 / (1.0 + jnp.exp(-x))


def _split_bf16(x):
    hi = x.astype(BF16)
    lo = (x - hi.astype(F32)).astype(BF16)
    return hi, lo


def _adaln_kernel(c_ref, w_ref, b_ref, o_ref):
    c = c_ref[...]
    sc = c * _sigmoid(c)
    o_ref[...] = jnp.dot(sc, w_ref[...], preferred_element_type=F32,
                         precision=lax.Precision.HIGHEST) + b_ref[...]


def _adaln(c, w, b, tn):
    bsz, d = c.shape
    n = w.shape[1]
    return pl.pallas_call(
        _adaln_kernel,
        out_shape=jax.ShapeDtypeStruct((bsz, n), F32),
        grid=(n // tn,),
        in_specs=[pl.BlockSpec((bsz, d), lambda j: (0, 0)),
                  pl.BlockSpec((d, tn), lambda j: (0, j)),
                  pl.BlockSpec((1, tn), lambda j: (0, j))],
        out_specs=pl.BlockSpec((bsz, tn), lambda j: (0, j)),
        compiler_params=pltpu.CompilerParams(dimension_semantics=("arbitrary",),
                                             vmem_limit_bytes=VMEM_LIMIT),
        name="adaln",
    )(c, w, b.reshape(1, n))


def _inproj_kernel(x_ref, mod_ref, n1g_ref, w_ref, seg_ref, qg_ref, kg_ref, cos_ref, sin_ref,
                   q_ref, k_ref, v_ref, u_ref):
    x = x_ref[...]
    mod = mod_ref[0]
    shift, scale = mod[0:1], mod[1:2]
    ms = jnp.mean(x * x, axis=-1, keepdims=True)
    h = x * lax.rsqrt(ms + EPS) * n1g_ref[...]
    h = h * (1.0 + scale) + shift
    z = _dot(h.astype(BF16), w_ref[...])
    seg = seg_ref[...]
    cos = cos_ref[...]
    sin = sin_ref[...]
    lane = lax.broadcasted_iota(I32, cos.shape, 1)
    first_half = (lane % ATT_HEAD_DIM) < (ATT_HEAD_DIM // 2)

    def norm_rope(t, g_ref, out_scale):
        hi, lo = _split_bf16(t * t)
        msq = _dot(hi, seg) + _dot(lo, seg)
        tn = t * lax.rsqrt(msq + EPS) * g_ref[...]
        outs = []
        for hd in range(N_ATT_HEADS):
            th = tn[:, hd * LANES:(hd + 1) * LANES]
            partner = jnp.where(first_half,
                                pltpu.roll(th, LANES - ATT_HEAD_DIM // 2, 1),
                                pltpu.roll(th, ATT_HEAD_DIM // 2, 1))
            outs.append((th * cos + partner * sin) * out_scale)
        return jnp.concatenate(outs, axis=1)

    q_ref[...] = norm_rope(z[:, :QK_WIDTH], qg_ref, ATT_HEAD_DIM ** -0.5).astype(BF16)
    k_ref[...] = norm_rope(z[:, QK_WIDTH:2 * QK_WIDTH], kg_ref, 1.0).astype(BF16)
    v_ref[...] = z[:, 2 * QK_WIDTH:2 * QK_WIDTH + ATT_WIDTH].astype(BF16)
    u_ref[...] = z[:, 2 * QK_WIDTH + ATT_WIDTH:]


def _inproj(x2d, mod3, n1g, w_in_bf, seg, qg, kg, cos_t, sin_t, seq_len, tm):
    n_tok, d = x2d.shape
    tiles_per_seq = seq_len // tm
    in_width = w_in_bf.shape[1]
    row = lambda i: (i, 0)
    const = lambda i: (0, 0)
    return pl.pallas_call(
        _inproj_kernel,
        out_shape=(jax.ShapeDtypeStruct((n_tok, QK_WIDTH), BF16),
                   jax.ShapeDtypeStruct((n_tok, QK_WIDTH), BF16),
                   jax.ShapeDtypeStruct((n_tok, ATT_WIDTH), BF16),
                   jax.ShapeDtypeStruct((n_tok, SSM_WIDTH), F32)),
        grid=(n_tok // tm,),
        in_specs=[pl.BlockSpec((tm, d), row),
                  pl.BlockSpec((1, 6, d), lambda i: (i // tiles_per_seq, 0, 0)),
                  pl.BlockSpec((1, d), const),
                  pl.BlockSpec((d, in_width), const),
                  pl.BlockSpec((QK_WIDTH, QK_WIDTH), const),
                  pl.BlockSpec((1, QK_WIDTH), const),
                  pl.BlockSpec((1, QK_WIDTH), const),
                  pl.BlockSpec((tm, LANES), lambda i: (i % tiles_per_seq, 0)),
                  pl.BlockSpec((tm, LANES), lambda i: (i % tiles_per_seq, 0))],
        out_specs=(pl.BlockSpec((tm, QK_WIDTH), row),
                   pl.BlockSpec((tm, QK_WIDTH), row),
                   pl.BlockSpec((tm, ATT_WIDTH), row),
                   pl.BlockSpec((tm, SSM_WIDTH), row)),
        compiler_params=pltpu.CompilerParams(dimension_semantics=("arbitrary",),
                                             vmem_limit_bytes=VMEM_LIMIT),
        name="inproj",
    )(x2d, mod3, n1g, w_in_bf, seg, qg, kg, cos_t, sin_t)


def _attn_kernel(qi_ref, kj_ref, q_ref, k_ref, v_ref, lam_ref, sg_ref, o_ref,
                 m_ref, l_ref, acc_ref, *, out_scale):
    p = pl.program_id(2)
    qi = qi_ref[p]
    kj = kj_ref[p]

    @pl.when(kj == 0)
    def _():
        m_ref[...] = jnp.full(m_ref.shape, -jnp.inf, F32)
        l_ref[...] = jnp.zeros(l_ref.shape, F32)
        acc_ref[...] = jnp.zeros(acc_ref.shape, F32)

    t = q_ref.shape[0]
    h = t // 2

    def stacked_queries():
        q = q_ref[...]
        lane = lax.broadcasted_iota(I32, q.shape, 1)
        zero = jnp.zeros_like(q)
        q0 = jnp.where(lane < ATT_HEAD_DIM, q, zero)
        q1 = jnp.where(lane >= ATT_HEAD_DIM, q, zero)
        return jnp.concatenate([q0[:h], q1[:h], q0[h:], q1[h:]], axis=0)

    def update(lo, hi, s, v):
        m_old = m_ref[lo:hi]
        m_new = jnp.maximum(m_old, jnp.max(s, axis=-1, keepdims=True))
        alpha = jnp.exp(m_old - m_new)
        pr = jnp.exp(s - jnp.concatenate([m_new] * (s.shape[1] // LANES), axis=1))
        l_ref[lo:hi] = alpha * l_ref[lo:hi] + jnp.sum(pr, axis=-1, keepdims=True)
        acc_ref[lo:hi] = alpha * acc_ref[lo:hi] + _dot(pr.astype(BF16), v)
        m_ref[lo:hi] = m_new

    @pl.when(kj < qi)
    def _():
        s = lax.dot_general(stacked_queries(), k_ref[...], NT_DIMS, preferred_element_type=F32)
        update(0, 2 * t, s, v_ref[...])

    @pl.when(kj == qi)
    def _():
        qq = stacked_queries()
        k = k_ref[...]
        v = v_ref[...]
        r = lax.broadcasted_iota(I32, (h, t), 0)
        col = lax.broadcasted_iota(I32, (h, t), 1)
        early_ok = (col <= r)[:, :h]
        late_ok = col <= r + h
        s = lax.dot_general(qq[:t], k[:h], NT_DIMS, preferred_element_type=F32)
        update(0, t, jnp.where(jnp.concatenate([early_ok, early_ok], axis=0), s, -jnp.inf), v[:h])
        s = lax.dot_general(qq[t:], k, NT_DIMS, preferred_element_type=F32)
        update(t, 2 * t, jnp.where(jnp.concatenate([late_ok, late_ok], axis=0), s, -jnp.inf), v)

        o = acc_ref[...] / l_ref[...]
        o = (jnp.concatenate([o[:h], o[t:t + h]], axis=0)
             - lam_ref[...] * jnp.concatenate([o[h:t], o[t + h:]], axis=0))
        ms = jnp.mean(o * o, axis=-1, keepdims=True)
        o_ref[...] = (o * lax.rsqrt(ms + EPS) * sg_ref[...] * out_scale).astype(o_ref.dtype)


def _attention(q, k, v, lam_row, subln_g, bsz, seq_len, tq, out_scale):
    nq = seq_len // tq
    pairs = [(i, j) for i in range(nq) for j in range(i + 1)]
    qi = jnp.asarray([p[0] for p in pairs], I32)
    kj = jnp.asarray([p[1] for p in pairs], I32)
    q_map = lambda b, h, p, qi, kj: (b * nq + qi[p], h)
    k_map = lambda b, h, p, qi, kj: (b * nq + kj[p], h)
    const = lambda b, h, p, qi, kj: (0, 0)
    return pl.pallas_call(
        functools.partial(_attn_kernel, out_scale=out_scale),
        out_shape=jax.ShapeDtypeStruct((bsz * seq_len, ATT_WIDTH), BF16),
        grid_spec=pltpu.PrefetchScalarGridSpec(
            num_scalar_prefetch=2,
            grid=(bsz, N_ATT_HEADS, len(pairs)),
            in_specs=[pl.BlockSpec((tq, LANES), q_map),
                      pl.BlockSpec((tq, LANES), k_map),
                      pl.BlockSpec((tq, LANES), k_map),
                      pl.BlockSpec((1, LANES), const),
                      pl.BlockSpec((1, LANES), const)],
            out_specs=pl.BlockSpec((tq, LANES), q_map),
            scratch_shapes=[pltpu.VMEM((2 * tq, LANES), F32),
                            pltpu.VMEM((2 * tq, LANES), F32),
                            pltpu.VMEM((2 * tq, ATT_V_DIM), F32)]),
        compiler_params=pltpu.CompilerParams(
            dimension_semantics=("arbitrary", "arbitrary", "arbitrary"),
            vmem_limit_bytes=VMEM_LIMIT),
        name="attn",
    )(qi, kj, q, k, v, lam_row, subln_g)


def _ssm_mats(a_re, a_im, log_dt, b_re, b_im, c_re, c_im, d_skip):
    a_re, a_im, b_re, b_im, c_re, c_im, d_skip = (
        t.astype(F32) for t in (a_re, a_im, b_re, b_im, c_re, c_im, d_skip))
    dt = jnp.exp(log_dt.astype(F32))[:, None]
    mag = jnp.exp(a_re * dt)
    abar_re = mag * jnp.cos(a_im * dt)
    abar_im = mag * jnp.sin(a_im * dt)
    den = a_re * a_re + a_im * a_im
    nr = abar_re - 1.0
    f_re = ((nr * a_re + abar_im * a_im) / den)[..., None]
    f_im = ((abar_im * a_re - nr * a_im) / den)[..., None]
    bb_re = f_re * b_re - f_im * b_im
    bb_im = f_re * b_im + f_im * b_re
    steps = jnp.arange(SSM_CHUNK + 1, dtype=F32)[:, None, None]
    pmag = jnp.exp(a_re * dt * steps)
    pw_re = pmag * jnp.cos(a_im * dt * steps)
    pw_im = pmag * jnp.sin(a_im * dt * steps)
    ca_re = c_re[None] * pw_re[:, :, None, :] - c_im[None] * pw_im[:, :, None, :]
    ca_im = c_re[None] * pw_im[:, :, None, :] + c_im[None] * pw_re[:, :, None, :]
    bbt_re = jnp.transpose(bb_re, (0, 2, 1))[None, :, None]
    bbt_im = jnp.transpose(bb_im, (0, 2, 1))[None, :, None]
    lag = jnp.sum(ca_re[:SSM_CHUNK, :, :, None, :] * bbt_re
                  - ca_im[:SSM_CHUNK, :, :, None, :] * bbt_im, axis=-1)
    lag = lag.at[0].add(d_skip[:, :, None] * jnp.eye(SSM_GROUP, dtype=F32)[None])
    nb, gpb = SSM_LANE_BLOCKS, GROUPS_PER_BLOCK
    n_state = gpb * SSM_STATE
    in_group = jnp.arange(LANES) // SSM_GROUP
    state_group = jnp.arange(n_state) // SSM_STATE

    def spread(small, row_group, col_group):
        w = small.shape[-1]
        tiled = jnp.tile(jnp.eye(w, dtype=BF16), (1, col_group.shape[0] // w))
        wide = jnp.einsum('...w,wn->...n', small.astype(BF16), tiled, preferred_element_type=F32)
        return jnp.where(row_group[:, None] == col_group[None, :], wide, 0.0).astype(BF16)

    lag_blocks = spread(jnp.transpose(lag, (0, 1, 3, 2)).reshape(SSM_CHUNK, nb, LANES, SSM_GROUP),
                        in_group, in_group)
    zero_block = jnp.zeros_like(lag_blocks[0])
    kmat = jnp.concatenate(
        [jnp.concatenate([lag_blocks[t - j] if t >= j else zero_block for t in range(SSM_CHUNK)], axis=-1)
         for j in range(SSM_CHUNK)], axis=1)
    rev = SSM_CHUNK - 1 - jnp.arange(SSM_CHUNK)
    w_re = pw_re[rev][..., None] * bb_re[None] - pw_im[rev][..., None] * bb_im[None]
    w_im = pw_re[rev][..., None] * bb_im[None] + pw_im[rev][..., None] * bb_re[None]

    def in_to_state(w):
        small = jnp.transpose(w, (0, 1, 3, 2)).reshape(SSM_CHUNK, nb, LANES, SSM_STATE)
        return spread(small, in_group, state_group)

    bmat = jnp.concatenate([in_to_state(w_re), in_to_state(w_im)], axis=-1)
    bmat = jnp.transpose(bmat, (1, 0, 2, 3)).reshape(nb, SSM_CHUNK * LANES, 2 * n_state)

    def state_to_out(ca):
        small = jnp.transpose(ca[1:], (0, 1, 3, 2)).reshape(SSM_CHUNK, nb, n_state, SSM_GROUP)
        blocks = spread(small, state_group, in_group)
        return jnp.transpose(blocks, (1, 2, 0, 3)).reshape(nb, n_state, SSM_CHUNK * LANES)

    cmat = jnp.concatenate([state_to_out(ca_re), -state_to_out(ca_im)], axis=1)
    chunk_steps = SSM_CHUNK * jnp.arange(2 * SUBLANES, dtype=F32)[:, None, None]
    cmag = jnp.exp(a_re * dt * chunk_steps)
    to_block = lambda t: jnp.transpose(t.reshape(2 * SUBLANES, nb, n_state), (1, 0, 2))
    bpow = jnp.concatenate([to_block(cmag * jnp.cos(a_im * dt * chunk_steps)),
                            to_block(cmag * jnp.sin(a_im * dt * chunk_steps))], axis=-1)
    return kmat, bmat, cmat, bpow


def _cmul(a_re, a_im, x_re, x_im):
    return a_re * x_re - a_im * x_im, a_re * x_im + a_im * x_re


def _ssm_kernel(u_ref, km_ref, bm_ref, cm_ref, bp_ref, y_ref, carry_ref, sp_ref):
    n_chunks = sp_ref.shape[0]
    half = sp_ref.shape[1] // 2

    @pl.when(pl.program_id(2) == 0)
    def _():
        carry_ref[...] = jnp.zeros(carry_ref.shape, F32)

    ucat = jnp.concatenate([u_ref[pl.ds(j, n_chunks, stride=SSM_CHUNK), :] for j in range(SSM_CHUNK)],
                           axis=1).astype(BF16)
    s_end = _dot(ucat, bm_ref[0])
    p_re, p_im = s_end[:, :half], s_end[:, half:]
    bp = bp_ref[0]
    sub = lax.broadcasted_iota(I32, (n_chunks, half), 0) % SUBLANES

    def shifted(v, d):
        return jnp.where(sub >= d, pltpu.roll(v, d, 0), 0.0)

    for d in (1, 2, 4):
        d_re, d_im = _cmul(bp[d:d + 1, :half], bp[d:d + 1, half:], shifted(p_re, d), shifted(p_im, d))
        p_re, p_im = p_re + d_re, p_im + d_im
    x_re, x_im = shifted(p_re, 1), shifted(p_im, 1)
    t_re, t_im = bp[:SUBLANES, :half], bp[:SUBLANES, half:]
    l_re, l_im = bp[SUBLANES:SUBLANES + 1, :half], bp[SUBLANES:SUBLANES + 1, half:]
    c_re, c_im = carry_ref[:, :half], carry_ref[:, half:]
    for g in range(n_chunks // SUBLANES):
        lo, hi = g * SUBLANES, (g + 1) * SUBLANES
        d_re, d_im = _cmul(t_re, t_im, c_re, c_im)
        sp_ref[lo:hi, :half] = x_re[lo:hi] + d_re
        sp_ref[lo:hi, half:] = x_im[lo:hi] + d_im
        e_re, e_im = _cmul(l_re, l_im, c_re, c_im)
        c_re, c_im = p_re[hi - 1:hi] + e_re, p_im[hi - 1:hi] + e_im
    carry_ref[...] = jnp.concatenate([c_re, c_im], axis=1)
    y = _dot(ucat, km_ref[0]) + _dot(sp_ref[...].astype(BF16), cm_ref[0])
    for t in range(SSM_CHUNK):
        y_ref[pl.ds(t, n_chunks, stride=SSM_CHUNK), :] = y[:, t * LANES:(t + 1) * LANES]


def _ssm(u, kmat, bmat, cmat, bpow, bsz, seq_len, tt):
    nt = seq_len // tt
    n_chunks = tt // SSM_CHUNK
    width = SSM_CHUNK * LANES
    u_map = lambda g, b, i: (b * nt + i, g)
    w_map = lambda g, b, i: (g, 0, 0)
    return pl.pallas_call(
        _ssm_kernel,
        out_shape=jax.ShapeDtypeStruct((bsz * seq_len, SSM_WIDTH), F32),
        grid=(SSM_LANE_BLOCKS, bsz, nt),
        in_specs=[pl.BlockSpec((tt, LANES), u_map),
                  pl.BlockSpec((1, width, width), w_map),
                  pl.BlockSpec((1, width, width), w_map),
                  pl.BlockSpec((1, width, width), w_map),
                  pl.BlockSpec((1, 2 * SUBLANES, width), w_map)],
        out_specs=pl.BlockSpec((tt, LANES), u_map),
        scratch_shapes=[pltpu.VMEM((1, width), F32),
                        pltpu.VMEM((n_chunks, width), F32)],
        compiler_params=pltpu.CompilerParams(
            dimension_semantics=("arbitrary", "arbitrary", "arbitrary"),
            vmem_limit_bytes=VMEM_LIMIT),
        name="ssm",
    )(u, kmat, bmat, cmat, bpow)


def _first_max(v, ids, n):
    m = jnp.max(v, axis=0, keepdims=True)
    ix = jnp.min(jnp.where(v == m, ids, n), axis=0, keepdims=True)
    return m, ix


def _mix_route_kernel(x_ref, y_ref, att_ref, mod_ref, wglu_ref, sng_ref, woa_ref, wos_ref, n2g_ref,
                      wrh_ref, wrl_ref, rb_ref, wgu_ref, wds_ref,
                      xs_ref, h2p_ref, eidx_ref, gate_ref, rank_ref, cnt_ref, carry_ref):
    tm = x_ref.shape[0]

    @pl.when(pl.program_id(0) == 0)
    def _():
        carry_ref[...] = jnp.zeros(carry_ref.shape, F32)

    mod = mod_ref[0]
    gate1, shift2, scale2, gate2 = mod[2:3], mod[3:4], mod[4:5], mod[5:6]

    y = y_ref[...]
    g = 0.5 * y * (1.0 + jnp.tanh(math.sqrt(2.0 / math.pi) * (y + 0.044715 * (y * y * y))))
    glu = g * _sigmoid(_dot(g.astype(BF16), wglu_ref[...]))
    ssm = glu * lax.rsqrt(jnp.mean(glu * glu, axis=-1, keepdims=True) + EPS) * sng_ref[...]

    mix = _dot(att_ref[...], woa_ref[...]) + _dot(ssm.astype(BF16), wos_ref[...])
    x1 = x_ref[...] + gate1 * mix
    h2 = x1 * lax.rsqrt(jnp.mean(x1 * x1, axis=-1, keepdims=True) + EPS) * n2g_ref[...]
    h2 = h2 * (1.0 + scale2) + shift2
    _store_rows(h2p_ref, _pack_rows(h2))
    hb, h_lo = _split_bf16(h2)

    gu = _dot(hb, wgu_ref[...])
    gs, us = gu[:, :SHARED_DIM], gu[:, SHARED_DIM:]
    act = (gs * _sigmoid(gs) * us).astype(BF16)
    xs_ref[...] = x1 + gate2 * _dot(act, wds_ref[...])

    wrh = wrh_ref[...]
    logits = (lax.dot_general(wrh, hb, NT_DIMS, preferred_element_type=F32)
              + lax.dot_general(wrh, h_lo, NT_DIMS, preferred_element_type=F32)
              + lax.dot_general(wrl_ref[...], hb, NT_DIMS, preferred_element_type=F32))
    score = _sigmoid(logits)
    biased = score + rb_ref[...]
    neg = -jnp.inf

    ids_g = lax.broadcasted_iota(I32, (EXPERTS_PER_GROUP, tm), 0)
    group_rows = []
    for gi in range(N_GROUPS):
        blk = biased[gi * EXPERTS_PER_GROUP:(gi + 1) * EXPERTS_PER_GROUP, :]
        m1, i1 = _first_max(blk, ids_g, EXPERTS_PER_GROUP)
        m2 = jnp.max(jnp.where(ids_g == i1, neg, blk), axis=0, keepdims=True)
        group_rows.append(m1 + m2)
    cur = jnp.concatenate(group_rows, axis=0)
    ids_8 = lax.broadcasted_iota(I32, cur.shape, 0)
    picked = jnp.zeros(cur.shape, F32)
    for _ in range(TOPK_GROUPS):
        _, ix = _first_max(cur, ids_8, N_GROUPS)
        hit = ids_8 == ix
        picked = jnp.where(hit, 1.0, picked)
        cur = jnp.where(hit, neg, cur)
    e_mask = jnp.concatenate(
        [jnp.broadcast_to(picked[gi:gi + 1, :], (EXPERTS_PER_GROUP, tm)) for gi in range(N_GROUPS)], axis=0)
    cand = jnp.where(e_mask > 0.0, biased, neg)

    ids_e = lax.broadcasted_iota(I32, cand.shape, 0)
    sel = jnp.zeros(cand.shape, F32)
    idx_rows, w_rows = [], []
    for _ in range(TOP_K):
        _, ix = _first_max(cand, ids_e, N_EXPERTS)
        hit = ids_e == ix
        idx_rows.append(ix)
        w_rows.append(jnp.sum(jnp.where(hit, score, 0.0), axis=0, keepdims=True))
        sel = jnp.where(hit, 1.0, sel)
        cand = jnp.where(hit, neg, cand)
    w_sum = w_rows[0]
    for w in w_rows[1:]:
        w_sum = w_sum + w
    gate_ref[...] = jnp.concatenate([w / w_sum * ROUTED_SCALE for w in w_rows], axis=0)
    eidx_ref[...] = jnp.concatenate(idx_rows, axis=0)

    t_row = lax.broadcasted_iota(I32, (tm, tm), 0)
    t_col = lax.broadcasted_iota(I32, (tm, tm), 1)
    earlier = jnp.where(t_row < t_col, 1.0, 0.0).astype(BF16)
    before = _dot(sel.astype(BF16), earlier) + carry_ref[...]
    rank_ref[...] = jnp.concatenate(
        [jnp.sum(jnp.where(ids_e == ix, before, 0.0), axis=0, keepdims=True) for ix in idx_rows],
        axis=0).astype(I32)
    carry_ref[...] = carry_ref[...] + jnp.sum(sel, axis=1, keepdims=True)
    cnt_ref[...] = carry_ref[...]


def _mix_route(x2d, y, att, mod3, wglu, sng, woa, wos, n2g, wrh, wrl, rb, wgu, wds, seq_len, tm):
    n_tok, d = x2d.shape
    tiles_per_seq = seq_len // tm
    row = lambda i: (i, 0)
    col = lambda i: (0, i)
    const = lambda i: (0, 0)
    full = lambda a: pl.BlockSpec(a.shape, const)
    return pl.pallas_call(
        _mix_route_kernel,
        out_shape=(jax.ShapeDtypeStruct((n_tok, d), F32),
                   jax.ShapeDtypeStruct((n_tok * ROW_TILE, LANES), U32),
                   jax.ShapeDtypeStruct((TOP_K, n_tok), I32),
                   jax.ShapeDtypeStruct((TOP_K, n_tok), F32),
                   jax.ShapeDtypeStruct((TOP_K, n_tok), I32),
                   jax.ShapeDtypeStruct((N_EXPERTS, 1), F32)),
        grid=(n_tok // tm,),
        in_specs=[pl.BlockSpec((tm, d), row),
                  pl.BlockSpec((tm, SSM_WIDTH), row),
                  pl.BlockSpec((tm, ATT_WIDTH), row),
                  pl.BlockSpec((1, 6, d), lambda i: (i // tiles_per_seq, 0, 0)),
                  full(wglu), full(sng), full(woa), full(wos), full(n2g),
                  full(wrh), full(wrl), full(rb), full(wgu), full(wds)],
        out_specs=(pl.BlockSpec((tm, d), row),
                   pl.BlockSpec((tm * ROW_TILE, LANES), row),
                   pl.BlockSpec((TOP_K, tm), col),
                   pl.BlockSpec((TOP_K, tm), col),
                   pl.BlockSpec((TOP_K, tm), col),
                   pl.BlockSpec((N_EXPERTS, 1), const)),
        scratch_shapes=[pltpu.VMEM((N_EXPERTS, 1), F32)],
        compiler_params=pltpu.CompilerParams(dimension_semantics=("arbitrary",),
                                             vmem_limit_bytes=VMEM_LIMIT),
        name="mix_route",
    )(x2d, y, att, mod3, wglu, sng, woa, wos, n2g, wrh, wrl, rb, wgu, wds)


ROW_TILE = 4
EXPERT_SUB = 128
COMBINE_PITCH = TOP_K + 1


def _pack_rows(v):
    half = v.shape[1] // 2
    return pltpu.pack_elementwise([v[:, :half], v[:, half:]], packed_dtype=BF16)


def _unpack_rows(w, index):
    return pltpu.unpack_elementwise(w, index=index, packed_dtype=BF16, unpacked_dtype=F32)


def _store_rows(ref, packed, first=0):
    m = packed.shape[0]
    for s in range(ROW_TILE):
        ref[pl.ds(first * ROW_TILE + s, m, stride=ROW_TILE), :] = packed[:, s * LANES:(s + 1) * LANES]


def _load_rows(ref, m, first=0):
    return jnp.concatenate([ref[pl.ds(first * ROW_TILE + s, m, stride=ROW_TILE), :] for s in range(ROW_TILE)],
                           axis=1)


def _row(ref, r):
    return ref.at[pl.ds(pl.multiple_of(r * ROW_TILE, ROW_TILE), ROW_TILE), :]


def _slot_kernel(e_ref, r_ref, ps_ref, d_ref):
    e = e_ref[...]
    ids = lax.broadcasted_iota(I32, (N_EXPERTS, e.shape[1]), 0)
    starts = ps_ref[...]
    rows = [jnp.sum(jnp.where(ids == e[k:k + 1, :], starts, 0.0), axis=0, keepdims=True)
            for k in range(TOP_K)]
    d_ref[...] = jnp.concatenate(rows, axis=0).astype(I32) + r_ref[...]


def _slots(eidx, rank, pad_starts, ts):
    n_tok = eidx.shape[1]
    col = lambda i: (0, i)
    return pl.pallas_call(
        _slot_kernel,
        out_shape=jax.ShapeDtypeStruct((TOP_K, n_tok), I32),
        grid=(n_tok // ts,),
        in_specs=[pl.BlockSpec((TOP_K, ts), col),
                  pl.BlockSpec((TOP_K, ts), col),
                  pl.BlockSpec((N_EXPERTS, 1), lambda i: (0, 0))],
        out_specs=pl.BlockSpec((TOP_K, ts), col),
        compiler_params=pltpu.CompilerParams(dimension_semantics=("arbitrary",),
                                             vmem_limit_bytes=VMEM_LIMIT),
        name="slots",
    )(eidx, rank, pad_starts.astype(F32).reshape(N_EXPERTS, 1))


def _dispatch_kernel(d_ref, h2p_ref, xs_ref, sem):
    td = d_ref.shape[1]

    def issue(t, carry):
        for k in range(TOP_K):
            pltpu.make_async_copy(_row(h2p_ref, t), _row(xs_ref, d_ref[k, t]), sem).start(priority=k % 2)
        return carry

    lax.fori_loop(0, td, issue, 0)
    for _ in range(TOP_K):
        pltpu.make_async_copy(h2p_ref, xs_ref.at[pl.ds(0, td * ROW_TILE), :], sem).wait()


def _dispatch(dest, h2p, n_slots, td):
    n_tok = dest.shape[1]
    return pl.pallas_call(
        _dispatch_kernel,
        out_shape=jax.ShapeDtypeStruct((n_slots * ROW_TILE, LANES), U32),
        grid=(n_tok // td,),
        in_specs=[pl.BlockSpec((TOP_K, td), lambda i: (0, i), memory_space=pltpu.SMEM),
                  pl.BlockSpec((td * ROW_TILE, LANES), lambda i: (i, 0))],
        out_specs=pl.BlockSpec(memory_space=pl.ANY),
        scratch_shapes=[pltpu.SemaphoreType.DMA],
        compiler_params=pltpu.CompilerParams(dimension_semantics=("arbitrary",),
                                             vmem_limit_bytes=VMEM_LIMIT),
        name="dispatch",
    )(dest, h2p)


def _expert_kernel(be_ref, nv_ref, nr_ref, nx_ref, last_ref, xs_hbm, wg_hbm, wu_hbm, wd_hbm, ys_hbm,
                   x_buf, y_buf, wg_f32, wu_f32, wd_f32, wg_bf, wu_bf, wd_bf, sem, x_sem, y_sem):
    i = pl.program_id(0)
    blk = x_buf.shape[1] // ROW_TILE
    sizes = range(EXPERT_SUB, blk + 1, EXPERT_SUB)
    expert = be_ref[i]
    n_valid = nv_ref[i]
    last = last_ref[0]
    slot = lax.rem(i, 2)
    changed = jnp.logical_or(i == 0, expert != be_ref[jnp.maximum(i - 1, 0)])

    def block_rows(ref, block, n_rows):
        return ref.at[pl.ds(pl.multiple_of(block * (blk * ROW_TILE), blk * ROW_TILE), n_rows * ROW_TILE), :]

    def x_copy(block, into, n_rows):
        return pltpu.make_async_copy(block_rows(xs_hbm, block, n_rows),
                                     x_buf.at[into, pl.ds(0, n_rows * ROW_TILE), :], x_sem.at[into])

    def y_copy(block, out_of, n_rows):
        return pltpu.make_async_copy(y_buf.at[out_of, pl.ds(0, n_rows * ROW_TILE), :],
                                     block_rows(ys_hbm, block, n_rows), y_sem.at[out_of])

    def for_rows(n, fn):
        for n_rows in sizes:
            pl.when(n == n_rows)(functools.partial(fn, n_rows))

    @pl.when(i == 0)
    def _():
        for_rows(nr_ref[0], lambda n_rows: x_copy(0, 0, n_rows).start())

    @pl.when(jnp.logical_and(i >= 2, i <= last))
    def _():
        for_rows(nr_ref[jnp.maximum(i - 2, 0)], lambda n_rows: y_copy(i - 2, slot, n_rows).wait())

    def weight_copies(e):
        return [pltpu.make_async_copy(wg_hbm.at[e], wg_f32, sem.at[0]),
                pltpu.make_async_copy(wu_hbm.at[e], wu_f32, sem.at[1]),
                pltpu.make_async_copy(wd_hbm.at[e], wd_f32, sem.at[2])]

    @pl.when(i == 0)
    def _():
        for cp in weight_copies(expert):
            cp.start()

    @pl.when(jnp.logical_and(n_valid > 0, changed))
    def _():
        for cp in weight_copies(expert):
            cp.wait()
        wg_bf[...] = wg_f32[...].astype(BF16)
        wu_bf[...] = wu_f32[...].astype(BF16)
        wd_bf[...] = wd_f32[...].astype(BF16)
        upcoming = nx_ref[i]

        @pl.when(upcoming >= 0)
        def _():
            for cp in weight_copies(upcoming):
                cp.start()

    def swiglu_rows(n_rows):
        x_copy(i, slot, n_rows).wait()
        upcoming_rows = nr_ref[jnp.minimum(i + 1, pl.num_programs(0) - 1)]
        for_rows(jnp.where(i < last, upcoming_rows, 0), lambda n_next: x_copy(i + 1, 1 - slot, n_next).start())
        packed = _load_rows(x_buf.at[slot], n_rows)
        x = jnp.concatenate([_unpack_rows(packed, 0), _unpack_rows(packed, 1)], axis=1)
        rows = lax.broadcasted_iota(I32, (n_rows, 1), 0)
        x = jnp.where(rows < n_valid, x, 0.0).astype(BF16)
        g = _dot(x, wg_bf[...])
        u = _dot(x, wu_bf[...])
        act = (g * _sigmoid(g) * u).astype(BF16)
        _store_rows(y_buf.at[slot], _pack_rows(_dot(act, wd_bf[...])))
        y_copy(i, slot, n_rows).start()

    for_rows(nr_ref[i], swiglu_rows)

    @pl.when(i == last)
    def _():
        for_rows(nr_ref[i], lambda n_rows: y_copy(i, slot, n_rows).wait())

        @pl.when(i >= 1)
        def _():
            for_rows(nr_ref[jnp.maximum(i - 1, 0)], lambda n_rows: y_copy(i - 1, 1 - slot, n_rows).wait())


def _experts(block_expert, block_valid, next_expert, last_block, xs, w_g, w_u, w_d, blk):
    n_blocks = block_expert.shape[0]
    d, de = w_g.shape[1], w_g.shape[2]
    block_rows = (block_valid + EXPERT_SUB - 1) // EXPERT_SUB * EXPERT_SUB
    hbm = pl.BlockSpec(memory_space=pl.ANY)
    return pl.pallas_call(
        _expert_kernel,
        out_shape=jax.ShapeDtypeStruct(xs.shape, U32),
        grid_spec=pltpu.PrefetchScalarGridSpec(
            num_scalar_prefetch=5,
            grid=(n_blocks,),
            in_specs=[hbm, hbm, hbm, hbm],
            out_specs=hbm,
            scratch_shapes=[pltpu.VMEM((2, blk * ROW_TILE, LANES), U32),
                            pltpu.VMEM((2, blk * ROW_TILE, LANES), U32),
                            pltpu.VMEM((d, de), F32),
                            pltpu.VMEM((d, de), F32),
                            pltpu.VMEM((de, d), F32),
                            pltpu.VMEM((d, de), BF16),
                            pltpu.VMEM((d, de), BF16),
                            pltpu.VMEM((de, d), BF16),
                            pltpu.SemaphoreType.DMA((3,)),
                            pltpu.SemaphoreType.DMA((2,)),
                            pltpu.SemaphoreType.DMA((2,))]),
        compiler_params=pltpu.CompilerParams(dimension_semantics=("arbitrary",),
                                             vmem_limit_bytes=VMEM_LIMIT),
        name="experts",
    )(block_expert, block_valid, block_rows, next_expert, last_block, xs, w_g, w_u, w_d)


def _combine_kernel(d_ref, dn_ref, gate_ref, xs_ref, mod_ref, ys_ref, o_ref, buf, sem):
    tc = d_ref.shape[1]
    i = pl.program_id(0)
    slot = lax.rem(i, 2)

    def gather(dest_ref, into):
        def issue(t, carry):
            for k in range(TOP_K):
                pltpu.make_async_copy(_row(ys_ref, dest_ref[k, t]), _row(buf.at[into], t * COMBINE_PITCH + k),
                                      sem.at[into]).start(priority=k % 2)
            return carry
        lax.fori_loop(0, tc, issue, 0)

    @pl.when(i == 0)
    def _():
        gather(d_ref, 0)

    @pl.when(i + 1 < pl.num_programs(0))
    def _():
        gather(dn_ref, 1 - slot)

    gathered = TOP_K * tc * ROW_TILE
    pltpu.make_async_copy(ys_ref.at[pl.ds(0, gathered), :], buf.at[slot, pl.ds(0, gathered), :],
                          sem.at[slot]).wait()

    gate2 = mod_ref[0][5:6]
    rows = buf.at[slot]
    acc = [jnp.zeros((tc, LANES), F32) for _ in range(2 * ROW_TILE)]
    for k in range(TOP_K):
        gk = jnp.transpose(jnp.broadcast_to(gate_ref[k:k + 1, :], (LANES, tc)))
        for s in range(ROW_TILE):
            w = rows[pl.ds(k * ROW_TILE + s, tc, stride=COMBINE_PITCH * ROW_TILE), :]
            acc[s] = acc[s] + gk * _unpack_rows(w, 0)
            acc[ROW_TILE + s] = acc[ROW_TILE + s] + gk * _unpack_rows(w, 1)
    o_ref[...] = xs_ref[...] + gate2 * jnp.concatenate(acc, axis=1)


def _combine(dest, gate, xs, mod3, ys, seq_len, tc):
    n_tok, d = xs.shape
    tiles_per_seq = seq_len // tc
    n_tiles = n_tok // tc
    return pl.pallas_call(
        _combine_kernel,
        out_shape=jax.ShapeDtypeStruct((n_tok, d), F32),
        grid=(n_tiles,),
        in_specs=[pl.BlockSpec((TOP_K, tc), lambda i: (0, i), memory_space=pltpu.SMEM),
                  pl.BlockSpec((TOP_K, tc), lambda i: (0, jnp.minimum(i + 1, n_tiles - 1)),
                               memory_space=pltpu.SMEM),
                  pl.BlockSpec((TOP_K, tc), lambda i: (0, i)),
                  pl.BlockSpec((tc, d), lambda i: (i, 0)),
                  pl.BlockSpec((1, 6, d), lambda i: (i // tiles_per_seq, 0, 0)),
                  pl.BlockSpec(memory_space=pl.ANY)],
        out_specs=pl.BlockSpec((tc, d), lambda i: (i, 0)),
        scratch_shapes=[pltpu.VMEM((2, COMBINE_PITCH * tc * ROW_TILE, LANES), U32),
                        pltpu.SemaphoreType.DMA((2,))],
        compiler_params=pltpu.CompilerParams(dimension_semantics=("arbitrary",),
                                             vmem_limit_bytes=VMEM_LIMIT),
        name="combine",
    )(dest, dest, gate, xs, mod3, ys)


def _tiles(seq_len):
    pick = lambda pref: min(pref, seq_len)
    return dict(adaln=1024, inproj=pick(512), attn=pick(512), ssm=pick(2048), mix=pick(512),
                dispatch=pick(512), combine=pick(256), slots=pick(2048), expert_block=8 * EXPERT_SUB)


def _rope_tables(seq_len):
    half = ATT_HEAD_DIM // 2
    inv_freq = 1.0 / (ROPE_THETA ** (jnp.arange(0, ATT_HEAD_DIM, 2, dtype=F32) / ATT_HEAD_DIM))
    ang = jnp.arange(seq_len, dtype=F32)[:, None] * inv_freq[None, :]
    cos, sin = jnp.cos(ang), jnp.sin(ang)
    reps = LANES // half
    sign = jnp.tile(jnp.concatenate([-jnp.ones((half,), F32), jnp.ones((half,), F32)]), reps // 2)
    return jnp.tile(cos, (1, reps)), jnp.tile(sin, (1, reps)) * sign[None, :]


def _block_tables(counts, blk, n_blocks):
    padded = (counts + blk - 1) // blk * blk
    pad_ends = jnp.cumsum(padded)
    pad_starts = pad_ends - padded
    used = pad_ends[-1] // blk
    last = jnp.maximum(used - 1, 0)
    starts = jnp.arange(n_blocks, dtype=I32) * blk
    expert = jnp.sum((pad_ends[None, :] <= starts[:, None]).astype(I32), axis=1)
    expert = jnp.minimum(expert, N_EXPERTS - 1)
    expert = jnp.where(starts < pad_ends[-1], expert, expert[last])
    valid = jnp.clip(counts[expert] - (starts - pad_starts[expert]), 0, blk)
    valid = jnp.where(starts < pad_ends[-1], valid, 0).astype(I32)
    ids = jnp.arange(N_EXPERTS, dtype=I32)
    later_used = jnp.logical_and(ids[None, :] > expert[:, None], (counts > 0)[None, :])
    upcoming = jnp.min(jnp.where(later_used, ids[None, :], N_EXPERTS), axis=1)
    upcoming = jnp.where(upcoming < N_EXPERTS, upcoming, -1).astype(I32)
    return pad_starts.astype(I32), expert, valid, upcoming, last.reshape(1).astype(I32)


def kernel(x, c, norm1_g, norm2_g, w_ada, b_ada, w_in, q_norm_g, k_norm_g, lambda_q1, lambda_k1, lambda_q2, lambda_k2, subln_g, ssm_a_re, ssm_a_im, ssm_log_dt, ssm_b_re, ssm_b_im, ssm_c_re, ssm_c_im, ssm_d, w_glu, ssm_norm_g, w_out, w_router, router_bias, w_gate_e, w_up_e, w_down_e, w_gate_s, w_up_s, w_down_s):
    bsz, seq_len, d = x.shape
    n_tok = bsz * seq_len
    tiles = _tiles(seq_len)
    assert d == 2 * ROW_TILE * LANES, "packed token rows are ROW_TILE x 128 words of two bf16 each"
    assert w_in.shape[1:] == (d, 2 * QK_WIDTH + ATT_WIDTH + SSM_WIDTH) and w_ada.shape[2] % tiles['adaln'] == 0
    assert w_router.shape[2] == N_EXPERTS and w_gate_e.shape[1] == N_EXPERTS
    assert all(seq_len % tiles[t] == 0 for t in ('inproj', 'attn', 'ssm', 'mix', 'dispatch',
                                                 'combine', 'slots')), "sequence length must be tile aligned"
    blk = tiles['expert_block']
    n_blocks = (n_tok * TOP_K + N_EXPERTS * (blk - 1) + blk - 1) // blk
    cos_t, sin_t = _rope_tables(seq_len)
    seg = jnp.kron(jnp.eye(QK_WIDTH // ATT_HEAD_DIM, dtype=F32),
                   jnp.full((ATT_HEAD_DIM, ATT_HEAD_DIM), 1.0 / ATT_HEAD_DIM, F32)).astype(BF16)
    reps = QK_WIDTH // ATT_HEAD_DIM
    x2d = x.reshape(n_tok, d).astype(F32)
    for layer in range(w_ada.shape[0]):
        lam_init = 0.8 - 0.6 * math.exp(-0.3 * layer)
        mod3 = _adaln(c.astype(F32), w_ada[layer].astype(F32), b_ada[layer].astype(F32),
                      tiles['adaln']).reshape(bsz, 6, d)
        q, k, v, u = _inproj(
            x2d, mod3, norm1_g[layer].astype(F32).reshape(1, d), w_in[layer].astype(BF16), seg,
            jnp.tile(q_norm_g[layer].astype(F32), reps).reshape(1, QK_WIDTH),
            jnp.tile(k_norm_g[layer].astype(F32), reps).reshape(1, QK_WIDTH),
            cos_t, sin_t, seq_len, tiles['inproj'])
        lam = (jnp.exp(jnp.sum(lambda_q1[layer].astype(F32) * lambda_k1[layer].astype(F32)))
               - jnp.exp(jnp.sum(lambda_q2[layer].astype(F32) * lambda_k2[layer].astype(F32))) + lam_init)
        att = _attention(q, k, v, jnp.full((1, LANES), lam, F32),
                         subln_g[layer].astype(F32).reshape(1, ATT_V_DIM),
                         bsz, seq_len, tiles['attn'], 1.0 - lam_init)
        kmat, bmat, cmat, bpow = _ssm_mats(ssm_a_re[layer], ssm_a_im[layer], ssm_log_dt[layer],
                                         ssm_b_re[layer], ssm_b_im[layer], ssm_c_re[layer],
                                         ssm_c_im[layer], ssm_d[layer])
        y = _ssm(u, kmat, bmat, cmat, bpow, bsz, seq_len, tiles['ssm'])
        wr_t = jnp.transpose(w_router[layer].astype(F32))
        wrh, wrl = _split_bf16(wr_t)
        xs, h2p, eidx, gate, rank, counts = _mix_route(
            x2d, y, att, mod3, w_glu[layer].astype(BF16),
            ssm_norm_g[layer].astype(F32).reshape(1, SSM_WIDTH),
            w_out[layer, :ATT_WIDTH].astype(BF16), w_out[layer, ATT_WIDTH:].astype(BF16),
            norm2_g[layer].astype(F32).reshape(1, d), wrh, wrl,
            router_bias[layer].astype(F32).reshape(N_EXPERTS, 1),
            jnp.concatenate([w_gate_s[layer], w_up_s[layer]], axis=1).astype(BF16),
            w_down_s[layer].astype(BF16), seq_len, tiles['mix'])
        pad_starts, block_expert, block_valid, next_expert, last_block = _block_tables(
            counts.reshape(N_EXPERTS).astype(I32), blk, n_blocks)
        dest = _slots(eidx, rank, pad_starts, tiles['slots'])
        x_slots = _dispatch(dest, h2p, n_blocks * blk, tiles['dispatch'])
        y_slots = _experts(block_expert, block_valid, next_expert, last_block, x_slots,
                           w_gate_e[layer], w_up_e[layer], w_down_e[layer], blk)
        x2d = _combine(dest, gate, xs, mod3, y_slots, seq_len, tiles['combine'])
    return x2d.reshape(bsz, seq_len, d).astype(x.dtype)
```

```python
import functools
import math

import jax
import jax.numpy as jnp
from jax import lax
from jax.experimental import pallas as pl
from jax.experimental.pallas import tpu as pltpu

F32 = jnp.float32
BF16 = jnp.bfloat16
I32 = jnp.int32
U32 = jnp.uint32

LANES = 128
SUBLANES = 8
V7X_VMEM_BYTES = 64 * 1024 * 1024

N_ATT_HEADS = 4
ATT_HEAD_DIM = 64
ATT_V_DIM = 2 * ATT_HEAD_DIM
QK_WIDTH = N_ATT_HEADS * 2 * ATT_HEAD_DIM
ATT_WIDTH = N_ATT_HEADS * ATT_V_DIM
ROPE_THETA = 10000.0
SSM_GROUP = 16
SSM_GROUPS = 32
SSM_STATE = 64
SSM_WIDTH = SSM_GROUPS * SSM_GROUP
SSM_CHUNK = SUBLANES
SSM_LANE_BLOCKS = SSM_WIDTH // LANES
GROUPS_PER_BLOCK = LANES // SSM_GROUP
N_EXPERTS = 256
TOP_K = 8
N_GROUPS = 8
TOPK_GROUPS = 4
EXPERTS_PER_GROUP = N_EXPERTS // N_GROUPS
SHARED_DIM = 256
ROUTED_SCALE = 2.5
EPS = 1e-6
MOD_SHIFT1, MOD_SCALE1, MOD_GATE1, MOD_SHIFT2, MOD_SCALE2, MOD_GATE2 = range(6)

NT_DIMS = (((1,), (1,)), ((), ()))

VMEM_LIMIT = V7X_VMEM_BYTES * 3 // 4


def _dot(a, b):
    return jnp.dot(a, b, preferred_element_type=F32)


def _sigmoid(x):
    return 1.0 / (1.0 + jnp.exp(-x))


def _mod_row(mod, row):
    return mod[row:row + 1]


def _split_bf16(x):
    hi = x.astype(BF16)
    lo = (x - hi.astype(F32)).astype(BF16)
    return hi, lo


def _adaln_kernel(c_ref, w_ref, b_ref, o_ref):
    c = c_ref[...]
    sc = c * _sigmoid(c)
    o_ref[...] = jnp.dot(sc, w_ref[...], preferred_element_type=F32,
                         precision=lax.Precision.HIGHEST) + b_ref[...]


def _adaln(c, w, b, tn):
    bsz, d = c.shape
    n = w.shape[1]
    return pl.pallas_call(
        _adaln_kernel,
        out_shape=jax.ShapeDtypeStruct((bsz, n), F32),
        grid=(n // tn,),
        in_specs=[pl.BlockSpec((bsz, d), lambda j: (0, 0)),
                  pl.BlockSpec((d, tn), lambda j: (0, j)),
                  pl.BlockSpec((1, tn), lambda j: (0, j))],
        out_specs=pl.BlockSpec((bsz, tn), lambda j: (0, j)),
        compiler_params=pltpu.CompilerParams(dimension_semantics=("arbitrary",),
                                             vmem_limit_bytes=VMEM_LIMIT),
        name="adaln",
    )(c, w, b.reshape(1, n))


def _inproj_kernel(x_ref, mod_ref, n1g_ref, w_ref, seg_ref, qg_ref, kg_ref, cos_ref, sin_ref,
                   q_ref, k_ref, v_ref, u_ref):
    x = x_ref[...]
    mod = mod_ref[0]
    shift, scale = _mod_row(mod, MOD_SHIFT1), _mod_row(mod, MOD_SCALE1)
    ms = jnp.mean(x * x, axis=-1, keepdims=True)
    h = x * lax.rsqrt(ms + EPS) * n1g_ref[...]
    h = h * (1.0 + scale) + shift
    z = _dot(h.astype(BF16), w_ref[...])
    seg = seg_ref[...]
    cos = cos_ref[...]
    sin = sin_ref[...]
    lane = lax.broadcasted_iota(I32, cos.shape, 1)
    first_half = (lane % ATT_HEAD_DIM) < (ATT_HEAD_DIM // 2)

    def norm_rope(t, g_ref, out_scale):
        hi, lo = _split_bf16(t * t)
        msq = _dot(hi, seg) + _dot(lo, seg)
        tn = t * lax.rsqrt(msq + EPS) * g_ref[...]
        outs = []
        for hd in range(N_ATT_HEADS):
            th = tn[:, hd * LANES:(hd + 1) * LANES]
            partner = jnp.where(first_half,
                                pltpu.roll(th, LANES - ATT_HEAD_DIM // 2, 1),
                                pltpu.roll(th, ATT_HEAD_DIM // 2, 1))
            outs.append((th * cos + partner * sin) * out_scale)
        return jnp.concatenate(outs, axis=1)

    q_ref[...] = norm_rope(z[:, :QK_WIDTH], qg_ref, ATT_HEAD_DIM ** -0.5).astype(BF16)
    k_ref[...] = norm_rope(z[:, QK_WIDTH:2 * QK_WIDTH], kg_ref, 1.0).astype(BF16)
    v_ref[...] = z[:, 2 * QK_WIDTH:2 * QK_WIDTH + ATT_WIDTH].astype(BF16)
    u_ref[...] = z[:, 2 * QK_WIDTH + ATT_WIDTH:]


def _inproj(x2d, mod3, n1g, w_in_bf, seg, qg, kg, cos_t, sin_t, seq_len, tm):
    n_tok, d = x2d.shape
    tiles_per_seq = seq_len // tm
    in_width = w_in_bf.shape[1]
    row = lambda i: (i, 0)
    const = lambda i: (0, 0)
    return pl.pallas_call(
        _inproj_kernel,
        out_shape=(jax.ShapeDtypeStruct((n_tok, QK_WIDTH), BF16),
                   jax.ShapeDtypeStruct((n_tok, QK_WIDTH), BF16),
                   jax.ShapeDtypeStruct((n_tok, ATT_WIDTH), BF16),
                   jax.ShapeDtypeStruct((n_tok, SSM_WIDTH), F32)),
        grid=(n_tok // tm,),
        in_specs=[pl.BlockSpec((tm, d), row),
                  pl.BlockSpec((1, 6, d), lambda i: (i // tiles_per_seq, 0, 0)),
                  pl.BlockSpec((1, d), const),
                  pl.BlockSpec((d, in_width), const),
                  pl.BlockSpec((QK_WIDTH, QK_WIDTH), const),
                  pl.BlockSpec((1, QK_WIDTH), const),
                  pl.BlockSpec((1, QK_WIDTH), const),
                  pl.BlockSpec((tm, LANES), lambda i: (i % tiles_per_seq, 0)),
                  pl.BlockSpec((tm, LANES), lambda i: (i % tiles_per_seq, 0))],
        out_specs=(pl.BlockSpec((tm, QK_WIDTH), row),
                   pl.BlockSpec((tm, QK_WIDTH), row),
                   pl.BlockSpec((tm, ATT_WIDTH), row),
                   pl.BlockSpec((tm, SSM_WIDTH), row)),
        compiler_params=pltpu.CompilerParams(dimension_semantics=("arbitrary",),
                                             vmem_limit_bytes=VMEM_LIMIT),
        name="inproj",
    )(x2d, mod3, n1g, w_in_bf, seg, qg, kg, cos_t, sin_t)


def _attn_kernel(qi_ref, kj_ref, q_ref, k_ref, v_ref, lam_ref, sg_ref, o_ref,
                 m_ref, l_ref, acc_ref, *, out_scale):
    p = pl.program_id(2)
    qi = qi_ref[p]
    kj = kj_ref[p]

    @pl.when(kj == 0)
    def _():
        m_ref[...] = jnp.full(m_ref.shape, -jnp.inf, F32)
        l_ref[...] = jnp.zeros(l_ref.shape, F32)
        acc_ref[...] = jnp.zeros(acc_ref.shape, F32)

    t = q_ref.shape[0]
    h = t // 2

    def stacked_queries():
        q = q_ref[...]
        lane = lax.broadcasted_iota(I32, q.shape, 1)
        zero = jnp.zeros_like(q)
        q0 = jnp.where(lane < ATT_HEAD_DIM, q, zero)
        q1 = jnp.where(lane >= ATT_HEAD_DIM, q, zero)
        return jnp.concatenate([q0[:h], q1[:h], q0[h:], q1[h:]], axis=0)

    def update(lo, hi, s, v):
        m_old = m_ref[lo:hi]
        m_new = jnp.maximum(m_old, jnp.max(s, axis=-1, keepdims=True))
        alpha = jnp.exp(m_old - m_new)
        pr = jnp.exp(s - jnp.concatenate([m_new] * (s.shape[1] // LANES), axis=1))
        l_ref[lo:hi] = alpha * l_ref[lo:hi] + jnp.sum(pr, axis=-1, keepdims=True)
        acc_ref[lo:hi] = alpha * acc_ref[lo:hi] + _dot(pr.astype(BF16), v)
        m_ref[lo:hi] = m_new

    @pl.when(kj < qi)
    def _():
        s = lax.dot_general(stacked_queries(), k_ref[...], NT_DIMS, preferred_element_type=F32)
        update(0, 2 * t, s, v_ref[...])

    @pl.when(kj == qi)
    def _():
        qq = stacked_queries()
        k = k_ref[...]
        v = v_ref[...]
        r = lax.broadcasted_iota(I32, (h, t), 0)
        col = lax.broadcasted_iota(I32, (h, t), 1)
        early_ok = (col <= r)[:, :h]
        late_ok = col <= r + h
        s = lax.dot_general(qq[:t], k[:h], NT_DIMS, preferred_element_type=F32)
        update(0, t, jnp.where(jnp.concatenate([early_ok, early_ok], axis=0), s, -jnp.inf), v[:h])
        s = lax.dot_general(qq[t:], k, NT_DIMS, preferred_element_type=F32)
        update(t, 2 * t, jnp.where(jnp.concatenate([late_ok, late_ok], axis=0), s, -jnp.inf), v)

        o = acc_ref[...] / l_ref[...]
        o = (jnp.concatenate([o[:h], o[t:t + h]], axis=0)
             - lam_ref[...] * jnp.concatenate([o[h:t], o[t + h:]], axis=0))
        ms = jnp.mean(o * o, axis=-1, keepdims=True)
        o_ref[...] = (o * lax.rsqrt(ms + EPS) * sg_ref[...] * out_scale).astype(o_ref.dtype)


def _attention(q, k, v, lam_row, subln_g, bsz, seq_len, tq, out_scale):
    nq = seq_len // tq
    pairs = [(i, j) for i in range(nq) for j in range(i + 1)]
    qi = jnp.asarray([p[0] for p in pairs], I32)
    kj = jnp.asarray([p[1] for p in pairs], I32)
    q_map = lambda b, h, p, qi, kj: (b * nq + qi[p], h)
    k_map = lambda b, h, p, qi, kj: (b * nq + kj[p], h)
    const = lambda b, h, p, qi, kj: (0, 0)
    return pl.pallas_call(
        functools.partial(_attn_kernel, out_scale=out_scale),
        out_shape=jax.ShapeDtypeStruct((bsz * seq_len, ATT_WIDTH), BF16),
        grid_spec=pltpu.PrefetchScalarGridSpec(
            num_scalar_prefetch=2,
            grid=(bsz, N_ATT_HEADS, len(pairs)),
            in_specs=[pl.BlockSpec((tq, LANES), q_map),
                      pl.BlockSpec((tq, LANES), k_map),
                      pl.BlockSpec((tq, LANES), k_map),
                      pl.BlockSpec((1, LANES), const),
                      pl.BlockSpec((1, LANES), const)],
            out_specs=pl.BlockSpec((tq, LANES), q_map),
            scratch_shapes=[pltpu.VMEM((2 * tq, LANES), F32),
                            pltpu.VMEM((2 * tq, LANES), F32),
                            pltpu.VMEM((2 * tq, ATT_V_DIM), F32)]),
        compiler_params=pltpu.CompilerParams(
            dimension_semantics=("arbitrary", "arbitrary", "arbitrary"),
            vmem_limit_bytes=VMEM_LIMIT),
        name="attn",
    )(qi, kj, q, k, v, lam_row, subln_g)


def _ssm_mats(a_re, a_im, log_dt, b_re, b_im, c_re, c_im, d_skip):
    a_re, a_im, b_re, b_im, c_re, c_im, d_skip = (
        t.astype(F32) for t in (a_re, a_im, b_re, b_im, c_re, c_im, d_skip))
    dt = jnp.exp(log_dt.astype(F32))[:, None]
    mag = jnp.exp(a_re * dt)
    abar_re = mag * jnp.cos(a_im * dt)
    abar_im = mag * jnp.sin(a_im * dt)
    den = a_re * a_re + a_im * a_im
    nr = abar_re - 1.0
    f_re = ((nr * a_re + abar_im * a_im) / den)[..., None]
    f_im = ((abar_im * a_re - nr * a_im) / den)[..., None]
    bb_re = f_re * b_re - f_im * b_im
    bb_im = f_re * b_im + f_im * b_re
    steps = jnp.arange(SSM_CHUNK + 1, dtype=F32)[:, None, None]
    pmag = jnp.exp(a_re * dt * steps)
    pw_re = pmag * jnp.cos(a_im * dt * steps)
    pw_im = pmag * jnp.sin(a_im * dt * steps)
    ca_re = c_re[None] * pw_re[:, :, None, :] - c_im[None] * pw_im[:, :, None, :]
    ca_im = c_re[None] * pw_im[:, :, None, :] + c_im[None] * pw_re[:, :, None, :]
    bbt_re = jnp.transpose(bb_re, (0, 2, 1))[None, :, None]
    bbt_im = jnp.transpose(bb_im, (0, 2, 1))[None, :, None]
    lag = jnp.sum(ca_re[:SSM_CHUNK, :, :, None, :] * bbt_re
                  - ca_im[:SSM_CHUNK, :, :, None, :] * bbt_im, axis=-1)
    lag = lag.at[0].add(d_skip[:, :, None] * jnp.eye(SSM_GROUP, dtype=F32)[None])
    nb, gpb = SSM_LANE_BLOCKS, GROUPS_PER_BLOCK
    n_state = gpb * SSM_STATE
    in_group = jnp.arange(LANES) // SSM_GROUP
    state_group = jnp.arange(n_state) // SSM_STATE

    def spread(small, row_group, col_group):
        w = small.shape[-1]
        tiled = jnp.tile(jnp.eye(w, dtype=BF16), (1, col_group.shape[0] // w))
        wide = jnp.einsum('...w,wn->...n', small.astype(BF16), tiled, preferred_element_type=F32)
        return jnp.where(row_group[:, None] == col_group[None, :], wide, 0.0).astype(BF16)

    lag_blocks = spread(jnp.transpose(lag, (0, 1, 3, 2)).reshape(SSM_CHUNK, nb, LANES, SSM_GROUP),
                        in_group, in_group)
    zero_block = jnp.zeros_like(lag_blocks[0])
    kmat = jnp.concatenate(
        [jnp.concatenate([lag_blocks[t - j] if t >= j else zero_block for t in range(SSM_CHUNK)], axis=-1)
         for j in range(SSM_CHUNK)], axis=1)
    rev = SSM_CHUNK - 1 - jnp.arange(SSM_CHUNK)
    w_re = pw_re[rev][..., None] * bb_re[None] - pw_im[rev][..., None] * bb_im[None]
    w_im = pw_re[rev][..., None] * bb_im[None] + pw_im[rev][..., None] * bb_re[None]

    def in_to_state(w):
        small = jnp.transpose(w, (0, 1, 3, 2)).reshape(SSM_CHUNK, nb, LANES, SSM_STATE)
        return spread(small, in_group, state_group)

    bmat = jnp.concatenate([in_to_state(w_re), in_to_state(w_im)], axis=-1)
    bmat = jnp.transpose(bmat, (1, 0, 2, 3)).reshape(nb, SSM_CHUNK * LANES, 2 * n_state)

    def state_to_out(ca):
        small = jnp.transpose(ca[1:], (0, 1, 3, 2)).reshape(SSM_CHUNK, nb, n_state, SSM_GROUP)
        blocks = spread(small, state_group, in_group)
        return jnp.transpose(blocks, (1, 2, 0, 3)).reshape(nb, n_state, SSM_CHUNK * LANES)

    cmat = jnp.concatenate([state_to_out(ca_re), -state_to_out(ca_im)], axis=1)
    chunk_steps = SSM_CHUNK * jnp.arange(2 * SUBLANES, dtype=F32)[:, None, None]
    cmag = jnp.exp(a_re * dt * chunk_steps)
    to_block = lambda t: jnp.transpose(t.reshape(2 * SUBLANES, nb, n_state), (1, 0, 2))
    bpow = jnp.concatenate([to_block(cmag * jnp.cos(a_im * dt * chunk_steps)),
                            to_block(cmag * jnp.sin(a_im * dt * chunk_steps))], axis=-1)
    return kmat, bmat, cmat, bpow


def _cmul(a_re, a_im, x_re, x_im):
    return a_re * x_re - a_im * x_im, a_re * x_im + a_im * x_re


def _ssm_kernel(u_ref, km_ref, bm_ref, cm_ref, bp_ref, y_ref, carry_ref, sp_ref):
    n_chunks = sp_ref.shape[0]
    half = sp_ref.shape[1] // 2

    @pl.when(pl.program_id(2) == 0)
    def _():
        carry_ref[...] = jnp.zeros(carry_ref.shape, F32)

    ucat = jnp.concatenate([u_ref[pl.ds(j, n_chunks, stride=SSM_CHUNK), :] for j in range(SSM_CHUNK)],
                           axis=1).astype(BF16)
    s_end = _dot(ucat, bm_ref[0])
    p_re, p_im = s_end[:, :half], s_end[:, half:]
    bp = bp_ref[0]
    sub = lax.broadcasted_iota(I32, (n_chunks, half), 0) % SUBLANES

    def shifted(v, d):
        return jnp.where(sub >= d, pltpu.roll(v, d, 0), 0.0)

    for d in (1, 2, 4):
        d_re, d_im = _cmul(bp[d:d + 1, :half], bp[d:d + 1, half:], shifted(p_re, d), shifted(p_im, d))
        p_re, p_im = p_re + d_re, p_im + d_im
    x_re, x_im = shifted(p_re, 1), shifted(p_im, 1)
    t_re, t_im = bp[:SUBLANES, :half], bp[:SUBLANES, half:]
    l_re, l_im = bp[SUBLANES:SUBLANES + 1, :half], bp[SUBLANES:SUBLANES + 1, half:]
    c_re, c_im = carry_ref[:, :half], carry_ref[:, half:]
    for g in range(n_chunks // SUBLANES):
        lo, hi = g * SUBLANES, (g + 1) * SUBLANES
        d_re, d_im = _cmul(t_re, t_im, c_re, c_im)
        sp_ref[lo:hi, :half] = x_re[lo:hi] + d_re
        sp_ref[lo:hi, half:] = x_im[lo:hi] + d_im
        e_re, e_im = _cmul(l_re, l_im, c_re, c_im)
        c_re, c_im = p_re[hi - 1:hi] + e_re, p_im[hi - 1:hi] + e_im
    carry_ref[...] = jnp.concatenate([c_re, c_im], axis=1)
    y = _dot(ucat, km_ref[0]) + _dot(sp_ref[...].astype(BF16), cm_ref[0])
    for t in range(SSM_CHUNK):
        y_ref[pl.ds(t, n_chunks, stride=SSM_CHUNK), :] = y[:, t * LANES:(t + 1) * LANES]


def _ssm(u, kmat, bmat, cmat, bpow, bsz, seq_len, tt):
    nt = seq_len // tt
    n_chunks = tt // SSM_CHUNK
    width = SSM_CHUNK * LANES
    u_map = lambda g, b, i: (b * nt + i, g)
    w_map = lambda g, b, i: (g, 0, 0)
    return pl.pallas_call(
        _ssm_kernel,
        out_shape=jax.ShapeDtypeStruct((bsz * seq_len, SSM_WIDTH), F32),
        grid=(SSM_LANE_BLOCKS, bsz, nt),
        in_specs=[pl.BlockSpec((tt, LANES), u_map),
                  pl.BlockSpec((1, width, width), w_map),
                  pl.BlockSpec((1, width, width), w_map),
                  pl.BlockSpec((1, width, width), w_map),
                  pl.BlockSpec((1, 2 * SUBLANES, width), w_map)],
        out_specs=pl.BlockSpec((tt, LANES), u_map),
        scratch_shapes=[pltpu.VMEM((1, width), F32),
                        pltpu.VMEM((n_chunks, width), F32)],
        compiler_params=pltpu.CompilerParams(
            dimension_semantics=("arbitrary", "arbitrary", "arbitrary"),
            vmem_limit_bytes=VMEM_LIMIT),
        name="ssm",
    )(u, kmat, bmat, cmat, bpow)


def _first_max(v, ids, n):
    m = jnp.max(v, axis=0, keepdims=True)
    ix = jnp.min(jnp.where(v == m, ids, n), axis=0, keepdims=True)
    return m, ix


def _mix_route_kernel(x_ref, y_ref, att_ref, mod_ref, wglu_ref, sng_ref, woa_ref, wos_ref, n2g_ref,
                      wrh_ref, wrl_ref, rb_ref, wgu_ref, wds_ref,
                      xs_ref, h2p_ref, eidx_ref, gate_ref, rank_ref, cnt_ref, carry_ref):
    tm = x_ref.shape[0]

    @pl.when(pl.program_id(0) == 0)
    def _():
        carry_ref[...] = jnp.zeros(carry_ref.shape, F32)

    mod = mod_ref[0]
    gate1, shift2, scale2, gate2 = (_mod_row(mod, r) for r in (MOD_GATE1, MOD_SHIFT2, MOD_SCALE2, MOD_GATE2))

    y = y_ref[...]
    g = 0.5 * y * (1.0 + jnp.tanh(math.sqrt(2.0 / math.pi) * (y + 0.044715 * (y * y * y))))
    glu = g * _sigmoid(_dot(g.astype(BF16), wglu_ref[...]))
    ssm = glu * lax.rsqrt(jnp.mean(glu * glu, axis=-1, keepdims=True) + EPS) * sng_ref[...]

    mix = _dot(att_ref[...], woa_ref[...]) + _dot(ssm.astype(BF16), wos_ref[...])
    x1 = x_ref[...] + gate1 * mix
    h2 = x1 * lax.rsqrt(jnp.mean(x1 * x1, axis=-1, keepdims=True) + EPS) * n2g_ref[...]
    h2 = h2 * (1.0 + scale2) + shift2
    _store_rows(h2p_ref, _pack_rows(h2))
    hb, h_lo = _split_bf16(h2)

    gu = _dot(hb, wgu_ref[...])
    gs, us = gu[:, :SHARED_DIM], gu[:, SHARED_DIM:]
    act = (gs * _sigmoid(gs) * us).astype(BF16)
    xs_ref[...] = x1 + gate2 * _dot(act, wds_ref[...])

    wrh = wrh_ref[...]
    logits = (lax.dot_general(wrh, hb, NT_DIMS, preferred_element_type=F32)
              + lax.dot_general(wrh, h_lo, NT_DIMS, preferred_element_type=F32)
              + lax.dot_general(wrl_ref[...], hb, NT_DIMS, preferred_element_type=F32))
    score = _sigmoid(logits)
    biased = score + rb_ref[...]
    neg = -jnp.inf

    ids_g = lax.broadcasted_iota(I32, (EXPERTS_PER_GROUP, tm), 0)
    group_rows = []
    for gi in range(N_GROUPS):
        blk = biased[gi * EXPERTS_PER_GROUP:(gi + 1) * EXPERTS_PER_GROUP, :]
        m1, i1 = _first_max(blk, ids_g, EXPERTS_PER_GROUP)
        m2 = jnp.max(jnp.where(ids_g == i1, neg, blk), axis=0, keepdims=True)
        group_rows.append(m1 + m2)
    cur = jnp.concatenate(group_rows, axis=0)
    ids_8 = lax.broadcasted_iota(I32, cur.shape, 0)
    picked = jnp.zeros(cur.shape, F32)
    for _ in range(TOPK_GROUPS):
        _, ix = _first_max(cur, ids_8, N_GROUPS)
        hit = ids_8 == ix
        picked = jnp.where(hit, 1.0, picked)
        cur = jnp.where(hit, neg, cur)
    e_mask = jnp.concatenate(
        [jnp.broadcast_to(picked[gi:gi + 1, :], (EXPERTS_PER_GROUP, tm)) for gi in range(N_GROUPS)], axis=0)
    cand = jnp.where(e_mask > 0.0, biased, neg)

    ids_e = lax.broadcasted_iota(I32, cand.shape, 0)
    sel = jnp.zeros(cand.shape, F32)
    idx_rows, w_rows = [], []
    for _ in range(TOP_K):
        _, ix = _first_max(cand, ids_e, N_EXPERTS)
        hit = ids_e == ix
        idx_rows.append(ix)
        w_rows.append(jnp.sum(jnp.where(hit, score, 0.0), axis=0, keepdims=True))
        sel = jnp.where(hit, 1.0, sel)
        cand = jnp.where(hit, neg, cand)
    w_sum = w_rows[0]
    for w in w_rows[1:]:
        w_sum = w_sum + w
    gate_ref[...] = jnp.concatenate([w / w_sum * ROUTED_SCALE for w in w_rows], axis=0)
    eidx_ref[...] = jnp.concatenate(idx_rows, axis=0)

    t_row = lax.broadcasted_iota(I32, (tm, tm), 0)
    t_col = lax.broadcasted_iota(I32, (tm, tm), 1)
    earlier = jnp.where(t_row < t_col, 1.0, 0.0).astype(BF16)
    before = _dot(sel.astype(BF16), earlier) + carry_ref[...]
    rank_ref[...] = jnp.concatenate(
        [jnp.sum(jnp.where(ids_e == ix, before, 0.0), axis=0, keepdims=True) for ix in idx_rows],
        axis=0).astype(I32)
    carry_ref[...] = carry_ref[...] + jnp.sum(sel, axis=1, keepdims=True)
    cnt_ref[...] = carry_ref[...]


def _mix_route(x2d, y, att, mod3, wglu, sng, woa, wos, n2g, wrh, wrl, rb, wgu, wds, seq_len, tm):
    n_tok, d = x2d.shape
    tiles_per_seq = seq_len // tm
    row = lambda i: (i, 0)
    col = lambda i: (0, i)
    const = lambda i: (0, 0)
    full = lambda a: pl.BlockSpec(a.shape, const)
    return pl.pallas_call(
        _mix_route_kernel,
        out_shape=(jax.ShapeDtypeStruct((n_tok, d), F32),
                   jax.ShapeDtypeStruct((n_tok * ROW_TILE, LANES), U32),
                   jax.ShapeDtypeStruct((TOP_K, n_tok), I32),
                   jax.ShapeDtypeStruct((TOP_K, n_tok), F32),
                   jax.ShapeDtypeStruct((TOP_K, n_tok), I32),
                   jax.ShapeDtypeStruct((N_EXPERTS, 1), F32)),
        grid=(n_tok // tm,),
        in_specs=[pl.BlockSpec((tm, d), row),
                  pl.BlockSpec((tm, SSM_WIDTH), row),
                  pl.BlockSpec((tm, ATT_WIDTH), row),
                  pl.BlockSpec((1, 6, d), lambda i: (i // tiles_per_seq, 0, 0)),
                  full(wglu), full(sng), full(woa), full(wos), full(n2g),
                  full(wrh), full(wrl), full(rb), full(wgu), full(wds)],
        out_specs=(pl.BlockSpec((tm, d), row),
                   pl.BlockSpec((tm * ROW_TILE, LANES), row),
                   pl.BlockSpec((TOP_K, tm), col),
                   pl.BlockSpec((TOP_K, tm), col),
                   pl.BlockSpec((TOP_K, tm), col),
                   pl.BlockSpec((N_EXPERTS, 1), const)),
        scratch_shapes=[pltpu.VMEM((N_EXPERTS, 1), F32)],
        compiler_params=pltpu.CompilerParams(dimension_semantics=("arbitrary",),
                                             vmem_limit_bytes=VMEM_LIMIT),
        name="mix_route",
    )(x2d, y, att, mod3, wglu, sng, woa, wos, n2g, wrh, wrl, rb, wgu, wds)


ROW_TILE = 4
EXPERT_SUB = 256
COMBINE_PITCH = TOP_K + 1


def _pack_rows(v):
    half = v.shape[1] // 2
    return pltpu.pack_elementwise([v[:, :half], v[:, half:]], packed_dtype=BF16)


def _unpack_rows(w, index):
    return pltpu.unpack_elementwise(w, index=index, packed_dtype=BF16, unpacked_dtype=F32)


def _store_rows(ref, packed, first=0):
    m = packed.shape[0]
    for s in range(ROW_TILE):
        ref[pl.ds(first * ROW_TILE + s, m, stride=ROW_TILE), :] = packed[:, s * LANES:(s + 1) * LANES]


def _load_rows(ref, m, first=0):
    return jnp.concatenate([ref[pl.ds(first * ROW_TILE + s, m, stride=ROW_TILE), :] for s in range(ROW_TILE)],
                           axis=1)


def _row(ref, r):
    return ref.at[pl.ds(pl.multiple_of(r * ROW_TILE, ROW_TILE), ROW_TILE), :]


def _slot_kernel(e_ref, r_ref, ps_ref, d_ref):
    e = e_ref[...]
    ids = lax.broadcasted_iota(I32, (N_EXPERTS, e.shape[1]), 0)
    starts = ps_ref[...]
    rows = [jnp.sum(jnp.where(ids == e[k:k + 1, :], starts, 0.0), axis=0, keepdims=True)
            for k in range(TOP_K)]
    d_ref[...] = jnp.concatenate(rows, axis=0).astype(I32) + r_ref[...]


def _slots(eidx, rank, pad_starts, ts):
    n_tok = eidx.shape[1]
    col = lambda i: (0, i)
    return pl.pallas_call(
        _slot_kernel,
        out_shape=jax.ShapeDtypeStruct((TOP_K, n_tok), I32),
        grid=(n_tok // ts,),
        in_specs=[pl.BlockSpec((TOP_K, ts), col),
                  pl.BlockSpec((TOP_K, ts), col),
                  pl.BlockSpec((N_EXPERTS, 1), lambda i: (0, 0))],
        out_specs=pl.BlockSpec((TOP_K, ts), col),
        compiler_params=pltpu.CompilerParams(dimension_semantics=("arbitrary",),
                                             vmem_limit_bytes=VMEM_LIMIT),
        name="slots",
    )(eidx, rank, pad_starts.astype(F32).reshape(N_EXPERTS, 1))


def _dispatch_kernel(d_ref, h2p_ref, xs_ref, sem):
    td = d_ref.shape[1]

    def issue(t, carry):
        for k in range(TOP_K):
            pltpu.make_async_copy(_row(h2p_ref, t), _row(xs_ref, d_ref[k, t]), sem).start(priority=k % 2)
        return carry

    lax.fori_loop(0, td, issue, 0)
    for _ in range(TOP_K):
        pltpu.make_async_copy(h2p_ref, xs_ref.at[pl.ds(0, td * ROW_TILE), :], sem).wait()


def _dispatch(dest, h2p, n_slots, td):
    n_tok = dest.shape[1]
    return pl.pallas_call(
        _dispatch_kernel,
        out_shape=jax.ShapeDtypeStruct((n_slots * ROW_TILE, LANES), U32),
        grid=(n_tok // td,),
        in_specs=[pl.BlockSpec((TOP_K, td), lambda i: (0, i), memory_space=pltpu.SMEM),
                  pl.BlockSpec((td * ROW_TILE, LANES), lambda i: (i, 0))],
        out_specs=pl.BlockSpec(memory_space=pl.ANY),
        scratch_shapes=[pltpu.SemaphoreType.DMA],
        compiler_params=pltpu.CompilerParams(dimension_semantics=("arbitrary",),
                                             vmem_limit_bytes=VMEM_LIMIT),
        name="dispatch",
    )(dest, h2p)


def _expert_kernel(be_ref, nv_ref, nr_ref, nx_ref, last_ref, xs_hbm, wg_hbm, wu_hbm, wd_hbm, ys_hbm,
                   x_buf, y_buf, wg_f32, wu_f32, wd_f32, wg_bf, wu_bf, wd_bf, sem, x_sem, y_sem):
    i = pl.program_id(0)
    blk = x_buf.shape[1] // ROW_TILE
    sizes = range(EXPERT_SUB, blk + 1, EXPERT_SUB)
    expert = be_ref[i]
    n_valid = nv_ref[i]
    last = last_ref[0]
    slot = lax.rem(i, 2)
    changed = jnp.logical_or(i == 0, expert != be_ref[jnp.maximum(i - 1, 0)])

    def block_rows(ref, block, n_rows):
        return ref.at[pl.ds(pl.multiple_of(block * (blk * ROW_TILE), blk * ROW_TILE), n_rows * ROW_TILE), :]

    def x_copy(block, into, n_rows):
        return pltpu.make_async_copy(block_rows(xs_hbm, block, n_rows),
                                     x_buf.at[into, pl.ds(0, n_rows * ROW_TILE), :], x_sem.at[into])

    def y_copy(block, out_of, n_rows):
        return pltpu.make_async_copy(y_buf.at[out_of, pl.ds(0, n_rows * ROW_TILE), :],
                                     block_rows(ys_hbm, block, n_rows), y_sem.at[out_of])

    def for_rows(n, fn):
        for n_rows in sizes:
            pl.when(n == n_rows)(functools.partial(fn, n_rows))

    @pl.when(i == 0)
    def _():
        for_rows(nr_ref[0], lambda n_rows: x_copy(0, 0, n_rows).start())

    @pl.when(jnp.logical_and(i >= 2, i <= last))
    def _():
        for_rows(nr_ref[jnp.maximum(i - 2, 0)], lambda n_rows: y_copy(i - 2, slot, n_rows).wait())

    def weight_copies(e):
        return [pltpu.make_async_copy(wg_hbm.at[e], wg_f32, sem.at[0]),
                pltpu.make_async_copy(wu_hbm.at[e], wu_f32, sem.at[1]),
                pltpu.make_async_copy(wd_hbm.at[e], wd_f32, sem.at[2])]

    @pl.when(i == 0)
    def _():
        for cp in weight_copies(expert):
            cp.start()

    @pl.when(jnp.logical_and(n_valid > 0, changed))
    def _():
        for cp in weight_copies(expert):
            cp.wait()
        wg_bf[...] = wg_f32[...].astype(BF16)
        wu_bf[...] = wu_f32[...].astype(BF16)
        wd_bf[...] = wd_f32[...].astype(BF16)
        upcoming = nx_ref[i]

        @pl.when(upcoming >= 0)
        def _():
            for cp in weight_copies(upcoming):
                cp.start()

    def swiglu_rows(n_rows):
        x_copy(i, slot, n_rows).wait()
        upcoming_rows = nr_ref[jnp.minimum(i + 1, pl.num_programs(0) - 1)]
        for_rows(jnp.where(i < last, upcoming_rows, 0), lambda n_next: x_copy(i + 1, 1 - slot, n_next).start())
        packed = _load_rows(x_buf.at[slot], n_rows)
        x = jnp.concatenate([_unpack_rows(packed, 0), _unpack_rows(packed, 1)], axis=1)
        rows = lax.broadcasted_iota(I32, (n_rows, 1), 0)
        x = jnp.where(rows < n_valid, x, 0.0).astype(BF16)
        g = _dot(x, wg_bf[...])
        u = _dot(x, wu_bf[...])
        act = (g * _sigmoid(g) * u).astype(BF16)
        _store_rows(y_buf.at[slot], _pack_rows(_dot(act, wd_bf[...])))
        y_copy(i, slot, n_rows).start()

    for_rows(nr_ref[i], swiglu_rows)

    @pl.when(i == last)
    def _():
        for_rows(nr_ref[i], lambda n_rows: y_copy(i, slot, n_rows).wait())

        @pl.when(i >= 1)
        def _():
            for_rows(nr_ref[jnp.maximum(i - 1, 0)], lambda n_rows: y_copy(i - 1, 1 - slot, n_rows).wait())


def _experts(block_expert, block_valid, next_expert, last_block, xs, w_g, w_u, w_d, blk):
    n_blocks = block_expert.shape[0]
    d, de = w_g.shape[1], w_g.shape[2]
    block_rows = (block_valid + EXPERT_SUB - 1) // EXPERT_SUB * EXPERT_SUB
    hbm = pl.BlockSpec(memory_space=pl.ANY)
    return pl.pallas_call(
        _expert_kernel,
        out_shape=jax.ShapeDtypeStruct(xs.shape, U32),
        grid_spec=pltpu.PrefetchScalarGridSpec(
            num_scalar_prefetch=5,
            grid=(n_blocks,),
            in_specs=[hbm, hbm, hbm, hbm],
            out_specs=hbm,
            scratch_shapes=[pltpu.VMEM((2, blk * ROW_TILE, LANES), U32),
                            pltpu.VMEM((2, blk * ROW_TILE, LANES), U32),
                            pltpu.VMEM((d, de), F32),
                            pltpu.VMEM((d, de), F32),
                            pltpu.VMEM((de, d), F32),
                            pltpu.VMEM((d, de), BF16),
                            pltpu.VMEM((d, de), BF16),
                            pltpu.VMEM((de, d), BF16),
                            pltpu.SemaphoreType.DMA((3,)),
                            pltpu.SemaphoreType.DMA((2,)),
                            pltpu.SemaphoreType.DMA((2,))]),
        compiler_params=pltpu.CompilerParams(dimension_semantics=("arbitrary",),
                                             vmem_limit_bytes=VMEM_LIMIT),
        name="experts",
    )(block_expert, block_valid, block_rows, next_expert, last_block, xs, w_g, w_u, w_d)


def _combine_kernel(d_ref, dn_ref, gate_ref, xs_ref, mod_ref, ys_ref, o_ref, buf, sem):
    tc = d_ref.shape[1]
    i = pl.program_id(0)
    slot = lax.rem(i, 2)

    def gather(dest_ref, into):
        def issue(t, carry):
            for k in range(TOP_K):
                pltpu.make_async_copy(_row(ys_ref, dest_ref[k, t]), _row(buf.at[into], t * COMBINE_PITCH + k),
                                      sem.at[into]).start(priority=k % 2)
            return carry
        lax.fori_loop(0, tc, issue, 0)

    @pl.when(i == 0)
    def _():
        gather(d_ref, 0)

    @pl.when(i + 1 < pl.num_programs(0))
    def _():
        gather(dn_ref, 1 - slot)

    gathered = TOP_K * tc * ROW_TILE
    pltpu.make_async_copy(ys_ref.at[pl.ds(0, gathered), :], buf.at[slot, pl.ds(0, gathered), :],
                          sem.at[slot]).wait()

    gate2 = _mod_row(mod_ref[0], MOD_GATE2)
    rows = buf.at[slot]
    acc = [jnp.zeros((tc, LANES), F32) for _ in range(2 * ROW_TILE)]
    for k in range(TOP_K):
        gk = jnp.transpose(jnp.broadcast_to(gate_ref[k:k + 1, :], (LANES, tc)))
        for s in range(ROW_TILE):
            w = rows[pl.ds(k * ROW_TILE + s, tc, stride=COMBINE_PITCH * ROW_TILE), :]
            acc[s] = acc[s] + gk * _unpack_rows(w, 0)
            acc[ROW_TILE + s] = acc[ROW_TILE + s] + gk * _unpack_rows(w, 1)
    o_ref[...] = xs_ref[...] + gate2 * jnp.concatenate(acc, axis=1)


def _combine(dest, gate, xs, mod3, ys, seq_len, tc):
    n_tok, d = xs.shape
    tiles_per_seq = seq_len // tc
    n_tiles = n_tok // tc
    return pl.pallas_call(
        _combine_kernel,
        out_shape=jax.ShapeDtypeStruct((n_tok, d), F32),
        grid=(n_tiles,),
        in_specs=[pl.BlockSpec((TOP_K, tc), lambda i: (0, i), memory_space=pltpu.SMEM),
                  pl.BlockSpec((TOP_K, tc), lambda i: (0, jnp.minimum(i + 1, n_tiles - 1)),
                               memory_space=pltpu.SMEM),
                  pl.BlockSpec((TOP_K, tc), lambda i: (0, i)),
                  pl.BlockSpec((tc, d), lambda i: (i, 0)),
                  pl.BlockSpec((1, 6, d), lambda i: (i // tiles_per_seq, 0, 0)),
                  pl.BlockSpec(memory_space=pl.ANY)],
        out_specs=pl.BlockSpec((tc, d), lambda i: (i, 0)),
        scratch_shapes=[pltpu.VMEM((2, COMBINE_PITCH * tc * ROW_TILE, LANES), U32),
                        pltpu.SemaphoreType.DMA((2,))],
        compiler_params=pltpu.CompilerParams(dimension_semantics=("arbitrary",),
                                             vmem_limit_bytes=VMEM_LIMIT),
        name="combine",
    )(dest, dest, gate, xs, mod3, ys)


def _tiles(seq_len):
    pick = lambda pref: min(pref, seq_len)
    return dict(adaln=1024, inproj=pick(512), attn=pick(512), ssm=pick(2048), mix=pick(512),
                dispatch=pick(512), combine=pick(256), slots=pick(2048), expert_block=4 * EXPERT_SUB)


def _rope_tables(seq_len):
    half = ATT_HEAD_DIM // 2
    inv_freq = 1.0 / (ROPE_THETA ** (jnp.arange(0, ATT_HEAD_DIM, 2, dtype=F32) / ATT_HEAD_DIM))
    ang = jnp.arange(seq_len, dtype=F32)[:, None] * inv_freq[None, :]
    cos, sin = jnp.cos(ang), jnp.sin(ang)
    reps = LANES // half
    sign = jnp.tile(jnp.concatenate([-jnp.ones((half,), F32), jnp.ones((half,), F32)]), reps // 2)
    return jnp.tile(cos, (1, reps)), jnp.tile(sin, (1, reps)) * sign[None, :]


def _block_tables(counts, blk, n_blocks):
    padded = (counts + blk - 1) // blk * blk
    pad_ends = jnp.cumsum(padded)
    pad_starts = pad_ends - padded
    used = pad_ends[-1] // blk
    last = jnp.maximum(used - 1, 0)
    starts = jnp.arange(n_blocks, dtype=I32) * blk
    expert = jnp.sum((pad_ends[None, :] <= starts[:, None]).astype(I32), axis=1)
    expert = jnp.minimum(expert, N_EXPERTS - 1)
    expert = jnp.where(starts < pad_ends[-1], expert, expert[last])
    valid = jnp.clip(counts[expert] - (starts - pad_starts[expert]), 0, blk)
    valid = jnp.where(starts < pad_ends[-1], valid, 0).astype(I32)
    ids = jnp.arange(N_EXPERTS, dtype=I32)
    later_used = jnp.logical_and(ids[None, :] > expert[:, None], (counts > 0)[None, :])
    upcoming = jnp.min(jnp.where(later_used, ids[None, :], N_EXPERTS), axis=1)
    upcoming = jnp.where(upcoming < N_EXPERTS, upcoming, -1).astype(I32)
    return pad_starts.astype(I32), expert, valid, upcoming, last.reshape(1).astype(I32)


def kernel(x, c, norm1_g, norm2_g, w_ada, b_ada, w_in, q_norm_g, k_norm_g, lambda_q1, lambda_k1, lambda_q2, lambda_k2, subln_g, ssm_a_re, ssm_a_im, ssm_log_dt, ssm_b_re, ssm_b_im, ssm_c_re, ssm_c_im, ssm_d, w_glu, ssm_norm_g, w_out, w_router, router_bias, w_gate_e, w_up_e, w_down_e, w_gate_s, w_up_s, w_down_s):
    bsz, seq_len, d = x.shape
    n_tok = bsz * seq_len
    tiles = _tiles(seq_len)
    assert d == 2 * ROW_TILE * LANES, "packed token rows are ROW_TILE x 128 words of two bf16 each"
    assert w_in.shape[1:] == (d, 2 * QK_WIDTH + ATT_WIDTH + SSM_WIDTH) and w_ada.shape[2] % tiles['adaln'] == 0
    assert w_router.shape[2] == N_EXPERTS and w_gate_e.shape[1] == N_EXPERTS
    assert all(seq_len % tiles[t] == 0 for t in ('inproj', 'attn', 'ssm', 'mix', 'dispatch',
                                                 'combine', 'slots')), "sequence length must be tile aligned"
    blk = tiles['expert_block']
    n_blocks = (n_tok * TOP_K + N_EXPERTS * (blk - 1) + blk - 1) // blk
    cos_t, sin_t = _rope_tables(seq_len)
    seg = jnp.kron(jnp.eye(QK_WIDTH // ATT_HEAD_DIM, dtype=F32),
                   jnp.full((ATT_HEAD_DIM, ATT_HEAD_DIM), 1.0 / ATT_HEAD_DIM, F32)).astype(BF16)
    reps = QK_WIDTH // ATT_HEAD_DIM
    x2d = x.reshape(n_tok, d).astype(F32)
    for layer in range(w_ada.shape[0]):
        lam_init = 0.8 - 0.6 * math.exp(-0.3 * layer)
        mod3 = _adaln(c.astype(F32), w_ada[layer].astype(F32), b_ada[layer].astype(F32),
                      tiles['adaln']).reshape(bsz, 6, d)
        q, k, v, u = _inproj(
            x2d, mod3, norm1_g[layer].astype(F32).reshape(1, d), w_in[layer].astype(BF16), seg,
            jnp.tile(q_norm_g[layer].astype(F32), reps).reshape(1, QK_WIDTH),
            jnp.tile(k_norm_g[layer].astype(F32), reps).reshape(1, QK_WIDTH),
            cos_t, sin_t, seq_len, tiles['inproj'])
        lam = (jnp.exp(jnp.sum(lambda_q1[layer].astype(F32) * lambda_k1[layer].astype(F32)))
               - jnp.exp(jnp.sum(lambda_q2[layer].astype(F32) * lambda_k2[layer].astype(F32))) + lam_init)
        att = _attention(q, k, v, jnp.full((1, LANES), lam, F32),
                         subln_g[layer].astype(F32).reshape(1, ATT_V_DIM),
                         bsz, seq_len, tiles['attn'], 1.0 - lam_init)
        kmat, bmat, cmat, bpow = _ssm_mats(ssm_a_re[layer], ssm_a_im[layer], ssm_log_dt[layer],
                                         ssm_b_re[layer], ssm_b_im[layer], ssm_c_re[layer],
                                         ssm_c_im[layer], ssm_d[layer])
        y = _ssm(u, kmat, bmat, cmat, bpow, bsz, seq_len, tiles['ssm'])
        wr_t = jnp.transpose(w_router[layer].astype(F32))
        wrh, wrl = _split_bf16(wr_t)
        xs, h2p, eidx, gate, rank, counts = _mix_route(
            x2d, y, att, mod3, w_glu[layer].astype(BF16),
            ssm_norm_g[layer].astype(F32).reshape(1, SSM_WIDTH),
            w_out[layer, :ATT_WIDTH].astype(BF16), w_out[layer, ATT_WIDTH:].astype(BF16),
            norm2_g[layer].astype(F32).reshape(1, d), wrh, wrl,
            router_bias[layer].astype(F32).reshape(N_EXPERTS, 1),
            jnp.concatenate([w_gate_s[layer], w_up_s[layer]], axis=1).astype(BF16),
            w_down_s[layer].astype(BF16), seq_len, tiles['mix'])
        pad_starts, block_expert, block_valid, next_expert, last_block = _block_tables(
            counts.reshape(N_EXPERTS).astype(I32), blk, n_blocks)
        dest = _slots(eidx, rank, pad_starts, tiles['slots'])
        x_slots = _dispatch(dest, h2p, n_blocks * blk, tiles['dispatch'])
        y_slots = _experts(block_expert, block_valid, next_expert, last_block, x_slots,
                           w_gate_e[layer], w_up_e[layer], w_down_e[layer], blk)
        x2d = _combine(dest, gate, xs, mod3, y_slots, seq_len, tiles['combine'])
    return x2d.reshape(bsz, seq_len, d).astype(x.dtype)
```

```python
import functools
import math

import jax
import jax.numpy as jnp
from jax import lax
from jax.experimental import pallas as pl
from jax.experimental.pallas import tpu as pltpu

F32 = jnp.float32
BF16 = jnp.bfloat16
I32 = jnp.int32
U32 = jnp.uint32

LANES = 128
SUBLANES = 8
V7X_VMEM_BYTES = 64 * 1024 * 1024

N_ATT_HEADS = 4
ATT_HEAD_DIM = 64
ATT_V_DIM = 2 * ATT_HEAD_DIM
QK_WIDTH = N_ATT_HEADS * 2 * ATT_HEAD_DIM
ATT_WIDTH = N_ATT_HEADS * ATT_V_DIM
ROPE_THETA = 10000.0
SSM_GROUP = 16
SSM_GROUPS = 32
SSM_STATE = 64
SSM_WIDTH = SSM_GROUPS * SSM_GROUP
SSM_CHUNK = SUBLANES
SSM_LANE_BLOCKS = SSM_WIDTH // LANES
GROUPS_PER_BLOCK = LANES // SSM_GROUP
N_EXPERTS = 256
TOP_K = 8
N_GROUPS = 8
TOPK_GROUPS = 4
EXPERTS_PER_GROUP = N_EXPERTS // N_GROUPS
SHARED_DIM = 256
ROUTED_SCALE = 2.5
EPS = 1e-6
MOD_SHIFT1, MOD_SCALE1, MOD_GATE1, MOD_SHIFT2, MOD_SCALE2, MOD_GATE2 = range(6)

NT_DIMS = (((1,), (1,)), ((), ()))

VMEM_LIMIT = V7X_VMEM_BYTES * 3 // 4


def _dot(a, b):
    return jnp.dot(a, b, preferred_element_type=F32)


def _sigmoid(x):
    return 1.0 / (1.0 + jnp.exp(-x))


def _mod_row(mod, row):
    return mod[row:row + 1]


def _split_bf16(x):
    hi = x.astype(BF16)
    lo = (x - hi.astype(F32)).astype(BF16)
    return hi, lo


def _adaln_kernel(c_ref, w_ref, b_ref, o_ref):
    c = c_ref[...]
    sc = c * _sigmoid(c)
    o_ref[...] = jnp.dot(sc, w_ref[...], preferred_element_type=F32,
                         precision=lax.Precision.HIGHEST) + b_ref[...]


def _adaln(c, w, b, tn):
    bsz, d = c.shape
    n = w.shape[1]
    return pl.pallas_call(
        _adaln_kernel,
        out_shape=jax.ShapeDtypeStruct((bsz, n), F32),
        grid=(n // tn,),
        in_specs=[pl.BlockSpec((bsz, d), lambda j: (0, 0)),
                  pl.BlockSpec((d, tn), lambda j: (0, j)),
                  pl.BlockSpec((1, tn), lambda j: (0, j))],
        out_specs=pl.BlockSpec((bsz, tn), lambda j: (0, j)),
        compiler_params=pltpu.CompilerParams(dimension_semantics=("arbitrary",),
                                             vmem_limit_bytes=VMEM_LIMIT),
        name="adaln",
    )(c, w, b.reshape(1, n))


def _inproj_kernel(x_ref, mod_ref, n1g_ref, w_ref, seg_ref, qg_ref, kg_ref, cos_ref, sin_ref,
                   q_ref, k_ref, v_ref, u_ref):
    x = x_ref[...]
    mod = mod_ref[0]
    shift, scale = _mod_row(mod, MOD_SHIFT1), _mod_row(mod, MOD_SCALE1)
    ms = jnp.mean(x * x, axis=-1, keepdims=True)
    h = x * lax.rsqrt(ms + EPS) * n1g_ref[...]
    h = h * (1.0 + scale) + shift
    z = _dot(h.astype(BF16), w_ref[...])
    seg = seg_ref[...]
    cos = cos_ref[...]
    sin = sin_ref[...]
    lane = lax.broadcasted_iota(I32, cos.shape, 1)
    first_half = (lane % ATT_HEAD_DIM) < (ATT_HEAD_DIM // 2)

    def norm_rope(t, g_ref, out_scale):
        hi, lo = _split_bf16(t * t)
        msq = _dot(hi, seg) + _dot(lo, seg)
        tn = t * lax.rsqrt(msq + EPS) * g_ref[...]
        outs = []
        for hd in range(N_ATT_HEADS):
            th = tn[:, hd * LANES:(hd + 1) * LANES]
            partner = jnp.where(first_half,
                                pltpu.roll(th, LANES - ATT_HEAD_DIM // 2, 1),
                                pltpu.roll(th, ATT_HEAD_DIM // 2, 1))
            outs.append((th * cos + partner * sin) * out_scale)
        return jnp.concatenate(outs, axis=1)

    q_ref[...] = norm_rope(z[:, :QK_WIDTH], qg_ref, ATT_HEAD_DIM ** -0.5).astype(BF16)
    k_ref[...] = norm_rope(z[:, QK_WIDTH:2 * QK_WIDTH], kg_ref, 1.0).astype(BF16)
    v_ref[...] = z[:, 2 * QK_WIDTH:2 * QK_WIDTH + ATT_WIDTH].astype(BF16)
    u_ref[...] = z[:, 2 * QK_WIDTH + ATT_WIDTH:]


def _inproj(x2d, mod3, n1g, w_in_bf, seg, qg, kg, cos_t, sin_t, seq_len, tm):
    n_tok, d = x2d.shape
    tiles_per_seq = seq_len // tm
    in_width = w_in_bf.shape[1]
    row = lambda i: (i, 0)
    const = lambda i: (0, 0)
    return pl.pallas_call(
        _inproj_kernel,
        out_shape=(jax.ShapeDtypeStruct((n_tok, QK_WIDTH), BF16),
                   jax.ShapeDtypeStruct((n_tok, QK_WIDTH), BF16),
                   jax.ShapeDtypeStruct((n_tok, ATT_WIDTH), BF16),
                   jax.ShapeDtypeStruct((n_tok, SSM_WIDTH), F32)),
        grid=(n_tok // tm,),
        in_specs=[pl.BlockSpec((tm, d), row),
                  pl.BlockSpec((1, 6, d), lambda i: (i // tiles_per_seq, 0, 0)),
                  pl.BlockSpec((1, d), const),
                  pl.BlockSpec((d, in_width), const),
                  pl.BlockSpec((QK_WIDTH, QK_WIDTH), const),
                  pl.BlockSpec((1, QK_WIDTH), const),
                  pl.BlockSpec((1, QK_WIDTH), const),
                  pl.BlockSpec((tm, LANES), lambda i: (i % tiles_per_seq, 0)),
                  pl.BlockSpec((tm, LANES), lambda i: (i % tiles_per_seq, 0))],
        out_specs=(pl.BlockSpec((tm, QK_WIDTH), row),
                   pl.BlockSpec((tm, QK_WIDTH), row),
                   pl.BlockSpec((tm, ATT_WIDTH), row),
                   pl.BlockSpec((tm, SSM_WIDTH), row)),
        compiler_params=pltpu.CompilerParams(dimension_semantics=("arbitrary",),
                                             vmem_limit_bytes=VMEM_LIMIT),
        name="inproj",
    )(x2d, mod3, n1g, w_in_bf, seg, qg, kg, cos_t, sin_t)


def _attn_kernel(qi_ref, kj_ref, q_ref, k_ref, v_ref, lam_ref, sg_ref, o_ref,
                 m_ref, l_ref, acc_ref, *, out_scale):
    p = pl.program_id(2)
    qi = qi_ref[p]
    kj = kj_ref[p]

    @pl.when(kj == 0)
    def _():
        m_ref[...] = jnp.full(m_ref.shape, -jnp.inf, F32)
        l_ref[...] = jnp.zeros(l_ref.shape, F32)
        acc_ref[...] = jnp.zeros(acc_ref.shape, F32)

    t = q_ref.shape[0]
    h = t // 2

    def stacked_queries():
        q = q_ref[...]
        lane = lax.broadcasted_iota(I32, q.shape, 1)
        zero = jnp.zeros_like(q)
        q0 = jnp.where(lane < ATT_HEAD_DIM, q, zero)
        q1 = jnp.where(lane >= ATT_HEAD_DIM, q, zero)
        return jnp.concatenate([q0[:h], q1[:h], q0[h:], q1[h:]], axis=0)

    def update(lo, hi, s, v):
        m_old = m_ref[lo:hi]
        m_new = jnp.maximum(m_old, jnp.max(s, axis=-1, keepdims=True))
        alpha = jnp.exp(m_old - m_new)
        pr = jnp.exp(s - jnp.concatenate([m_new] * (s.shape[1] // LANES), axis=1))
        l_ref[lo:hi] = alpha * l_ref[lo:hi] + jnp.sum(pr, axis=-1, keepdims=True)
        acc_ref[lo:hi] = alpha * acc_ref[lo:hi] + _dot(pr.astype(BF16), v)
        m_ref[lo:hi] = m_new

    @pl.when(kj < qi)
    def _():
        s = lax.dot_general(stacked_queries(), k_ref[...], NT_DIMS, preferred_element_type=F32)
        update(0, 2 * t, s, v_ref[...])

    @pl.when(kj == qi)
    def _():
        qq = stacked_queries()
        k = k_ref[...]
        v = v_ref[...]
        r = lax.broadcasted_iota(I32, (h, t), 0)
        col = lax.broadcasted_iota(I32, (h, t), 1)
        early_ok = (col <= r)[:, :h]
        late_ok = col <= r + h
        s = lax.dot_general(qq[:t], k[:h], NT_DIMS, preferred_element_type=F32)
        update(0, t, jnp.where(jnp.concatenate([early_ok, early_ok], axis=0), s, -jnp.inf), v[:h])
        s = lax.dot_general(qq[t:], k, NT_DIMS, preferred_element_type=F32)
        update(t, 2 * t, jnp.where(jnp.concatenate([late_ok, late_ok], axis=0), s, -jnp.inf), v)

        o = acc_ref[...] / l_ref[...]
        o = (jnp.concatenate([o[:h], o[t:t + h]], axis=0)
             - lam_ref[...] * jnp.concatenate([o[h:t], o[t + h:]], axis=0))
        ms = jnp.mean(o * o, axis=-1, keepdims=True)
        o_ref[...] = (o * lax.rsqrt(ms + EPS) * sg_ref[...] * out_scale).astype(o_ref.dtype)


def _attention(q, k, v, lam_row, subln_g, bsz, seq_len, tq, out_scale):
    nq = seq_len // tq
    pairs = [(i, j) for i in range(nq) for j in range(i + 1)]
    qi = jnp.asarray([p[0] for p in pairs], I32)
    kj = jnp.asarray([p[1] for p in pairs], I32)
    q_map = lambda b, h, p, qi, kj: (b * nq + qi[p], h)
    k_map = lambda b, h, p, qi, kj: (b * nq + kj[p], h)
    const = lambda b, h, p, qi, kj: (0, 0)
    return pl.pallas_call(
        functools.partial(_attn_kernel, out_scale=out_scale),
        out_shape=jax.ShapeDtypeStruct((bsz * seq_len, ATT_WIDTH), BF16),
        grid_spec=pltpu.PrefetchScalarGridSpec(
            num_scalar_prefetch=2,
            grid=(bsz, N_ATT_HEADS, len(pairs)),
            in_specs=[pl.BlockSpec((tq, LANES), q_map),
                      pl.BlockSpec((tq, LANES), k_map),
                      pl.BlockSpec((tq, LANES), k_map),
                      pl.BlockSpec((1, LANES), const),
                      pl.BlockSpec((1, LANES), const)],
            out_specs=pl.BlockSpec((tq, LANES), q_map),
            scratch_shapes=[pltpu.VMEM((2 * tq, LANES), F32),
                            pltpu.VMEM((2 * tq, LANES), F32),
                            pltpu.VMEM((2 * tq, ATT_V_DIM), F32)]),
        compiler_params=pltpu.CompilerParams(
            dimension_semantics=("arbitrary", "arbitrary", "arbitrary"),
            vmem_limit_bytes=VMEM_LIMIT),
        name="attn",
    )(qi, kj, q, k, v, lam_row, subln_g)


def _ssm_mats(a_re, a_im, log_dt, b_re, b_im, c_re, c_im, d_skip):
    a_re, a_im, b_re, b_im, c_re, c_im, d_skip = (
        t.astype(F32) for t in (a_re, a_im, b_re, b_im, c_re, c_im, d_skip))
    dt = jnp.exp(log_dt.astype(F32))[:, None]
    mag = jnp.exp(a_re * dt)
    abar_re = mag * jnp.cos(a_im * dt)
    abar_im = mag * jnp.sin(a_im * dt)
    den = a_re * a_re + a_im * a_im
    nr = abar_re - 1.0
    f_re = ((nr * a_re + abar_im * a_im) / den)[..., None]
    f_im = ((abar_im * a_re - nr * a_im) / den)[..., None]
    bb_re = f_re * b_re - f_im * b_im
    bb_im = f_re * b_im + f_im * b_re
    steps = jnp.arange(SSM_CHUNK + 1, dtype=F32)[:, None, None]
    pmag = jnp.exp(a_re * dt * steps)
    pw_re = pmag * jnp.cos(a_im * dt * steps)
    pw_im = pmag * jnp.sin(a_im * dt * steps)
    ca_re = c_re[None] * pw_re[:, :, None, :] - c_im[None] * pw_im[:, :, None, :]
    ca_im = c_re[None] * pw_im[:, :, None, :] + c_im[None] * pw_re[:, :, None, :]
    bbt_re = jnp.transpose(bb_re, (0, 2, 1))[None, :, None]
    bbt_im = jnp.transpose(bb_im, (0, 2, 1))[None, :, None]
    lag = jnp.sum(ca_re[:SSM_CHUNK, :, :, None, :] * bbt_re
                  - ca_im[:SSM_CHUNK, :, :, None, :] * bbt_im, axis=-1)
    lag = lag.at[0].add(d_skip[:, :, None] * jnp.eye(SSM_GROUP, dtype=F32)[None])
    nb, gpb = SSM_LANE_BLOCKS, GROUPS_PER_BLOCK
    n_state = gpb * SSM_STATE
    in_group = jnp.arange(LANES) // SSM_GROUP
    state_group = jnp.arange(n_state) // SSM_STATE

    def spread(small, row_group, col_group):
        w = small.shape[-1]
        tiled = jnp.tile(jnp.eye(w, dtype=BF16), (1, col_group.shape[0] // w))
        wide = jnp.einsum('...w,wn->...n', small.astype(BF16), tiled, preferred_element_type=F32)
        return jnp.where(row_group[:, None] == col_group[None, :], wide, 0.0).astype(BF16)

    lag_blocks = spread(jnp.transpose(lag, (0, 1, 3, 2)).reshape(SSM_CHUNK, nb, LANES, SSM_GROUP),
                        in_group, in_group)
    zero_block = jnp.zeros_like(lag_blocks[0])
    kmat = jnp.concatenate(
        [jnp.concatenate([lag_blocks[t - j] if t >= j else zero_block for t in range(SSM_CHUNK)], axis=-1)
         for j in range(SSM_CHUNK)], axis=1)
    rev = SSM_CHUNK - 1 - jnp.arange(SSM_CHUNK)
    w_re = pw_re[rev][..., None] * bb_re[None] - pw_im[rev][..., None] * bb_im[None]
    w_im = pw_re[rev][..., None] * bb_im[None] + pw_im[rev][..., None] * bb_re[None]

    def in_to_state(w):
        small = jnp.transpose(w, (0, 1, 3, 2)).reshape(SSM_CHUNK, nb, LANES, SSM_STATE)
        return spread(small, in_group, state_group)

    bmat = jnp.concatenate([in_to_state(w_re), in_to_state(w_im)], axis=-1)
    bmat = jnp.transpose(bmat, (1, 0, 2, 3)).reshape(nb, SSM_CHUNK * LANES, 2 * n_state)

    def state_to_out(ca):
        small = jnp.transpose(ca[1:], (0, 1, 3, 2)).reshape(SSM_CHUNK, nb, n_state, SSM_GROUP)
        blocks = spread(small, state_group, in_group)
        return jnp.transpose(blocks, (1, 2, 0, 3)).reshape(nb, n_state, SSM_CHUNK * LANES)

    cmat = jnp.concatenate([state_to_out(ca_re), -state_to_out(ca_im)], axis=1)
    chunk_steps = SSM_CHUNK * jnp.arange(2 * SUBLANES, dtype=F32)[:, None, None]
    cmag = jnp.exp(a_re * dt * chunk_steps)
    to_block = lambda t: jnp.transpose(t.reshape(2 * SUBLANES, nb, n_state), (1, 0, 2))
    bpow = jnp.concatenate([to_block(cmag * jnp.cos(a_im * dt * chunk_steps)),
                            to_block(cmag * jnp.sin(a_im * dt * chunk_steps))], axis=-1)
    return kmat, bmat, cmat, bpow


def _cmul(a_re, a_im, x_re, x_im):
    return a_re * x_re - a_im * x_im, a_re * x_im + a_im * x_re


def _ssm_kernel(u_ref, km_ref, bm_ref, cm_ref, bp_ref, y_ref, carry_ref, sp_ref):
    n_chunks = sp_ref.shape[0]
    half = sp_ref.shape[1] // 2

    @pl.when(pl.program_id(2) == 0)
    def _():
        carry_ref[...] = jnp.zeros(carry_ref.shape, F32)

    ucat = jnp.concatenate([u_ref[pl.ds(j, n_chunks, stride=SSM_CHUNK), :] for j in range(SSM_CHUNK)],
                           axis=1).astype(BF16)
    s_end = _dot(ucat, bm_ref[0])
    p_re, p_im = s_end[:, :half], s_end[:, half:]
    bp = bp_ref[0]
    sub = lax.broadcasted_iota(I32, (n_chunks, half), 0) % SUBLANES

    def shifted(v, d):
        return jnp.where(sub >= d, pltpu.roll(v, d, 0), 0.0)

    for d in (1, 2, 4):
        d_re, d_im = _cmul(bp[d:d + 1, :half], bp[d:d + 1, half:], shifted(p_re, d), shifted(p_im, d))
        p_re, p_im = p_re + d_re, p_im + d_im
    x_re, x_im = shifted(p_re, 1), shifted(p_im, 1)
    t_re, t_im = bp[:SUBLANES, :half], bp[:SUBLANES, half:]
    l_re, l_im = bp[SUBLANES:SUBLANES + 1, :half], bp[SUBLANES:SUBLANES + 1, half:]
    c_re, c_im = carry_ref[:, :half], carry_ref[:, half:]
    for g in range(n_chunks // SUBLANES):
        lo, hi = g * SUBLANES, (g + 1) * SUBLANES
        d_re, d_im = _cmul(t_re, t_im, c_re, c_im)
        sp_ref[lo:hi, :half] = x_re[lo:hi] + d_re
        sp_ref[lo:hi, half:] = x_im[lo:hi] + d_im
        e_re, e_im = _cmul(l_re, l_im, c_re, c_im)
        c_re, c_im = p_re[hi - 1:hi] + e_re, p_im[hi - 1:hi] + e_im
    carry_ref[...] = jnp.concatenate([c_re, c_im], axis=1)
    y = _dot(ucat, km_ref[0]) + _dot(sp_ref[...].astype(BF16), cm_ref[0])
    for t in range(SSM_CHUNK):
        y_ref[pl.ds(t, n_chunks, stride=SSM_CHUNK), :] = y[:, t * LANES:(t + 1) * LANES]


def _ssm(u, kmat, bmat, cmat, bpow, bsz, seq_len, tt):
    nt = seq_len // tt
    n_chunks = tt // SSM_CHUNK
    width = SSM_CHUNK * LANES
    u_map = lambda g, b, i: (b * nt + i, g)
    w_map = lambda g, b, i: (g, 0, 0)
    return pl.pallas_call(
        _ssm_kernel,
        out_shape=jax.ShapeDtypeStruct((bsz * seq_len, SSM_WIDTH), F32),
        grid=(SSM_LANE_BLOCKS, bsz, nt),
        in_specs=[pl.BlockSpec((tt, LANES), u_map),
                  pl.BlockSpec((1, width, width), w_map),
                  pl.BlockSpec((1, width, width), w_map),
                  pl.BlockSpec((1, width, width), w_map),
                  pl.BlockSpec((1, 2 * SUBLANES, width), w_map)],
        out_specs=pl.BlockSpec((tt, LANES), u_map),
        scratch_shapes=[pltpu.VMEM((1, width), F32),
                        pltpu.VMEM((n_chunks, width), F32)],
        compiler_params=pltpu.CompilerParams(
            dimension_semantics=("arbitrary", "arbitrary", "arbitrary"),
            vmem_limit_bytes=VMEM_LIMIT),
        name="ssm",
    )(u, kmat, bmat, cmat, bpow)


def _first_max(v, ids, n):
    m = jnp.max(v, axis=0, keepdims=True)
    ix = jnp.min(jnp.where(v == m, ids, n), axis=0, keepdims=True)
    return m, ix


def _mix_route_kernel(x_ref, y_ref, att_ref, mod_ref, wglu_ref, sng_ref, woa_ref, wos_ref, n2g_ref,
                      wrh_ref, wrl_ref, rb_ref, wgu_ref, wds_ref,
                      xs_ref, h2p_ref, eidx_ref, gate_ref, rank_ref, cnt_ref, carry_ref):
    tm = x_ref.shape[0]

    @pl.when(pl.program_id(0) == 0)
    def _():
        carry_ref[...] = jnp.zeros(carry_ref.shape, F32)

    mod = mod_ref[0]
    gate1, shift2, scale2, gate2 = (_mod_row(mod, r) for r in (MOD_GATE1, MOD_SHIFT2, MOD_SCALE2, MOD_GATE2))

    y = y_ref[...]
    g = 0.5 * y * (1.0 + jnp.tanh(math.sqrt(2.0 / math.pi) * (y + 0.044715 * (y * y * y))))
    glu = g * _sigmoid(_dot(g.astype(BF16), wglu_ref[...]))
    ssm = glu * lax.rsqrt(jnp.mean(glu * glu, axis=-1, keepdims=True) + EPS) * sng_ref[...]

    mix = _dot(att_ref[...], woa_ref[...]) + _dot(ssm.astype(BF16), wos_ref[...])
    x1 = x_ref[...] + gate1 * mix
    h2 = x1 * lax.rsqrt(jnp.mean(x1 * x1, axis=-1, keepdims=True) + EPS) * n2g_ref[...]
    h2 = h2 * (1.0 + scale2) + shift2
    _store_rows(h2p_ref, _pack_rows(h2))
    hb, h_lo = _split_bf16(h2)

    xs_ref[...] = x1

    wrh = wrh_ref[...]
    logits = (lax.dot_general(wrh, hb, NT_DIMS, preferred_element_type=F32)
              + lax.dot_general(wrh, h_lo, NT_DIMS, preferred_element_type=F32)
              + lax.dot_general(wrl_ref[...], hb, NT_DIMS, preferred_element_type=F32))
    score = _sigmoid(logits)
    biased = score + rb_ref[...]
    neg = -jnp.inf

    ids_g = lax.broadcasted_iota(I32, (EXPERTS_PER_GROUP, tm), 0)
    group_rows = []
    for gi in range(N_GROUPS):
        blk = biased[gi * EXPERTS_PER_GROUP:(gi + 1) * EXPERTS_PER_GROUP, :]
        m1, i1 = _first_max(blk, ids_g, EXPERTS_PER_GROUP)
        m2 = jnp.max(jnp.where(ids_g == i1, neg, blk), axis=0, keepdims=True)
        group_rows.append(m1 + m2)
    cur = jnp.concatenate(group_rows, axis=0)
    ids_8 = lax.broadcasted_iota(I32, cur.shape, 0)
    picked = jnp.zeros(cur.shape, F32)
    for _ in range(TOPK_GROUPS):
        _, ix = _first_max(cur, ids_8, N_GROUPS)
        hit = ids_8 == ix
        picked = jnp.where(hit, 1.0, picked)
        cur = jnp.where(hit, neg, cur)
    e_mask = jnp.concatenate(
        [jnp.broadcast_to(picked[gi:gi + 1, :], (EXPERTS_PER_GROUP, tm)) for gi in range(N_GROUPS)], axis=0)
    cand = jnp.where(e_mask > 0.0, biased, neg)

    ids_e = lax.broadcasted_iota(I32, cand.shape, 0)
    sel = jnp.zeros(cand.shape, F32)
    idx_rows, w_rows = [], []
    for _ in range(TOP_K):
        _, ix = _first_max(cand, ids_e, N_EXPERTS)
        hit = ids_e == ix
        idx_rows.append(ix)
        w_rows.append(jnp.sum(jnp.where(hit, score, 0.0), axis=0, keepdims=True))
        sel = jnp.where(hit, 1.0, sel)
        cand = jnp.where(hit, neg, cand)
    w_sum = w_rows[0]
    for w in w_rows[1:]:
        w_sum = w_sum + w
    gate_ref[...] = jnp.concatenate([w / w_sum * ROUTED_SCALE for w in w_rows], axis=0)
    eidx_ref[...] = jnp.concatenate(idx_rows, axis=0)

    t_row = lax.broadcasted_iota(I32, (tm, tm), 0)
    t_col = lax.broadcasted_iota(I32, (tm, tm), 1)
    earlier = jnp.where(t_row < t_col, 1.0, 0.0).astype(BF16)
    before = _dot(sel.astype(BF16), earlier) + carry_ref[...]
    rank_ref[...] = jnp.concatenate(
        [jnp.sum(jnp.where(ids_e == ix, before, 0.0), axis=0, keepdims=True) for ix in idx_rows],
        axis=0).astype(I32)
    carry_ref[...] = carry_ref[...] + jnp.sum(sel, axis=1, keepdims=True)
    cnt_ref[...] = carry_ref[...]


def _mix_route(x2d, y, att, mod3, wglu, sng, woa, wos, n2g, wrh, wrl, rb, wgu, wds, seq_len, tm):
    n_tok, d = x2d.shape
    tiles_per_seq = seq_len // tm
    row = lambda i: (i, 0)
    col = lambda i: (0, i)
    const = lambda i: (0, 0)
    full = lambda a: pl.BlockSpec(a.shape, const)
    return pl.pallas_call(
        _mix_route_kernel,
        out_shape=(jax.ShapeDtypeStruct((n_tok, d), F32),
                   jax.ShapeDtypeStruct((n_tok * ROW_TILE, LANES), U32),
                   jax.ShapeDtypeStruct((TOP_K, n_tok), I32),
                   jax.ShapeDtypeStruct((TOP_K, n_tok), F32),
                   jax.ShapeDtypeStruct((TOP_K, n_tok), I32),
                   jax.ShapeDtypeStruct((N_EXPERTS, 1), F32)),
        grid=(n_tok // tm,),
        in_specs=[pl.BlockSpec((tm, d), row),
                  pl.BlockSpec((tm, SSM_WIDTH), row),
                  pl.BlockSpec((tm, ATT_WIDTH), row),
                  pl.BlockSpec((1, 6, d), lambda i: (i // tiles_per_seq, 0, 0)),
                  full(wglu), full(sng), full(woa), full(wos), full(n2g),
                  full(wrh), full(wrl), full(rb), full(wgu), full(wds)],
        out_specs=(pl.BlockSpec((tm, d), row),
                   pl.BlockSpec((tm * ROW_TILE, LANES), row),
                   pl.BlockSpec((TOP_K, tm), col),
                   pl.BlockSpec((TOP_K, tm), col),
                   pl.BlockSpec((TOP_K, tm), col),
                   pl.BlockSpec((N_EXPERTS, 1), const)),
        scratch_shapes=[pltpu.VMEM((N_EXPERTS, 1), F32)],
        compiler_params=pltpu.CompilerParams(dimension_semantics=("arbitrary",),
                                             vmem_limit_bytes=VMEM_LIMIT),
        name="mix_route",
    )(x2d, y, att, mod3, wglu, sng, woa, wos, n2g, wrh, wrl, rb, wgu, wds)


ROW_TILE = 4
EXPERT_SUB = 256
COMBINE_PITCH = TOP_K + 1


def _pack_rows(v):
    half = v.shape[1] // 2
    return pltpu.pack_elementwise([v[:, :half], v[:, half:]], packed_dtype=BF16)


def _unpack_rows(w, index):
    return pltpu.unpack_elementwise(w, index=index, packed_dtype=BF16, unpacked_dtype=F32)


def _store_rows(ref, packed, first=0):
    m = packed.shape[0]
    for s in range(ROW_TILE):
        ref[pl.ds(first * ROW_TILE + s, m, stride=ROW_TILE), :] = packed[:, s * LANES:(s + 1) * LANES]


def _load_rows(ref, m, first=0):
    return jnp.concatenate([ref[pl.ds(first * ROW_TILE + s, m, stride=ROW_TILE), :] for s in range(ROW_TILE)],
                           axis=1)


def _row(ref, r):
    return ref.at[pl.ds(pl.multiple_of(r * ROW_TILE, ROW_TILE), ROW_TILE), :]


def _slot_kernel(e_ref, r_ref, ps_ref, d_ref):
    e = e_ref[...]
    ids = lax.broadcasted_iota(I32, (N_EXPERTS, e.shape[1]), 0)
    starts = ps_ref[...]
    rows = [jnp.sum(jnp.where(ids == e[k:k + 1, :], starts, 0.0), axis=0, keepdims=True)
            for k in range(TOP_K)]
    d_ref[...] = jnp.concatenate(rows, axis=0).astype(I32) + r_ref[...]


def _slots(eidx, rank, pad_starts, ts):
    n_tok = eidx.shape[1]
    col = lambda i: (0, i)
    return pl.pallas_call(
        _slot_kernel,
        out_shape=jax.ShapeDtypeStruct((TOP_K, n_tok), I32),
        grid=(n_tok // ts,),
        in_specs=[pl.BlockSpec((TOP_K, ts), col),
                  pl.BlockSpec((TOP_K, ts), col),
                  pl.BlockSpec((N_EXPERTS, 1), lambda i: (0, 0))],
        out_specs=pl.BlockSpec((TOP_K, ts), col),
        compiler_params=pltpu.CompilerParams(dimension_semantics=("arbitrary",),
                                             vmem_limit_bytes=VMEM_LIMIT),
        name="slots",
    )(eidx, rank, pad_starts.astype(F32).reshape(N_EXPERTS, 1))


def _dispatch_kernel(d_ref, h2p_ref, x1_ref, mod_ref, wgu_ref, wds_ref, xs_ref, xsh_ref, sem):
    td = d_ref.shape[1]

    def issue(t, carry):
        for k in range(TOP_K):
            pltpu.make_async_copy(_row(h2p_ref, t), _row(xs_ref, d_ref[k, t]), sem).start(priority=k % 2)
        return carry

    lax.fori_loop(0, td, issue, 0)
    packed = _load_rows(h2p_ref, td)
    hb = jnp.concatenate([_unpack_rows(packed, 0), _unpack_rows(packed, 1)], axis=1).astype(BF16)
    gu = _dot(hb, wgu_ref[...])
    gs, us = gu[:, :SHARED_DIM], gu[:, SHARED_DIM:]
    act = (gs * _sigmoid(gs) * us).astype(BF16)
    xsh_ref[...] = x1_ref[...] + _mod_row(mod_ref[0], MOD_GATE2) * _dot(act, wds_ref[...])
    for _ in range(TOP_K):
        pltpu.make_async_copy(h2p_ref, xs_ref.at[pl.ds(0, td * ROW_TILE), :], sem).wait()


def _dispatch(dest, h2p, x1, mod3, wgu, wds, n_slots, seq_len, td):
    n_tok, d = x1.shape
    tiles_per_seq = seq_len // td
    const = lambda i: (0, 0)
    return pl.pallas_call(
        _dispatch_kernel,
        out_shape=(jax.ShapeDtypeStruct((n_slots * ROW_TILE, LANES), U32),
                   jax.ShapeDtypeStruct((n_tok, d), F32)),
        grid=(n_tok // td,),
        in_specs=[pl.BlockSpec((TOP_K, td), lambda i: (0, i), memory_space=pltpu.SMEM),
                  pl.BlockSpec((td * ROW_TILE, LANES), lambda i: (i, 0)),
                  pl.BlockSpec((td, d), lambda i: (i, 0)),
                  pl.BlockSpec((1, 6, d), lambda i: (i // tiles_per_seq, 0, 0)),
                  pl.BlockSpec(wgu.shape, const),
                  pl.BlockSpec(wds.shape, const)],
        out_specs=(pl.BlockSpec(memory_space=pl.ANY),
                   pl.BlockSpec((td, d), lambda i: (i, 0))),
        scratch_shapes=[pltpu.SemaphoreType.DMA],
        compiler_params=pltpu.CompilerParams(dimension_semantics=("arbitrary",),
                                             vmem_limit_bytes=VMEM_LIMIT),
        name="dispatch",
    )(dest, h2p, x1, mod3, wgu, wds)


def _expert_kernel(be_ref, nv_ref, nr_ref, nx_ref, last_ref, xs_hbm, wg_hbm, wu_hbm, wd_hbm, ys_hbm,
                   x_buf, y_buf, wg_f32, wu_f32, wd_f32, wg_bf, wu_bf, wd_bf, sem, x_sem, y_sem):
    i = pl.program_id(0)
    blk = x_buf.shape[1] // ROW_TILE
    sizes = range(EXPERT_SUB, blk + 1, EXPERT_SUB)
    expert = be_ref[i]
    n_valid = nv_ref[i]
    last = last_ref[0]
    slot = lax.rem(i, 2)
    changed = jnp.logical_or(i == 0, expert != be_ref[jnp.maximum(i - 1, 0)])

    def block_rows(ref, block, n_rows):
        return ref.at[pl.ds(pl.multiple_of(block * (blk * ROW_TILE), blk * ROW_TILE), n_rows * ROW_TILE), :]

    def x_copy(block, into, n_rows):
        return pltpu.make_async_copy(block_rows(xs_hbm, block, n_rows),
                                     x_buf.at[into, pl.ds(0, n_rows * ROW_TILE), :], x_sem.at[into])

    def y_copy(block, out_of, n_rows):
        return pltpu.make_async_copy(y_buf.at[out_of, pl.ds(0, n_rows * ROW_TILE), :],
                                     block_rows(ys_hbm, block, n_rows), y_sem.at[out_of])

    def for_rows(n, fn):
        for n_rows in sizes:
            pl.when(n == n_rows)(functools.partial(fn, n_rows))

    @pl.when(i == 0)
    def _():
        for_rows(nr_ref[0], lambda n_rows: x_copy(0, 0, n_rows).start())

    @pl.when(jnp.logical_and(i >= 2, i <= last))
    def _():
        for_rows(nr_ref[jnp.maximum(i - 2, 0)], lambda n_rows: y_copy(i - 2, slot, n_rows).wait())

    def weight_copies(e):
        return [pltpu.make_async_copy(wg_hbm.at[e], wg_f32, sem.at[0]),
                pltpu.make_async_copy(wu_hbm.at[e], wu_f32, sem.at[1]),
                pltpu.make_async_copy(wd_hbm.at[e], wd_f32, sem.at[2])]

    @pl.when(i == 0)
    def _():
        for cp in weight_copies(expert):
            cp.start()

    @pl.when(jnp.logical_and(n_valid > 0, changed))
    def _():
        for cp in weight_copies(expert):
            cp.wait()
        wg_bf[...] = wg_f32[...].astype(BF16)
        wu_bf[...] = wu_f32[...].astype(BF16)
        wd_bf[...] = wd_f32[...].astype(BF16)
        upcoming = nx_ref[i]

        @pl.when(upcoming >= 0)
        def _():
            for cp in weight_copies(upcoming):
                cp.start()

    def swiglu_rows(n_rows):
        x_copy(i, slot, n_rows).wait()
        upcoming_rows = nr_ref[jnp.minimum(i + 1, pl.num_programs(0) - 1)]
        for_rows(jnp.where(i < last, upcoming_rows, 0), lambda n_next: x_copy(i + 1, 1 - slot, n_next).start())
        packed = _load_rows(x_buf.at[slot], n_rows)
        x = jnp.concatenate([_unpack_rows(packed, 0), _unpack_rows(packed, 1)], axis=1)
        rows = lax.broadcasted_iota(I32, (n_rows, 1), 0)
        x = jnp.where(rows < n_valid, x, 0.0).astype(BF16)
        g = _dot(x, wg_bf[...])
        u = _dot(x, wu_bf[...])
        act = (g * _sigmoid(g) * u).astype(BF16)
        _store_rows(y_buf.at[slot], _pack_rows(_dot(act, wd_bf[...])))
        y_copy(i, slot, n_rows).start()

    for_rows(nr_ref[i], swiglu_rows)

    @pl.when(i == last)
    def _():
        for_rows(nr_ref[i], lambda n_rows: y_copy(i, slot, n_rows).wait())

        @pl.when(i >= 1)
        def _():
            for_rows(nr_ref[jnp.maximum(i - 1, 0)], lambda n_rows: y_copy(i - 1, 1 - slot, n_rows).wait())


def _experts(block_expert, block_valid, next_expert, last_block, xs, w_g, w_u, w_d, blk):
    n_blocks = block_expert.shape[0]
    d, de = w_g.shape[1], w_g.shape[2]
    block_rows = (block_valid + EXPERT_SUB - 1) // EXPERT_SUB * EXPERT_SUB
    hbm = pl.BlockSpec(memory_space=pl.ANY)
    return pl.pallas_call(
        _expert_kernel,
        out_shape=jax.ShapeDtypeStruct(xs.shape, U32),
        grid_spec=pltpu.PrefetchScalarGridSpec(
            num_scalar_prefetch=5,
            grid=(n_blocks,),
            in_specs=[hbm, hbm, hbm, hbm],
            out_specs=hbm,
            scratch_shapes=[pltpu.VMEM((2, blk * ROW_TILE, LANES), U32),
                            pltpu.VMEM((2, blk * ROW_TILE, LANES), U32),
                            pltpu.VMEM((d, de), F32),
                            pltpu.VMEM((d, de), F32),
                            pltpu.VMEM((de, d), F32),
                            pltpu.VMEM((d, de), BF16),
                            pltpu.VMEM((d, de), BF16),
                            pltpu.VMEM((de, d), BF16),
                            pltpu.SemaphoreType.DMA((3,)),
                            pltpu.SemaphoreType.DMA((2,)),
                            pltpu.SemaphoreType.DMA((2,))]),
        compiler_params=pltpu.CompilerParams(dimension_semantics=("arbitrary",),
                                             vmem_limit_bytes=VMEM_LIMIT),
        name="experts",
    )(block_expert, block_valid, block_rows, next_expert, last_block, xs, w_g, w_u, w_d)


def _combine_kernel(d_ref, dn_ref, gate_ref, xs_ref, mod_ref, ys_ref, o_ref, buf, sem):
    tc = d_ref.shape[1]
    i = pl.program_id(0)
    slot = lax.rem(i, 2)

    def gather(dest_ref, into):
        def issue(t, carry):
            for k in range(TOP_K):
                pltpu.make_async_copy(_row(ys_ref, dest_ref[k, t]), _row(buf.at[into], t * COMBINE_PITCH + k),
                                      sem.at[into]).start(priority=k % 2)
            return carry
        lax.fori_loop(0, tc, issue, 0)

    @pl.when(i == 0)
    def _():
        gather(d_ref, 0)

    @pl.when(i + 1 < pl.num_programs(0))
    def _():
        gather(dn_ref, 1 - slot)

    gathered = TOP_K * tc * ROW_TILE
    pltpu.make_async_copy(ys_ref.at[pl.ds(0, gathered), :], buf.at[slot, pl.ds(0, gathered), :],
                          sem.at[slot]).wait()

    gate2 = _mod_row(mod_ref[0], MOD_GATE2)
    rows = buf.at[slot]
    acc = [jnp.zeros((tc, LANES), F32) for _ in range(2 * ROW_TILE)]
    for k in range(TOP_K):
        gk = jnp.transpose(jnp.broadcast_to(gate_ref[k:k + 1, :], (LANES, tc)))
        for s in range(ROW_TILE):
            w = rows[pl.ds(k * ROW_TILE + s, tc, stride=COMBINE_PITCH * ROW_TILE), :]
            acc[s] = acc[s] + gk * _unpack_rows(w, 0)
            acc[ROW_TILE + s] = acc[ROW_TILE + s] + gk * _unpack_rows(w, 1)
    o_ref[...] = xs_ref[...] + gate2 * jnp.concatenate(acc, axis=1)


def _combine(dest, gate, xs, mod3, ys, seq_len, tc):
    n_tok, d = xs.shape
    tiles_per_seq = seq_len // tc
    n_tiles = n_tok // tc
    return pl.pallas_call(
        _combine_kernel,
        out_shape=jax.ShapeDtypeStruct((n_tok, d), F32),
        grid=(n_tiles,),
        in_specs=[pl.BlockSpec((TOP_K, tc), lambda i: (0, i), memory_space=pltpu.SMEM),
                  pl.BlockSpec((TOP_K, tc), lambda i: (0, jnp.minimum(i + 1, n_tiles - 1)),
                               memory_space=pltpu.SMEM),
                  pl.BlockSpec((TOP_K, tc), lambda i: (0, i)),
                  pl.BlockSpec((tc, d), lambda i: (i, 0)),
                  pl.BlockSpec((1, 6, d), lambda i: (i // tiles_per_seq, 0, 0)),
                  pl.BlockSpec(memory_space=pl.ANY)],
        out_specs=pl.BlockSpec((tc, d), lambda i: (i, 0)),
        scratch_shapes=[pltpu.VMEM((2, COMBINE_PITCH * tc * ROW_TILE, LANES), U32),
                        pltpu.SemaphoreType.DMA((2,))],
        compiler_params=pltpu.CompilerParams(dimension_semantics=("arbitrary",),
                                             vmem_limit_bytes=VMEM_LIMIT),
        name="combine",
    )(dest, dest, gate, xs, mod3, ys)


def _tiles(seq_len):
    pick = lambda pref: min(pref, seq_len)
    return dict(adaln=1024, inproj=pick(512), attn=pick(512), ssm=pick(2048), mix=pick(512),
                dispatch=pick(512), combine=pick(256), slots=pick(2048), expert_block=4 * EXPERT_SUB)


def _rope_tables(seq_len):
    half = ATT_HEAD_DIM // 2
    inv_freq = 1.0 / (ROPE_THETA ** (jnp.arange(0, ATT_HEAD_DIM, 2, dtype=F32) / ATT_HEAD_DIM))
    ang = jnp.arange(seq_len, dtype=F32)[:, None] * inv_freq[None, :]
    cos, sin = jnp.cos(ang), jnp.sin(ang)
    reps = LANES // half
    sign = jnp.tile(jnp.concatenate([-jnp.ones((half,), F32), jnp.ones((half,), F32)]), reps // 2)
    return jnp.tile(cos, (1, reps)), jnp.tile(sin, (1, reps)) * sign[None, :]


def _block_tables(counts, blk, n_blocks):
    padded = (counts + blk - 1) // blk * blk
    pad_ends = jnp.cumsum(padded)
    pad_starts = pad_ends - padded
    used = pad_ends[-1] // blk
    last = jnp.maximum(used - 1, 0)
    starts = jnp.arange(n_blocks, dtype=I32) * blk
    expert = jnp.sum((pad_ends[None, :] <= starts[:, None]).astype(I32), axis=1)
    expert = jnp.minimum(expert, N_EXPERTS - 1)
    expert = jnp.where(starts < pad_ends[-1], expert, expert[last])
    valid = jnp.clip(counts[expert] - (starts - pad_starts[expert]), 0, blk)
    valid = jnp.where(starts < pad_ends[-1], valid, 0).astype(I32)
    ids = jnp.arange(N_EXPERTS, dtype=I32)
    later_used = jnp.logical_and(ids[None, :] > expert[:, None], (counts > 0)[None, :])
    upcoming = jnp.min(jnp.where(later_used, ids[None, :], N_EXPERTS), axis=1)
    upcoming = jnp.where(upcoming < N_EXPERTS, upcoming, -1).astype(I32)
    return pad_starts.astype(I32), expert, valid, upcoming, last.reshape(1).astype(I32)


def kernel(x, c, norm1_g, norm2_g, w_ada, b_ada, w_in, q_norm_g, k_norm_g, lambda_q1, lambda_k1, lambda_q2, lambda_k2, subln_g, ssm_a_re, ssm_a_im, ssm_log_dt, ssm_b_re, ssm_b_im, ssm_c_re, ssm_c_im, ssm_d, w_glu, ssm_norm_g, w_out, w_router, router_bias, w_gate_e, w_up_e, w_down_e, w_gate_s, w_up_s, w_down_s):
    bsz, seq_len, d = x.shape
    n_tok = bsz * seq_len
    tiles = _tiles(seq_len)
    assert d == 2 * ROW_TILE * LANES, "packed token rows are ROW_TILE x 128 words of two bf16 each"
    assert w_in.shape[1:] == (d, 2 * QK_WIDTH + ATT_WIDTH + SSM_WIDTH) and w_ada.shape[2] % tiles['adaln'] == 0
    assert w_router.shape[2] == N_EXPERTS and w_gate_e.shape[1] == N_EXPERTS
    assert all(seq_len % tiles[t] == 0 for t in ('inproj', 'attn', 'ssm', 'mix', 'dispatch',
                                                 'combine', 'slots')), "sequence length must be tile aligned"
    blk = tiles['expert_block']
    n_blocks = (n_tok * TOP_K + N_EXPERTS * (blk - 1) + blk - 1) // blk
    cos_t, sin_t = _rope_tables(seq_len)
    seg = jnp.kron(jnp.eye(QK_WIDTH // ATT_HEAD_DIM, dtype=F32),
                   jnp.full((ATT_HEAD_DIM, ATT_HEAD_DIM), 1.0 / ATT_HEAD_DIM, F32)).astype(BF16)
    reps = QK_WIDTH // ATT_HEAD_DIM
    x2d = x.reshape(n_tok, d).astype(F32)
    for layer in range(w_ada.shape[0]):
        lam_init = 0.8 - 0.6 * math.exp(-0.3 * layer)
        mod3 = _adaln(c.astype(F32), w_ada[layer].astype(F32), b_ada[layer].astype(F32),
                      tiles['adaln']).reshape(bsz, 6, d)
        q, k, v, u = _inproj(
            x2d, mod3, norm1_g[layer].astype(F32).reshape(1, d), w_in[layer].astype(BF16), seg,
            jnp.tile(q_norm_g[layer].astype(F32), reps).reshape(1, QK_WIDTH),
            jnp.tile(k_norm_g[layer].astype(F32), reps).reshape(1, QK_WIDTH),
            cos_t, sin_t, seq_len, tiles['inproj'])
        lam = (jnp.exp(jnp.sum(lambda_q1[layer].astype(F32) * lambda_k1[layer].astype(F32)))
               - jnp.exp(jnp.sum(lambda_q2[layer].astype(F32) * lambda_k2[layer].astype(F32))) + lam_init)
        att = _attention(q, k, v, jnp.full((1, LANES), lam, F32),
                         subln_g[layer].astype(F32).reshape(1, ATT_V_DIM),
                         bsz, seq_len, tiles['attn'], 1.0 - lam_init)
        kmat, bmat, cmat, bpow = _ssm_mats(ssm_a_re[layer], ssm_a_im[layer], ssm_log_dt[layer],
                                         ssm_b_re[layer], ssm_b_im[layer], ssm_c_re[layer],
                                         ssm_c_im[layer], ssm_d[layer])
        y = _ssm(u, kmat, bmat, cmat, bpow, bsz, seq_len, tiles['ssm'])
        wr_t = jnp.transpose(w_router[layer].astype(F32))
        wrh, wrl = _split_bf16(wr_t)
        w_shared_in = jnp.concatenate([w_gate_s[layer], w_up_s[layer]], axis=1).astype(BF16)
        w_shared_out = w_down_s[layer].astype(BF16)
        x1, h2p, eidx, gate, rank, counts = _mix_route(
            x2d, y, att, mod3, w_glu[layer].astype(BF16),
            ssm_norm_g[layer].astype(F32).reshape(1, SSM_WIDTH),
            w_out[layer, :ATT_WIDTH].astype(BF16), w_out[layer, ATT_WIDTH:].astype(BF16),
            norm2_g[layer].astype(F32).reshape(1, d), wrh, wrl,
            router_bias[layer].astype(F32).reshape(N_EXPERTS, 1),
            w_shared_in, w_shared_out, seq_len, tiles['mix'])
        pad_starts, block_expert, block_valid, next_expert, last_block = _block_tables(
            counts.reshape(N_EXPERTS).astype(I32), blk, n_blocks)
        dest = _slots(eidx, rank, pad_starts, tiles['slots'])
        x_slots, xs = _dispatch(dest, h2p, x1, mod3, w_shared_in, w_shared_out, n_blocks * blk, seq_len,
                                tiles['dispatch'])
        y_slots = _experts(block_expert, block_valid, next_expert, last_block, x_slots,
                           w_gate_e[layer], w_up_e[layer], w_down_e[layer], blk)
        x2d = _combine(dest, gate, xs, mod3, y_slots, seq_len, tiles['combine'])
    return x2d.reshape(bsz, seq_len, d).astype(x.dtype)
```
